```python
import math
import jax, jax.numpy as jnp
from jax import lax
import numpy as np

D_MODEL = 1024
BATCH = 8
SEQ = 8192
DEPTH = 4

PLE_DIM = 256
BRANCH_WIDTH = 512
N_BRANCH = 3
SSM_WIDTH = BRANCH_WIDTH
SSM_GROUP = 16
SSM_GROUPS = SSM_WIDTH // SSM_GROUP
SSM_STATE = 64
DT_MIN = 1e-3
DT_MAX = 1e-1
CONV_WIDTH = BRANCH_WIDTH
CONV_TAPS = 3
HEAD_DIM = 64
N_Q_HEADS = BRANCH_WIDTH // HEAD_DIM
N_KV_HEADS = 2
GQA_GROUP = N_Q_HEADS // N_KV_HEADS
ATTN_WIDTH = N_Q_HEADS * HEAD_DIM
KV_WIDTH = N_KV_HEADS * HEAD_DIM
WINDOW = 128
BLOCK = WINDOW
ATTN_SCALE = 1.0 / math.sqrt(HEAD_DIM)
REL_BUCKETS = 32
REL_MAX_DIST = 128
FFN_HIDDEN = -(-8 * D_MODEL // (3 * 256)) * 256
RMS_EPS = 1e-6

IN_SIZES = (SSM_WIDTH, CONV_WIDTH, CONV_WIDTH, CONV_WIDTH, ATTN_WIDTH, KV_WIDTH, KV_WIDTH, N_BRANCH * D_MODEL)
IN_WIDTH = SSM_WIDTH + 3 * CONV_WIDTH + ATTN_WIDTH + 2 * KV_WIDTH + N_BRANCH * D_MODEL

kernel_name = "hybrid_s5_shortconv_swa_gated_trunk"


def rms_norm(x, g):
    xf = x.astype(jnp.float32)
    y = xf * lax.rsqrt(jnp.mean(xf * xf, axis=-1, keepdims=True) + RMS_EPS)
    return (y * g.astype(jnp.float32)).astype(x.dtype)


def split_columns(z):
    offs, acc = [], 0
    for s in IN_SIZES[:-1]:
        acc += s
        offs.append(acc)
    return jnp.split(z, offs, axis=-1)


def t5_bucket(dist):
    exact = REL_BUCKETS // 2
    df = jnp.maximum(dist, 1).astype(jnp.float32)
    large = exact + (jnp.log(df / exact) / math.log(REL_MAX_DIST / exact) * (REL_BUCKETS - exact)).astype(jnp.int32)
    large = jnp.minimum(large, REL_BUCKETS - 1)
    return jnp.where(dist < exact, dist, large)


def band_bias_and_mask(rel_table, n_blocks):
    qi = jnp.arange(BLOCK)[:, None]
    kj = jnp.arange(2 * BLOCK)[None, :]
    dist = qi + BLOCK - kj
    band = (dist >= 0) & (dist < WINDOW)
    bucket = t5_bucket(jnp.clip(dist, 0, REL_MAX_DIST - 1))
    bias = jnp.transpose(rel_table[bucket], (2, 0, 1)).astype(jnp.float32)
    blk = jnp.arange(n_blocks)[:, None, None]
    valid = band[None] & ((blk > 0) | (kj[None] >= BLOCK))
    return bias, valid


def s5_ssm(u, lam_re, lam_im, b_re, b_im, c_re, c_im, d_skip, log_dt, w_glu):
    bsz, seq, _ = u.shape
    ug = u.reshape(bsz, seq, SSM_GROUPS, SSM_GROUP)
    dt = jnp.exp(log_dt)[:, None]
    mag = jnp.exp(lam_re * dt)
    ang = lam_im * dt
    a_re = mag * jnp.cos(ang)
    a_im = mag * jnp.sin(ang)
    den = lam_re * lam_re + lam_im * lam_im
    nr = a_re - 1.0
    coef_re = (nr * lam_re + a_im * lam_im) / den
    coef_im = (a_im * lam_re - nr * lam_im) / den
    bb_re = coef_re[..., None] * b_re - coef_im[..., None] * b_im
    bb_im = coef_re[..., None] * b_im + coef_im[..., None] * b_re
    bu_re = jnp.einsum('bsgp,gnp->bsgn', ug, bb_re)
    bu_im = jnp.einsum('bsgp,gnp->bsgn', ug, bb_im)
    a_re_t = jnp.broadcast_to(a_re[None, None], (1, seq, SSM_GROUPS, SSM_STATE))
    a_im_t = jnp.broadcast_to(a_im[None, None], (1, seq, SSM_GROUPS, SSM_STATE))

    def combine(left, right):
        a1r, a1i, b1r, b1i = left
        a2r, a2i, b2r, b2i = right
        return (a2r * a1r - a2i * a1i,
                a2r * a1i + a2i * a1r,
                a2r * b1r - a2i * b1i + b2r,
                a2r * b1i + a2i * b1r + b2i)

    _, _, h_re, h_im = lax.associative_scan(combine, (a_re_t, a_im_t, bu_re, bu_im), axis=1)
    y = jnp.einsum('gpn,bsgn->bsgp', c_re, h_re) - jnp.einsum('gpn,bsgn->bsgp', c_im, h_im)
    y = y.reshape(bsz, seq, SSM_WIDTH) + d_skip * u
    y = jax.nn.gelu(y)
    return y * jax.nn.sigmoid(y @ w_glu)


def short_conv(b_gate, c_gate, xc, conv_w):
    v = c_gate * xc
    vp = jnp.pad(v, ((0, 0), (CONV_TAPS - 1, 0), (0, 0)))
    seq = v.shape[1]
    y = conv_w[0] * vp[:, 0:seq] + conv_w[1] * vp[:, 1:seq + 1] + conv_w[2] * vp[:, 2:seq + 2]
    return b_gate * y


def swa_attention(q, k, v, sinks, bias, valid):
    bsz, seq, _ = q.shape
    nb = seq // BLOCK
    qb = q.reshape(bsz, nb, BLOCK, N_KV_HEADS, GQA_GROUP, HEAD_DIM)

    def with_prev(t):
        tb = t.reshape(bsz, nb, BLOCK, N_KV_HEADS, HEAD_DIM)
        prev = jnp.pad(tb, ((0, 0), (1, 0), (0, 0), (0, 0), (0, 0)))[:, :-1]
        return jnp.concatenate([prev, tb], axis=2)

    kb = with_prev(k)
    vb = with_prev(v)
    s = jnp.einsum('bnqhgd,bnkhd->bnhgqk', qb, kb).astype(jnp.float32) * ATTN_SCALE
    s = s + bias.reshape(N_KV_HEADS, GQA_GROUP, BLOCK, 2 * BLOCK)
    s = jnp.where(valid[None, :, None, None], s, -jnp.inf)
    sink = sinks.astype(jnp.float32).reshape(N_KV_HEADS, GQA_GROUP)[None, None, :, :, None, None]
    m = jnp.maximum(jnp.max(s, axis=-1, keepdims=True), sink)
    pexp = jnp.exp(s - m)
    w = pexp / (jnp.sum(pexp, axis=-1, keepdims=True) + jnp.exp(sink - m))
    o = jnp.einsum('bnhgqk,bnkhd->bnqhgd', w.astype(v.dtype), vb)
    return o.reshape(bsz, seq, ATTN_WIDTH)


def _fwd_setup_inputs(seed: int = 0) -> dict:
    key = jax.random.key(seed)
    ks = jax.random.split(key, 26)
    f32 = jnp.float32

    def nrm(k, shape, scale):
        return jax.random.normal(k, shape, f32) * scale

    n_idx = jnp.arange(SSM_STATE, dtype=f32)
    log_dt = jax.random.uniform(ks[10], (DEPTH, SSM_GROUPS), f32, math.log(DT_MIN), math.log(DT_MAX))
    return {
        "x": nrm(ks[0], (BATCH, SEQ, D_MODEL), 1.0),
        "p": nrm(ks[1], (DEPTH, BATCH, SEQ, PLE_DIM), 1.0),
        "rel_bias": nrm(ks[2], (REL_BUCKETS, N_Q_HEADS), 0.1),
        "norm_mix": 1.0 + nrm(ks[3], (DEPTH, D_MODEL), 0.02),
        "w_in": nrm(ks[4], (DEPTH, D_MODEL, IN_WIDTH), D_MODEL ** -0.5),
        "ssm_lambda_re": -0.5 + nrm(ks[5], (DEPTH, SSM_GROUPS, SSM_STATE), 0.01),
        "ssm_lambda_im": jnp.pi * n_idx + nrm(ks[6], (DEPTH, SSM_GROUPS, SSM_STATE), 0.01),
        "ssm_b_re": nrm(ks[7], (DEPTH, SSM_GROUPS, SSM_STATE, SSM_GROUP), (2 * SSM_GROUP) ** -0.5),
        "ssm_b_im": nrm(ks[8], (DEPTH, SSM_GROUPS, SSM_STATE, SSM_GROUP), (2 * SSM_GROUP) ** -0.5),
        "ssm_c_re": nrm(ks[9], (DEPTH, SSM_GROUPS, SSM_GROUP, SSM_STATE), SSM_STATE ** -0.5),
        "ssm_c_im": nrm(ks[11], (DEPTH, SSM_GROUPS, SSM_GROUP, SSM_STATE), SSM_STATE ** -0.5),
        "ssm_d": nrm(ks[12], (DEPTH, SSM_WIDTH), 1.0),
        "ssm_log_dt": log_dt,
        "ssm_w_glu": nrm(ks[13], (DEPTH, SSM_WIDTH, SSM_WIDTH), SSM_WIDTH ** -0.5),
        "conv_w": nrm(ks[14], (DEPTH, CONV_TAPS, CONV_WIDTH), CONV_TAPS ** -0.5),
        "attn_sinks": nrm(ks[15], (DEPTH, N_Q_HEADS), 0.5),
        "w_branch": nrm(ks[16], (DEPTH, N_BRANCH, BRANCH_WIDTH, D_MODEL), BRANCH_WIDTH ** -0.5),
        "w_out": nrm(ks[17], (DEPTH, D_MODEL, D_MODEL), D_MODEL ** -0.5),
        "norm_ffn": 1.0 + nrm(ks[18], (DEPTH, D_MODEL), 0.02),
        "w_ffn_in": nrm(ks[19], (DEPTH, D_MODEL, 2 * FFN_HIDDEN), D_MODEL ** -0.5),
        "w_ffn_out": nrm(ks[20], (DEPTH, FFN_HIDDEN, D_MODEL), FFN_HIDDEN ** -0.5),
        "norm_ple": 1.0 + nrm(ks[21], (DEPTH, D_MODEL), 0.02),
        "w_ple_gate": nrm(ks[22], (DEPTH, D_MODEL, D_MODEL), D_MODEL ** -0.5),
        "w_ple_proj": nrm(ks[23], (DEPTH, PLE_DIM, D_MODEL), PLE_DIM ** -0.5),
        "norm_final": 1.0 + nrm(ks[24], (D_MODEL,), 0.02),
    }


def _fwd_reference(x, p, rel_bias, norm_mix, w_in, ssm_lambda_re, ssm_lambda_im, ssm_b_re, ssm_b_im,
              ssm_c_re, ssm_c_im, ssm_d, ssm_log_dt, ssm_w_glu, conv_w, attn_sinks, w_branch, w_out,
              norm_ffn, w_ffn_in, w_ffn_out, norm_ple, w_ple_gate, w_ple_proj, norm_final):
    seq = x.shape[1]
    bias, valid = band_bias_and_mask(rel_bias, seq // BLOCK)
    for i in range(DEPTH):
        h = rms_norm(x, norm_mix[i])
        z = h @ w_in[i]
        u, cb, cc, cx, q, k, v, gates = split_columns(z)
        y_ssm = s5_ssm(u, ssm_lambda_re[i], ssm_lambda_im[i], ssm_b_re[i], ssm_b_im[i],
                       ssm_c_re[i], ssm_c_im[i], ssm_d[i], ssm_log_dt[i], ssm_w_glu[i])
        y_conv = short_conv(cb, cc, cx, conv_w[i])
        y_attn = swa_attention(q, k, v, attn_sinks[i], bias, valid)
        g = jax.nn.sigmoid(gates)
        merged = (g[..., 0:D_MODEL] * (y_ssm @ w_branch[i, 0])
                  + g[..., D_MODEL:2 * D_MODEL] * (y_conv @ w_branch[i, 1])
                  + g[..., 2 * D_MODEL:3 * D_MODEL] * (y_attn @ w_branch[i, 2]))
        x = x + merged @ w_out[i]
        hf = rms_norm(x, norm_ffn[i]) @ w_ffn_in[i]
        x = x + (jax.nn.silu(hf[..., :FFN_HIDDEN]) * hf[..., FFN_HIDDEN:]) @ w_ffn_out[i]
        pg = jax.nn.sigmoid(rms_norm(x, norm_ple[i]) @ w_ple_gate[i])
        x = x + pg * (p[i] @ w_ple_proj[i])
    return rms_norm(x, norm_final)


import jax as _jax
import jax.numpy as _jnp

TWIN_FORMAT = 'train_step'
FWD_PARAMS = ['x', 'p', 'rel_bias', 'norm_mix', 'w_in', 'ssm_lambda_re', 'ssm_lambda_im', 'ssm_b_re', 'ssm_b_im', 'ssm_c_re', 'ssm_c_im', 'ssm_d', 'ssm_log_dt', 'ssm_w_glu', 'conv_w', 'attn_sinks', 'w_branch', 'w_out', 'norm_ffn', 'w_ffn_in', 'w_ffn_out', 'norm_ple', 'w_ple_gate', 'w_ple_proj', 'norm_final']
TWIN_WEIGHTS = ['rel_bias', 'norm_mix', 'w_in', 'ssm_lambda_re', 'ssm_lambda_im', 'ssm_b_re', 'ssm_b_im', 'ssm_c_re', 'ssm_c_im', 'ssm_d', 'ssm_log_dt', 'ssm_w_glu', 'conv_w', 'attn_sinks', 'w_branch', 'w_out', 'norm_ffn', 'w_ffn_in', 'w_ffn_out', 'norm_ple', 'w_ple_gate', 'w_ple_proj', 'norm_final']
TWIN_DIFF_INPUT = 'x'
TWIN_INPUTS = ['x', 'p', 'rel_bias', 'norm_mix', 'w_in', 'ssm_lambda_re', 'ssm_lambda_im', 'ssm_b_re', 'ssm_b_im', 'ssm_c_re', 'ssm_c_im', 'ssm_d', 'ssm_log_dt', 'ssm_w_glu', 'conv_w', 'attn_sinks', 'w_branch', 'w_out', 'norm_ffn', 'w_ffn_in', 'w_ffn_out', 'norm_ple', 'w_ple_gate', 'w_ple_proj', 'norm_final', 'loss_target', 'm_rel_bias', 'm_norm_mix', 'm_w_in', 'm_ssm_lambda_re', 'm_ssm_lambda_im', 'm_ssm_b_re', 'm_ssm_b_im', 'm_ssm_c_re', 'm_ssm_c_im', 'm_ssm_d', 'm_ssm_log_dt', 'm_ssm_w_glu', 'm_conv_w', 'm_attn_sinks', 'm_w_branch', 'm_w_out', 'm_norm_ffn', 'm_w_ffn_in', 'm_w_ffn_out', 'm_norm_ple', 'm_w_ple_gate', 'm_w_ple_proj', 'm_norm_final', 'v_rel_bias', 'v_norm_mix', 'v_w_in', 'v_ssm_lambda_re', 'v_ssm_lambda_im', 'v_ssm_b_re', 'v_ssm_b_im', 'v_ssm_c_re', 'v_ssm_c_im', 'v_ssm_d', 'v_ssm_log_dt', 'v_ssm_w_glu', 'v_conv_w', 'v_attn_sinks', 'v_w_branch', 'v_w_out', 'v_norm_ffn', 'v_w_ffn_in', 'v_w_ffn_out', 'v_norm_ple', 'v_w_ple_gate', 'v_w_ple_proj', 'v_norm_final']
TWIN_OUTPUTS = ['loss', 'grad_x', 'grad_rel_bias', 'grad_norm_mix', 'grad_w_in', 'grad_ssm_lambda_re', 'grad_ssm_lambda_im', 'grad_ssm_b_re', 'grad_ssm_b_im', 'grad_ssm_c_re', 'grad_ssm_c_im', 'grad_ssm_d', 'grad_ssm_log_dt', 'grad_ssm_w_glu', 'grad_conv_w', 'grad_attn_sinks', 'grad_w_branch', 'grad_w_out', 'grad_norm_ffn', 'grad_w_ffn_in', 'grad_w_ffn_out', 'grad_norm_ple', 'grad_w_ple_gate', 'grad_w_ple_proj', 'grad_norm_final', 'delta_rel_bias', 'delta_norm_mix', 'delta_w_in', 'delta_ssm_lambda_re', 'delta_ssm_lambda_im', 'delta_ssm_b_re', 'delta_ssm_b_im', 'delta_ssm_c_re', 'delta_ssm_c_im', 'delta_ssm_d', 'delta_ssm_log_dt', 'delta_ssm_w_glu', 'delta_conv_w', 'delta_attn_sinks', 'delta_w_branch', 'delta_w_out', 'delta_norm_ffn', 'delta_w_ffn_in', 'delta_w_ffn_out', 'delta_norm_ple', 'delta_w_ple_gate', 'delta_w_ple_proj', 'delta_norm_final', 'new_m_rel_bias', 'new_m_norm_mix', 'new_m_w_in', 'new_m_ssm_lambda_re', 'new_m_ssm_lambda_im', 'new_m_ssm_b_re', 'new_m_ssm_b_im', 'new_m_ssm_c_re', 'new_m_ssm_c_im', 'new_m_ssm_d', 'new_m_ssm_log_dt', 'new_m_ssm_w_glu', 'new_m_conv_w', 'new_m_attn_sinks', 'new_m_w_branch', 'new_m_w_out', 'new_m_norm_ffn', 'new_m_w_ffn_in', 'new_m_w_ffn_out', 'new_m_norm_ple', 'new_m_w_ple_gate', 'new_m_w_ple_proj', 'new_m_norm_final', 'new_v_rel_bias', 'new_v_norm_mix', 'new_v_w_in', 'new_v_ssm_lambda_re', 'new_v_ssm_lambda_im', 'new_v_ssm_b_re', 'new_v_ssm_b_im', 'new_v_ssm_c_re', 'new_v_ssm_c_im', 'new_v_ssm_d', 'new_v_ssm_log_dt', 'new_v_ssm_w_glu', 'new_v_conv_w', 'new_v_attn_sinks', 'new_v_w_branch', 'new_v_w_out', 'new_v_norm_ffn', 'new_v_w_ffn_in', 'new_v_w_ffn_out', 'new_v_norm_ple', 'new_v_w_ple_gate', 'new_v_w_ple_proj', 'new_v_norm_final']
TWIN_LEAF_KINDS = {'loss': 'loss', 'grad_x': 'grad_x', 'grad_rel_bias': 'grad_w', 'grad_norm_mix': 'grad_w', 'grad_w_in': 'grad_w', 'grad_ssm_lambda_re': 'grad_w', 'grad_ssm_lambda_im': 'grad_w', 'grad_ssm_b_re': 'grad_w', 'grad_ssm_b_im': 'grad_w', 'grad_ssm_c_re': 'grad_w', 'grad_ssm_c_im': 'grad_w', 'grad_ssm_d': 'grad_w', 'grad_ssm_log_dt': 'grad_w', 'grad_ssm_w_glu': 'grad_w', 'grad_conv_w': 'grad_w', 'grad_attn_sinks': 'grad_w', 'grad_w_branch': 'grad_w', 'grad_w_out': 'grad_w', 'grad_norm_ffn': 'grad_w', 'grad_w_ffn_in': 'grad_w', 'grad_w_ffn_out': 'grad_w', 'grad_norm_ple': 'grad_w', 'grad_w_ple_gate': 'grad_w', 'grad_w_ple_proj': 'grad_w', 'grad_norm_final': 'grad_w', 'delta_rel_bias': 'delta_w', 'delta_norm_mix': 'delta_w', 'delta_w_in': 'delta_w', 'delta_ssm_lambda_re': 'delta_w', 'delta_ssm_lambda_im': 'delta_w', 'delta_ssm_b_re': 'delta_w', 'delta_ssm_b_im': 'delta_w', 'delta_ssm_c_re': 'delta_w', 'delta_ssm_c_im': 'delta_w', 'delta_ssm_d': 'delta_w', 'delta_ssm_log_dt': 'delta_w', 'delta_ssm_w_glu': 'delta_w', 'delta_conv_w': 'delta_w', 'delta_attn_sinks': 'delta_w', 'delta_w_branch': 'delta_w', 'delta_w_out': 'delta_w', 'delta_norm_ffn': 'delta_w', 'delta_w_ffn_in': 'delta_w', 'delta_w_ffn_out': 'delta_w', 'delta_norm_ple': 'delta_w', 'delta_w_ple_gate': 'delta_w', 'delta_w_ple_proj': 'delta_w', 'delta_norm_final': 'delta_w', 'new_m_rel_bias': 'new_m', 'new_m_norm_mix': 'new_m', 'new_m_w_in': 'new_m', 'new_m_ssm_lambda_re': 'new_m', 'new_m_ssm_lambda_im': 'new_m', 'new_m_ssm_b_re': 'new_m', 'new_m_ssm_b_im': 'new_m', 'new_m_ssm_c_re': 'new_m', 'new_m_ssm_c_im': 'new_m', 'new_m_ssm_d': 'new_m', 'new_m_ssm_log_dt': 'new_m', 'new_m_ssm_w_glu': 'new_m', 'new_m_conv_w': 'new_m', 'new_m_attn_sinks': 'new_m', 'new_m_w_branch': 'new_m', 'new_m_w_out': 'new_m', 'new_m_norm_ffn': 'new_m', 'new_m_w_ffn_in': 'new_m', 'new_m_w_ffn_out': 'new_m', 'new_m_norm_ple': 'new_m', 'new_m_w_ple_gate': 'new_m', 'new_m_w_ple_proj': 'new_m', 'new_m_norm_final': 'new_m', 'new_v_rel_bias': 'new_v', 'new_v_norm_mix': 'new_v', 'new_v_w_in': 'new_v', 'new_v_ssm_lambda_re': 'new_v', 'new_v_ssm_lambda_im': 'new_v', 'new_v_ssm_b_re': 'new_v', 'new_v_ssm_b_im': 'new_v', 'new_v_ssm_c_re': 'new_v', 'new_v_ssm_c_im': 'new_v', 'new_v_ssm_d': 'new_v', 'new_v_ssm_log_dt': 'new_v', 'new_v_ssm_w_glu': 'new_v', 'new_v_conv_w': 'new_v', 'new_v_attn_sinks': 'new_v', 'new_v_w_branch': 'new_v', 'new_v_w_out': 'new_v', 'new_v_norm_ffn': 'new_v', 'new_v_w_ffn_in': 'new_v', 'new_v_w_ffn_out': 'new_v', 'new_v_norm_ple': 'new_v', 'new_v_w_ple_gate': 'new_v', 'new_v_w_ple_proj': 'new_v', 'new_v_norm_final': 'new_v'}


def _forward(args):
    return _fwd_reference(*[args[k] for k in FWD_PARAMS])


def _output_shape():
    def fwd():
        inp = _fwd_setup_inputs(0)
        return _fwd_reference(*[inp[k] for k in FWD_PARAMS])
    out = _jax.eval_shape(fwd)
    return out.shape, out.dtype

N_MICROBATCH = 1
ADAM_LR = 0.001
ADAM_B1 = 0.9
ADAM_B2 = 0.999
ADAM_EPS = 1e-08
ADAM_WD = 0.01
ADAM_STEP = 10
PER_EXAMPLE_BATCH_AXIS = {'x': 0, 'p': 1, 'loss_target': 0}
SHARED_INPUTS = []
_WEIGHT_DTYPES = {'rel_bias': _jnp.float32, 'norm_mix': _jnp.float32, 'w_in': _jnp.float32, 'ssm_lambda_re': _jnp.float32, 'ssm_lambda_im': _jnp.float32, 'ssm_b_re': _jnp.float32, 'ssm_b_im': _jnp.float32, 'ssm_c_re': _jnp.float32, 'ssm_c_im': _jnp.float32, 'ssm_d': _jnp.float32, 'ssm_log_dt': _jnp.float32, 'ssm_w_glu': _jnp.float32, 'conv_w': _jnp.float32, 'attn_sinks': _jnp.float32, 'w_branch': _jnp.float32, 'w_out': _jnp.float32, 'norm_ffn': _jnp.float32, 'w_ffn_in': _jnp.float32, 'w_ffn_out': _jnp.float32, 'norm_ple': _jnp.float32, 'w_ple_gate': _jnp.float32, 'w_ple_proj': _jnp.float32, 'norm_final': _jnp.float32}
MOMENT_SCALE = {'rel_bias': 8.046151e-02, 'norm_mix': 2.120709e-01, 'w_in': 8.771812e-02, 'ssm_lambda_re': 4.295642e-03, 'ssm_lambda_im': 4.785501e-03, 'ssm_b_re': 2.698595e-03, 'ssm_b_im': 2.759668e-03, 'ssm_c_re': 3.912812e-03, 'ssm_c_im': 4.028836e-03, 'ssm_d': 6.169962e-02, 'ssm_log_dt': 3.491034e+00, 'ssm_w_glu': 1.641419e-02, 'conv_w': 1.652885e-01, 'attn_sinks': 2.706126e-02, 'w_branch': 7.073818e-02, 'w_out': 1.225823e-01, 'norm_ffn': 1.556760e-01, 'w_ffn_in': 6.552438e-02, 'w_ffn_out': 1.068518e-01, 'norm_ple': 3.741834e-02, 'w_ple_gate': 3.755565e-02, 'w_ple_proj': 9.613159e-02, 'norm_final': 6.398402e+01}


def _to_microbatches(a, axis):
    t = _jnp.moveaxis(a, axis, 0)
    t = t.reshape((N_MICROBATCH, t.shape[0] // N_MICROBATCH) + t.shape[1:])
    return _jnp.moveaxis(t, 1, axis + 1)


def setup_inputs(seed: int = 0) -> dict:
    inp = _fwd_setup_inputs(seed)
    key = _jax.random.fold_in(_jax.random.key(seed), 7919)
    shape, _ = _output_shape()
    out = dict(inp)
    out["loss_target"] = _jax.random.normal(_jax.random.fold_in(key, 0), shape, _jnp.float32)
    for i, name in enumerate(TWIN_WEIGHTS):
        w = inp[name].astype(_jnp.float32)
        if MOMENT_SCALE is None:
            s = _jnp.sqrt(_jnp.mean(_jnp.square(w)) + 1e-30)
        else:
            s = MOMENT_SCALE[name]
        km, kv = _jax.random.split(_jax.random.fold_in(key, i + 1))
        out[name] = w
        out["m_" + name] = s * _jax.random.normal(km, w.shape, _jnp.float32)
        out["v_" + name] = (s * s) * _jax.random.uniform(kv, w.shape, _jnp.float32, 0.5, 1.5)
    if N_MICROBATCH > 1:
        for name, axis in PER_EXAMPLE_BATCH_AXIS.items():
            out[name] = _to_microbatches(out[name], axis)
    return {'x': out['x'], 'p': out['p'], 'rel_bias': out['rel_bias'], 'norm_mix': out['norm_mix'], 'w_in': out['w_in'], 'ssm_lambda_re': out['ssm_lambda_re'], 'ssm_lambda_im': out['ssm_lambda_im'], 'ssm_b_re': out['ssm_b_re'], 'ssm_b_im': out['ssm_b_im'], 'ssm_c_re': out['ssm_c_re'], 'ssm_c_im': out['ssm_c_im'], 'ssm_d': out['ssm_d'], 'ssm_log_dt': out['ssm_log_dt'], 'ssm_w_glu': out['ssm_w_glu'], 'conv_w': out['conv_w'], 'attn_sinks': out['attn_sinks'], 'w_branch': out['w_branch'], 'w_out': out['w_out'], 'norm_ffn': out['norm_ffn'], 'w_ffn_in': out['w_ffn_in'], 'w_ffn_out': out['w_ffn_out'], 'norm_ple': out['norm_ple'], 'w_ple_gate': out['w_ple_gate'], 'w_ple_proj': out['w_ple_proj'], 'norm_final': out['norm_final'], 'loss_target': out['loss_target'], 'm_rel_bias': out['m_rel_bias'], 'm_norm_mix': out['m_norm_mix'], 'm_w_in': out['m_w_in'], 'm_ssm_lambda_re': out['m_ssm_lambda_re'], 'm_ssm_lambda_im': out['m_ssm_lambda_im'], 'm_ssm_b_re': out['m_ssm_b_re'], 'm_ssm_b_im': out['m_ssm_b_im'], 'm_ssm_c_re': out['m_ssm_c_re'], 'm_ssm_c_im': out['m_ssm_c_im'], 'm_ssm_d': out['m_ssm_d'], 'm_ssm_log_dt': out['m_ssm_log_dt'], 'm_ssm_w_glu': out['m_ssm_w_glu'], 'm_conv_w': out['m_conv_w'], 'm_attn_sinks': out['m_attn_sinks'], 'm_w_branch': out['m_w_branch'], 'm_w_out': out['m_w_out'], 'm_norm_ffn': out['m_norm_ffn'], 'm_w_ffn_in': out['m_w_ffn_in'], 'm_w_ffn_out': out['m_w_ffn_out'], 'm_norm_ple': out['m_norm_ple'], 'm_w_ple_gate': out['m_w_ple_gate'], 'm_w_ple_proj': out['m_w_ple_proj'], 'm_norm_final': out['m_norm_final'], 'v_rel_bias': out['v_rel_bias'], 'v_norm_mix': out['v_norm_mix'], 'v_w_in': out['v_w_in'], 'v_ssm_lambda_re': out['v_ssm_lambda_re'], 'v_ssm_lambda_im': out['v_ssm_lambda_im'], 'v_ssm_b_re': out['v_ssm_b_re'], 'v_ssm_b_im': out['v_ssm_b_im'], 'v_ssm_c_re': out['v_ssm_c_re'], 'v_ssm_c_im': out['v_ssm_c_im'], 'v_ssm_d': out['v_ssm_d'], 'v_ssm_log_dt': out['v_ssm_log_dt'], 'v_ssm_w_glu': out['v_ssm_w_glu'], 'v_conv_w': out['v_conv_w'], 'v_attn_sinks': out['v_attn_sinks'], 'v_w_branch': out['v_w_branch'], 'v_w_out': out['v_w_out'], 'v_norm_ffn': out['v_norm_ffn'], 'v_w_ffn_in': out['v_w_ffn_in'], 'v_w_ffn_out': out['v_w_ffn_out'], 'v_norm_ple': out['v_norm_ple'], 'v_w_ple_gate': out['v_w_ple_gate'], 'v_w_ple_proj': out['v_w_ple_proj'], 'v_norm_final': out['v_norm_final']}


def _loss(weights, diff, rest, loss_target):
    with _jax.named_scope("forward"):
        args = {**rest, TWIN_DIFF_INPUT: diff, **{k: w.astype(_WEIGHT_DTYPES[k]) for k, w in weights.items()}}
        y = _forward(args)
    with _jax.named_scope("loss_head"):
        err = _jnp.square(y.astype(_jnp.float32) - loss_target)
        return 0.5 * _jnp.sum(_jnp.mean(err, axis=-1)) if err.ndim else 0.5 * err


def _adamw(w, g, m, v):
    m = ADAM_B1 * m + (1.0 - ADAM_B1) * g
    v = ADAM_B2 * v + (1.0 - ADAM_B2) * _jnp.square(g)
    m_hat = m / (1.0 - ADAM_B1 ** ADAM_STEP)
    v_hat = v / (1.0 - ADAM_B2 ** ADAM_STEP)
    delta = -ADAM_LR * (m_hat / (_jnp.sqrt(v_hat) + ADAM_EPS) + ADAM_WD * w)
    return delta, m, v


def reference(x, p, rel_bias, norm_mix, w_in, ssm_lambda_re, ssm_lambda_im, ssm_b_re, ssm_b_im, ssm_c_re, ssm_c_im, ssm_d, ssm_log_dt, ssm_w_glu, conv_w, attn_sinks, w_branch, w_out, norm_ffn, w_ffn_in, w_ffn_out, norm_ple, w_ple_gate, w_ple_proj, norm_final, loss_target, m_rel_bias, m_norm_mix, m_w_in, m_ssm_lambda_re, m_ssm_lambda_im, m_ssm_b_re, m_ssm_b_im, m_ssm_c_re, m_ssm_c_im, m_ssm_d, m_ssm_log_dt, m_ssm_w_glu, m_conv_w, m_attn_sinks, m_w_branch, m_w_out, m_norm_ffn, m_w_ffn_in, m_w_ffn_out, m_norm_ple, m_w_ple_gate, m_w_ple_proj, m_norm_final, v_rel_bias, v_norm_mix, v_w_in, v_ssm_lambda_re, v_ssm_lambda_im, v_ssm_b_re, v_ssm_b_im, v_ssm_c_re, v_ssm_c_im, v_ssm_d, v_ssm_log_dt, v_ssm_w_glu, v_conv_w, v_attn_sinks, v_w_branch, v_w_out, v_norm_ffn, v_w_ffn_in, v_w_ffn_out, v_norm_ple, v_w_ple_gate, v_w_ple_proj, v_norm_final):
    given = dict(x=x, p=p, rel_bias=rel_bias, norm_mix=norm_mix, w_in=w_in, ssm_lambda_re=ssm_lambda_re, ssm_lambda_im=ssm_lambda_im, ssm_b_re=ssm_b_re, ssm_b_im=ssm_b_im, ssm_c_re=ssm_c_re, ssm_c_im=ssm_c_im, ssm_d=ssm_d, ssm_log_dt=ssm_log_dt, ssm_w_glu=ssm_w_glu, conv_w=conv_w, attn_sinks=attn_sinks, w_branch=w_branch, w_out=w_out, norm_ffn=norm_ffn, w_ffn_in=w_ffn_in, w_ffn_out=w_ffn_out, norm_ple=norm_ple, w_ple_gate=w_ple_gate, w_ple_proj=w_ple_proj, norm_final=norm_final, loss_target=loss_target, m_rel_bias=m_rel_bias, m_norm_mix=m_norm_mix, m_w_in=m_w_in, m_ssm_lambda_re=m_ssm_lambda_re, m_ssm_lambda_im=m_ssm_lambda_im, m_ssm_b_re=m_ssm_b_re, m_ssm_b_im=m_ssm_b_im, m_ssm_c_re=m_ssm_c_re, m_ssm_c_im=m_ssm_c_im, m_ssm_d=m_ssm_d, m_ssm_log_dt=m_ssm_log_dt, m_ssm_w_glu=m_ssm_w_glu, m_conv_w=m_conv_w, m_attn_sinks=m_attn_sinks, m_w_branch=m_w_branch, m_w_out=m_w_out, m_norm_ffn=m_norm_ffn, m_w_ffn_in=m_w_ffn_in, m_w_ffn_out=m_w_ffn_out, m_norm_ple=m_norm_ple, m_w_ple_gate=m_w_ple_gate, m_w_ple_proj=m_w_ple_proj, m_norm_final=m_norm_final, v_rel_bias=v_rel_bias, v_norm_mix=v_norm_mix, v_w_in=v_w_in, v_ssm_lambda_re=v_ssm_lambda_re, v_ssm_lambda_im=v_ssm_lambda_im, v_ssm_b_re=v_ssm_b_re, v_ssm_b_im=v_ssm_b_im, v_ssm_c_re=v_ssm_c_re, v_ssm_c_im=v_ssm_c_im, v_ssm_d=v_ssm_d, v_ssm_log_dt=v_ssm_log_dt, v_ssm_w_glu=v_ssm_w_glu, v_conv_w=v_conv_w, v_attn_sinks=v_attn_sinks, v_w_branch=v_w_branch, v_w_out=v_w_out, v_norm_ffn=v_norm_ffn, v_w_ffn_in=v_w_ffn_in, v_w_ffn_out=v_w_ffn_out, v_norm_ple=v_norm_ple, v_w_ple_gate=v_w_ple_gate, v_w_ple_proj=v_w_ple_proj, v_norm_final=v_norm_final)
    weights = {n: given[n] for n in TWIN_WEIGHTS}
    shared = {n: given[n] for n in SHARED_INPUTS}
    per_example = {n: given[n] for n in ['x', 'p']}
    grad_fn = _jax.value_and_grad(_loss, argnums=(0, 1))

    def one_microbatch(ex, loss_target):
        ex = dict(ex)
        diff = ex.pop(TWIN_DIFF_INPUT)
        return grad_fn(weights, diff, {**shared, **ex}, loss_target)

    if N_MICROBATCH == 1:
        loss, (grad_w, grad_x) = one_microbatch(per_example, given["loss_target"])
    else:
        def body(carry, xs):
            loss_sum, grad_sum = carry
            l_k, (gw_k, gx_k) = one_microbatch(xs[0], xs[1])
            with _jax.named_scope("update"):
                return (loss_sum + l_k, _jax.tree.map(_jnp.add, grad_sum, gw_k)), gx_k

        init = (_jnp.zeros((), _jnp.float32), _jax.tree.map(_jnp.zeros_like, weights))
        (loss, grad_w), grad_x = _jax.lax.scan(body, init, (per_example, given["loss_target"]))
    with _jax.named_scope("update"):
        delta_w, new_m, new_v = {}, {}, {}
        for n in TWIN_WEIGHTS:
            delta_w[n], new_m[n], new_v[n] = _adamw(weights[n], grad_w[n], given["m_" + n], given["v_" + n])
    return (loss, grad_x, *[grad_w[n] for n in TWIN_WEIGHTS], *[delta_w[n] for n in TWIN_WEIGHTS],
            *[new_m[n] for n in TWIN_WEIGHTS], *[new_v[n] for n in TWIN_WEIGHTS])
```

```python
import functools
import math

import numpy as np

import jax
import jax.numpy as jnp
from jax import lax
from jax.experimental import pallas as pl
from jax.experimental.pallas import tpu as pltpu

F32, BF16 = jnp.float32, jnp.bfloat16
MESH = pl.DeviceIdType.MESH

D_MODEL = 1024
DEPTH = 4
PLE_DIM = 256
BRANCH = 512
N_GROUPS = 32
GROUP_CH = 16
N_STATE = 64
SSM_STATES = N_GROUPS * N_STATE
SSM_BLOCKS = 4
HEAD_DIM = 64
N_Q = 8
N_KV = 2
GQA = N_Q // N_KV
WINDOW = 128
ATTN_SCALE = 1.0 / math.sqrt(HEAD_DIM)
REL_BUCKETS = 32
REL_MAX_DIST = 128
FFN_HIDDEN = 2816
IN_WIDTH = 5888
RMS_EPS = 1e-6
NEG = -1e30

ADAM_LR, ADAM_B1, ADAM_B2, ADAM_EPS, ADAM_WD, ADAM_STEP = 0.001, 0.9, 0.999, 1e-08, 0.01, 10

N_SHARD = 4
LANES = 1024

OFF_U, OFF_CB, OFF_CC, OFF_CX, OFF_Q, OFF_K, OFF_V, OFF_G = 0, 512, 1024, 1536, 2048, 2560, 2688, 2816

BIG = (
    ("w_in", (D_MODEL, IN_WIDTH), 1),
    ("ssm_w_glu", (BRANCH, BRANCH), 0),
    ("conv_w", (3, BRANCH), 1),
    ("w_branch", (3, BRANCH, D_MODEL), 2),
    ("w_out", (D_MODEL, D_MODEL), 0),
    ("w_ffn_in", (D_MODEL, 2 * FFN_HIDDEN), 1),
    ("w_ffn_out", (FFN_HIDDEN, D_MODEL), 0),
    ("w_ple_gate", (D_MODEL, D_MODEL), 0),
    ("w_ple_proj", (PLE_DIM, D_MODEL), 1),
)
SMALL = ("rel_bias", "norm_mix", "ssm_lambda_re", "ssm_lambda_im", "ssm_b_re", "ssm_b_im", "ssm_c_re", "ssm_c_im",
         "ssm_d", "ssm_log_dt", "attn_sinks", "norm_ffn", "norm_ple", "norm_final")
WEIGHTS = ("rel_bias", "norm_mix", "w_in", "ssm_lambda_re", "ssm_lambda_im", "ssm_b_re", "ssm_b_im", "ssm_c_re",
           "ssm_c_im", "ssm_d", "ssm_log_dt", "ssm_w_glu", "conv_w", "attn_sinks", "w_branch", "w_out", "norm_ffn",
           "w_ffn_in", "w_ffn_out", "norm_ple", "w_ple_gate", "w_ple_proj", "norm_final")


def _c0(j):
    return 0


def _pick(n, cap, unit=128):
    if n <= cap:
        return n
    best = None
    for t in range(unit, cap + 1, unit):
        if n % t == 0:
            best = t
    assert best is not None, (n, cap, unit)
    return best


_DIMS = {"nn": ((1,), (0,)), "nt": ((1,), (1,)), "tn": ((0,), (0,))}


def _mm(a, b, mode, *, name, out_dtype=F32, add=None, tm=512, tn=512, tk=1024):
    if mode == "nn":
        (m, k), (k2, n) = a.shape, b.shape
    elif mode == "nt":
        (m, k), (n, k2) = a.shape, b.shape
    else:
        (k, m), (k2, n) = a.shape, b.shape
    assert k == k2, (a.shape, b.shape, mode)
    tm, tn, tk = _pick(m, tm, 128 if mode == "tn" else 8), _pick(n, tn), _pick(k, tk, 128 if mode != "tn" else 8)
    nk = k // tk
    a_spec = pl.BlockSpec((tk, tm), lambda i, j, kk: (kk, i)) if mode == "tn" else pl.BlockSpec((tm, tk), lambda i, j, kk: (i, kk))
    b_spec = pl.BlockSpec((tn, tk), lambda i, j, kk: (j, kk)) if mode == "nt" else pl.BlockSpec((tk, tn), lambda i, j, kk: (kk, j))
    o_spec = pl.BlockSpec((tm, tn), lambda i, j, kk: (i, j))
    dims = (_DIMS[mode], ((), ()))
    has_add = add is not None

    def body(*refs):
        a_ref, b_ref = refs[0], refs[1]
        add_ref = refs[2] if has_add else None
        o_ref, acc_ref = refs[-2], refs[-1]
        part = lax.dot_general(a_ref[...].astype(BF16), b_ref[...].astype(BF16), dims, preferred_element_type=F32)

        def finish(acc):
            if has_add:
                acc = acc + add_ref[...]
            o_ref[...] = acc.astype(o_ref.dtype)

        if nk == 1:
            finish(part)
        else:
            kk = pl.program_id(2)

            @pl.when(kk == 0)
            def _():
                acc_ref[...] = part

            @pl.when(kk > 0)
            def _():
                acc_ref[...] += part

            @pl.when(kk == nk - 1)
            def _():
                finish(acc_ref[...])

    operands = [a, b] + ([add] if has_add else [])
    in_specs = [a_spec, b_spec] + ([o_spec] if has_add else [])
    return pl.pallas_call(
        body, name=name, grid=(m // tm, n // tn, nk), in_specs=in_specs, out_specs=o_spec,
        out_shape=jax.ShapeDtypeStruct((m, n), out_dtype),
        scratch_shapes=[pltpu.VMEM((tm, tn) if nk > 1 else (8, 128), F32)],
        compiler_params=pltpu.CompilerParams(dimension_semantics=("parallel", "parallel", "arbitrary")),
    )(*operands)


def _rw(fn, ins, outs, *, name, params=(), reds=(), tm=256, ncol=1, with_j=False):
    t = ins[0][0].shape[0]
    tm = _pick(t, tm, 8)
    nrow = t // tm
    n_in, n_p, n_out = len(ins), len(params), len(outs)

    in_specs = [pl.BlockSpec((tm, bw or arr.shape[1]), lambda j, i, cf=cf: (i, cf(j))) for arr, bw, cf in ins]
    in_specs += [pl.BlockSpec(p.shape, lambda j, i: (0, 0)) for p in params]
    out_specs = [pl.BlockSpec((tm, bw or w), lambda j, i, cf=cf: (i, cf(j))) for w, bw, _, cf in outs]
    out_specs += [pl.BlockSpec((shp[0], bw or shp[1]), lambda j, i, cf=cf: (0, cf(j))) for shp, bw, cf in reds]
    out_shape = [jax.ShapeDtypeStruct((t, w), dt) for w, _, dt, _ in outs]
    out_shape += [jax.ShapeDtypeStruct(shp, F32) for shp, _, _ in reds]

    def body(*refs):
        in_refs, p_refs = refs[:n_in], refs[n_in:n_in + n_p]
        o_refs, r_refs = refs[n_in + n_p:n_in + n_p + n_out], refs[n_in + n_p + n_out:]
        args = [r[...] for r in in_refs] + [r[...] for r in p_refs]
        if with_j:
            args = [pl.program_id(0)] + args
        o_vals, r_vals = fn(*args)
        for r, v in zip(o_refs, o_vals, strict=True):
            r[...] = v.astype(r.dtype)
        if r_refs:
            i = pl.program_id(1)
            for r, v in zip(r_refs, r_vals, strict=True):
                @pl.when(i == 0)
                def _(r=r, v=v):
                    r[...] = v

                @pl.when(i > 0)
                def _(r=r, v=v):
                    r[...] += v

    res = pl.pallas_call(
        body, name=name, grid=(ncol, nrow), in_specs=in_specs, out_specs=out_specs, out_shape=out_shape,
        compiler_params=pltpu.CompilerParams(dimension_semantics=("parallel", "arbitrary" if reds else "parallel")),
    )(*[a for a, _, _ in ins], *params)
    return res


def _rms(x, g):
    return x * lax.rsqrt(jnp.mean(x * x, axis=-1, keepdims=True) + RMS_EPS) * g


def _rms_fwd(x, g, name):
    return _rw(lambda xv, gv: ([_rms(xv, gv)], []), [(x, None, _c0)], [(D_MODEL, None, BF16, _c0)], params=[g], name=name)[0]


def _rms_bwd(x, dh, dres, g, name):
    def fn(xv, dhv, drv, gv):
        _, vjp = jax.vjp(_rms, xv, gv)
        dx, dg = vjp(dhv)
        return [drv + dx], [dg]

    return _rw(fn, [(x, None, _c0), (dh, None, _c0), (dres, None, _c0)], [(D_MODEL, None, F32, _c0)], params=[g],
               reds=[((1, D_MODEL), None, _c0)], name=name)


def _bd_nn(a, w, *, name, tm=512):
    t = a.shape[0]
    nb, ka, n = w.shape
    tm = _pick(t, tm, 8)

    def body(a_ref, w_ref, o_ref):
        o_ref[...] = jnp.dot(a_ref[...].astype(BF16), w_ref[0], preferred_element_type=F32)

    return pl.pallas_call(
        body, name=name, grid=(t // tm, nb),
        in_specs=[pl.BlockSpec((tm, ka), lambda i, j: (i, j)), pl.BlockSpec((1, ka, n), lambda i, j: (j, 0, 0))],
        out_specs=pl.BlockSpec((tm, n), lambda i, j: (i, j)), out_shape=jax.ShapeDtypeStruct((t, nb * n), F32),
        compiler_params=pltpu.CompilerParams(dimension_semantics=("parallel", "parallel")),
    )(a, w)


def _bd_nt(a, w, *, name, tm=512):
    t = a.shape[0]
    nb, ka, n = w.shape
    tm = _pick(t, tm, 8)

    def body(a_ref, w_ref, o_ref):
        o_ref[...] = lax.dot_general(a_ref[...].astype(BF16), w_ref[0], (_DIMS["nt"], ((), ())), preferred_element_type=F32)

    return pl.pallas_call(
        body, name=name, grid=(t // tm, nb),
        in_specs=[pl.BlockSpec((tm, n), lambda i, j: (i, j)), pl.BlockSpec((1, ka, n), lambda i, j: (j, 0, 0))],
        out_specs=pl.BlockSpec((tm, ka), lambda i, j: (i, j)), out_shape=jax.ShapeDtypeStruct((t, nb * ka), F32),
        compiler_params=pltpu.CompilerParams(dimension_semantics=("parallel", "parallel")),
    )(a, w)


def _bd_tn(a, b, ka, n, *, name, tk=512):
    t = a.shape[0]
    tk = _pick(t, tk, 8)

    def body(a_ref, b_ref, o_ref):
        part = lax.dot_general(a_ref[...].astype(BF16), b_ref[...].astype(BF16), (_DIMS["tn"], ((), ())), preferred_element_type=F32)
        kk = pl.program_id(1)

        @pl.when(kk == 0)
        def _():
            o_ref[0] = part

        @pl.when(kk > 0)
        def _():
            o_ref[0] += part

    return pl.pallas_call(
        body, name=name, grid=(SSM_BLOCKS, t // tk),
        in_specs=[pl.BlockSpec((tk, ka), lambda j, kk: (kk, j)), pl.BlockSpec((tk, n), lambda j, kk: (kk, j))],
        out_specs=pl.BlockSpec((1, ka, n), lambda j, kk: (j, 0, 0)), out_shape=jax.ShapeDtypeStruct((SSM_BLOCKS, ka, n), F32),
        compiler_params=pltpu.CompilerParams(dimension_semantics=("parallel", "arbitrary")),
    )(a, b)


SCAN_LW = 256
SCAN_ROWS = 512
_DOUBLING = ((1, 0), (2, 1), (4, 2))


def _scan(xr, xi, pr, pi, dr, di, *, reverse, name, hr=None, hi=None):
    t, s = xr.shape
    lc = _pick(t, SCAN_ROWS, 8)
    nt, ngroups = t // lc, lc // 8
    with_da = hr is not None

    def tmap(l, tt):
        return ((nt - 1 - tt) if reverse else tt, l)

    x_spec = pl.BlockSpec((lc, SCAN_LW), tmap)
    tab_spec = pl.BlockSpec((8, SCAN_LW), lambda l, tt: (0, l))

    def body(*refs):
        xr_ref, xi_ref, pr_ref, pi_ref, dr_ref, di_ref = refs[:6]
        if with_da:
            hr_ref, hi_ref, or_ref, oi_ref, ar_ref, ai_ref, cr_ref, ci_ref = refs[6:]
        else:
            or_ref, oi_ref, cr_ref, ci_ref = refs[6:]
        tt = pl.program_id(1)

        @pl.when(tt == 0)
        def _():
            cr_ref[...] = jnp.zeros_like(cr_ref)
            ci_ref[...] = jnp.zeros_like(ci_ref)
            if with_da:
                ar_ref[...] = jnp.zeros_like(ar_ref)
                ai_ref[...] = jnp.zeros_like(ai_ref)

        sub = lax.broadcasted_iota(jnp.int32, (8, SCAN_LW), 0)
        pw_r, pw_i = pr_ref[...], pi_ref[...]

        def step(g, carry):
            g = (ngroups - 1 - g) if reverse else g
            r0 = pl.multiple_of(g * 8, 8)
            vr, vi = xr_ref[pl.ds(r0, 8), :], xi_ref[pl.ds(r0, 8), :]
            for shift, row in _DOUBLING:
                a_r, a_i = dr_ref[row:row + 1, :], di_ref[row:row + 1, :]
                if reverse:
                    keep = sub < 8 - shift
                    sr, si = pltpu.roll(vr, 8 - shift, 0), pltpu.roll(vi, 8 - shift, 0)
                else:
                    keep = sub >= shift
                    sr, si = pltpu.roll(vr, shift, 0), pltpu.roll(vi, shift, 0)
                sr, si = jnp.where(keep, sr, 0.0), jnp.where(keep, si, 0.0)
                vr, vi = vr + a_r * sr - a_i * si, vi + a_r * si + a_i * sr
            if with_da:
                cr, ci, acc_r, acc_i = carry
            else:
                cr, ci = carry
            vr, vi = vr + pw_r * cr - pw_i * ci, vi + pw_r * ci + pw_i * cr
            or_ref[pl.ds(r0, 8), :] = vr
            oi_ref[pl.ds(r0, 8), :] = vi
            if with_da:
                nr = jnp.where(sub < 7, pltpu.roll(vr, 7, 0), cr)
                ni = jnp.where(sub < 7, pltpu.roll(vi, 7, 0), ci)
                h_r, h_i = hr_ref[pl.ds(r0, 8), :], hi_ref[pl.ds(r0, 8), :]
                acc_r = acc_r + h_r * nr + h_i * ni
                acc_i = acc_i + h_r * ni - h_i * nr
            edge = 0 if reverse else 7
            cr = jnp.broadcast_to(vr[edge:edge + 1, :], vr.shape)
            ci = jnp.broadcast_to(vi[edge:edge + 1, :], vi.shape)
            return (cr, ci, acc_r, acc_i) if with_da else (cr, ci)

        zero = jnp.zeros((8, SCAN_LW), F32)
        init = (cr_ref[...], ci_ref[...]) + ((zero, zero) if with_da else ())
        fin = lax.fori_loop(0, ngroups, step, init)
        cr_ref[...] = fin[0]
        ci_ref[...] = fin[1]
        if with_da:
            ar_ref[...] += fin[2]
            ai_ref[...] += fin[3]

    n_x = 4 if with_da else 2
    out_specs = [x_spec, x_spec] + ([tab_spec, tab_spec] if with_da else [])
    out_shape = [jax.ShapeDtypeStruct((t, s), F32)] * 2 + ([jax.ShapeDtypeStruct((8, s), F32)] * 2 if with_da else [])
    operands = [xr, xi, pr, pi, dr, di] + ([hr, hi] if with_da else [])
    return pl.pallas_call(
        body, name=name, grid=(s // SCAN_LW, nt),
        in_specs=[x_spec, x_spec] + [tab_spec] * 4 + [x_spec] * (n_x - 2),
        out_specs=out_specs, out_shape=out_shape,
        scratch_shapes=[pltpu.VMEM((8, SCAN_LW), F32), pltpu.VMEM((8, SCAN_LW), F32)],
        compiler_params=pltpu.CompilerParams(dimension_semantics=("parallel", "arbitrary")),
    )(*operands)


CONV_TM = 256


def _conv_specs(t, tm):
    nrow = t // tm
    hb = tm // 8

    def col(cidx):
        return pl.BlockSpec((tm, BRANCH), lambda i: (i, cidx))

    def prev(cidx):
        return pl.BlockSpec((8, BRANCH), lambda i: (jnp.maximum(i * hb - 1, 0), cidx))

    def nxt(cidx):
        return pl.BlockSpec((8, BRANCH), lambda i: (jnp.minimum((i + 1) * hb, nrow * hb - 1), cidx))

    return nrow, col, prev, nxt


def _conv_taps(cc, cx, cc_prev, cx_prev, first):
    tm = cc.shape[0]
    v = cc * cx
    halo = cc_prev * cx_prev * jnp.where(first, 0.0, 1.0)
    ext = jnp.concatenate([halo, v], axis=0)
    return v, pltpu.roll(ext, 1, 0)[8:8 + tm], pltpu.roll(ext, 2, 0)[8:8 + tm]


def _conv_fwd(z, conv_w, name):
    t = z.shape[0]
    tm = _pick(t, CONV_TM, 8)
    nrow, col, prev, _ = _conv_specs(t, tm)

    def body(cb_ref, cc_ref, cx_ref, ccp_ref, cxp_ref, w_ref, o_ref):
        first = pl.program_id(0) == 0
        v, v1, v2 = _conv_taps(cc_ref[...], cx_ref[...], ccp_ref[...], cxp_ref[...], first)
        y = w_ref[0:1, :] * v2 + w_ref[1:2, :] * v1 + w_ref[2:3, :] * v
        o_ref[...] = (cb_ref[...] * y).astype(o_ref.dtype)

    return pl.pallas_call(
        body, name=name, grid=(nrow,),
        in_specs=[col(1), col(2), col(3), prev(2), prev(3), pl.BlockSpec((3, BRANCH), lambda i: (0, 0))],
        out_specs=pl.BlockSpec((tm, BRANCH), lambda i: (i, 0)), out_shape=jax.ShapeDtypeStruct((t, BRANCH), BF16),
        compiler_params=pltpu.CompilerParams(dimension_semantics=("parallel",)),
    )(z, z, z, z, z, conv_w)


def _conv_bwd(dyc, z, conv_w, name):
    t = z.shape[0]
    tm = _pick(t, CONV_TM, 8)
    nrow, col, prev, nxt = _conv_specs(t, tm)
    d_cur = pl.BlockSpec((tm, BRANCH), lambda i: (i, 0))
    d_nxt = pl.BlockSpec((8, BRANCH), lambda i: (jnp.minimum((i + 1) * (tm // 8), nrow * (tm // 8) - 1), 0))

    def body(dy_ref, dyn_ref, cb_ref, cbn_ref, cc_ref, cx_ref, ccp_ref, cxp_ref, w_ref, dcb_ref, dcc_ref, dcx_ref, dw_ref):
        i = pl.program_id(0)
        cc, cx, cb = cc_ref[...], cx_ref[...], cb_ref[...]
        v, v1, v2 = _conv_taps(cc, cx, ccp_ref[...], cxp_ref[...], i == 0)
        w0, w1, w2 = w_ref[0:1, :], w_ref[1:2, :], w_ref[2:3, :]
        y = w0 * v2 + w1 * v1 + w2 * v
        dyc_v = dy_ref[...]
        dcb_ref[...] = (dyc_v * y).astype(dcb_ref.dtype)
        dy = dyc_v * cb
        halo = dyn_ref[...] * cbn_ref[...] * jnp.where(i == nrow - 1, 0.0, 1.0)
        ext = jnp.concatenate([dy, halo], axis=0)
        dy1 = pltpu.roll(ext, tm + 8 - 1, 0)[0:tm]
        dy2 = pltpu.roll(ext, tm + 8 - 2, 0)[0:tm]
        dv = w2 * dy + w1 * dy1 + w0 * dy2
        dcc_ref[...] = (dv * cx).astype(dcc_ref.dtype)
        dcx_ref[...] = (dv * cc).astype(dcx_ref.dtype)
        dw = jnp.concatenate([jnp.sum(dy * v2, axis=0, keepdims=True), jnp.sum(dy * v1, axis=0, keepdims=True),
                              jnp.sum(dy * v, axis=0, keepdims=True), jnp.zeros((5, BRANCH), F32)], axis=0)

        @pl.when(i == 0)
        def _():
            dw_ref[...] = dw

        @pl.when(i > 0)
        def _():
            dw_ref[...] += dw

    o_spec = pl.BlockSpec((tm, BRANCH), lambda i: (i, 0))
    return pl.pallas_call(
        body, name=name, grid=(nrow,),
        in_specs=[d_cur, d_nxt, col(1), nxt(1), col(2), col(3), prev(2), prev(3), pl.BlockSpec((3, BRANCH), lambda i: (0, 0))],
        out_specs=[o_spec, o_spec, o_spec, pl.BlockSpec((8, BRANCH), lambda i: (0, 0))],
        out_shape=[jax.ShapeDtypeStruct((t, BRANCH), BF16)] * 3 + [jax.ShapeDtypeStruct((8, BRANCH), F32)],
        compiler_params=pltpu.CompilerParams(dimension_semantics=("arbitrary",)),
    )(dyc, dyc, z, z, z, z, z, z, conv_w)


def _attn_specs():
    q_spec = pl.BlockSpec((N_Q, WINDOW, HEAD_DIM), lambda n: (0, n, 0))
    kv_cur = pl.BlockSpec((N_KV, WINDOW, HEAD_DIM), lambda n: (0, n, 0))
    kv_prev = pl.BlockSpec((N_KV, WINDOW, HEAD_DIM), lambda n: (0, jnp.maximum(n - 1, 0), 0))
    bias_spec = pl.BlockSpec((N_Q, WINDOW, 2 * WINDOW), lambda n: (0, 0, 0))
    sink_spec = pl.BlockSpec((N_Q, 1), lambda n: (0, 0))
    return q_spec, kv_cur, kv_prev, bias_spec, sink_spec


def _attn_valid(n):
    qi = lax.broadcasted_iota(jnp.int32, (WINDOW, 2 * WINDOW), 0)
    kj = lax.broadcasted_iota(jnp.int32, (WINDOW, 2 * WINDOW), 1)
    dist = qi + WINDOW - kj
    first_key = jnp.where(n > 0, 0, WINDOW)
    return (dist >= 0) & (dist < WINDOW) & (kj >= first_key)


def _attn_probs(q, kc, bias, sink, valid):
    s = lax.dot_general(q, kc, (_DIMS["nt"], ((), ())), preferred_element_type=F32) * ATTN_SCALE + bias
    s = jnp.where(valid, s, NEG)
    m = jnp.maximum(jnp.max(s, axis=1, keepdims=True), sink)
    p = jnp.exp(s - m)
    e_sink = jnp.exp(sink - m)
    inv = 1.0 / (jnp.sum(p, axis=1, keepdims=True) + e_sink)
    return p * inv, e_sink * inv


def _attn_fwd(qh, kh, vh, bias, sinks, name):
    t = qh.shape[1]
    q_spec, kv_cur, kv_prev, bias_spec, sink_spec = _attn_specs()

    def body(q_ref, kp_ref, kc_ref, vp_ref, vc_ref, b_ref, s_ref, o_ref):
        valid = _attn_valid(pl.program_id(0))
        for h in range(N_KV):
            kc = jnp.concatenate([kp_ref[h], kc_ref[h]], axis=0)
            vc = jnp.concatenate([vp_ref[h], vc_ref[h]], axis=0)
            for g in range(GQA):
                hq = GQA * h + g
                w, _ = _attn_probs(q_ref[hq], kc, b_ref[hq], s_ref[hq:hq + 1, :], valid)
                o_ref[hq] = jnp.dot(w.astype(BF16), vc, preferred_element_type=F32).astype(o_ref.dtype)

    return pl.pallas_call(
        body, name=name, grid=(t // WINDOW,),
        in_specs=[q_spec, kv_prev, kv_cur, kv_prev, kv_cur, bias_spec, sink_spec],
        out_specs=q_spec, out_shape=jax.ShapeDtypeStruct((N_Q, t, HEAD_DIM), BF16),
        compiler_params=pltpu.CompilerParams(dimension_semantics=("parallel",)),
    )(qh, kh, kh, vh, vh, bias, sinks)


def _attn_bwd(qh, kh, vh, doh, bias, sinks, name):
    t = qh.shape[1]
    q_spec, kv_cur, kv_prev, bias_spec, sink_spec = _attn_specs()

    def body(q_ref, kp_ref, kc_ref, vp_ref, vc_ref, do_ref, b_ref, s_ref,
             dq_ref, dkc_ref, dkp_ref, dvc_ref, dvp_ref, db_ref, ds_ref):
        n = pl.program_id(0)
        valid = _attn_valid(n)

        @pl.when(n == 0)
        def _():
            db_ref[...] = jnp.zeros_like(db_ref)
            ds_ref[...] = jnp.zeros_like(ds_ref)

        for h in range(N_KV):
            kc = jnp.concatenate([kp_ref[h], kc_ref[h]], axis=0)
            vc = jnp.concatenate([vp_ref[h], vc_ref[h]], axis=0)
            dk = jnp.zeros((2 * WINDOW, HEAD_DIM), F32)
            dv = jnp.zeros((2 * WINDOW, HEAD_DIM), F32)
            for g in range(GQA):
                hq = GQA * h + g
                q, do = q_ref[hq], do_ref[hq]
                w, w_sink = _attn_probs(q, kc, b_ref[hq], s_ref[hq:hq + 1, :], valid)
                dw = lax.dot_general(do, vc, (_DIMS["nt"], ((), ())), preferred_element_type=F32)
                delta = jnp.sum(w * dw, axis=1, keepdims=True)
                dscore = w * (dw - delta)
                ds_ref[hq] += -w_sink * delta
                db_ref[hq] += dscore
                dsb = dscore.astype(BF16)
                dq_ref[hq] = jnp.dot(dsb, kc, preferred_element_type=F32) * ATTN_SCALE
                dk = dk + lax.dot_general(dsb, q, (_DIMS["tn"], ((), ())), preferred_element_type=F32) * ATTN_SCALE
                dv = dv + lax.dot_general(w.astype(BF16), do, (_DIMS["tn"], ((), ())), preferred_element_type=F32)
            dkp_ref[h], dkc_ref[h] = dk[0:WINDOW], dk[WINDOW:2 * WINDOW]
            dvp_ref[h], dvc_ref[h] = dv[0:WINDOW], dv[WINDOW:2 * WINDOW]

    kv_shape = jax.ShapeDtypeStruct((N_KV, t, HEAD_DIM), F32)
    return pl.pallas_call(
        body, name=name, grid=(t // WINDOW,),
        in_specs=[q_spec, kv_prev, kv_cur, kv_prev, kv_cur, q_spec, bias_spec, sink_spec],
        out_specs=[q_spec, kv_cur, kv_cur, kv_cur, kv_cur, bias_spec, pl.BlockSpec((N_Q, WINDOW, 1), lambda n: (0, 0, 0))],
        out_shape=[jax.ShapeDtypeStruct((N_Q, t, HEAD_DIM), F32), kv_shape, kv_shape, kv_shape, kv_shape,
                   jax.ShapeDtypeStruct((N_Q, WINDOW, 2 * WINDOW), F32), jax.ShapeDtypeStruct((N_Q, WINDOW, 1), F32)],
        compiler_params=pltpu.CompilerParams(dimension_semantics=("arbitrary",)),
    )(qh, kh, kh, vh, vh, doh, bias, sinks)


def _heads(a, n_heads):
    t = a.shape[0]
    return a.astype(BF16).reshape(t, n_heads, HEAD_DIM).transpose(1, 0, 2)


def _unheads(a):
    n_heads, t, _ = a.shape
    return a.transpose(1, 0, 2).reshape(t, n_heads * HEAD_DIM)


def _shift_blocks(cur, prev):
    return cur + jnp.concatenate([prev[:, WINDOW:], jnp.zeros_like(prev[:, :WINDOW])], axis=1)


def _t5_bucket_table():
    qi = np.arange(WINDOW)[:, None]
    kj = np.arange(2 * WINDOW)[None, :]
    dist = np.clip(qi + WINDOW - kj, 0, REL_MAX_DIST - 1)
    exact = REL_BUCKETS // 2
    df = np.maximum(dist, 1).astype(np.float32)
    large = exact + (np.log(df / np.float32(exact)) / np.float32(math.log(REL_MAX_DIST / exact)) * (REL_BUCKETS - exact)).astype(np.int32)
    large = np.minimum(large, REL_BUCKETS - 1)
    bucket = np.where(dist < exact, dist, large)
    onehot = np.zeros((WINDOW * 2 * WINDOW, REL_BUCKETS), np.float32)
    onehot[np.arange(WINDOW * 2 * WINDOW), bucket.reshape(-1)] = 1.0
    return onehot


def _band_bias(rel_bias):
    onehot = jnp.asarray(_t5_bucket_table())
    sel = jnp.sum(onehot[:, :, None] * rel_bias[None, :, :], axis=1)
    return sel.T.reshape(N_Q, WINDOW, 2 * WINDOW)


def _block_diag(a):
    g, r, c = a.shape
    a4 = a.reshape(SSM_BLOCKS, g // SSM_BLOCKS, r, c)
    eye = jnp.eye(g // SSM_BLOCKS, dtype=a.dtype)
    full = a4[:, :, :, None, :] * eye[None, :, None, :, None]
    return full.reshape(SSM_BLOCKS, (g // SSM_BLOCKS) * r, (g // SSM_BLOCKS) * c)


def _ssm_disc(lam_re, lam_im, b_re, b_im, c_re, c_im, log_dt):
    dt = jnp.exp(log_dt)[:, None]
    mag = jnp.exp(lam_re * dt)
    ang = lam_im * dt
    a_re = mag * jnp.cos(ang)
    a_im = mag * jnp.sin(ang)
    den = lam_re * lam_re + lam_im * lam_im
    nr = a_re - 1.0
    coef_re = (nr * lam_re + a_im * lam_im) / den
    coef_im = (a_im * lam_re - nr * lam_im) / den
    bb_re = coef_re[..., None] * b_re - coef_im[..., None] * b_im
    bb_im = coef_re[..., None] * b_im + coef_im[..., None] * b_re
    wb_re = _block_diag(jnp.swapaxes(bb_re, 1, 2))
    wb_im = _block_diag(jnp.swapaxes(bb_im, 1, 2))
    cm_re = _block_diag(jnp.swapaxes(c_re, 1, 2))
    cm_imn = _block_diag(-jnp.swapaxes(c_im, 1, 2))
    return a_re.reshape(-1), a_im.reshape(-1), wb_re, wb_im, cm_re, cm_imn


def _scan_tables(a_re, a_im):
    pr, pi = [a_re], [a_im]
    for _ in range(7):
        pr, pi = pr + [pr[-1] * a_re - pi[-1] * a_im], pi + [pr[-1] * a_im + pi[-1] * a_re]
    pr, pi = jnp.stack(pr), jnp.stack(pi)
    pad = jnp.zeros((5,) + a_re.shape, F32)
    dr = jnp.concatenate([jnp.stack([pr[0], pr[1], pr[3]]), pad])
    di = jnp.concatenate([jnp.stack([pi[0], pi[1], pi[3]]), pad])
    fwd = (pr, pi, dr, di)
    rev = (pr[::-1], -pi[::-1], dr, -di)
    return jax.tree.map(lax.stop_gradient, (fwd, rev))


def _gate_col(r):
    return lambda j: (OFF_G + r * D_MODEL) // 256 + j


def _layer_fwd(x, p_i, w, bias, li):
    nm = lambda s: f"{s}_l{li}"
    h = _rms_fwd(x, w["norm_mix"], nm("rms_mix"))
    z = _mm(h, w["w_in"], "nn", name=nm("mm_in"))
    bu_re = _bd_nn(z, w["wb_re"], name=nm("ssm_bu_re"))
    bu_im = _bd_nn(z, w["wb_im"], name=nm("ssm_bu_im"))
    h_re, h_im = _scan(bu_re, bu_im, *w["scan_fwd"], reverse=False, name=nm("ssm_scan"))
    y0a = _bd_nn(h_re, w["cm_re"], name=nm("ssm_c_re"))
    y0b = _bd_nn(h_im, w["cm_imn"], name=nm("ssm_c_im"))
    (y1,) = _rw(lambda a, b, u, d: ([jax.nn.gelu(a + b + d * u)], []),
                [(y0a, None, _c0), (y0b, None, _c0), (z, BRANCH, _c0)], [(BRANCH, None, F32, _c0)],
                params=[w["ssm_d"]], name=nm("ssm_gelu"))
    gl = _mm(y1, w["ssm_w_glu"], "nn", name=nm("mm_glu"))
    (y_ssm,) = _rw(lambda a, b: ([a * jax.nn.sigmoid(b)], []), [(y1, None, _c0), (gl, None, _c0)],
                   [(BRANCH, None, BF16, _c0)], name=nm("ssm_glu"))
    y_conv = _conv_fwd(z, w["conv_w"], nm("conv_fwd"))
    qh, kh, vh = _heads(z[:, OFF_Q:OFF_K], N_Q), _heads(z[:, OFF_K:OFF_V], N_KV), _heads(z[:, OFF_V:OFF_G], N_KV)
    y_attn = _unheads(_attn_fwd(qh, kh, vh, bias, w["sinks"], nm("attn_fwd")))
    ys = (y_ssm, y_conv, y_attn)
    bs = [_mm(ys[r], w["w_branch"][r], "nn", name=nm(f"mm_branch{r}")) for r in range(3)]

    def merge(g0, g1, g2, b0, b1, b2):
        return [jax.nn.sigmoid(g0) * b0 + jax.nn.sigmoid(g1) * b1 + jax.nn.sigmoid(g2) * b2], []

    (merged,) = _rw(merge, [(z, 256, _gate_col(r)) for r in range(3)] + [(b, 256, lambda j: j) for b in bs],
                    [(D_MODEL, 256, BF16, lambda j: j)], ncol=4, name=nm("merge"))
    x1 = _mm(merged, w["w_out"], "nn", add=x, name=nm("mm_out"))
    hf_in = _rms_fwd(x1, w["norm_ffn"], nm("rms_ffn"))
    hf = _mm(hf_in, w["w_ffn_in"], "nn", name=nm("mm_ffn_in"))
    ncf = FFN_HIDDEN // 256
    (act,) = _rw(lambda a, b: ([jax.nn.silu(a) * b], []), [(hf, 256, lambda j: j), (hf, 256, lambda j: ncf + j)],
                 [(FFN_HIDDEN, 256, BF16, lambda j: j)], ncol=ncf, name=nm("swiglu"))
    x2 = _mm(act, w["w_ffn_out"], "nn", add=x1, tk=1408, name=nm("mm_ffn_out"))
    hp = _rms_fwd(x2, w["norm_ple"], nm("rms_ple"))
    pgl = _mm(hp, w["w_ple_gate"], "nn", name=nm("mm_ple_gate"))
    pp = _mm(p_i, w["w_ple_proj"], "nn", name=nm("mm_ple_proj"))
    (x3,) = _rw(lambda xv, a, b: ([xv + jax.nn.sigmoid(a) * b], []), [(x2, None, _c0), (pgl, None, _c0), (pp, None, _c0)],
                [(D_MODEL, None, F32, _c0)], name=nm("ple_add"))
    saved = dict(x=x, p=p_i, h=h, z=z, h_re=h_re, h_im=h_im, y0a=y0a, y0b=y0b, y1=y1, gl=gl, ys=ys, qh=qh, kh=kh, vh=vh,
                 bs=bs, merged=merged, x1=x1, hf_in=hf_in, hf=hf, act=act, x2=x2, hp=hp, pgl=pgl, pp=pp)
    return x3, saved


def _layer_bwd(dx3, s, w, bias, li):
    nm = lambda n: f"{n}_l{li}"
    g = {}
    z = s["z"]
    def ple_b(d, a, b):
        _, vjp = jax.vjp(lambda a_, b_: jax.nn.sigmoid(a_) * b_, a, b)
        return list(vjp(d)), []

    dpgl, dpp = _rw(ple_b, [(dx3, None, _c0), (s["pgl"], None, _c0), (s["pp"], None, _c0)],
                    [(D_MODEL, None, BF16, _c0)] * 2, name=nm("ple_bwd"))
    g["w_ple_proj"] = _mm(s["p"], dpp, "tn", name=nm("mmg_ple_proj"))
    g["w_ple_gate"] = _mm(s["hp"], dpgl, "tn", name=nm("mmg_ple_gate"))
    dhp = _mm(dpgl, w["w_ple_gate"], "nt", name=nm("mmb_ple_gate"))
    dx2, g["norm_ple"] = _rms_bwd(s["x2"], dhp, dx3, w["norm_ple"], nm("rmsb_ple"))
    dact = _mm(dx2, w["w_ffn_out"], "nt", name=nm("mmb_ffn_out"))
    g["w_ffn_out"] = _mm(s["act"], dx2, "tn", name=nm("mmg_ffn_out"))
    ncf = FFN_HIDDEN // 256

    def swiglu_b(j, a, b, d):
        _, vjp = jax.vjp(lambda a_, b_: jax.nn.silu(a_) * b_, a, b)
        da, db = vjp(d)
        return [jnp.where(j >= ncf, db, da)], []

    (dhf,) = _rw(swiglu_b, [(s["hf"], 256, lambda j: j % ncf), (s["hf"], 256, lambda j: ncf + j % ncf), (dact, 256, lambda j: j % ncf)],
                 [(2 * FFN_HIDDEN, 256, BF16, lambda j: j)], ncol=2 * ncf, with_j=True, name=nm("swiglu_bwd"))
    g["w_ffn_in"] = _mm(s["hf_in"], dhf, "tn", name=nm("mmg_ffn_in"))
    dhf_in = _mm(dhf, w["w_ffn_in"], "nt", tk=1408, name=nm("mmb_ffn_in"))
    dx1, g["norm_ffn"] = _rms_bwd(s["x1"], dhf_in, dx2, w["norm_ffn"], nm("rmsb_ffn"))
    dmerged = _mm(dx1, w["w_out"], "nt", name=nm("mmb_out"))
    g["w_out"] = _mm(s["merged"], dx1, "tn", name=nm("mmg_out"))

    def merge_b(d, g0, g1, g2, b0, b1, b2):
        outs_g, outs_b = [], []
        for gate, br in ((g0, b0), (g1, b1), (g2, b2)):
            sg = jax.nn.sigmoid(gate)
            outs_g.append(d * br * sg * (1.0 - sg))
            outs_b.append(d * sg)
        return outs_g + outs_b, []

    res = _rw(merge_b, [(dmerged, 256, lambda j: j)] + [(z, 256, _gate_col(r)) for r in range(3)] + [(b, 256, lambda j: j) for b in s["bs"]],
              [(D_MODEL, 256, BF16, lambda j: j)] * 6, ncol=4, name=nm("merge_bwd"))
    dgates, dbs = res[:3], res[3:]
    dys = [_mm(dbs[r], w["w_branch"][r], "nt", name=nm(f"mmb_branch{r}")) for r in range(3)]
    g["w_branch"] = jnp.stack([_mm(s["ys"][r], dbs[r], "tn", name=nm(f"mmg_branch{r}")) for r in range(3)])
    doh = _heads(dys[2], N_Q)
    dqh, dkc, dkp, dvc, dvp, dbias, dsink = _attn_bwd(s["qh"], s["kh"], s["vh"], doh, bias, w["sinks"], nm("attn_bwd"))
    dq, dk, dv = _unheads(dqh), _unheads(_shift_blocks(dkc, dkp)), _unheads(_shift_blocks(dvc, dvp))
    g["sinks"] = jnp.sum(dsink, axis=(1, 2))
    dcb, dcc, dcx, dconv = _conv_bwd(dys[1], z, w["conv_w"], nm("conv_bwd"))
    g["conv_w"] = dconv[0:3]
    def glu_b(d, y1, gl):
        sg = jax.nn.sigmoid(gl)
        return [d * y1 * sg * (1.0 - sg), d * sg], []

    dgl, dy1a = _rw(glu_b, [(dys[0], None, _c0), (s["y1"], None, _c0), (s["gl"], None, _c0)],
                    [(BRANCH, None, BF16, _c0), (BRANCH, None, F32, _c0)], name=nm("ssm_glu_bwd"))
    g["ssm_w_glu"] = _mm(s["y1"], dgl, "tn", name=nm("mmg_glu"))
    dy1b = _mm(dgl, w["ssm_w_glu"], "nt", name=nm("mmb_glu"))

    def gelu_b(da, db, a, b, u, d):
        _, vjp = jax.vjp(lambda pre: jax.nn.gelu(pre), a + b + d * u)
        (dy0,) = vjp(da + db)
        return [dy0, dy0 * d], [jnp.sum(dy0 * u, axis=0, keepdims=True)]

    dy0, du_a, g["ssm_d"] = _rw(gelu_b, [(dy1a, None, _c0), (dy1b, None, _c0), (s["y0a"], None, _c0), (s["y0b"], None, _c0), (z, BRANCH, _c0)],
                                [(BRANCH, None, BF16, _c0), (BRANCH, None, F32, _c0)], params=[w["ssm_d"]],
                                reds=[((1, BRANCH), None, _c0)], name=nm("ssm_gelu_bwd"))
    dh_re = _bd_nt(dy0, w["cm_re"], name=nm("ssmb_c_re"))
    dh_im = _bd_nt(dy0, w["cm_imn"], name=nm("ssmb_c_im"))
    sb = SSM_STATES // SSM_BLOCKS
    g["cm_re"] = _bd_tn(s["h_re"], dy0, sb, BRANCH // SSM_BLOCKS, name=nm("ssmg_c_re"))
    g["cm_imn"] = _bd_tn(s["h_im"], dy0, sb, BRANCH // SSM_BLOCKS, name=nm("ssmg_c_im"))
    l_re, l_im, da_re, da_im = _scan(dh_re, dh_im, *w["scan_rev"], reverse=True, hr=s["h_re"], hi=s["h_im"], name=nm("ssm_scan_bwd"))
    g["a_re"], g["a_im"] = jnp.sum(da_re, axis=0), jnp.sum(da_im, axis=0)
    du_b = _bd_nt(l_re, w["wb_re"], name=nm("ssmb_bu_re")) + _bd_nt(l_im, w["wb_im"], name=nm("ssmb_bu_im"))
    g["wb_re"] = _bd_tn(z, l_re, BRANCH // SSM_BLOCKS, sb, name=nm("ssmg_bu_re"))
    g["wb_im"] = _bd_tn(z, l_im, BRANCH // SSM_BLOCKS, sb, name=nm("ssmg_bu_im"))
    dz = jnp.concatenate([(du_a + du_b).astype(BF16), dcb, dcc, dcx, dq.astype(BF16), dk.astype(BF16), dv.astype(BF16)] + list(dgates), axis=1)
    g["w_in"] = _mm(s["h"], dz, "tn", name=nm("mmg_in"))
    dh = _mm(dz, w["w_in"], "nt", tk=2944, name=nm("mmb_in"))
    dx, g["norm_mix"] = _rms_bwd(s["x"], dh, dx1, w["norm_mix"], nm("rmsb_mix"))
    return dx, g, dbias


def _loss_and_seed(x, target, g_final):
    def fn(xv, tv, gv):
        y, vjp = jax.vjp(_rms, xv, gv)
        err = y - tv
        dx, dg = vjp(err * (1.0 / D_MODEL))
        return [dx], [jnp.sum(err * err, axis=0, keepdims=True) * (0.5 / D_MODEL), dg]

    return _rw(fn, [(x, None, _c0), (target, None, _c0)], [(D_MODEL, None, F32, _c0)], params=[g_final],
               reds=[((1, D_MODEL), None, _c0)] * 2, name="loss_head")


def _local_step(x, p, target, wt):
    bias, bias_vjp = jax.vjp(_band_bias, wt["rel_bias"])
    layers, disc_vjps = [], []
    for i in range(DEPTH):
        ssm_p = [wt[k][i] for k in ("ssm_lambda_re", "ssm_lambda_im", "ssm_b_re", "ssm_b_im", "ssm_c_re", "ssm_c_im", "ssm_log_dt")]
        (a_re, a_im, wb_re, wb_im, cm_re, cm_imn), disc_vjp = jax.vjp(_ssm_disc, *ssm_p)
        scan_fwd, scan_rev = _scan_tables(a_re, a_im)
        layers.append(dict(
            norm_mix=wt["norm_mix"][i][None], w_in=wt["w_in"][i], wb_re=wb_re.astype(BF16), wb_im=wb_im.astype(BF16),
            cm_re=cm_re.astype(BF16), cm_imn=cm_imn.astype(BF16), scan_fwd=scan_fwd, scan_rev=scan_rev,
            ssm_d=wt["ssm_d"][i][None], ssm_w_glu=wt["ssm_w_glu"][i], conv_w=wt["conv_w"][i],
            sinks=wt["attn_sinks"][i][:, None], w_branch=wt["w_branch"][i], w_out=wt["w_out"][i],
            norm_ffn=wt["norm_ffn"][i][None], w_ffn_in=wt["w_ffn_in"][i], w_ffn_out=wt["w_ffn_out"][i],
            norm_ple=wt["norm_ple"][i][None], w_ple_gate=wt["w_ple_gate"][i], w_ple_proj=wt["w_ple_proj"][i]))
        disc_vjps.append(disc_vjp)

    saved = []
    for i in range(DEPTH):
        x, s = _layer_fwd(x, p[i], layers[i], bias, i)
        saved.append(s)
    dx, loss_cols, g_final = _loss_and_seed(x, target, wt["norm_final"][None])
    loss = jnp.sum(loss_cols)

    per_layer = [None] * DEPTH
    dbias = None
    for i in reversed(range(DEPTH)):
        dx, g, db = _layer_bwd(dx, saved[i], layers[i], bias, i)
        dbias = db if dbias is None else dbias + db
        (g["ssm_lambda_re"], g["ssm_lambda_im"], g["ssm_b_re"], g["ssm_b_im"], g["ssm_c_re"], g["ssm_c_im"], g["ssm_log_dt"]) = \
            disc_vjps[i]((g.pop("a_re"), g.pop("a_im"), g.pop("wb_re"), g.pop("wb_im"), g.pop("cm_re"), g.pop("cm_imn")))
        g["attn_sinks"] = g.pop("sinks")
        for k in ("norm_mix", "norm_ffn", "norm_ple", "ssm_d"):
            g[k] = g[k][0]
        per_layer[i] = g
    grads = {k: jnp.stack([per_layer[i][k] for i in range(DEPTH)]) for k in per_layer[0]}
    (grads["rel_bias"],) = bias_vjp(dbias)
    grads["norm_final"] = g_final[0]
    return loss, dx, grads


HBM_SPEC = pl.BlockSpec(memory_space=pltpu.HBM)


def _position():
    x, y, c = lax.axis_index("x"), lax.axis_index("y"), lax.axis_index("c")
    other_chips = [(1 - x, y), (x, 1 - y), (1 - x, 1 - y)]
    return x, y, c, other_chips


def _allgather_weights(w_local):
    _, r, lanes = w_local.shape

    def body(w_ref, out_ref, send_sems, recv_sems, local_sem):
        x, y, c, chips = _position()
        me = 2 * x + y
        sibling = (x, y, 1 - c)

        def copy(k, src, dst, to):
            return pltpu.make_async_remote_copy(src_ref=src, dst_ref=dst, send_sem=send_sems.at[k], recv_sem=recv_sems.at[k],
                                                device_id=to, device_id_type=MESH)

        mine = pltpu.make_async_copy(w_ref, out_ref.at[me], local_sem)
        mine.start()
        first = [copy(j, w_ref.at[c], out_ref.at[me, c], (*chip, c)) for j, chip in enumerate(chips)]
        for cp in first:
            cp.start()
        passed = []
        for j, (px, py) in enumerate(chips):
            landed = out_ref.at[2 * px + py, c]
            copy(j, landed, landed, (px, py, c)).wait_recv()
            fwd = copy(3 + j, landed, landed, sibling)
            fwd.start()
            passed.append(fwd)
        for j, (px, py) in enumerate(chips):
            landed = out_ref.at[2 * px + py, 1 - c]
            copy(3 + j, landed, landed, sibling).wait_recv()
        for cp in first + passed:
            cp.wait_send()
        mine.wait()

    return pl.pallas_call(
        body, name="allgather_weights", in_specs=[HBM_SPEC], out_specs=HBM_SPEC,
        out_shape=jax.ShapeDtypeStruct((N_SHARD, 2, r, lanes), w_local.dtype),
        scratch_shapes=[pltpu.SemaphoreType.DMA((6,)), pltpu.SemaphoreType.DMA((6,)), pltpu.SemaphoreType.DMA],
    )(w_local)


def _sibling_exchange(g2):
    _, ns, r, lanes = g2.shape

    def body(g_ref, own_ref, got_ref, send_sem, recv_sem, local_sem):
        x, y, c, _ = _position()
        keep = pltpu.make_async_copy(g_ref.at[c], own_ref, local_sem)
        keep.start()
        swap = pltpu.make_async_remote_copy(src_ref=g_ref.at[1 - c], dst_ref=got_ref, send_sem=send_sem, recv_sem=recv_sem,
                                            device_id=(x, y, 1 - c), device_id_type=MESH)
        swap.start()
        swap.wait()
        keep.wait()

    shape = jax.ShapeDtypeStruct((ns, r, lanes), g2.dtype)
    return pl.pallas_call(
        body, name="grad_sibling_exchange", in_specs=[HBM_SPEC], out_specs=[HBM_SPEC, HBM_SPEC], out_shape=[shape, shape],
        scratch_shapes=[pltpu.SemaphoreType.DMA, pltpu.SemaphoreType.DMA, pltpu.SemaphoreType.DMA],
    )(g2)


def _chip_exchange(part_f32, part_bf16):
    ns, r, lanes = part_f32.shape

    def body(f_ref, b_ref, own_ref, got_ref, send_sems, recv_sems, local_sem):
        x, y, c, chips = _position()
        keep = pltpu.make_async_copy(f_ref.at[2 * x + y], own_ref, local_sem)
        keep.start()
        sends = [pltpu.make_async_remote_copy(src_ref=b_ref.at[2 * px + py], dst_ref=got_ref.at[j], send_sem=send_sems.at[j],
                                              recv_sem=recv_sems.at[j], device_id=(px, py, c), device_id_type=MESH)
                 for j, (px, py) in enumerate(chips)]
        for cp in sends:
            cp.start()
        for cp in sends:
            cp.wait()
        keep.wait()

    return pl.pallas_call(
        body, name="grad_chip_exchange", in_specs=[HBM_SPEC, HBM_SPEC], out_specs=[HBM_SPEC, HBM_SPEC],
        out_shape=[jax.ShapeDtypeStruct((r, lanes), F32), jax.ShapeDtypeStruct((ns - 1, r, lanes), BF16)],
        scratch_shapes=[pltpu.SemaphoreType.DMA((3,)), pltpu.SemaphoreType.DMA((3,)), pltpu.SemaphoreType.DMA],
    )(part_f32, part_bf16)


def _sibling_gather(half):
    r, lanes = half.shape

    def body(h_ref, out_ref, send_sem, recv_sem, local_sem):
        x, y, c, _ = _position()
        keep = pltpu.make_async_copy(h_ref, out_ref.at[c], local_sem)
        keep.start()
        push = pltpu.make_async_remote_copy(src_ref=h_ref, dst_ref=out_ref.at[c], send_sem=send_sem, recv_sem=recv_sem,
                                            device_id=(x, y, 1 - c), device_id_type=MESH)
        push.start()
        pltpu.make_async_remote_copy(src_ref=h_ref, dst_ref=out_ref.at[1 - c], send_sem=send_sem, recv_sem=recv_sem,
                                     device_id=(x, y, 1 - c), device_id_type=MESH).wait_recv()
        push.wait_send()
        keep.wait()

    return pl.pallas_call(
        body, name="grad_sibling_gather", in_specs=[HBM_SPEC], out_specs=HBM_SPEC,
        out_shape=jax.ShapeDtypeStruct((2, r, lanes), half.dtype),
        scratch_shapes=[pltpu.SemaphoreType.DMA, pltpu.SemaphoreType.DMA, pltpu.SemaphoreType.DMA],
    )(half)


def _gather_partials(part):
    r, lanes = part.shape

    def body(p_ref, out_ref, send_sems, recv_sems, local_sem):
        x, y, c, _ = _position()
        keep = pltpu.make_async_copy(p_ref, out_ref.at[4 * x + 2 * y + c], local_sem)
        keep.start()
        flips = [(fx, fy, fc) for fx in (0, 1) for fy in (0, 1) for fc in (0, 1)][1:]
        sends = []
        for k, (fx, fy, fc) in enumerate(flips):
            cp = pltpu.make_async_remote_copy(src_ref=p_ref, dst_ref=out_ref.at[4 * x + 2 * y + c], send_sem=send_sems.at[k],
                                              recv_sem=recv_sems.at[k], device_id=(x ^ fx, y ^ fy, c ^ fc), device_id_type=MESH)
            cp.start()
            sends.append(cp)
        for k, (fx, fy, fc) in enumerate(flips):
            src = out_ref.at[4 * (x ^ fx) + 2 * (y ^ fy) + (c ^ fc)]
            pltpu.make_async_remote_copy(src_ref=src, dst_ref=src, send_sem=send_sems.at[k], recv_sem=recv_sems.at[k],
                                         device_id=(x ^ fx, y ^ fy, c ^ fc), device_id_type=MESH).wait_recv()
        for cp in sends:
            cp.wait_send()
        keep.wait()

    return pl.pallas_call(
        body, name="small_gather_partials", in_specs=[HBM_SPEC], out_specs=HBM_SPEC,
        out_shape=jax.ShapeDtypeStruct((8, r, lanes), part.dtype),
        scratch_shapes=[pltpu.SemaphoreType.DMA((7,)), pltpu.SemaphoreType.DMA((7,)), pltpu.SemaphoreType.DMA],
    )(part)


def _sum_leading(stack, name, also_bf16=False):
    k, r, lanes = stack.shape
    tm = _pick(r, 256, 16)
    outs = [jax.ShapeDtypeStruct((r, lanes), F32)] + ([jax.ShapeDtypeStruct((r, lanes), BF16)] if also_bf16 else [])

    def body(s_ref, *o_refs):
        acc = s_ref[0].astype(F32)
        for i in range(1, k):
            acc = acc + s_ref[i].astype(F32)
        for o in o_refs:
            o[...] = acc.astype(o.dtype)

    spec = pl.BlockSpec((tm, lanes), lambda i: (i, 0))
    return pl.pallas_call(
        body, name=name, grid=(r // tm,), in_specs=[pl.BlockSpec((k, tm, lanes), lambda i: (0, i, 0))],
        out_specs=[spec] * len(outs), out_shape=outs,
        compiler_params=pltpu.CompilerParams(dimension_semantics=("parallel",)),
    )(stack)


def _add_pair(a, b, name):
    ns, r, lanes = a.shape
    tm = _pick(r, 256, 16)
    spec = pl.BlockSpec((1, tm, lanes), lambda s, i: (s, i, 0))

    def body(a_ref, b_ref, f_ref, h_ref):
        acc = a_ref[...] + b_ref[...]
        f_ref[...] = acc
        h_ref[...] = acc.astype(BF16)

    return pl.pallas_call(
        body, name=name, grid=(ns, r // tm), in_specs=[spec, spec], out_specs=[spec, spec],
        out_shape=[jax.ShapeDtypeStruct(a.shape, F32), jax.ShapeDtypeStruct(a.shape, BF16)],
        compiler_params=pltpu.CompilerParams(dimension_semantics=("parallel", "parallel")),
    )(a, b)


def _add_own(own, got, name):
    r, lanes = own.shape
    tm = _pick(r, 256, 16)

    def body(o_ref, g_ref, out_ref):
        acc = o_ref[...]
        for j in range(g_ref.shape[0]):
            acc = acc + g_ref[j].astype(F32)
        out_ref[...] = acc

    spec = pl.BlockSpec((tm, lanes), lambda i: (i, 0))
    return pl.pallas_call(
        body, name=name, grid=(r // tm,), in_specs=[spec, pl.BlockSpec((got.shape[0], tm, lanes), lambda i: (0, i, 0))],
        out_specs=spec, out_shape=jax.ShapeDtypeStruct((r, lanes), F32),
        compiler_params=pltpu.CompilerParams(dimension_semantics=("parallel",)),
    )(own, got)


def _local_shape(shape, axis):
    return tuple(d // N_SHARD if a == axis else d for a, d in enumerate(shape))


def _big_sizes():
    return [DEPTH * int(np.prod(_local_shape(shape, axis))) for _, shape, axis in BIG]


FLAT_ROW_TILE = 256
ELEMENTWISE_BIG = ("conv_w",)


def _flat_rows(n):
    rows = -(-n // (2 * LANES))
    return -(-rows // FLAT_ROW_TILE) * FLAT_ROW_TILE


def _pack_local_weights(wl):
    parts = [(lax.bitcast_convert_type(wl[name], BF16) if name in ELEMENTWISE_BIG else wl[name].astype(BF16)).reshape(-1)
             for name, _, _ in BIG]
    flat = jnp.concatenate(parts)
    r = _flat_rows(flat.shape[0])
    flat = jnp.pad(flat, (0, 2 * r * LANES - flat.shape[0]))
    return flat.reshape(2, r, LANES)


def _unpack_local(flat):
    flat = flat.reshape(-1)
    out, off = {}, 0
    for (name, shape, axis), size in zip(BIG, _big_sizes(), strict=True):
        out[name] = flat[off:off + size].reshape((DEPTH,) + _local_shape(shape, axis))
        off += size
    return out


def _unpack_gathered(gathered):
    flat = gathered.reshape(N_SHARD, -1)
    out, off = {}, 0
    for (name, shape, axis), size in zip(BIG, _big_sizes(), strict=True):
        local = (N_SHARD, DEPTH) + _local_shape(shape, axis)
        if name in ELEMENTWISE_BIG:
            seg = lax.bitcast_convert_type(flat[:, off:off + 2 * size].reshape(local + (2,)), F32)
            off += 2 * size
        else:
            seg = flat[:, off:off + size].reshape(local)
            off += size
        out[name] = jnp.moveaxis(seg, 0, 1 + axis).reshape((DEPTH,) + shape)
    return out


def _pack_full_grads(grads):
    rows = []
    for name, shape, axis in BIG:
        gfull = grads[name]
        split = gfull.reshape((DEPTH,) + shape[:axis] + (N_SHARD, shape[axis] // N_SHARD) + shape[axis + 1:])
        rows.append(jnp.moveaxis(split, 1 + axis, 0).reshape(N_SHARD, -1))
    flat = jnp.concatenate(rows, axis=1)
    r = _flat_rows(flat.shape[1])
    flat = jnp.pad(flat, ((0, 0), (0, 2 * r * LANES - flat.shape[1])))
    return flat.reshape(N_SHARD, 2, r, LANES).transpose(1, 0, 2, 3)


def _pack_small(grads):
    flat = jnp.concatenate([grads[name].reshape(-1) for name in SMALL])
    r = -(-flat.shape[0] // (8 * LANES)) * 8
    return jnp.pad(flat, (0, r * LANES - flat.shape[0])).reshape(r, LANES)


def _unpack_small(flat, like):
    flat = flat.reshape(-1)
    out, off = {}, 0
    for name in SMALL:
        size = int(np.prod(like[name].shape))
        out[name] = flat[off:off + size].reshape(like[name].shape)
        off += size
    return out


def _adamw(w, g, m, v, name):
    shape = w.shape
    cols = shape[-1]
    rows = int(np.prod(shape[:-1])) if len(shape) > 1 else 1
    w2, g2, m2, v2 = (a.reshape(rows, cols) for a in (w, g, m, v))

    def fn(wv, gv, mv, vv):
        mn = ADAM_B1 * mv + (1.0 - ADAM_B1) * gv
        vn = ADAM_B2 * vv + (1.0 - ADAM_B2) * jnp.square(gv)
        m_hat = mn / (1.0 - ADAM_B1 ** ADAM_STEP)
        v_hat = vn / (1.0 - ADAM_B2 ** ADAM_STEP)
        delta = -ADAM_LR * (m_hat / (jnp.sqrt(v_hat) + ADAM_EPS) + ADAM_WD * wv)
        return [delta, mn, vn], []

    tm = 256 if rows % 8 == 0 and rows > 256 else rows
    res = _rw(fn, [(a, None, _c0) for a in (w2, g2, m2, v2)], [(cols, None, F32, _c0)] * 3, tm=tm, name=name)
    return [r.reshape(shape) for r in res]


def _step(x, p, target, weights, moments_m, moments_v):
    gathered = _allgather_weights(_pack_local_weights(weights))
    wt = dict(_unpack_gathered(gathered))
    for name in SMALL:
        wt[name] = weights[name]
    loss, dx, grads = _local_step(x[0], p[:, 0], target[0], wt)
    loss = lax.psum(loss, ("x", "y", "c"))
    own, got = _sibling_exchange(_pack_full_grads(grads))
    chip_f32, chip_bf16 = _add_pair(own, got, "grad_add_sibling")
    own_rows, other = _chip_exchange(chip_f32, chip_bf16)
    reduced = _unpack_local(_sibling_gather(_add_own(own_rows, other, "grad_add_chips")))
    small = _unpack_small(_sum_leading(_gather_partials(_pack_small(grads)), "small_sum")[0], {k: weights[k] for k in SMALL})
    reduced.update(small)
    outs_g, outs_d, outs_m, outs_v = [], [], [], []
    for name in WEIGHTS:
        d, mn, vn = _adamw(weights[name], reduced[name], moments_m[name], moments_v[name], f"adamw_{name}")
        outs_g.append(reduced[name])
        outs_d.append(d)
        outs_m.append(mn)
        outs_v.append(vn)
    return (loss, dx[None], *outs_g, *outs_d, *outs_m, *outs_v)


def kernel(x, p, rel_bias, norm_mix, w_in, ssm_lambda_re, ssm_lambda_im, ssm_b_re, ssm_b_im, ssm_c_re, ssm_c_im, ssm_d, ssm_log_dt, ssm_w_glu, conv_w, attn_sinks, w_branch, w_out, norm_ffn, w_ffn_in, w_ffn_out, norm_ple, w_ple_gate, w_ple_proj, norm_final, loss_target, m_rel_bias, m_norm_mix, m_w_in, m_ssm_lambda_re, m_ssm_lambda_im, m_ssm_b_re, m_ssm_b_im, m_ssm_c_re, m_ssm_c_im, m_ssm_d, m_ssm_log_dt, m_ssm_w_glu, m_conv_w, m_attn_sinks, m_w_branch, m_w_out, m_norm_ffn, m_w_ffn_in, m_w_ffn_out, m_norm_ple, m_w_ple_gate, m_w_ple_proj, m_norm_final, v_rel_bias, v_norm_mix, v_w_in, v_ssm_lambda_re, v_ssm_lambda_im, v_ssm_b_re, v_ssm_b_im, v_ssm_c_re, v_ssm_c_im, v_ssm_d, v_ssm_log_dt, v_ssm_w_glu, v_conv_w, v_attn_sinks, v_w_branch, v_w_out, v_norm_ffn, v_w_ffn_in, v_w_ffn_out, v_norm_ple, v_w_ple_gate, v_w_ple_proj, v_norm_final):
    weights = dict(rel_bias=rel_bias, norm_mix=norm_mix, w_in=w_in, ssm_lambda_re=ssm_lambda_re, ssm_lambda_im=ssm_lambda_im,
                   ssm_b_re=ssm_b_re, ssm_b_im=ssm_b_im, ssm_c_re=ssm_c_re, ssm_c_im=ssm_c_im, ssm_d=ssm_d, ssm_log_dt=ssm_log_dt,
                   ssm_w_glu=ssm_w_glu, conv_w=conv_w, attn_sinks=attn_sinks, w_branch=w_branch, w_out=w_out, norm_ffn=norm_ffn,
                   w_ffn_in=w_ffn_in, w_ffn_out=w_ffn_out, norm_ple=norm_ple, w_ple_gate=w_ple_gate, w_ple_proj=w_ple_proj,
                   norm_final=norm_final)
    moments_m = dict(rel_bias=m_rel_bias, norm_mix=m_norm_mix, w_in=m_w_in, ssm_lambda_re=m_ssm_lambda_re, ssm_lambda_im=m_ssm_lambda_im,
                     ssm_b_re=m_ssm_b_re, ssm_b_im=m_ssm_b_im, ssm_c_re=m_ssm_c_re, ssm_c_im=m_ssm_c_im, ssm_d=m_ssm_d,
                     ssm_log_dt=m_ssm_log_dt, ssm_w_glu=m_ssm_w_glu, conv_w=m_conv_w, attn_sinks=m_attn_sinks, w_branch=m_w_branch,
                     w_out=m_w_out, norm_ffn=m_norm_ffn, w_ffn_in=m_w_ffn_in, w_ffn_out=m_w_ffn_out, norm_ple=m_norm_ple,
                     w_ple_gate=m_w_ple_gate, w_ple_proj=m_w_ple_proj, norm_final=m_norm_final)
    moments_v = dict(rel_bias=v_rel_bias, norm_mix=v_norm_mix, w_in=v_w_in, ssm_lambda_re=v_ssm_lambda_re, ssm_lambda_im=v_ssm_lambda_im,
                     ssm_b_re=v_ssm_b_re, ssm_b_im=v_ssm_b_im, ssm_c_re=v_ssm_c_re, ssm_c_im=v_ssm_c_im, ssm_d=v_ssm_d,
                     ssm_log_dt=v_ssm_log_dt, ssm_w_glu=v_ssm_w_glu, conv_w=v_conv_w, attn_sinks=v_attn_sinks, w_branch=v_w_branch,
                     w_out=v_w_out, norm_ffn=v_norm_ffn, w_ffn_in=v_w_ffn_in, w_ffn_out=v_w_ffn_out, norm_ple=v_norm_ple,
                     w_ple_gate=v_w_ple_gate, w_ple_proj=v_w_ple_proj, norm_final=v_norm_final)
    return _step(x, p, loss_target, weights, moments_m, moments_v)
```

```python
import functools
import math

import numpy as np

import jax
import jax.numpy as jnp
from jax import lax
from jax.experimental import pallas as pl
from jax.experimental.pallas import tpu as pltpu

F32, BF16 = jnp.float32, jnp.bfloat16
MESH = pl.DeviceIdType.MESH

D_MODEL = 1024
DEPTH = 4
PLE_DIM = 256
BRANCH = 512
N_GROUPS = 32
GROUP_CH = 16
N_STATE = 64
SSM_STATES = N_GROUPS * N_STATE
SSM_BLOCKS = 4
HEAD_DIM = 64
N_Q = 8
N_KV = 2
GQA = N_Q // N_KV
WINDOW = 128
ATTN_SCALE = 1.0 / math.sqrt(HEAD_DIM)
REL_BUCKETS = 32
REL_MAX_DIST = 128
FFN_HIDDEN = 2816
FFN_COLS = 1408
FFN_NCOL = FFN_HIDDEN // FFN_COLS
IN_WIDTH = 5888
RMS_EPS = 1e-6
NEG = -1e30

ADAM_LR, ADAM_B1, ADAM_B2, ADAM_EPS, ADAM_WD, ADAM_STEP = 0.001, 0.9, 0.999, 1e-08, 0.01, 10

N_SHARD = 4
LANES = 1024
COPY_CHUNKS = 4

OFF_U, OFF_CB, OFF_CC, OFF_CX, OFF_Q, OFF_K, OFF_V, OFF_G = 0, 512, 1024, 1536, 2048, 2560, 2688, 2816

BIG = (
    ("w_in", (D_MODEL, IN_WIDTH), 1),
    ("ssm_w_glu", (BRANCH, BRANCH), 0),
    ("conv_w", (3, BRANCH), 1),
    ("w_branch", (3, BRANCH, D_MODEL), 2),
    ("w_out", (D_MODEL, D_MODEL), 0),
    ("w_ffn_in", (D_MODEL, 2 * FFN_HIDDEN), 1),
    ("w_ffn_out", (FFN_HIDDEN, D_MODEL), 0),
    ("w_ple_gate", (D_MODEL, D_MODEL), 0),
    ("w_ple_proj", (PLE_DIM, D_MODEL), 1),
)
SMALL = ("rel_bias", "norm_mix", "ssm_lambda_re", "ssm_lambda_im", "ssm_b_re", "ssm_b_im", "ssm_c_re", "ssm_c_im",
         "ssm_d", "ssm_log_dt", "attn_sinks", "norm_ffn", "norm_ple", "norm_final")
WEIGHTS = ("rel_bias", "norm_mix", "w_in", "ssm_lambda_re", "ssm_lambda_im", "ssm_b_re", "ssm_b_im", "ssm_c_re",
           "ssm_c_im", "ssm_d", "ssm_log_dt", "ssm_w_glu", "conv_w", "attn_sinks", "w_branch", "w_out", "norm_ffn",
           "w_ffn_in", "w_ffn_out", "norm_ple", "w_ple_gate", "w_ple_proj", "norm_final")


def _c0(j):
    return 0


def _pick(n, cap, unit=128):
    if n <= cap:
        return n
    best = None
    for t in range(unit, cap + 1, unit):
        if n % t == 0:
            best = t
    assert best is not None, (n, cap, unit)
    return best


_DIMS = {"nn": ((1,), (0,)), "nt": ((1,), (1,)), "tn": ((0,), (0,))}


def _mm(a, b, mode, *, name, out_dtype=F32, add=None, tm=1024, tn=1024, tk=1024, b_k0=0):
    if mode == "nn":
        (m, k), (k2, n) = a.shape, b.shape
    elif mode == "nt":
        (m, k), (n, k2) = a.shape, b.shape
    else:
        (k, m), (k2, n) = a.shape, b.shape
    assert k == k2 or (mode == "nt" and b_k0 + k <= k2), (a.shape, b.shape, mode)
    tm, tn, tk = _pick(m, tm, 128 if mode == "tn" else 8), _pick(n, tn), _pick(k, tk, 128 if mode != "tn" else 8)
    nk = k // tk
    assert b_k0 % tk == 0
    kb0 = b_k0 // tk
    a_spec = pl.BlockSpec((tk, tm), lambda i, j, kk: (kk, i)) if mode == "tn" else pl.BlockSpec((tm, tk), lambda i, j, kk: (i, kk))
    b_spec = pl.BlockSpec((tn, tk), lambda i, j, kk: (j, kb0 + kk)) if mode == "nt" else pl.BlockSpec((tk, tn), lambda i, j, kk: (kk, j))
    o_spec = pl.BlockSpec((tm, tn), lambda i, j, kk: (i, j))
    dims = (_DIMS[mode], ((), ()))
    has_add = add is not None

    def body(*refs):
        a_ref, b_ref = refs[0], refs[1]
        add_ref = refs[2] if has_add else None
        o_ref, acc_ref = refs[-2], refs[-1]
        part = lax.dot_general(a_ref[...].astype(BF16), b_ref[...].astype(BF16), dims, preferred_element_type=F32)

        def finish(acc):
            if has_add:
                acc = acc + add_ref[...]
            o_ref[...] = acc.astype(o_ref.dtype)

        if nk == 1:
            finish(part)
        else:
            kk = pl.program_id(2)

            @pl.when(kk == 0)
            def _():
                acc_ref[...] = part

            @pl.when(kk > 0)
            def _():
                acc_ref[...] += part

            @pl.when(kk == nk - 1)
            def _():
                finish(acc_ref[...])

    operands = [a, b] + ([add] if has_add else [])
    in_specs = [a_spec, b_spec] + ([o_spec] if has_add else [])
    return pl.pallas_call(
        body, name=name, grid=(m // tm, n // tn, nk), in_specs=in_specs, out_specs=o_spec,
        out_shape=jax.ShapeDtypeStruct((m, n), out_dtype),
        scratch_shapes=[pltpu.VMEM((tm, tn) if nk > 1 else (8, 128), F32)],
        compiler_params=pltpu.CompilerParams(dimension_semantics=("parallel", "parallel", "arbitrary")),
    )(*operands)


def _rw(fn, ins, outs, *, name, params=(), reds=(), tm=256, ncol=1, with_j=False):
    t = ins[0][0].shape[0]
    tm = _pick(t, tm, 8)
    nrow = t // tm
    n_in, n_p, n_out = len(ins), len(params), len(outs)

    in_specs = [pl.BlockSpec((tm, bw or arr.shape[1]), lambda j, i, cf=cf: (i, cf(j))) for arr, bw, cf in ins]
    in_specs += [pl.BlockSpec(p.shape, lambda j, i: (0, 0)) for p in params]
    out_specs = [pl.BlockSpec((tm, bw or w), lambda j, i, cf=cf: (i, cf(j))) for w, bw, _, cf in outs]
    out_specs += [pl.BlockSpec((shp[0], bw or shp[1]), lambda j, i, cf=cf: (0, cf(j))) for shp, bw, cf in reds]
    out_shape = [jax.ShapeDtypeStruct((t, w), dt) for w, _, dt, _ in outs]
    out_shape += [jax.ShapeDtypeStruct(shp, F32) for shp, _, _ in reds]

    def body(*refs):
        in_refs, p_refs = refs[:n_in], refs[n_in:n_in + n_p]
        o_refs, r_refs = refs[n_in + n_p:n_in + n_p + n_out], refs[n_in + n_p + n_out:]
        args = [r[...] for r in in_refs] + [r[...] for r in p_refs]
        if with_j:
            args = [pl.program_id(0)] + args
        o_vals, r_vals = fn(*args)
        for r, v in zip(o_refs, o_vals, strict=True):
            r[...] = v.astype(r.dtype)
        if r_refs:
            i = pl.program_id(1)
            for r, v in zip(r_refs, r_vals, strict=True):
                @pl.when(i == 0)
                def _(r=r, v=v):
                    r[...] = v

                @pl.when(i > 0)
                def _(r=r, v=v):
                    r[...] += v

    res = pl.pallas_call(
        body, name=name, grid=(ncol, nrow), in_specs=in_specs, out_specs=out_specs, out_shape=out_shape,
        compiler_params=pltpu.CompilerParams(dimension_semantics=("parallel", "arbitrary" if reds else "parallel")),
    )(*[a for a, _, _ in ins], *params)
    return res


def _rms(x, g):
    return x * lax.rsqrt(jnp.mean(x * x, axis=-1, keepdims=True) + RMS_EPS) * g


def _rms_fwd(x, g, name):
    return _rw(lambda xv, gv: ([_rms(xv, gv)], []), [(x, None, _c0)], [(D_MODEL, None, BF16, _c0)], params=[g], name=name)[0]


def _rms_bwd(x, dh, dres, g, name):
    def fn(xv, dhv, drv, gv):
        _, vjp = jax.vjp(_rms, xv, gv)
        dx, dg = vjp(dhv)
        return [drv + dx], [dg]

    return _rw(fn, [(x, None, _c0), (dh, None, _c0), (dres, None, _c0)], [(D_MODEL, None, F32, _c0)], params=[g],
               reds=[((1, D_MODEL), None, _c0)], name=name)


def _bd_apply(acts, mats, combos, mode, *, name, tm=512):
    t = acts[0].shape[0]
    tm = _pick(t, tm, 8)
    nb, r, c = mats[0].shape
    win, wout = (r, c) if mode == "nn" else (c, r)
    dims = (_DIMS[mode], ((), ()))
    n_a, n_m = len(acts), len(mats)

    def body(*refs):
        a_vals = [ar[...].astype(BF16) for ar in refs[:n_a]]
        m_refs, o_refs = refs[n_a:n_a + n_m], refs[n_a + n_m:]
        for o_ref, terms in zip(o_refs, combos, strict=True):
            for j in range(nb):
                acc = None
                for ai, mi in terms:
                    part = lax.dot_general(a_vals[ai][:, j * win:(j + 1) * win], m_refs[mi][j], dims, preferred_element_type=F32)
                    acc = part if acc is None else acc + part
                o_ref[:, j * wout:(j + 1) * wout] = acc

    return pl.pallas_call(
        body, name=name, grid=(t // tm,),
        in_specs=[pl.BlockSpec((tm, nb * win), lambda i: (i, 0))] * n_a + [pl.BlockSpec(m.shape, lambda i: (0, 0, 0)) for m in mats],
        out_specs=[pl.BlockSpec((tm, nb * wout), lambda i: (i, 0))] * len(combos),
        out_shape=[jax.ShapeDtypeStruct((t, nb * wout), F32)] * len(combos),
        compiler_params=pltpu.CompilerParams(dimension_semantics=("parallel",)),
    )(*acts, *mats)


def _bd_grads(arrs, widths, pairs, *, name, tk=512):
    t = arrs[0].shape[0]
    tk = _pick(t, tk, 8)
    n_a = len(arrs)
    dims = (_DIMS["tn"], ((), ()))

    def body(*refs):
        vals = [ar[...].astype(BF16) for ar in refs[:n_a]]
        o_refs = refs[n_a:]
        @pl.when(pl.program_id(0) == 0)
        def _():
            for o_ref in o_refs:
                o_ref[...] = jnp.zeros_like(o_ref)

        for o_ref, (ai, bi) in zip(o_refs, pairs, strict=True):
            wa, wb = widths[ai], widths[bi]
            for j in range(SSM_BLOCKS):
                o_ref[j] += lax.dot_general(vals[ai][:, j * wa:(j + 1) * wa], vals[bi][:, j * wb:(j + 1) * wb], dims,
                                            preferred_element_type=F32)

    return pl.pallas_call(
        body, name=name, grid=(t // tk,),
        in_specs=[pl.BlockSpec((tk, SSM_BLOCKS * w), lambda k: (k, 0)) for w in widths],
        out_specs=[pl.BlockSpec((SSM_BLOCKS, widths[ai], widths[bi]), lambda k: (0, 0, 0)) for ai, bi in pairs],
        out_shape=[jax.ShapeDtypeStruct((SSM_BLOCKS, widths[ai], widths[bi]), F32) for ai, bi in pairs],
        compiler_params=pltpu.CompilerParams(dimension_semantics=("arbitrary",)),
    )(*arrs)


SCAN_LW = 256
SCAN_ROWS = 512
_DOUBLING = ((1, 0), (2, 1), (4, 2))


def _scan(xr, xi, pr, pi, dr, di, *, reverse, name, hr=None, hi=None):
    t, s = xr.shape
    lc = _pick(t, SCAN_ROWS, 8)
    nt, ngroups = t // lc, lc // 8
    with_da = hr is not None

    def tmap(l, tt):
        return ((nt - 1 - tt) if reverse else tt, l)

    x_spec = pl.BlockSpec((lc, SCAN_LW), tmap)
    tab_spec = pl.BlockSpec((8, SCAN_LW), lambda l, tt: (0, l))

    def body(*refs):
        xr_ref, xi_ref, pr_ref, pi_ref, dr_ref, di_ref = refs[:6]
        if with_da:
            hr_ref, hi_ref, or_ref, oi_ref, ar_ref, ai_ref, cr_ref, ci_ref = refs[6:]
        else:
            or_ref, oi_ref, cr_ref, ci_ref = refs[6:]
        tt = pl.program_id(1)

        @pl.when(tt == 0)
        def _():
            cr_ref[...] = jnp.zeros_like(cr_ref)
            ci_ref[...] = jnp.zeros_like(ci_ref)
            if with_da:
                ar_ref[...] = jnp.zeros_like(ar_ref)
                ai_ref[...] = jnp.zeros_like(ai_ref)

        sub = lax.broadcasted_iota(jnp.int32, (8, SCAN_LW), 0)
        pw_r, pw_i = pr_ref[...], pi_ref[...]

        def step(g, carry):
            g = (ngroups - 1 - g) if reverse else g
            r0 = pl.multiple_of(g * 8, 8)
            vr, vi = xr_ref[pl.ds(r0, 8), :], xi_ref[pl.ds(r0, 8), :]
            for shift, row in _DOUBLING:
                a_r, a_i = dr_ref[row:row + 1, :], di_ref[row:row + 1, :]
                if reverse:
                    keep = sub < 8 - shift
                    sr, si = pltpu.roll(vr, 8 - shift, 0), pltpu.roll(vi, 8 - shift, 0)
                else:
                    keep = sub >= shift
                    sr, si = pltpu.roll(vr, shift, 0), pltpu.roll(vi, shift, 0)
                sr, si = jnp.where(keep, sr, 0.0), jnp.where(keep, si, 0.0)
                vr, vi = vr + a_r * sr - a_i * si, vi + a_r * si + a_i * sr
            if with_da:
                cr, ci, acc_r, acc_i = carry
            else:
                cr, ci = carry
            vr, vi = vr + pw_r * cr - pw_i * ci, vi + pw_r * ci + pw_i * cr
            or_ref[pl.ds(r0, 8), :] = vr
            oi_ref[pl.ds(r0, 8), :] = vi
            if with_da:
                nr = jnp.where(sub < 7, pltpu.roll(vr, 7, 0), cr)
                ni = jnp.where(sub < 7, pltpu.roll(vi, 7, 0), ci)
                h_r, h_i = hr_ref[pl.ds(r0, 8), :], hi_ref[pl.ds(r0, 8), :]
                acc_r = acc_r + h_r * nr + h_i * ni
                acc_i = acc_i + h_r * ni - h_i * nr
            edge = 0 if reverse else 7
            cr = jnp.broadcast_to(vr[edge:edge + 1, :], vr.shape)
            ci = jnp.broadcast_to(vi[edge:edge + 1, :], vi.shape)
            return (cr, ci, acc_r, acc_i) if with_da else (cr, ci)

        zero = jnp.zeros((8, SCAN_LW), F32)
        init = (cr_ref[...], ci_ref[...]) + ((zero, zero) if with_da else ())
        fin = lax.fori_loop(0, ngroups, step, init)
        cr_ref[...] = fin[0]
        ci_ref[...] = fin[1]
        if with_da:
            ar_ref[...] += fin[2]
            ai_ref[...] += fin[3]

    n_x = 4 if with_da else 2
    out_specs = [x_spec, x_spec] + ([tab_spec, tab_spec] if with_da else [])
    out_shape = [jax.ShapeDtypeStruct((t, s), F32)] * 2 + ([jax.ShapeDtypeStruct((8, s), F32)] * 2 if with_da else [])
    operands = [xr, xi, pr, pi, dr, di] + ([hr, hi] if with_da else [])
    return pl.pallas_call(
        body, name=name, grid=(s // SCAN_LW, nt),
        in_specs=[x_spec, x_spec] + [tab_spec] * 4 + [x_spec] * (n_x - 2),
        out_specs=out_specs, out_shape=out_shape,
        scratch_shapes=[pltpu.VMEM((8, SCAN_LW), F32), pltpu.VMEM((8, SCAN_LW), F32)],
        compiler_params=pltpu.CompilerParams(dimension_semantics=("parallel", "arbitrary")),
    )(*operands)


CONV_TM = 256


def _conv_specs(t, tm):
    nrow = t // tm
    hb = tm // 8

    def col(cidx):
        return pl.BlockSpec((tm, BRANCH), lambda i: (i, cidx))

    def prev(cidx):
        return pl.BlockSpec((8, BRANCH), lambda i: (jnp.maximum(i * hb - 1, 0), cidx))

    def nxt(cidx):
        return pl.BlockSpec((8, BRANCH), lambda i: (jnp.minimum((i + 1) * hb, nrow * hb - 1), cidx))

    return nrow, col, prev, nxt


def _conv_taps(cc, cx, cc_prev, cx_prev, first):
    tm = cc.shape[0]
    v = cc * cx
    halo = cc_prev * cx_prev * jnp.where(first, 0.0, 1.0)
    ext = jnp.concatenate([halo, v], axis=0)
    return v, pltpu.roll(ext, 1, 0)[8:8 + tm], pltpu.roll(ext, 2, 0)[8:8 + tm]


def _conv_fwd(z, conv_w, name):
    t = z.shape[0]
    tm = _pick(t, CONV_TM, 8)
    nrow, col, prev, _ = _conv_specs(t, tm)

    def body(cb_ref, cc_ref, cx_ref, ccp_ref, cxp_ref, w_ref, o_ref):
        first = pl.program_id(0) == 0
        v, v1, v2 = _conv_taps(cc_ref[...], cx_ref[...], ccp_ref[...], cxp_ref[...], first)
        y = w_ref[0:1, :] * v2 + w_ref[1:2, :] * v1 + w_ref[2:3, :] * v
        o_ref[...] = (cb_ref[...] * y).astype(o_ref.dtype)

    return pl.pallas_call(
        body, name=name, grid=(nrow,),
        in_specs=[col(1), col(2), col(3), prev(2), prev(3), pl.BlockSpec((3, BRANCH), lambda i: (0, 0))],
        out_specs=pl.BlockSpec((tm, BRANCH), lambda i: (i, 0)), out_shape=jax.ShapeDtypeStruct((t, BRANCH), BF16),
        compiler_params=pltpu.CompilerParams(dimension_semantics=("parallel",)),
    )(z, z, z, z, z, conv_w)


def _conv_bwd(dyc, z, conv_w, name):
    t = z.shape[0]
    tm = _pick(t, CONV_TM, 8)
    nrow, col, prev, nxt = _conv_specs(t, tm)
    d_cur = pl.BlockSpec((tm, BRANCH), lambda i: (i, 0))
    d_nxt = pl.BlockSpec((8, BRANCH), lambda i: (jnp.minimum((i + 1) * (tm // 8), nrow * (tm // 8) - 1), 0))

    def body(dy_ref, dyn_ref, cb_ref, cbn_ref, cc_ref, cx_ref, ccp_ref, cxp_ref, w_ref, dcb_ref, dcc_ref, dcx_ref, dw_ref):
        i = pl.program_id(0)
        cc, cx, cb = cc_ref[...], cx_ref[...], cb_ref[...]
        v, v1, v2 = _conv_taps(cc, cx, ccp_ref[...], cxp_ref[...], i == 0)
        w0, w1, w2 = w_ref[0:1, :], w_ref[1:2, :], w_ref[2:3, :]
        y = w0 * v2 + w1 * v1 + w2 * v
        dyc_v = dy_ref[...]
        dcb_ref[...] = (dyc_v * y).astype(dcb_ref.dtype)
        dy = dyc_v * cb
        halo = dyn_ref[...] * cbn_ref[...] * jnp.where(i == nrow - 1, 0.0, 1.0)
        ext = jnp.concatenate([dy, halo], axis=0)
        dy1 = pltpu.roll(ext, tm + 8 - 1, 0)[0:tm]
        dy2 = pltpu.roll(ext, tm + 8 - 2, 0)[0:tm]
        dv = w2 * dy + w1 * dy1 + w0 * dy2
        dcc_ref[...] = (dv * cx).astype(dcc_ref.dtype)
        dcx_ref[...] = (dv * cc).astype(dcx_ref.dtype)
        dw = jnp.concatenate([jnp.sum(dy * v2, axis=0, keepdims=True), jnp.sum(dy * v1, axis=0, keepdims=True),
                              jnp.sum(dy * v, axis=0, keepdims=True), jnp.zeros((5, BRANCH), F32)], axis=0)

        @pl.when(i == 0)
        def _():
            dw_ref[...] = dw

        @pl.when(i > 0)
        def _():
            dw_ref[...] += dw

    o_spec = pl.BlockSpec((tm, BRANCH), lambda i: (i, 0))
    return pl.pallas_call(
        body, name=name, grid=(nrow,),
        in_specs=[d_cur, d_nxt, col(1), nxt(1), col(2), col(3), prev(2), prev(3), pl.BlockSpec((3, BRANCH), lambda i: (0, 0))],
        out_specs=[o_spec, o_spec, o_spec, pl.BlockSpec((8, BRANCH), lambda i: (0, 0))],
        out_shape=[jax.ShapeDtypeStruct((t, BRANCH), BF16)] * 3 + [jax.ShapeDtypeStruct((8, BRANCH), F32)],
        compiler_params=pltpu.CompilerParams(dimension_semantics=("arbitrary",)),
    )(dyc, dyc, z, z, z, z, z, z, conv_w)


def _attn_specs():
    q_spec = pl.BlockSpec((N_Q, WINDOW, HEAD_DIM), lambda n: (0, n, 0))
    kv_cur = pl.BlockSpec((N_KV, WINDOW, HEAD_DIM), lambda n: (0, n, 0))
    kv_prev = pl.BlockSpec((N_KV, WINDOW, HEAD_DIM), lambda n: (0, jnp.maximum(n - 1, 0), 0))
    bias_spec = pl.BlockSpec((N_Q, WINDOW, 2 * WINDOW), lambda n: (0, 0, 0))
    sink_spec = pl.BlockSpec((N_Q, 1), lambda n: (0, 0))
    return q_spec, kv_cur, kv_prev, bias_spec, sink_spec


def _attn_valid(n):
    qi = lax.broadcasted_iota(jnp.int32, (WINDOW, 2 * WINDOW), 0)
    kj = lax.broadcasted_iota(jnp.int32, (WINDOW, 2 * WINDOW), 1)
    dist = qi + WINDOW - kj
    first_key = jnp.where(n > 0, 0, WINDOW)
    return (dist >= 0) & (dist < WINDOW) & (kj >= first_key)


def _attn_probs(q, kc, bias, sink, valid):
    s = lax.dot_general(q, kc, (_DIMS["nt"], ((), ())), preferred_element_type=F32) * ATTN_SCALE + bias
    s = jnp.where(valid, s, NEG)
    m = jnp.maximum(jnp.max(s, axis=1, keepdims=True), sink)
    p = jnp.exp(s - m)
    e_sink = jnp.exp(sink - m)
    inv = 1.0 / (jnp.sum(p, axis=1, keepdims=True) + e_sink)
    return p * inv, e_sink * inv


def _attn_fwd(qh, kh, vh, bias, sinks, name):
    t = qh.shape[1]
    q_spec, kv_cur, kv_prev, bias_spec, sink_spec = _attn_specs()

    def body(q_ref, kp_ref, kc_ref, vp_ref, vc_ref, b_ref, s_ref, o_ref):
        valid = _attn_valid(pl.program_id(0))
        for h in range(N_KV):
            kc = jnp.concatenate([kp_ref[h], kc_ref[h]], axis=0)
            vc = jnp.concatenate([vp_ref[h], vc_ref[h]], axis=0)
            for g in range(GQA):
                hq = GQA * h + g
                w, _ = _attn_probs(q_ref[hq], kc, b_ref[hq], s_ref[hq:hq + 1, :], valid)
                o_ref[hq] = jnp.dot(w.astype(BF16), vc, preferred_element_type=F32).astype(o_ref.dtype)

    return pl.pallas_call(
        body, name=name, grid=(t // WINDOW,),
        in_specs=[q_spec, kv_prev, kv_cur, kv_prev, kv_cur, bias_spec, sink_spec],
        out_specs=q_spec, out_shape=jax.ShapeDtypeStruct((N_Q, t, HEAD_DIM), BF16),
        compiler_params=pltpu.CompilerParams(dimension_semantics=("parallel",)),
    )(qh, kh, kh, vh, vh, bias, sinks)


def _attn_bwd(qh, kh, vh, doh, bias, sinks, name):
    t = qh.shape[1]
    q_spec, kv_cur, kv_prev, bias_spec, sink_spec = _attn_specs()

    def body(q_ref, kp_ref, kc_ref, vp_ref, vc_ref, do_ref, b_ref, s_ref,
             dq_ref, dkc_ref, dkp_ref, dvc_ref, dvp_ref, db_ref, ds_ref):
        n = pl.program_id(0)
        valid = _attn_valid(n)

        @pl.when(n == 0)
        def _():
            db_ref[...] = jnp.zeros_like(db_ref)
            ds_ref[...] = jnp.zeros_like(ds_ref)

        for h in range(N_KV):
            kc = jnp.concatenate([kp_ref[h], kc_ref[h]], axis=0)
            vc = jnp.concatenate([vp_ref[h], vc_ref[h]], axis=0)
            dk = jnp.zeros((2 * WINDOW, HEAD_DIM), F32)
            dv = jnp.zeros((2 * WINDOW, HEAD_DIM), F32)
            for g in range(GQA):
                hq = GQA * h + g
                q, do = q_ref[hq], do_ref[hq]
                w, w_sink = _attn_probs(q, kc, b_ref[hq], s_ref[hq:hq + 1, :], valid)
                dw = lax.dot_general(do, vc, (_DIMS["nt"], ((), ())), preferred_element_type=F32)
                delta = jnp.sum(w * dw, axis=1, keepdims=True)
                dscore = w * (dw - delta)
                ds_ref[hq] += -w_sink * delta
                db_ref[hq] += dscore
                dsb = dscore.astype(BF16)
                dq_ref[hq] = jnp.dot(dsb, kc, preferred_element_type=F32) * ATTN_SCALE
                dk = dk + lax.dot_general(dsb, q, (_DIMS["tn"], ((), ())), preferred_element_type=F32) * ATTN_SCALE
                dv = dv + lax.dot_general(w.astype(BF16), do, (_DIMS["tn"], ((), ())), preferred_element_type=F32)
            dkp_ref[h], dkc_ref[h] = dk[0:WINDOW], dk[WINDOW:2 * WINDOW]
            dvp_ref[h], dvc_ref[h] = dv[0:WINDOW], dv[WINDOW:2 * WINDOW]

    kv_shape = jax.ShapeDtypeStruct((N_KV, t, HEAD_DIM), F32)
    return pl.pallas_call(
        body, name=name, grid=(t // WINDOW,),
        in_specs=[q_spec, kv_prev, kv_cur, kv_prev, kv_cur, q_spec, bias_spec, sink_spec],
        out_specs=[q_spec, kv_cur, kv_cur, kv_cur, kv_cur, bias_spec, pl.BlockSpec((N_Q, WINDOW, 1), lambda n: (0, 0, 0))],
        out_shape=[jax.ShapeDtypeStruct((N_Q, t, HEAD_DIM), F32), kv_shape, kv_shape, kv_shape, kv_shape,
                   jax.ShapeDtypeStruct((N_Q, WINDOW, 2 * WINDOW), F32), jax.ShapeDtypeStruct((N_Q, WINDOW, 1), F32)],
        compiler_params=pltpu.CompilerParams(dimension_semantics=("arbitrary",)),
    )(qh, kh, kh, vh, vh, doh, bias, sinks)


def _heads(a, n_heads):
    t = a.shape[0]
    return a.astype(BF16).reshape(t, n_heads, HEAD_DIM).transpose(1, 0, 2)


def _unheads(a):
    n_heads, t, _ = a.shape
    return a.transpose(1, 0, 2).reshape(t, n_heads * HEAD_DIM)


def _shift_blocks(cur, prev):
    return cur + jnp.concatenate([prev[:, WINDOW:], jnp.zeros_like(prev[:, :WINDOW])], axis=1)


def _t5_bucket_table():
    qi = np.arange(WINDOW)[:, None]
    kj = np.arange(2 * WINDOW)[None, :]
    dist = np.clip(qi + WINDOW - kj, 0, REL_MAX_DIST - 1)
    exact = REL_BUCKETS // 2
    df = np.maximum(dist, 1).astype(np.float32)
    large = exact + (np.log(df / np.float32(exact)) / np.float32(math.log(REL_MAX_DIST / exact)) * (REL_BUCKETS - exact)).astype(np.int32)
    large = np.minimum(large, REL_BUCKETS - 1)
    bucket = np.where(dist < exact, dist, large)
    onehot = np.zeros((WINDOW * 2 * WINDOW, REL_BUCKETS), np.float32)
    onehot[np.arange(WINDOW * 2 * WINDOW), bucket.reshape(-1)] = 1.0
    return onehot


def _band_bias(rel_bias):
    onehot = jnp.asarray(_t5_bucket_table())
    sel = jnp.sum(onehot[:, :, None] * rel_bias[None, :, :], axis=1)
    return sel.T.reshape(N_Q, WINDOW, 2 * WINDOW)


def _block_diag(a):
    g, r, c = a.shape
    a4 = a.reshape(SSM_BLOCKS, g // SSM_BLOCKS, r, c)
    eye = jnp.eye(g // SSM_BLOCKS, dtype=a.dtype)
    full = a4[:, :, :, None, :] * eye[None, :, None, :, None]
    return full.reshape(SSM_BLOCKS, (g // SSM_BLOCKS) * r, (g // SSM_BLOCKS) * c)


def _ssm_disc(lam_re, lam_im, b_re, b_im, c_re, c_im, log_dt):
    dt = jnp.exp(log_dt)[:, None]
    mag = jnp.exp(lam_re * dt)
    ang = lam_im * dt
    a_re = mag * jnp.cos(ang)
    a_im = mag * jnp.sin(ang)
    den = lam_re * lam_re + lam_im * lam_im
    nr = a_re - 1.0
    coef_re = (nr * lam_re + a_im * lam_im) / den
    coef_im = (a_im * lam_re - nr * lam_im) / den
    bb_re = coef_re[..., None] * b_re - coef_im[..., None] * b_im
    bb_im = coef_re[..., None] * b_im + coef_im[..., None] * b_re
    wb_re = _block_diag(jnp.swapaxes(bb_re, 1, 2))
    wb_im = _block_diag(jnp.swapaxes(bb_im, 1, 2))
    cm_re = _block_diag(jnp.swapaxes(c_re, 1, 2))
    cm_imn = _block_diag(-jnp.swapaxes(c_im, 1, 2))
    return a_re.reshape(-1), a_im.reshape(-1), wb_re, wb_im, cm_re, cm_imn


def _scan_tables(a_re, a_im):
    pr, pi = [a_re], [a_im]
    for _ in range(7):
        pr, pi = pr + [pr[-1] * a_re - pi[-1] * a_im], pi + [pr[-1] * a_im + pi[-1] * a_re]
    pr, pi = jnp.stack(pr), jnp.stack(pi)
    pad = jnp.zeros((5,) + a_re.shape, F32)
    dr = jnp.concatenate([jnp.stack([pr[0], pr[1], pr[3]]), pad])
    di = jnp.concatenate([jnp.stack([pi[0], pi[1], pi[3]]), pad])
    fwd = (pr, pi, dr, di)
    rev = (pr[::-1], -pi[::-1], dr, -di)
    return jax.tree.map(lax.stop_gradient, (fwd, rev))


def _gate_col(r):
    return lambda j: (OFF_G + r * D_MODEL) // 256 + j


def _layer_fwd(x, p_i, w, bias, li):
    nm = lambda s: f"{s}_l{li}"
    h = _rms_fwd(x, w["norm_mix"], nm("rms_mix"))
    z = _mm(h, w["w_in"], "nn", tm=512, tn=2944, name=nm("mm_in"))
    bu_re, bu_im = _bd_apply([z], [w["wb_re"], w["wb_im"]], [[(0, 0)], [(0, 1)]], "nn", name=nm("ssm_bu"))
    h_re, h_im = _scan(bu_re, bu_im, *w["scan_fwd"], reverse=False, name=nm("ssm_scan"))
    (y0,) = _bd_apply([h_re, h_im], [w["cm_re"], w["cm_imn"]], [[(0, 0), (1, 1)]], "nn", name=nm("ssm_c"))
    (y1,) = _rw(lambda a, u, d: ([jax.nn.gelu(a + d * u)], []),
                [(y0, None, _c0), (z, BRANCH, _c0)], [(BRANCH, None, F32, _c0)],
                params=[w["ssm_d"]], name=nm("ssm_gelu"))
    gl = _mm(y1, w["ssm_w_glu"], "nn", name=nm("mm_glu"))
    (y_ssm,) = _rw(lambda a, b: ([a * jax.nn.sigmoid(b)], []), [(y1, None, _c0), (gl, None, _c0)],
                   [(BRANCH, None, BF16, _c0)], name=nm("ssm_glu"))
    y_conv = _conv_fwd(z, w["conv_w"], nm("conv_fwd"))
    qh, kh, vh = _heads(z[:, OFF_Q:OFF_K], N_Q), _heads(z[:, OFF_K:OFF_V], N_KV), _heads(z[:, OFF_V:OFF_G], N_KV)
    y_attn = _unheads(_attn_fwd(qh, kh, vh, bias, w["sinks"], nm("attn_fwd")))
    ys = (y_ssm, y_conv, y_attn)
    bs = [_mm(ys[r], w["w_branch"][r], "nn", name=nm(f"mm_branch{r}")) for r in range(3)]

    def merge(g0, g1, g2, b0, b1, b2):
        return [jax.nn.sigmoid(g0) * b0 + jax.nn.sigmoid(g1) * b1 + jax.nn.sigmoid(g2) * b2], []

    (merged,) = _rw(merge, [(z, 256, _gate_col(r)) for r in range(3)] + [(b, 256, lambda j: j) for b in bs],
                    [(D_MODEL, 256, BF16, lambda j: j)], ncol=4, name=nm("merge"))
    x1 = _mm(merged, w["w_out"], "nn", add=x, name=nm("mm_out"))
    hf_in = _rms_fwd(x1, w["norm_ffn"], nm("rms_ffn"))
    hf = _mm(hf_in, w["w_ffn_in"], "nn", tn=1408, name=nm("mm_ffn_in"))
    (act,) = _rw(lambda a, b: ([jax.nn.silu(a) * b], []), [(hf, FFN_COLS, lambda j: j), (hf, FFN_COLS, lambda j: FFN_NCOL + j)],
                 [(FFN_HIDDEN, FFN_COLS, BF16, lambda j: j)], ncol=FFN_NCOL, name=nm("swiglu"))
    x2 = _mm(act, w["w_ffn_out"], "nn", add=x1, tk=1408, name=nm("mm_ffn_out"))
    hp = _rms_fwd(x2, w["norm_ple"], nm("rms_ple"))
    pgl = _mm(hp, w["w_ple_gate"], "nn", name=nm("mm_ple_gate"))
    pp = _mm(p_i, w["w_ple_proj"], "nn", name=nm("mm_ple_proj"))
    (x3,) = _rw(lambda xv, a, b: ([xv + jax.nn.sigmoid(a) * b], []), [(x2, None, _c0), (pgl, None, _c0), (pp, None, _c0)],
                [(D_MODEL, None, F32, _c0)], name=nm("ple_add"))
    saved = dict(x=x, p=p_i, h=h, z=z, h_re=h_re, h_im=h_im, y0=y0, y1=y1, gl=gl, ys=ys, qh=qh, kh=kh, vh=vh,
                 bs=bs, merged=merged, x1=x1, hf_in=hf_in, hf=hf, act=act, x2=x2, hp=hp, pgl=pgl, pp=pp)
    return x3, saved


def _layer_bwd(dx3, s, w, bias, li):
    nm = lambda n: f"{n}_l{li}"
    g = {}
    z = s["z"]
    def ple_b(d, a, b):
        _, vjp = jax.vjp(lambda a_, b_: jax.nn.sigmoid(a_) * b_, a, b)
        return list(vjp(d)), []

    dpgl, dpp = _rw(ple_b, [(dx3, None, _c0), (s["pgl"], None, _c0), (s["pp"], None, _c0)],
                    [(D_MODEL, None, BF16, _c0)] * 2, name=nm("ple_bwd"))
    g["w_ple_proj"] = _mm(s["p"], dpp, "tn", name=nm("mmg_ple_proj"))
    g["w_ple_gate"] = _mm(s["hp"], dpgl, "tn", name=nm("mmg_ple_gate"))
    dhp = _mm(dpgl, w["w_ple_gate"], "nt", name=nm("mmb_ple_gate"))
    dx2, g["norm_ple"] = _rms_bwd(s["x2"], dhp, dx3, w["norm_ple"], nm("rmsb_ple"))
    dact = _mm(dx2, w["w_ffn_out"], "nt", tn=1408, name=nm("mmb_ffn_out"))
    g["w_ffn_out"] = _mm(s["act"], dx2, "tn", tm=1408, name=nm("mmg_ffn_out"))

    def swiglu_b(a, b, d):
        _, vjp = jax.vjp(lambda a_, b_: jax.nn.silu(a_) * b_, a, b)
        return list(vjp(d)), []

    dhf_a, dhf_b = _rw(swiglu_b, [(s["hf"], FFN_COLS, lambda j: j), (s["hf"], FFN_COLS, lambda j: FFN_NCOL + j), (dact, FFN_COLS, lambda j: j)],
                       [(FFN_HIDDEN, FFN_COLS, BF16, lambda j: j)] * 2, ncol=FFN_NCOL, name=nm("swiglu_bwd"))
    g["w_ffn_in"] = jnp.concatenate([_mm(s["hf_in"], dhf_a, "tn", tn=1408, name=nm("mmg_ffn_in_a")),
                                     _mm(s["hf_in"], dhf_b, "tn", tn=1408, name=nm("mmg_ffn_in_b"))], axis=1)
    dhf_in = _mm(dhf_a, w["w_ffn_in"], "nt", tk=1408, name=nm("mmb_ffn_in_a"))
    dhf_in = _mm(dhf_b, w["w_ffn_in"], "nt", tk=1408, b_k0=FFN_HIDDEN, add=dhf_in, name=nm("mmb_ffn_in_b"))
    dx1, g["norm_ffn"] = _rms_bwd(s["x1"], dhf_in, dx2, w["norm_ffn"], nm("rmsb_ffn"))
    dmerged = _mm(dx1, w["w_out"], "nt", name=nm("mmb_out"))
    g["w_out"] = _mm(s["merged"], dx1, "tn", name=nm("mmg_out"))

    def merge_b(d, g0, g1, g2, b0, b1, b2):
        outs_g, outs_b = [], []
        for gate, br in ((g0, b0), (g1, b1), (g2, b2)):
            sg = jax.nn.sigmoid(gate)
            outs_g.append(d * br * sg * (1.0 - sg))
            outs_b.append(d * sg)
        return outs_g + outs_b, []

    res = _rw(merge_b, [(dmerged, 256, lambda j: j)] + [(z, 256, _gate_col(r)) for r in range(3)] + [(b, 256, lambda j: j) for b in s["bs"]],
              [(D_MODEL, 256, BF16, lambda j: j)] * 6, ncol=4, name=nm("merge_bwd"))
    dgates, dbs = res[:3], res[3:]
    dys = [_mm(dbs[r], w["w_branch"][r], "nt", name=nm(f"mmb_branch{r}")) for r in range(3)]
    g["w_branch"] = jnp.stack([_mm(s["ys"][r], dbs[r], "tn", name=nm(f"mmg_branch{r}")) for r in range(3)])
    doh = _heads(dys[2], N_Q)
    dqh, dkc, dkp, dvc, dvp, dbias, dsink = _attn_bwd(s["qh"], s["kh"], s["vh"], doh, bias, w["sinks"], nm("attn_bwd"))
    dq, dk, dv = _unheads(dqh), _unheads(_shift_blocks(dkc, dkp)), _unheads(_shift_blocks(dvc, dvp))
    g["sinks"] = jnp.sum(dsink, axis=(1, 2))
    dcb, dcc, dcx, dconv = _conv_bwd(dys[1], z, w["conv_w"], nm("conv_bwd"))
    g["conv_w"] = dconv[0:3]
    def glu_b(d, y1, gl):
        sg = jax.nn.sigmoid(gl)
        return [d * y1 * sg * (1.0 - sg), d * sg], []

    dgl, dy1a = _rw(glu_b, [(dys[0], None, _c0), (s["y1"], None, _c0), (s["gl"], None, _c0)],
                    [(BRANCH, None, BF16, _c0), (BRANCH, None, F32, _c0)], name=nm("ssm_glu_bwd"))
    g["ssm_w_glu"] = _mm(s["y1"], dgl, "tn", name=nm("mmg_glu"))
    dy1b = _mm(dgl, w["ssm_w_glu"], "nt", name=nm("mmb_glu"))

    def gelu_b(da, db, a, u, d):
        _, vjp = jax.vjp(lambda pre: jax.nn.gelu(pre), a + d * u)
        (dy0,) = vjp(da + db)
        return [dy0, dy0 * d], [jnp.sum(dy0 * u, axis=0, keepdims=True)]

    dy0, du_a, g["ssm_d"] = _rw(gelu_b, [(dy1a, None, _c0), (dy1b, None, _c0), (s["y0"], None, _c0), (z, BRANCH, _c0)],
                                [(BRANCH, None, BF16, _c0), (BRANCH, None, F32, _c0)], params=[w["ssm_d"]],
                                reds=[((1, BRANCH), None, _c0)], name=nm("ssm_gelu_bwd"))
    dh_re, dh_im = _bd_apply([dy0], [w["cm_re"], w["cm_imn"]], [[(0, 0)], [(0, 1)]], "nt", name=nm("ssmb_c"))
    sb, cb = SSM_STATES // SSM_BLOCKS, BRANCH // SSM_BLOCKS
    g["cm_re"], g["cm_imn"] = _bd_grads([s["h_re"], s["h_im"], dy0], [sb, sb, cb], [(0, 2), (1, 2)], name=nm("ssmg_c"))
    l_re, l_im, da_re, da_im = _scan(dh_re, dh_im, *w["scan_rev"], reverse=True, hr=s["h_re"], hi=s["h_im"], name=nm("ssm_scan_bwd"))
    g["a_re"], g["a_im"] = jnp.sum(da_re, axis=0), jnp.sum(da_im, axis=0)
    (du_b,) = _bd_apply([l_re, l_im], [w["wb_re"], w["wb_im"]], [[(0, 0), (1, 1)]], "nt", name=nm("ssmb_bu"))
    g["wb_re"], g["wb_im"] = _bd_grads([z, l_re, l_im], [cb, sb, sb], [(0, 1), (0, 2)], name=nm("ssmg_bu"))
    dz = jnp.concatenate([(du_a + du_b).astype(BF16), dcb, dcc, dcx, dq.astype(BF16), dk.astype(BF16), dv.astype(BF16)] + list(dgates), axis=1)
    g["w_in"] = _mm(s["h"], dz, "tn", tm=512, tn=2944, name=nm("mmg_in"))
    dh = _mm(dz, w["w_in"], "nt", tk=2944, name=nm("mmb_in"))
    dx, g["norm_mix"] = _rms_bwd(s["x"], dh, dx1, w["norm_mix"], nm("rmsb_mix"))
    return dx, g, dbias


def _loss_and_seed(x, target, g_final):
    def fn(xv, tv, gv):
        y, vjp = jax.vjp(_rms, xv, gv)
        err = y - tv
        dx, dg = vjp(err * (1.0 / D_MODEL))
        return [dx], [jnp.sum(err * err, axis=0, keepdims=True) * (0.5 / D_MODEL), dg]

    return _rw(fn, [(x, None, _c0), (target, None, _c0)], [(D_MODEL, None, F32, _c0)], params=[g_final],
               reds=[((1, D_MODEL), None, _c0)] * 2, name="loss_head")


def _local_step(x, p, target, wt):
    bias, bias_vjp = jax.vjp(_band_bias, wt["rel_bias"])
    layers, disc_vjps = [], []
    for i in range(DEPTH):
        ssm_p = [wt[k][i] for k in ("ssm_lambda_re", "ssm_lambda_im", "ssm_b_re", "ssm_b_im", "ssm_c_re", "ssm_c_im", "ssm_log_dt")]
        (a_re, a_im, wb_re, wb_im, cm_re, cm_imn), disc_vjp = jax.vjp(_ssm_disc, *ssm_p)
        scan_fwd, scan_rev = _scan_tables(a_re, a_im)
        layers.append(dict(
            norm_mix=wt["norm_mix"][i][None], w_in=wt["w_in"][i], wb_re=wb_re.astype(BF16), wb_im=wb_im.astype(BF16),
            cm_re=cm_re.astype(BF16), cm_imn=cm_imn.astype(BF16), scan_fwd=scan_fwd, scan_rev=scan_rev,
            ssm_d=wt["ssm_d"][i][None], ssm_w_glu=wt["ssm_w_glu"][i], conv_w=wt["conv_w"][i],
            sinks=wt["attn_sinks"][i][:, None], w_branch=wt["w_branch"][i], w_out=wt["w_out"][i],
            norm_ffn=wt["norm_ffn"][i][None], w_ffn_in=wt["w_ffn_in"][i], w_ffn_out=wt["w_ffn_out"][i],
            norm_ple=wt["norm_ple"][i][None], w_ple_gate=wt["w_ple_gate"][i], w_ple_proj=wt["w_ple_proj"][i]))
        disc_vjps.append(disc_vjp)

    saved = []
    for i in range(DEPTH):
        x, s = _layer_fwd(x, p[i], layers[i], bias, i)
        saved.append(s)
    dx, loss_cols, g_final = _loss_and_seed(x, target, wt["norm_final"][None])
    loss = jnp.sum(loss_cols)

    per_layer = [None] * DEPTH
    dbias = None
    for i in reversed(range(DEPTH)):
        dx, g, db = _layer_bwd(dx, saved[i], layers[i], bias, i)
        dbias = db if dbias is None else dbias + db
        (g["ssm_lambda_re"], g["ssm_lambda_im"], g["ssm_b_re"], g["ssm_b_im"], g["ssm_c_re"], g["ssm_c_im"], g["ssm_log_dt"]) = \
            disc_vjps[i]((g.pop("a_re"), g.pop("a_im"), g.pop("wb_re"), g.pop("wb_im"), g.pop("cm_re"), g.pop("cm_imn")))
        g["attn_sinks"] = g.pop("sinks")
        for k in ("norm_mix", "norm_ffn", "norm_ple", "ssm_d"):
            g[k] = g[k][0]
        per_layer[i] = g
    big_names = [name for name, _, _ in BIG]
    big = {k: [per_layer[i][k] for i in range(DEPTH)] for k in big_names}
    small = {k: jnp.stack([per_layer[i][k] for i in range(DEPTH)]) for k in per_layer[0] if k not in big_names}
    (small["rel_bias"],) = bias_vjp(dbias)
    small["norm_final"] = g_final[0]
    return loss, dx, small, big


HBM_SPEC = pl.BlockSpec(memory_space=pltpu.HBM)


def _position():
    x, y, c = lax.axis_index("x"), lax.axis_index("y"), lax.axis_index("c")
    other_chips = [(1 - x, y), (x, 1 - y), (1 - x, 1 - y)]
    return x, y, c, other_chips


def _allgather_weights(w_local):
    _, r, lanes = w_local.shape
    nq = COPY_CHUNKS
    rq = r // nq
    assert rq * nq == r and rq % 16 == 0

    def body(w_ref, out_ref, send_sems, recv_sems, local_sem):
        x, y, c, chips = _position()
        me = 2 * x + y
        sibling = (x, y, 1 - c)

        def copy(k, src, dst, to):
            return pltpu.make_async_remote_copy(src_ref=src, dst_ref=dst, send_sem=send_sems.at[k], recv_sem=recv_sems.at[k],
                                                device_id=to, device_id_type=MESH)

        def rows(q):
            return pl.ds(q * rq, rq)

        mine = pltpu.make_async_copy(w_ref, out_ref.at[me], local_sem)
        mine.start()
        first = [copy(j * nq + q, w_ref.at[c, rows(q)], out_ref.at[me, c, rows(q)], (*chip, c))
                 for q in range(nq) for j, chip in enumerate(chips)]
        for cp in first:
            cp.start()
        passed = []
        for q in range(nq):
            for j, (px, py) in enumerate(chips):
                landed = out_ref.at[2 * px + py, c, rows(q)]
                copy(j * nq + q, landed, landed, (px, py, c)).wait_recv()
                fwd = copy((3 + j) * nq + q, landed, landed, sibling)
                fwd.start()
                passed.append(fwd)
        for q in range(nq):
            for j, (px, py) in enumerate(chips):
                landed = out_ref.at[2 * px + py, 1 - c, rows(q)]
                copy((3 + j) * nq + q, landed, landed, sibling).wait_recv()
        for cp in first + passed:
            cp.wait_send()
        mine.wait()

    return pl.pallas_call(
        body, name="allgather_weights", in_specs=[HBM_SPEC], out_specs=HBM_SPEC,
        out_shape=jax.ShapeDtypeStruct((N_SHARD, 2, r, lanes), w_local.dtype),
        scratch_shapes=[pltpu.SemaphoreType.DMA((6 * nq,)), pltpu.SemaphoreType.DMA((6 * nq,)), pltpu.SemaphoreType.DMA],
    )(w_local)


def _sibling_exchange(g2):
    ns, _, r, lanes = g2.shape

    nq = COPY_CHUNKS
    rq = r // nq
    assert rq * nq == r and rq % 16 == 0

    def body(g_ref, own_ref, got_ref, send_sems, recv_sems, local_sems):
        x, y, c, _ = _position()
        keeps = [pltpu.make_async_copy(g_ref.at[s, c], own_ref.at[s], local_sems.at[s]) for s in range(ns)]
        for cp in keeps:
            cp.start()
        swaps = [pltpu.make_async_remote_copy(src_ref=g_ref.at[s, 1 - c, pl.ds(q * rq, rq)], dst_ref=got_ref.at[s, pl.ds(q * rq, rq)],
                                              send_sem=send_sems.at[s * nq + q], recv_sem=recv_sems.at[s * nq + q],
                                              device_id=(x, y, 1 - c), device_id_type=MESH)
                 for s in range(ns) for q in range(nq)]
        for cp in swaps:
            cp.start()
        for cp in swaps:
            cp.wait()
        for cp in keeps:
            cp.wait()

    shape = jax.ShapeDtypeStruct((ns, r, lanes), g2.dtype)
    return pl.pallas_call(
        body, name="grad_sibling_exchange", in_specs=[HBM_SPEC], out_specs=[HBM_SPEC, HBM_SPEC], out_shape=[shape, shape],
        scratch_shapes=[pltpu.SemaphoreType.DMA((ns * nq,)), pltpu.SemaphoreType.DMA((ns * nq,)), pltpu.SemaphoreType.DMA((ns,))],
    )(g2)


def _chip_exchange(part_f32, part_bf16):
    ns, r, lanes = part_f32.shape

    nq = COPY_CHUNKS
    rq = r // nq
    assert rq * nq == r and rq % 16 == 0

    def body(f_ref, b_ref, own_ref, got_ref, send_sems, recv_sems, local_sem):
        x, y, c, chips = _position()
        keep = pltpu.make_async_copy(f_ref.at[2 * x + y], own_ref, local_sem)
        keep.start()
        sends = [pltpu.make_async_remote_copy(src_ref=b_ref.at[2 * px + py, pl.ds(q * rq, rq)], dst_ref=got_ref.at[j, pl.ds(q * rq, rq)],
                                              send_sem=send_sems.at[j * nq + q], recv_sem=recv_sems.at[j * nq + q],
                                              device_id=(px, py, c), device_id_type=MESH)
                 for q in range(nq) for j, (px, py) in enumerate(chips)]
        for cp in sends:
            cp.start()
        for cp in sends:
            cp.wait()
        keep.wait()

    return pl.pallas_call(
        body, name="grad_chip_exchange", in_specs=[HBM_SPEC, HBM_SPEC], out_specs=[HBM_SPEC, HBM_SPEC],
        out_shape=[jax.ShapeDtypeStruct((r, lanes), F32), jax.ShapeDtypeStruct((ns - 1, r, lanes), BF16)],
        scratch_shapes=[pltpu.SemaphoreType.DMA((3 * nq,)), pltpu.SemaphoreType.DMA((3 * nq,)), pltpu.SemaphoreType.DMA],
    )(part_f32, part_bf16)


def _sibling_gather(half):
    r, lanes = half.shape

    nq = 2 * COPY_CHUNKS
    rq = r // nq
    assert rq * nq == r and rq % 16 == 0

    def body(h_ref, out_ref, send_sems, recv_sems, local_sem):
        x, y, c, _ = _position()
        keep = pltpu.make_async_copy(h_ref, out_ref.at[c], local_sem)
        keep.start()

        def chunk(q, half_idx):
            rows = pl.ds(q * rq, rq)
            return pltpu.make_async_remote_copy(src_ref=h_ref.at[rows], dst_ref=out_ref.at[half_idx, rows], send_sem=send_sems.at[q],
                                                recv_sem=recv_sems.at[q], device_id=(x, y, 1 - c), device_id_type=MESH)

        pushes = [chunk(q, c) for q in range(nq)]
        for cp in pushes:
            cp.start()
        for q in range(nq):
            chunk(q, 1 - c).wait_recv()
        for cp in pushes:
            cp.wait_send()
        keep.wait()

    return pl.pallas_call(
        body, name="grad_sibling_gather", in_specs=[HBM_SPEC], out_specs=HBM_SPEC,
        out_shape=jax.ShapeDtypeStruct((2, r, lanes), half.dtype),
        scratch_shapes=[pltpu.SemaphoreType.DMA((nq,)), pltpu.SemaphoreType.DMA((nq,)), pltpu.SemaphoreType.DMA],
    )(half)


def _gather_partials(part):
    r, lanes = part.shape

    def body(p_ref, out_ref, send_sems, recv_sems, local_sem):
        x, y, c, _ = _position()
        keep = pltpu.make_async_copy(p_ref, out_ref.at[4 * x + 2 * y + c], local_sem)
        keep.start()
        flips = [(fx, fy, fc) for fx in (0, 1) for fy in (0, 1) for fc in (0, 1)][1:]
        sends = []
        for k, (fx, fy, fc) in enumerate(flips):
            cp = pltpu.make_async_remote_copy(src_ref=p_ref, dst_ref=out_ref.at[4 * x + 2 * y + c], send_sem=send_sems.at[k],
                                              recv_sem=recv_sems.at[k], device_id=(x ^ fx, y ^ fy, c ^ fc), device_id_type=MESH)
            cp.start()
            sends.append(cp)
        for k, (fx, fy, fc) in enumerate(flips):
            src = out_ref.at[4 * (x ^ fx) + 2 * (y ^ fy) + (c ^ fc)]
            pltpu.make_async_remote_copy(src_ref=src, dst_ref=src, send_sem=send_sems.at[k], recv_sem=recv_sems.at[k],
                                         device_id=(x ^ fx, y ^ fy, c ^ fc), device_id_type=MESH).wait_recv()
        for cp in sends:
            cp.wait_send()
        keep.wait()

    return pl.pallas_call(
        body, name="small_gather_partials", in_specs=[HBM_SPEC], out_specs=HBM_SPEC,
        out_shape=jax.ShapeDtypeStruct((8, r, lanes), part.dtype),
        scratch_shapes=[pltpu.SemaphoreType.DMA((7,)), pltpu.SemaphoreType.DMA((7,)), pltpu.SemaphoreType.DMA],
    )(part)


def _sum_leading(stack, name, also_bf16=False):
    k, r, lanes = stack.shape
    tm = _pick(r, 256, 16)
    outs = [jax.ShapeDtypeStruct((r, lanes), F32)] + ([jax.ShapeDtypeStruct((r, lanes), BF16)] if also_bf16 else [])

    def body(s_ref, *o_refs):
        acc = s_ref[0].astype(F32)
        for i in range(1, k):
            acc = acc + s_ref[i].astype(F32)
        for o in o_refs:
            o[...] = acc.astype(o.dtype)

    spec = pl.BlockSpec((tm, lanes), lambda i: (i, 0))
    return pl.pallas_call(
        body, name=name, grid=(r // tm,), in_specs=[pl.BlockSpec((k, tm, lanes), lambda i: (0, i, 0))],
        out_specs=[spec] * len(outs), out_shape=outs,
        compiler_params=pltpu.CompilerParams(dimension_semantics=("parallel",)),
    )(stack)


def _add_pair(a, b, name):
    ns, r, lanes = a.shape
    tm = _pick(r, 256, 16)
    spec = pl.BlockSpec((1, tm, lanes), lambda s, i: (s, i, 0))

    def body(a_ref, b_ref, f_ref, h_ref):
        acc = a_ref[...] + b_ref[...]
        f_ref[...] = acc
        h_ref[...] = acc.astype(BF16)

    return pl.pallas_call(
        body, name=name, grid=(ns, r // tm), in_specs=[spec, spec], out_specs=[spec, spec],
        out_shape=[jax.ShapeDtypeStruct(a.shape, F32), jax.ShapeDtypeStruct(a.shape, BF16)],
        compiler_params=pltpu.CompilerParams(dimension_semantics=("parallel", "parallel")),
    )(a, b)


def _add_own(own, got, name):
    r, lanes = own.shape
    tm = _pick(r, 256, 16)

    def body(o_ref, g_ref, out_ref):
        acc = o_ref[...]
        for j in range(g_ref.shape[0]):
            acc = acc + g_ref[j].astype(F32)
        out_ref[...] = acc

    spec = pl.BlockSpec((tm, lanes), lambda i: (i, 0))
    return pl.pallas_call(
        body, name=name, grid=(r // tm,), in_specs=[spec, pl.BlockSpec((got.shape[0], tm, lanes), lambda i: (0, i, 0))],
        out_specs=spec, out_shape=jax.ShapeDtypeStruct((r, lanes), F32),
        compiler_params=pltpu.CompilerParams(dimension_semantics=("parallel",)),
    )(own, got)


def _local_shape(shape, axis):
    return tuple(d // N_SHARD if a == axis else d for a, d in enumerate(shape))


def _big_sizes():
    return [DEPTH * int(np.prod(_local_shape(shape, axis))) for _, shape, axis in BIG]


FLAT_ROW_TILE = 256
ELEMENTWISE_BIG = ("conv_w",)


def _flat_rows(n):
    rows = -(-n // (2 * LANES))
    return -(-rows // FLAT_ROW_TILE) * FLAT_ROW_TILE


def _three_bf16(w):
    hi = w.astype(BF16)
    r1 = w - hi.astype(F32)
    mid = r1.astype(BF16)
    lo = (r1 - mid.astype(F32)).astype(BF16)
    return jnp.stack([hi, mid, lo], axis=-1)


def _pack_local_weights(wl):
    parts = [(_three_bf16(wl[name]) if name in ELEMENTWISE_BIG else wl[name].astype(BF16)).reshape(-1) for name, _, _ in BIG]
    flat = jnp.concatenate(parts)
    r = _flat_rows(flat.shape[0])
    flat = jnp.pad(flat, (0, 2 * r * LANES - flat.shape[0]))
    return flat.reshape(2, r, LANES)


def _unpack_local(flat):
    flat = flat.reshape(-1)
    out, off = {}, 0
    for (name, shape, axis), size in zip(BIG, _big_sizes(), strict=True):
        out[name] = flat[off:off + size].reshape((DEPTH,) + _local_shape(shape, axis))
        off += size
    return out


def _unpack_gathered(gathered):
    flat = gathered.reshape(N_SHARD, -1)
    out, off = {}, 0
    for (name, shape, axis), size in zip(BIG, _big_sizes(), strict=True):
        local = (N_SHARD, DEPTH) + _local_shape(shape, axis)
        if name in ELEMENTWISE_BIG:
            parts = flat[:, off:off + 3 * size].reshape(local + (3,)).astype(F32)
            seg = (parts[..., 0] + parts[..., 1]) + parts[..., 2]
            off += 3 * size
        else:
            seg = flat[:, off:off + size].reshape(local)
            off += size
        if seg.shape[-1] % 128 and axis == len(shape) - 1 and len(shape) == 2:
            out[name] = _join_col_shards(seg)
        else:
            out[name] = jnp.moveaxis(seg, 0, 1 + axis).reshape((DEPTH,) + shape)
    return out


def _join_col_shards(shards):
    ns, depth, rows, c = shards.shape
    tm = _pick(rows, 256, 16)

    def body(i_ref, o_ref):
        for s in range(ns):
            o_ref[0, :, c * s:c * (s + 1)] = i_ref[s, 0]

    return pl.pallas_call(
        body, name="join_col_shards", grid=(depth, rows // tm),
        in_specs=[pl.BlockSpec((ns, 1, tm, c), lambda l, i: (0, l, i, 0))],
        out_specs=pl.BlockSpec((1, tm, ns * c), lambda l, i: (l, i, 0)),
        out_shape=jax.ShapeDtypeStruct((depth, rows, ns * c), shards.dtype),
        compiler_params=pltpu.CompilerParams(dimension_semantics=("parallel", "parallel")),
    )(shards)


def _split_col_shards(full, name):
    rows, width = full.shape
    c = width // N_SHARD
    tm = _pick(rows, 256, 16)

    def body(i_ref, o_ref):
        for s in range(N_SHARD):
            o_ref[s] = i_ref[:, c * s:c * (s + 1)]

    return pl.pallas_call(
        body, name=name, grid=(rows // tm,), in_specs=[pl.BlockSpec((tm, width), lambda i: (i, 0))],
        out_specs=pl.BlockSpec((N_SHARD, tm, c), lambda i: (0, i, 0)),
        out_shape=jax.ShapeDtypeStruct((N_SHARD, rows, c), full.dtype),
        compiler_params=pltpu.CompilerParams(dimension_semantics=("parallel",)),
    )(full)


def _pack_full_grads(big_grads):
    rows = []
    for name, shape, axis in BIG:
        for li, gfull in enumerate(big_grads[name]):
            if shape[axis] // N_SHARD % 128 and axis == len(shape) - 1:
                rows.append(_split_col_shards(gfull, f"split_col_shards_{name}_l{li}").reshape(N_SHARD, -1))
            else:
                split = gfull.reshape(shape[:axis] + (N_SHARD, shape[axis] // N_SHARD) + shape[axis + 1:])
                rows.append(jnp.moveaxis(split, axis, 0).reshape(N_SHARD, -1))
    flat = jnp.concatenate(rows, axis=1)
    r = _flat_rows(flat.shape[1])
    flat = jnp.pad(flat, ((0, 0), (0, 2 * r * LANES - flat.shape[1])))
    return flat.reshape(N_SHARD, 2, r, LANES)


def _pack_small(grads):
    flat = jnp.concatenate([grads[name].reshape(-1) for name in SMALL])
    r = -(-flat.shape[0] // (8 * LANES)) * 8
    return jnp.pad(flat, (0, r * LANES - flat.shape[0])).reshape(r, LANES)


def _unpack_small(flat, like):
    flat = flat.reshape(-1)
    out, off = {}, 0
    for name in SMALL:
        size = int(np.prod(like[name].shape))
        out[name] = flat[off:off + size].reshape(like[name].shape)
        off += size
    return out


def _adamw(w, g, m, v, name):
    shape = w.shape
    cols = shape[-1]
    rows = int(np.prod(shape[:-1])) if len(shape) > 1 else 1
    w2, g2, m2, v2 = (a.reshape(rows, cols) for a in (w, g, m, v))

    def fn(wv, gv, mv, vv):
        mn = ADAM_B1 * mv + (1.0 - ADAM_B1) * gv
        vn = ADAM_B2 * vv + (1.0 - ADAM_B2) * jnp.square(gv)
        m_hat = mn / (1.0 - ADAM_B1 ** ADAM_STEP)
        v_hat = vn / (1.0 - ADAM_B2 ** ADAM_STEP)
        delta = -ADAM_LR * (m_hat / (jnp.sqrt(v_hat) + ADAM_EPS) + ADAM_WD * wv)
        return [delta, mn, vn], []

    tm = 256 if rows % 8 == 0 and rows > 256 else rows
    res = _rw(fn, [(a, None, _c0) for a in (w2, g2, m2, v2)], [(cols, None, F32, _c0)] * 3, tm=tm, name=name)
    return [r.reshape(shape) for r in res]


def _step(x, p, target, weights, moments_m, moments_v):
    gathered = _allgather_weights(_pack_local_weights(weights))
    wt = dict(_unpack_gathered(gathered))
    for name in SMALL:
        wt[name] = weights[name]
    loss, dx, small_grads, big_grads = _local_step(x[0], p[:, 0], target[0], wt)
    loss = lax.psum(loss, ("x", "y", "c"))
    own, got = _sibling_exchange(_pack_full_grads(big_grads))
    chip_f32, chip_bf16 = _add_pair(own, got, "grad_add_sibling")
    own_rows, other = _chip_exchange(chip_f32, chip_bf16)
    reduced = _unpack_local(_sibling_gather(_add_own(own_rows, other, "grad_add_chips")))
    small = _unpack_small(_sum_leading(_gather_partials(_pack_small(small_grads)), "small_sum")[0], {k: weights[k] for k in SMALL})
    reduced.update(small)
    outs_g, outs_d, outs_m, outs_v = [], [], [], []
    for name in WEIGHTS:
        d, mn, vn = _adamw(weights[name], reduced[name], moments_m[name], moments_v[name], f"adamw_{name}")
        outs_g.append(reduced[name])
        outs_d.append(d)
        outs_m.append(mn)
        outs_v.append(vn)
    return (loss, dx[None], *outs_g, *outs_d, *outs_m, *outs_v)


def kernel(x, p, rel_bias, norm_mix, w_in, ssm_lambda_re, ssm_lambda_im, ssm_b_re, ssm_b_im, ssm_c_re, ssm_c_im, ssm_d, ssm_log_dt, ssm_w_glu, conv_w, attn_sinks, w_branch, w_out, norm_ffn, w_ffn_in, w_ffn_out, norm_ple, w_ple_gate, w_ple_proj, norm_final, loss_target, m_rel_bias, m_norm_mix, m_w_in, m_ssm_lambda_re, m_ssm_lambda_im, m_ssm_b_re, m_ssm_b_im, m_ssm_c_re, m_ssm_c_im, m_ssm_d, m_ssm_log_dt, m_ssm_w_glu, m_conv_w, m_attn_sinks, m_w_branch, m_w_out, m_norm_ffn, m_w_ffn_in, m_w_ffn_out, m_norm_ple, m_w_ple_gate, m_w_ple_proj, m_norm_final, v_rel_bias, v_norm_mix, v_w_in, v_ssm_lambda_re, v_ssm_lambda_im, v_ssm_b_re, v_ssm_b_im, v_ssm_c_re, v_ssm_c_im, v_ssm_d, v_ssm_log_dt, v_ssm_w_glu, v_conv_w, v_attn_sinks, v_w_branch, v_w_out, v_norm_ffn, v_w_ffn_in, v_w_ffn_out, v_norm_ple, v_w_ple_gate, v_w_ple_proj, v_norm_final):
    weights = dict(rel_bias=rel_bias, norm_mix=norm_mix, w_in=w_in, ssm_lambda_re=ssm_lambda_re, ssm_lambda_im=ssm_lambda_im,
                   ssm_b_re=ssm_b_re, ssm_b_im=ssm_b_im, ssm_c_re=ssm_c_re, ssm_c_im=ssm_c_im, ssm_d=ssm_d, ssm_log_dt=ssm_log_dt,
                   ssm_w_glu=ssm_w_glu, conv_w=conv_w, attn_sinks=attn_sinks, w_branch=w_branch, w_out=w_out, norm_ffn=norm_ffn,
                   w_ffn_in=w_ffn_in, w_ffn_out=w_ffn_out, norm_ple=norm_ple, w_ple_gate=w_ple_gate, w_ple_proj=w_ple_proj,
                   norm_final=norm_final)
    moments_m = dict(rel_bias=m_rel_bias, norm_mix=m_norm_mix, w_in=m_w_in, ssm_lambda_re=m_ssm_lambda_re, ssm_lambda_im=m_ssm_lambda_im,
                     ssm_b_re=m_ssm_b_re, ssm_b_im=m_ssm_b_im, ssm_c_re=m_ssm_c_re, ssm_c_im=m_ssm_c_im, ssm_d=m_ssm_d,
                     ssm_log_dt=m_ssm_log_dt, ssm_w_glu=m_ssm_w_glu, conv_w=m_conv_w, attn_sinks=m_attn_sinks, w_branch=m_w_branch,
                     w_out=m_w_out, norm_ffn=m_norm_ffn, w_ffn_in=m_w_ffn_in, w_ffn_out=m_w_ffn_out, norm_ple=m_norm_ple,
                     w_ple_gate=m_w_ple_gate, w_ple_proj=m_w_ple_proj, norm_final=m_norm_final)
    moments_v = dict(rel_bias=v_rel_bias, norm_mix=v_norm_mix, w_in=v_w_in, ssm_lambda_re=v_ssm_lambda_re, ssm_lambda_im=v_ssm_lambda_im,
                     ssm_b_re=v_ssm_b_re, ssm_b_im=v_ssm_b_im, ssm_c_re=v_ssm_c_re, ssm_c_im=v_ssm_c_im, ssm_d=v_ssm_d,
                     ssm_log_dt=v_ssm_log_dt, ssm_w_glu=v_ssm_w_glu, conv_w=v_conv_w, attn_sinks=v_attn_sinks, w_branch=v_w_branch,
                     w_out=v_w_out, norm_ffn=v_norm_ffn, w_ffn_in=v_w_ffn_in, w_ffn_out=v_w_ffn_out, norm_ple=v_norm_ple,
                     w_ple_gate=v_w_ple_gate, w_ple_proj=v_w_ple_proj, norm_final=v_norm_final)
    return _step(x, p, loss_target, weights, moments_m, moments_v)
```

```python
import functools
import math

import numpy as np

import jax
import jax.numpy as jnp
from jax import lax
from jax.experimental import pallas as pl
from jax.experimental.pallas import tpu as pltpu

F32, BF16 = jnp.float32, jnp.bfloat16
MESH = pl.DeviceIdType.MESH

D_MODEL = 1024
DEPTH = 4
PLE_DIM = 256
BRANCH = 512
N_GROUPS = 32
GROUP_CH = 16
N_STATE = 64
SSM_STATES = N_GROUPS * N_STATE
SSM_BLOCKS = 4
HEAD_DIM = 64
N_Q = 8
N_KV = 2
GQA = N_Q // N_KV
WINDOW = 128
ATTN_SCALE = 1.0 / math.sqrt(HEAD_DIM)
REL_BUCKETS = 32
REL_MAX_DIST = 128
FFN_HIDDEN = 2816
FFN_COLS = 1408
FFN_NCOL = FFN_HIDDEN // FFN_COLS
IN_WIDTH = 5888
RMS_EPS = 1e-6
NEG = -1e30

ADAM_LR, ADAM_B1, ADAM_B2, ADAM_EPS, ADAM_WD, ADAM_STEP = 0.001, 0.9, 0.999, 1e-08, 0.01, 10

N_SHARD = 4
LANES = 1024
COPY_CHUNKS = 4

OFF_U, OFF_CB, OFF_CC, OFF_CX, OFF_Q, OFF_K, OFF_V, OFF_G = 0, 512, 1024, 1536, 2048, 2560, 2688, 2816

BIG = (
    ("w_in", (D_MODEL, IN_WIDTH), 1),
    ("ssm_w_glu", (BRANCH, BRANCH), 0),
    ("conv_w", (3, BRANCH), 1),
    ("w_branch", (3, BRANCH, D_MODEL), 2),
    ("w_out", (D_MODEL, D_MODEL), 0),
    ("w_ffn_in", (D_MODEL, 2 * FFN_HIDDEN), 1),
    ("w_ffn_out", (FFN_HIDDEN, D_MODEL), 0),
    ("w_ple_gate", (D_MODEL, D_MODEL), 0),
    ("w_ple_proj", (PLE_DIM, D_MODEL), 1),
)
SMALL = ("rel_bias", "norm_mix", "ssm_lambda_re", "ssm_lambda_im", "ssm_b_re", "ssm_b_im", "ssm_c_re", "ssm_c_im",
         "ssm_d", "ssm_log_dt", "attn_sinks", "norm_ffn", "norm_ple", "norm_final")
WEIGHTS = ("rel_bias", "norm_mix", "w_in", "ssm_lambda_re", "ssm_lambda_im", "ssm_b_re", "ssm_b_im", "ssm_c_re",
           "ssm_c_im", "ssm_d", "ssm_log_dt", "ssm_w_glu", "conv_w", "attn_sinks", "w_branch", "w_out", "norm_ffn",
           "w_ffn_in", "w_ffn_out", "norm_ple", "w_ple_gate", "w_ple_proj", "norm_final")


def _c0(j):
    return 0


def _pick(n, cap, unit=128):
    if n <= cap:
        return n
    best = None
    for t in range(unit, cap + 1, unit):
        if n % t == 0:
            best = t
    assert best is not None, (n, cap, unit)
    return best


_DIMS = {"nn": ((1,), (0,)), "nt": ((1,), (1,)), "tn": ((0,), (0,))}


def _mm(a, b, mode, *, name, out_dtype=F32, add=None, tm=1024, tn=1024, tk=1024, b_k0=0):
    if mode == "nn":
        (m, k), (k2, n) = a.shape, b.shape
    elif mode == "nt":
        (m, k), (n, k2) = a.shape, b.shape
    else:
        (k, m), (k2, n) = a.shape, b.shape
    assert k == k2 or (mode == "nt" and b_k0 + k <= k2), (a.shape, b.shape, mode)
    tm, tn, tk = _pick(m, tm, 128 if mode == "tn" else 8), _pick(n, tn), _pick(k, tk, 128 if mode != "tn" else 8)
    nk = k // tk
    assert b_k0 % tk == 0
    kb0 = b_k0 // tk
    a_spec = pl.BlockSpec((tk, tm), lambda i, j, kk: (kk, i)) if mode == "tn" else pl.BlockSpec((tm, tk), lambda i, j, kk: (i, kk))
    b_spec = pl.BlockSpec((tn, tk), lambda i, j, kk: (j, kb0 + kk)) if mode == "nt" else pl.BlockSpec((tk, tn), lambda i, j, kk: (kk, j))
    o_spec = pl.BlockSpec((tm, tn), lambda i, j, kk: (i, j))
    dims = (_DIMS[mode], ((), ()))
    has_add = add is not None

    def body(*refs):
        a_ref, b_ref = refs[0], refs[1]
        add_ref = refs[2] if has_add else None
        o_ref, acc_ref = refs[-2], refs[-1]
        part = lax.dot_general(a_ref[...].astype(BF16), b_ref[...].astype(BF16), dims, preferred_element_type=F32)

        def finish(acc):
            if has_add:
                acc = acc + add_ref[...]
            o_ref[...] = acc.astype(o_ref.dtype)

        if nk == 1:
            finish(part)
        else:
            kk = pl.program_id(2)

            @pl.when(kk == 0)
            def _():
                acc_ref[...] = part

            @pl.when(kk > 0)
            def _():
                acc_ref[...] += part

            @pl.when(kk == nk - 1)
            def _():
                finish(acc_ref[...])

    operands = [a, b] + ([add] if has_add else [])
    in_specs = [a_spec, b_spec] + ([o_spec] if has_add else [])
    return pl.pallas_call(
        body, name=name, grid=(m // tm, n // tn, nk), in_specs=in_specs, out_specs=o_spec,
        out_shape=jax.ShapeDtypeStruct((m, n), out_dtype),
        scratch_shapes=[pltpu.VMEM((tm, tn) if nk > 1 else (8, 128), F32)],
        compiler_params=pltpu.CompilerParams(dimension_semantics=("parallel", "parallel", "arbitrary")),
    )(*operands)


def _rw(fn, ins, outs, *, name, params=(), reds=(), tm=256, ncol=1, with_j=False):
    t = ins[0][0].shape[0]
    tm = _pick(t, tm, 8)
    nrow = t // tm
    n_in, n_p, n_out = len(ins), len(params), len(outs)

    in_specs = [pl.BlockSpec((tm, bw or arr.shape[1]), lambda j, i, cf=cf: (i, cf(j))) for arr, bw, cf in ins]
    in_specs += [pl.BlockSpec(p.shape, lambda j, i: (0, 0)) for p in params]
    out_specs = [pl.BlockSpec((tm, bw or w), lambda j, i, cf=cf: (i, cf(j))) for w, bw, _, cf in outs]
    out_specs += [pl.BlockSpec((shp[0], bw or shp[1]), lambda j, i, cf=cf: (0, cf(j))) for shp, bw, cf in reds]
    out_shape = [jax.ShapeDtypeStruct((t, w), dt) for w, _, dt, _ in outs]
    out_shape += [jax.ShapeDtypeStruct(shp, F32) for shp, _, _ in reds]

    def body(*refs):
        in_refs, p_refs = refs[:n_in], refs[n_in:n_in + n_p]
        o_refs, r_refs = refs[n_in + n_p:n_in + n_p + n_out], refs[n_in + n_p + n_out:]
        args = [r[...] for r in in_refs] + [r[...] for r in p_refs]
        if with_j:
            args = [pl.program_id(0)] + args
        o_vals, r_vals = fn(*args)
        for r, v in zip(o_refs, o_vals, strict=True):
            r[...] = v.astype(r.dtype)
        if r_refs:
            i = pl.program_id(1)
            for r, v in zip(r_refs, r_vals, strict=True):
                @pl.when(i == 0)
                def _(r=r, v=v):
                    r[...] = v

                @pl.when(i > 0)
                def _(r=r, v=v):
                    r[...] += v

    res = pl.pallas_call(
        body, name=name, grid=(ncol, nrow), in_specs=in_specs, out_specs=out_specs, out_shape=out_shape,
        compiler_params=pltpu.CompilerParams(dimension_semantics=("parallel", "arbitrary" if reds else "parallel")),
    )(*[a for a, _, _ in ins], *params)
    return res


def _rms(x, g):
    return x * lax.rsqrt(jnp.mean(x * x, axis=-1, keepdims=True) + RMS_EPS) * g


def _rms_fwd(x, g, name):
    return _rw(lambda xv, gv: ([_rms(xv, gv)], []), [(x, None, _c0)], [(D_MODEL, None, BF16, _c0)], params=[g], name=name)[0]


def _rms_bwd(x, dh, dres, g, name):
    def fn(xv, dhv, drv, gv):
        _, vjp = jax.vjp(_rms, xv, gv)
        dx, dg = vjp(dhv)
        return [drv + dx], [dg]

    return _rw(fn, [(x, None, _c0), (dh, None, _c0), (dres, None, _c0)], [(D_MODEL, None, F32, _c0)], params=[g],
               reds=[((1, D_MODEL), None, _c0)], name=name)


def _bd_apply(acts, mats, combos, mode, *, name, tm=512):
    t = acts[0].shape[0]
    tm = _pick(t, tm, 8)
    nb, r, c = mats[0].shape
    win, wout = (r, c) if mode == "nn" else (c, r)
    dims = (_DIMS[mode], ((), ()))
    n_a, n_m = len(acts), len(mats)

    def body(*refs):
        a_vals = [ar[...].astype(BF16) for ar in refs[:n_a]]
        m_refs, o_refs = refs[n_a:n_a + n_m], refs[n_a + n_m:]
        for o_ref, terms in zip(o_refs, combos, strict=True):
            for j in range(nb):
                acc = None
                for ai, mi in terms:
                    part = lax.dot_general(a_vals[ai][:, j * win:(j + 1) * win], m_refs[mi][j], dims, preferred_element_type=F32)
                    acc = part if acc is None else acc + part
                o_ref[:, j * wout:(j + 1) * wout] = acc

    return pl.pallas_call(
        body, name=name, grid=(t // tm,),
        in_specs=[pl.BlockSpec((tm, nb * win), lambda i: (i, 0))] * n_a + [pl.BlockSpec(m.shape, lambda i: (0, 0, 0)) for m in mats],
        out_specs=[pl.BlockSpec((tm, nb * wout), lambda i: (i, 0))] * len(combos),
        out_shape=[jax.ShapeDtypeStruct((t, nb * wout), F32)] * len(combos),
        compiler_params=pltpu.CompilerParams(dimension_semantics=("parallel",)),
    )(*acts, *mats)


def _bd_grads(arrs, widths, pairs, *, name, tk=512):
    t = arrs[0].shape[0]
    tk = _pick(t, tk, 8)
    n_a = len(arrs)
    dims = (_DIMS["tn"], ((), ()))

    def body(*refs):
        vals = [ar[...].astype(BF16) for ar in refs[:n_a]]
        o_refs = refs[n_a:]
        @pl.when(pl.program_id(0) == 0)
        def _():
            for o_ref in o_refs:
                o_ref[...] = jnp.zeros_like(o_ref)

        for o_ref, (ai, bi) in zip(o_refs, pairs, strict=True):
            wa, wb = widths[ai], widths[bi]
            for j in range(SSM_BLOCKS):
                o_ref[j] += lax.dot_general(vals[ai][:, j * wa:(j + 1) * wa], vals[bi][:, j * wb:(j + 1) * wb], dims,
                                            preferred_element_type=F32)

    return pl.pallas_call(
        body, name=name, grid=(t // tk,),
        in_specs=[pl.BlockSpec((tk, SSM_BLOCKS * w), lambda k: (k, 0)) for w in widths],
        out_specs=[pl.BlockSpec((SSM_BLOCKS, widths[ai], widths[bi]), lambda k: (0, 0, 0)) for ai, bi in pairs],
        out_shape=[jax.ShapeDtypeStruct((SSM_BLOCKS, widths[ai], widths[bi]), F32) for ai, bi in pairs],
        compiler_params=pltpu.CompilerParams(dimension_semantics=("arbitrary",)),
    )(*arrs)


SCAN_LW = 256
SCAN_ROWS = 512
_DOUBLING = ((1, 0), (2, 1), (4, 2))


def _scan(xr, xi, pr, pi, dr, di, *, reverse, name, hr=None, hi=None):
    t, s = xr.shape
    lc = _pick(t, SCAN_ROWS, 8)
    nt, ngroups = t // lc, lc // 8
    with_da = hr is not None

    def tmap(l, tt):
        return ((nt - 1 - tt) if reverse else tt, l)

    x_spec = pl.BlockSpec((lc, SCAN_LW), tmap)
    tab_spec = pl.BlockSpec((8, SCAN_LW), lambda l, tt: (0, l))

    def body(*refs):
        xr_ref, xi_ref, pr_ref, pi_ref, dr_ref, di_ref = refs[:6]
        if with_da:
            hr_ref, hi_ref, or_ref, oi_ref, ar_ref, ai_ref, cr_ref, ci_ref = refs[6:]
        else:
            or_ref, oi_ref, cr_ref, ci_ref = refs[6:]
        tt = pl.program_id(1)

        @pl.when(tt == 0)
        def _():
            cr_ref[...] = jnp.zeros_like(cr_ref)
            ci_ref[...] = jnp.zeros_like(ci_ref)
            if with_da:
                ar_ref[...] = jnp.zeros_like(ar_ref)
                ai_ref[...] = jnp.zeros_like(ai_ref)

        sub = lax.broadcasted_iota(jnp.int32, (8, SCAN_LW), 0)
        pw_r, pw_i = pr_ref[...], pi_ref[...]

        def step(g, carry):
            g = (ngroups - 1 - g) if reverse else g
            r0 = pl.multiple_of(g * 8, 8)
            vr, vi = xr_ref[pl.ds(r0, 8), :], xi_ref[pl.ds(r0, 8), :]
            for shift, row in _DOUBLING:
                a_r, a_i = dr_ref[row:row + 1, :], di_ref[row:row + 1, :]
                if reverse:
                    keep = sub < 8 - shift
                    sr, si = pltpu.roll(vr, 8 - shift, 0), pltpu.roll(vi, 8 - shift, 0)
                else:
                    keep = sub >= shift
                    sr, si = pltpu.roll(vr, shift, 0), pltpu.roll(vi, shift, 0)
                sr, si = jnp.where(keep, sr, 0.0), jnp.where(keep, si, 0.0)
                vr, vi = vr + a_r * sr - a_i * si, vi + a_r * si + a_i * sr
            if with_da:
                cr, ci, acc_r, acc_i = carry
            else:
                cr, ci = carry
            vr, vi = vr + pw_r * cr - pw_i * ci, vi + pw_r * ci + pw_i * cr
            or_ref[pl.ds(r0, 8), :] = vr
            oi_ref[pl.ds(r0, 8), :] = vi
            if with_da:
                nr = jnp.where(sub < 7, pltpu.roll(vr, 7, 0), cr)
                ni = jnp.where(sub < 7, pltpu.roll(vi, 7, 0), ci)
                h_r, h_i = hr_ref[pl.ds(r0, 8), :], hi_ref[pl.ds(r0, 8), :]
                acc_r = acc_r + h_r * nr + h_i * ni
                acc_i = acc_i + h_r * ni - h_i * nr
            edge = 0 if reverse else 7
            cr = jnp.broadcast_to(vr[edge:edge + 1, :], vr.shape)
            ci = jnp.broadcast_to(vi[edge:edge + 1, :], vi.shape)
            return (cr, ci, acc_r, acc_i) if with_da else (cr, ci)

        zero = jnp.zeros((8, SCAN_LW), F32)
        init = (cr_ref[...], ci_ref[...]) + ((zero, zero) if with_da else ())
        fin = lax.fori_loop(0, ngroups, step, init, unroll=2)
        cr_ref[...] = fin[0]
        ci_ref[...] = fin[1]
        if with_da:
            ar_ref[...] += fin[2]
            ai_ref[...] += fin[3]

    n_x = 4 if with_da else 2
    out_specs = [x_spec, x_spec] + ([tab_spec, tab_spec] if with_da else [])
    out_shape = [jax.ShapeDtypeStruct((t, s), F32)] * 2 + ([jax.ShapeDtypeStruct((8, s), F32)] * 2 if with_da else [])
    operands = [xr, xi, pr, pi, dr, di] + ([hr, hi] if with_da else [])
    return pl.pallas_call(
        body, name=name, grid=(s // SCAN_LW, nt),
        in_specs=[x_spec, x_spec] + [tab_spec] * 4 + [x_spec] * (n_x - 2),
        out_specs=out_specs, out_shape=out_shape,
        scratch_shapes=[pltpu.VMEM((8, SCAN_LW), F32), pltpu.VMEM((8, SCAN_LW), F32)],
        compiler_params=pltpu.CompilerParams(dimension_semantics=("parallel", "arbitrary")),
    )(*operands)


CONV_TM = 256


def _conv_specs(t, tm):
    nrow = t // tm
    hb = tm // 8

    def col(cidx):
        return pl.BlockSpec((tm, BRANCH), lambda i: (i, cidx))

    def prev(cidx):
        return pl.BlockSpec((8, BRANCH), lambda i: (jnp.maximum(i * hb - 1, 0), cidx))

    def nxt(cidx):
        return pl.BlockSpec((8, BRANCH), lambda i: (jnp.minimum((i + 1) * hb, nrow * hb - 1), cidx))

    return nrow, col, prev, nxt


def _conv_taps(cc, cx, cc_prev, cx_prev, first):
    tm = cc.shape[0]
    v = cc * cx
    halo = cc_prev * cx_prev * jnp.where(first, 0.0, 1.0)
    ext = jnp.concatenate([halo, v], axis=0)
    return v, pltpu.roll(ext, 1, 0)[8:8 + tm], pltpu.roll(ext, 2, 0)[8:8 + tm]


def _conv_fwd(z, conv_w, name):
    t = z.shape[0]
    tm = _pick(t, CONV_TM, 8)
    nrow, col, prev, _ = _conv_specs(t, tm)

    def body(cb_ref, cc_ref, cx_ref, ccp_ref, cxp_ref, w_ref, o_ref):
        first = pl.program_id(0) == 0
        v, v1, v2 = _conv_taps(cc_ref[...], cx_ref[...], ccp_ref[...], cxp_ref[...], first)
        y = w_ref[0:1, :] * v2 + w_ref[1:2, :] * v1 + w_ref[2:3, :] * v
        o_ref[...] = (cb_ref[...] * y).astype(o_ref.dtype)

    return pl.pallas_call(
        body, name=name, grid=(nrow,),
        in_specs=[col(1), col(2), col(3), prev(2), prev(3), pl.BlockSpec((3, BRANCH), lambda i: (0, 0))],
        out_specs=pl.BlockSpec((tm, BRANCH), lambda i: (i, 0)), out_shape=jax.ShapeDtypeStruct((t, BRANCH), BF16),
        compiler_params=pltpu.CompilerParams(dimension_semantics=("parallel",)),
    )(z, z, z, z, z, conv_w)


def _conv_bwd(dyc, z, conv_w, name):
    t = z.shape[0]
    tm = _pick(t, CONV_TM, 8)
    nrow, col, prev, nxt = _conv_specs(t, tm)
    d_cur = pl.BlockSpec((tm, BRANCH), lambda i: (i, 0))
    d_nxt = pl.BlockSpec((8, BRANCH), lambda i: (jnp.minimum((i + 1) * (tm // 8), nrow * (tm // 8) - 1), 0))

    def body(dy_ref, dyn_ref, cb_ref, cbn_ref, cc_ref, cx_ref, ccp_ref, cxp_ref, w_ref, dcb_ref, dcc_ref, dcx_ref, dw_ref):
        i = pl.program_id(0)
        cc, cx, cb = cc_ref[...], cx_ref[...], cb_ref[...]
        v, v1, v2 = _conv_taps(cc, cx, ccp_ref[...], cxp_ref[...], i == 0)
        w0, w1, w2 = w_ref[0:1, :], w_ref[1:2, :], w_ref[2:3, :]
        y = w0 * v2 + w1 * v1 + w2 * v
        dyc_v = dy_ref[...]
        dcb_ref[...] = (dyc_v * y).astype(dcb_ref.dtype)
        dy = dyc_v * cb
        halo = dyn_ref[...] * cbn_ref[...] * jnp.where(i == nrow - 1, 0.0, 1.0)
        ext = jnp.concatenate([dy, halo], axis=0)
        dy1 = pltpu.roll(ext, tm + 8 - 1, 0)[0:tm]
        dy2 = pltpu.roll(ext, tm + 8 - 2, 0)[0:tm]
        dv = w2 * dy + w1 * dy1 + w0 * dy2
        dcc_ref[...] = (dv * cx).astype(dcc_ref.dtype)
        dcx_ref[...] = (dv * cc).astype(dcx_ref.dtype)
        dw = jnp.concatenate([jnp.sum(dy * v2, axis=0, keepdims=True), jnp.sum(dy * v1, axis=0, keepdims=True),
                              jnp.sum(dy * v, axis=0, keepdims=True), jnp.zeros((5, BRANCH), F32)], axis=0)

        @pl.when(i == 0)
        def _():
            dw_ref[...] = dw

        @pl.when(i > 0)
        def _():
            dw_ref[...] += dw

    o_spec = pl.BlockSpec((tm, BRANCH), lambda i: (i, 0))
    return pl.pallas_call(
        body, name=name, grid=(nrow,),
        in_specs=[d_cur, d_nxt, col(1), nxt(1), col(2), col(3), prev(2), prev(3), pl.BlockSpec((3, BRANCH), lambda i: (0, 0))],
        out_specs=[o_spec, o_spec, o_spec, pl.BlockSpec((8, BRANCH), lambda i: (0, 0))],
        out_shape=[jax.ShapeDtypeStruct((t, BRANCH), BF16)] * 3 + [jax.ShapeDtypeStruct((8, BRANCH), F32)],
        compiler_params=pltpu.CompilerParams(dimension_semantics=("arbitrary",)),
    )(dyc, dyc, z, z, z, z, z, z, conv_w)


def _attn_specs():
    q_spec = pl.BlockSpec((N_Q, WINDOW, HEAD_DIM), lambda n: (0, n, 0))
    kv_cur = pl.BlockSpec((N_KV, WINDOW, HEAD_DIM), lambda n: (0, n, 0))
    kv_prev = pl.BlockSpec((N_KV, WINDOW, HEAD_DIM), lambda n: (0, jnp.maximum(n - 1, 0), 0))
    bias_spec = pl.BlockSpec((N_Q, WINDOW, 2 * WINDOW), lambda n: (0, 0, 0))
    sink_spec = pl.BlockSpec((N_Q, 1), lambda n: (0, 0))
    return q_spec, kv_cur, kv_prev, bias_spec, sink_spec


GROUP_ROWS = GQA * WINDOW


def _attn_valid(n):
    qi = lax.broadcasted_iota(jnp.int32, (GROUP_ROWS, 2 * WINDOW), 0) & (WINDOW - 1)
    kj = lax.broadcasted_iota(jnp.int32, (GROUP_ROWS, 2 * WINDOW), 1)
    dist = qi + WINDOW - kj
    first_key = jnp.where(n > 0, 0, WINDOW)
    return (dist >= 0) & (dist < WINDOW) & (kj >= first_key)


def _group(ref, h, width):
    return ref[GQA * h:GQA * (h + 1)].reshape(GROUP_ROWS, width)


def _group_sinks(s_ref, h):
    return jnp.concatenate([jnp.broadcast_to(s_ref[GQA * h + g:GQA * h + g + 1, :], (WINDOW, 1)) for g in range(GQA)], axis=0)


def _attn_probs(q, kc, bias, sink, valid):
    s = lax.dot_general(q, kc, (_DIMS["nt"], ((), ())), preferred_element_type=F32) * ATTN_SCALE + bias
    s = jnp.where(valid, s, NEG)
    m = jnp.maximum(jnp.max(s, axis=1, keepdims=True), sink)
    p = jnp.exp(s - m)
    e_sink = jnp.exp(sink - m)
    inv = 1.0 / (jnp.sum(p, axis=1, keepdims=True) + e_sink)
    return p * inv, e_sink * inv


def _attn_fwd(qh, kh, vh, bias, sinks, name):
    t = qh.shape[1]
    q_spec, kv_cur, kv_prev, bias_spec, sink_spec = _attn_specs()

    def body(q_ref, kp_ref, kc_ref, vp_ref, vc_ref, b_ref, s_ref, o_ref):
        valid = _attn_valid(pl.program_id(0))
        for h in range(N_KV):
            kc = jnp.concatenate([kp_ref[h], kc_ref[h]], axis=0)
            vc = jnp.concatenate([vp_ref[h], vc_ref[h]], axis=0)
            w, _ = _attn_probs(_group(q_ref, h, HEAD_DIM), kc, _group(b_ref, h, 2 * WINDOW), _group_sinks(s_ref, h), valid)
            o = jnp.dot(w.astype(BF16), vc, preferred_element_type=F32)
            o_ref[GQA * h:GQA * (h + 1)] = o.reshape(GQA, WINDOW, HEAD_DIM).astype(o_ref.dtype)

    return pl.pallas_call(
        body, name=name, grid=(t // WINDOW,),
        in_specs=[q_spec, kv_prev, kv_cur, kv_prev, kv_cur, bias_spec, sink_spec],
        out_specs=q_spec, out_shape=jax.ShapeDtypeStruct((N_Q, t, HEAD_DIM), BF16),
        compiler_params=pltpu.CompilerParams(dimension_semantics=("parallel",)),
    )(qh, kh, kh, vh, vh, bias, sinks)


def _attn_bwd(qh, kh, vh, doh, bias, sinks, name):
    t = qh.shape[1]
    q_spec, kv_cur, kv_prev, bias_spec, sink_spec = _attn_specs()

    def body(q_ref, kp_ref, kc_ref, vp_ref, vc_ref, do_ref, b_ref, s_ref,
             dq_ref, dkc_ref, dkp_ref, dvc_ref, dvp_ref, db_ref, ds_ref):
        n = pl.program_id(0)
        valid = _attn_valid(n)

        @pl.when(n == 0)
        def _():
            db_ref[...] = jnp.zeros_like(db_ref)
            ds_ref[...] = jnp.zeros_like(ds_ref)

        for h in range(N_KV):
            kc = jnp.concatenate([kp_ref[h], kc_ref[h]], axis=0)
            vc = jnp.concatenate([vp_ref[h], vc_ref[h]], axis=0)
            heads = slice(GQA * h, GQA * (h + 1))
            q, do = _group(q_ref, h, HEAD_DIM), _group(do_ref, h, HEAD_DIM)
            w, w_sink = _attn_probs(q, kc, _group(b_ref, h, 2 * WINDOW), _group_sinks(s_ref, h), valid)
            dw = lax.dot_general(do, vc, (_DIMS["nt"], ((), ())), preferred_element_type=F32)
            delta = jnp.sum(w * dw, axis=1, keepdims=True)
            dscore = w * (dw - delta)
            ds_ref[heads] += (-w_sink * delta).reshape(GQA, WINDOW, 1)
            db_ref[heads] += dscore.reshape(GQA, WINDOW, 2 * WINDOW)
            dsb = dscore.astype(BF16)
            dq_ref[heads] = (jnp.dot(dsb, kc, preferred_element_type=F32) * ATTN_SCALE).reshape(GQA, WINDOW, HEAD_DIM)
            dk = lax.dot_general(dsb, q, (_DIMS["tn"], ((), ())), preferred_element_type=F32) * ATTN_SCALE
            dv = lax.dot_general(w.astype(BF16), do, (_DIMS["tn"], ((), ())), preferred_element_type=F32)
            dkp_ref[h], dkc_ref[h] = dk[0:WINDOW], dk[WINDOW:2 * WINDOW]
            dvp_ref[h], dvc_ref[h] = dv[0:WINDOW], dv[WINDOW:2 * WINDOW]

    kv_shape = jax.ShapeDtypeStruct((N_KV, t, HEAD_DIM), F32)
    return pl.pallas_call(
        body, name=name, grid=(t // WINDOW,),
        in_specs=[q_spec, kv_prev, kv_cur, kv_prev, kv_cur, q_spec, bias_spec, sink_spec],
        out_specs=[q_spec, kv_cur, kv_cur, kv_cur, kv_cur, bias_spec, pl.BlockSpec((N_Q, WINDOW, 1), lambda n: (0, 0, 0))],
        out_shape=[jax.ShapeDtypeStruct((N_Q, t, HEAD_DIM), F32), kv_shape, kv_shape, kv_shape, kv_shape,
                   jax.ShapeDtypeStruct((N_Q, WINDOW, 2 * WINDOW), F32), jax.ShapeDtypeStruct((N_Q, WINDOW, 1), F32)],
        compiler_params=pltpu.CompilerParams(dimension_semantics=("arbitrary",)),
    )(qh, kh, kh, vh, vh, doh, bias, sinks)


def _heads(a, n_heads):
    t = a.shape[0]
    return a.astype(BF16).reshape(t, n_heads, HEAD_DIM).transpose(1, 0, 2)


def _unheads(a):
    n_heads, t, _ = a.shape
    return a.transpose(1, 0, 2).reshape(t, n_heads * HEAD_DIM)


def _shift_blocks(cur, prev):
    return cur + jnp.concatenate([prev[:, WINDOW:], jnp.zeros_like(prev[:, :WINDOW])], axis=1)


def _t5_bucket_table():
    qi = np.arange(WINDOW)[:, None]
    kj = np.arange(2 * WINDOW)[None, :]
    dist = np.clip(qi + WINDOW - kj, 0, REL_MAX_DIST - 1)
    exact = REL_BUCKETS // 2
    df = np.maximum(dist, 1).astype(np.float32)
    large = exact + (np.log(df / np.float32(exact)) / np.float32(math.log(REL_MAX_DIST / exact)) * (REL_BUCKETS - exact)).astype(np.int32)
    large = np.minimum(large, REL_BUCKETS - 1)
    bucket = np.where(dist < exact, dist, large)
    onehot = np.zeros((WINDOW * 2 * WINDOW, REL_BUCKETS), np.float32)
    onehot[np.arange(WINDOW * 2 * WINDOW), bucket.reshape(-1)] = 1.0
    return onehot


def _band_bias(rel_bias):
    onehot = jnp.asarray(_t5_bucket_table())
    sel = jnp.sum(onehot[:, :, None] * rel_bias[None, :, :], axis=1)
    return sel.T.reshape(N_Q, WINDOW, 2 * WINDOW)


def _block_diag(a):
    g, r, c = a.shape
    a4 = a.reshape(SSM_BLOCKS, g // SSM_BLOCKS, r, c)
    eye = jnp.eye(g // SSM_BLOCKS, dtype=a.dtype)
    full = a4[:, :, :, None, :] * eye[None, :, None, :, None]
    return full.reshape(SSM_BLOCKS, (g // SSM_BLOCKS) * r, (g // SSM_BLOCKS) * c)


def _ssm_disc(lam_re, lam_im, b_re, b_im, c_re, c_im, log_dt):
    dt = jnp.exp(log_dt)[:, None]
    mag = jnp.exp(lam_re * dt)
    ang = lam_im * dt
    a_re = mag * jnp.cos(ang)
    a_im = mag * jnp.sin(ang)
    den = lam_re * lam_re + lam_im * lam_im
    nr = a_re - 1.0
    coef_re = (nr * lam_re + a_im * lam_im) / den
    coef_im = (a_im * lam_re - nr * lam_im) / den
    bb_re = coef_re[..., None] * b_re - coef_im[..., None] * b_im
    bb_im = coef_re[..., None] * b_im + coef_im[..., None] * b_re
    wb_re = _block_diag(jnp.swapaxes(bb_re, 1, 2))
    wb_im = _block_diag(jnp.swapaxes(bb_im, 1, 2))
    cm_re = _block_diag(jnp.swapaxes(c_re, 1, 2))
    cm_imn = _block_diag(-jnp.swapaxes(c_im, 1, 2))
    return a_re.reshape(-1), a_im.reshape(-1), wb_re, wb_im, cm_re, cm_imn


def _scan_tables(a_re, a_im):
    pr, pi = [a_re], [a_im]
    for _ in range(7):
        pr, pi = pr + [pr[-1] * a_re - pi[-1] * a_im], pi + [pr[-1] * a_im + pi[-1] * a_re]
    pr, pi = jnp.stack(pr), jnp.stack(pi)
    pad = jnp.zeros((5,) + a_re.shape, F32)
    dr = jnp.concatenate([jnp.stack([pr[0], pr[1], pr[3]]), pad])
    di = jnp.concatenate([jnp.stack([pi[0], pi[1], pi[3]]), pad])
    fwd = (pr, pi, dr, di)
    rev = (pr[::-1], -pi[::-1], dr, -di)
    return jax.tree.map(lax.stop_gradient, (fwd, rev))


def _gate_col(r):
    return lambda j: (OFF_G + r * D_MODEL) // 256 + j


def _layer_fwd(x, p_i, w, bias, li):
    nm = lambda s: f"{s}_l{li}"
    h = _rms_fwd(x, w["norm_mix"], nm("rms_mix"))
    z = _mm(h, w["w_in"], "nn", tm=512, tn=2944, name=nm("mm_in"))
    bu_re, bu_im = _bd_apply([z], [w["wb_re"], w["wb_im"]], [[(0, 0)], [(0, 1)]], "nn", name=nm("ssm_bu"))
    h_re, h_im = _scan(bu_re, bu_im, *w["scan_fwd"], reverse=False, name=nm("ssm_scan"))
    (y0,) = _bd_apply([h_re, h_im], [w["cm_re"], w["cm_imn"]], [[(0, 0), (1, 1)]], "nn", name=nm("ssm_c"))
    (y1,) = _rw(lambda a, u, d: ([jax.nn.gelu(a + d * u)], []),
                [(y0, None, _c0), (z, BRANCH, _c0)], [(BRANCH, None, F32, _c0)],
                params=[w["ssm_d"]], name=nm("ssm_gelu"))
    gl = _mm(y1, w["ssm_w_glu"], "nn", name=nm("mm_glu"))
    (y_ssm,) = _rw(lambda a, b: ([a * jax.nn.sigmoid(b)], []), [(y1, None, _c0), (gl, None, _c0)],
                   [(BRANCH, None, BF16, _c0)], name=nm("ssm_glu"))
    y_conv = _conv_fwd(z, w["conv_w"], nm("conv_fwd"))
    qh, kh, vh = _heads(z[:, OFF_Q:OFF_K], N_Q), _heads(z[:, OFF_K:OFF_V], N_KV), _heads(z[:, OFF_V:OFF_G], N_KV)
    y_attn = _unheads(_attn_fwd(qh, kh, vh, bias, w["sinks"], nm("attn_fwd")))
    ys = (y_ssm, y_conv, y_attn)
    bs = [_mm(ys[r], w["w_branch"][r], "nn", name=nm(f"mm_branch{r}")) for r in range(3)]

    def merge(g0, g1, g2, b0, b1, b2):
        return [jax.nn.sigmoid(g0) * b0 + jax.nn.sigmoid(g1) * b1 + jax.nn.sigmoid(g2) * b2], []

    (merged,) = _rw(merge, [(z, 256, _gate_col(r)) for r in range(3)] + [(b, 256, lambda j: j) for b in bs],
                    [(D_MODEL, 256, BF16, lambda j: j)], ncol=4, name=nm("merge"))
    x1 = _mm(merged, w["w_out"], "nn", add=x, name=nm("mm_out"))
    hf_in = _rms_fwd(x1, w["norm_ffn"], nm("rms_ffn"))
    hf = _mm(hf_in, w["w_ffn_in"], "nn", tn=1408, name=nm("mm_ffn_in"))
    (act,) = _rw(lambda a, b: ([jax.nn.silu(a) * b], []), [(hf, FFN_COLS, lambda j: j), (hf, FFN_COLS, lambda j: FFN_NCOL + j)],
                 [(FFN_HIDDEN, FFN_COLS, BF16, lambda j: j)], ncol=FFN_NCOL, name=nm("swiglu"))
    x2 = _mm(act, w["w_ffn_out"], "nn", add=x1, tk=1408, name=nm("mm_ffn_out"))
    hp = _rms_fwd(x2, w["norm_ple"], nm("rms_ple"))
    pgl = _mm(hp, w["w_ple_gate"], "nn", name=nm("mm_ple_gate"))
    pp = _mm(p_i, w["w_ple_proj"], "nn", name=nm("mm_ple_proj"))
    (x3,) = _rw(lambda xv, a, b: ([xv + jax.nn.sigmoid(a) * b], []), [(x2, None, _c0), (pgl, None, _c0), (pp, None, _c0)],
                [(D_MODEL, None, F32, _c0)], name=nm("ple_add"))
    saved = dict(x=x, p=p_i, h=h, z=z, h_re=h_re, h_im=h_im, y0=y0, y1=y1, gl=gl, ys=ys, qh=qh, kh=kh, vh=vh,
                 bs=bs, merged=merged, x1=x1, hf_in=hf_in, hf=hf, act=act, x2=x2, hp=hp, pgl=pgl, pp=pp)
    return x3, saved


def _layer_bwd(dx3, s, w, bias, li):
    nm = lambda n: f"{n}_l{li}"
    g = {}
    z = s["z"]
    def ple_b(d, a, b):
        _, vjp = jax.vjp(lambda a_, b_: jax.nn.sigmoid(a_) * b_, a, b)
        return list(vjp(d)), []

    dpgl, dpp = _rw(ple_b, [(dx3, None, _c0), (s["pgl"], None, _c0), (s["pp"], None, _c0)],
                    [(D_MODEL, None, BF16, _c0)] * 2, name=nm("ple_bwd"))
    g["w_ple_proj"] = _mm(s["p"], dpp, "tn", name=nm("mmg_ple_proj"))
    g["w_ple_gate"] = _mm(s["hp"], dpgl, "tn", name=nm("mmg_ple_gate"))
    dhp = _mm(dpgl, w["w_ple_gate"], "nt", name=nm("mmb_ple_gate"))
    dx2, g["norm_ple"] = _rms_bwd(s["x2"], dhp, dx3, w["norm_ple"], nm("rmsb_ple"))
    dact = _mm(dx2, w["w_ffn_out"], "nt", tn=1408, name=nm("mmb_ffn_out"))
    g["w_ffn_out"] = _mm(s["act"], dx2, "tn", tm=1408, name=nm("mmg_ffn_out"))

    def swiglu_b(a, b, d):
        _, vjp = jax.vjp(lambda a_, b_: jax.nn.silu(a_) * b_, a, b)
        return list(vjp(d)), []

    dhf_a, dhf_b = _rw(swiglu_b, [(s["hf"], FFN_COLS, lambda j: j), (s["hf"], FFN_COLS, lambda j: FFN_NCOL + j), (dact, FFN_COLS, lambda j: j)],
                       [(FFN_HIDDEN, FFN_COLS, BF16, lambda j: j)] * 2, ncol=FFN_NCOL, name=nm("swiglu_bwd"))
    g["w_ffn_in"] = jnp.concatenate([_mm(s["hf_in"], dhf_a, "tn", tn=1408, name=nm("mmg_ffn_in_a")),
                                     _mm(s["hf_in"], dhf_b, "tn", tn=1408, name=nm("mmg_ffn_in_b"))], axis=1)
    dhf_in = _mm(dhf_a, w["w_ffn_in"], "nt", tk=1408, name=nm("mmb_ffn_in_a"))
    dhf_in = _mm(dhf_b, w["w_ffn_in"], "nt", tk=1408, b_k0=FFN_HIDDEN, add=dhf_in, name=nm("mmb_ffn_in_b"))
    dx1, g["norm_ffn"] = _rms_bwd(s["x1"], dhf_in, dx2, w["norm_ffn"], nm("rmsb_ffn"))
    dmerged = _mm(dx1, w["w_out"], "nt", name=nm("mmb_out"))
    g["w_out"] = _mm(s["merged"], dx1, "tn", name=nm("mmg_out"))

    def merge_b(d, g0, g1, g2, b0, b1, b2):
        outs_g, outs_b = [], []
        for gate, br in ((g0, b0), (g1, b1), (g2, b2)):
            sg = jax.nn.sigmoid(gate)
            outs_g.append(d * br * sg * (1.0 - sg))
            outs_b.append(d * sg)
        return outs_g + outs_b, []

    res = _rw(merge_b, [(dmerged, 256, lambda j: j)] + [(z, 256, _gate_col(r)) for r in range(3)] + [(b, 256, lambda j: j) for b in s["bs"]],
              [(D_MODEL, 256, BF16, lambda j: j)] * 6, ncol=4, name=nm("merge_bwd"))
    dgates, dbs = res[:3], res[3:]
    dys = [_mm(dbs[r], w["w_branch"][r], "nt", name=nm(f"mmb_branch{r}")) for r in range(3)]
    g["w_branch"] = jnp.stack([_mm(s["ys"][r], dbs[r], "tn", name=nm(f"mmg_branch{r}")) for r in range(3)])
    doh = _heads(dys[2], N_Q)
    dqh, dkc, dkp, dvc, dvp, dbias, dsink = _attn_bwd(s["qh"], s["kh"], s["vh"], doh, bias, w["sinks"], nm("attn_bwd"))
    dq, dk, dv = _unheads(dqh), _unheads(_shift_blocks(dkc, dkp)), _unheads(_shift_blocks(dvc, dvp))
    g["sinks"] = jnp.sum(dsink, axis=(1, 2))
    dcb, dcc, dcx, dconv = _conv_bwd(dys[1], z, w["conv_w"], nm("conv_bwd"))
    g["conv_w"] = dconv[0:3]
    def glu_b(d, y1, gl):
        sg = jax.nn.sigmoid(gl)
        return [d * y1 * sg * (1.0 - sg), d * sg], []

    dgl, dy1a = _rw(glu_b, [(dys[0], None, _c0), (s["y1"], None, _c0), (s["gl"], None, _c0)],
                    [(BRANCH, None, BF16, _c0), (BRANCH, None, F32, _c0)], name=nm("ssm_glu_bwd"))
    g["ssm_w_glu"] = _mm(s["y1"], dgl, "tn", name=nm("mmg_glu"))
    dy1b = _mm(dgl, w["ssm_w_glu"], "nt", name=nm("mmb_glu"))

    def gelu_b(da, db, a, u, d):
        _, vjp = jax.vjp(lambda pre: jax.nn.gelu(pre), a + d * u)
        (dy0,) = vjp(da + db)
        return [dy0, dy0 * d], [jnp.sum(dy0 * u, axis=0, keepdims=True)]

    dy0, du_a, g["ssm_d"] = _rw(gelu_b, [(dy1a, None, _c0), (dy1b, None, _c0), (s["y0"], None, _c0), (z, BRANCH, _c0)],
                                [(BRANCH, None, BF16, _c0), (BRANCH, None, F32, _c0)], params=[w["ssm_d"]],
                                reds=[((1, BRANCH), None, _c0)], name=nm("ssm_gelu_bwd"))
    dh_re, dh_im = _bd_apply([dy0], [w["cm_re"], w["cm_imn"]], [[(0, 0)], [(0, 1)]], "nt", name=nm("ssmb_c"))
    sb, cb = SSM_STATES // SSM_BLOCKS, BRANCH // SSM_BLOCKS
    g["cm_re"], g["cm_imn"] = _bd_grads([s["h_re"], s["h_im"], dy0], [sb, sb, cb], [(0, 2), (1, 2)], name=nm("ssmg_c"))
    l_re, l_im, da_re, da_im = _scan(dh_re, dh_im, *w["scan_rev"], reverse=True, hr=s["h_re"], hi=s["h_im"], name=nm("ssm_scan_bwd"))
    g["a_re"], g["a_im"] = jnp.sum(da_re, axis=0), jnp.sum(da_im, axis=0)
    (du_b,) = _bd_apply([l_re, l_im], [w["wb_re"], w["wb_im"]], [[(0, 0), (1, 1)]], "nt", name=nm("ssmb_bu"))
    g["wb_re"], g["wb_im"] = _bd_grads([z, l_re, l_im], [cb, sb, sb], [(0, 1), (0, 2)], name=nm("ssmg_bu"))
    dz = jnp.concatenate([(du_a + du_b).astype(BF16), dcb, dcc, dcx, dq.astype(BF16), dk.astype(BF16), dv.astype(BF16)] + list(dgates), axis=1)
    g["w_in"] = _mm(s["h"], dz, "tn", tm=512, tn=2944, name=nm("mmg_in"))
    dh = _mm(dz, w["w_in"], "nt", tk=2944, name=nm("mmb_in"))
    dx, g["norm_mix"] = _rms_bwd(s["x"], dh, dx1, w["norm_mix"], nm("rmsb_mix"))
    return dx, g, dbias


def _loss_and_seed(x, target, g_final):
    def fn(xv, tv, gv):
        y, vjp = jax.vjp(_rms, xv, gv)
        err = y - tv
        dx, dg = vjp(err * (1.0 / D_MODEL))
        return [dx], [jnp.sum(err * err, axis=0, keepdims=True) * (0.5 / D_MODEL), dg]

    return _rw(fn, [(x, None, _c0), (target, None, _c0)], [(D_MODEL, None, F32, _c0)], params=[g_final],
               reds=[((1, D_MODEL), None, _c0)] * 2, name="loss_head")


def _local_step(x, p, target, wt):
    bias, bias_vjp = jax.vjp(_band_bias, wt["rel_bias"])
    layers, disc_vjps = [], []
    for i in range(DEPTH):
        ssm_p = [wt[k][i] for k in ("ssm_lambda_re", "ssm_lambda_im", "ssm_b_re", "ssm_b_im", "ssm_c_re", "ssm_c_im", "ssm_log_dt")]
        (a_re, a_im, wb_re, wb_im, cm_re, cm_imn), disc_vjp = jax.vjp(_ssm_disc, *ssm_p)
        scan_fwd, scan_rev = _scan_tables(a_re, a_im)
        layers.append(dict(
            norm_mix=wt["norm_mix"][i][None], w_in=wt["w_in"][i], wb_re=wb_re.astype(BF16), wb_im=wb_im.astype(BF16),
            cm_re=cm_re.astype(BF16), cm_imn=cm_imn.astype(BF16), scan_fwd=scan_fwd, scan_rev=scan_rev,
            ssm_d=wt["ssm_d"][i][None], ssm_w_glu=wt["ssm_w_glu"][i], conv_w=wt["conv_w"][i],
            sinks=wt["attn_sinks"][i][:, None], w_branch=wt["w_branch"][i], w_out=wt["w_out"][i],
            norm_ffn=wt["norm_ffn"][i][None], w_ffn_in=wt["w_ffn_in"][i], w_ffn_out=wt["w_ffn_out"][i],
            norm_ple=wt["norm_ple"][i][None], w_ple_gate=wt["w_ple_gate"][i], w_ple_proj=wt["w_ple_proj"][i]))
        disc_vjps.append(disc_vjp)

    saved = []
    for i in range(DEPTH):
        x, s = _layer_fwd(x, p[i], layers[i], bias, i)
        saved.append(s)
    dx, loss_cols, g_final = _loss_and_seed(x, target, wt["norm_final"][None])
    loss = jnp.sum(loss_cols)

    per_layer = [None] * DEPTH
    dbias = None
    for i in reversed(range(DEPTH)):
        dx, g, db = _layer_bwd(dx, saved[i], layers[i], bias, i)
        dbias = db if dbias is None else dbias + db
        (g["ssm_lambda_re"], g["ssm_lambda_im"], g["ssm_b_re"], g["ssm_b_im"], g["ssm_c_re"], g["ssm_c_im"], g["ssm_log_dt"]) = \
            disc_vjps[i]((g.pop("a_re"), g.pop("a_im"), g.pop("wb_re"), g.pop("wb_im"), g.pop("cm_re"), g.pop("cm_imn")))
        g["attn_sinks"] = g.pop("sinks")
        for k in ("norm_mix", "norm_ffn", "norm_ple", "ssm_d"):
            g[k] = g[k][0]
        per_layer[i] = g
    big_names = [name for name, _, _ in BIG]
    big = {k: [per_layer[i][k] for i in range(DEPTH)] for k in big_names}
    small = {k: jnp.stack([per_layer[i][k] for i in range(DEPTH)]) for k in per_layer[0] if k not in big_names}
    (small["rel_bias"],) = bias_vjp(dbias)
    small["norm_final"] = g_final[0]
    return loss, dx, small, big


HBM_SPEC = pl.BlockSpec(memory_space=pltpu.HBM)


def _position():
    x, y, c = lax.axis_index("x"), lax.axis_index("y"), lax.axis_index("c")
    other_chips = [(1 - x, y), (x, 1 - y), (1 - x, 1 - y)]
    return x, y, c, other_chips


def _row_chunks(rows, n=COPY_CHUNKS):
    rq = rows // n
    assert rq * n == rows and rq % 16 == 0, rows
    return [pl.ds(q * rq, rq) for q in range(n)]


def _place(buf, val, idx, name):
    n, rows, width = buf.shape
    tm = _pick(rows, 512, 16)

    def body(idx_ref, buf_ref, v_ref, o_ref):
        o_ref[0] = v_ref[...]

    grid_spec = pltpu.PrefetchScalarGridSpec(
        num_scalar_prefetch=1, grid=(rows // tm,),
        in_specs=[pl.BlockSpec(memory_space=pl.ANY), pl.BlockSpec((tm, width), lambda i, idx_ref: (i, 0))],
        out_specs=pl.BlockSpec((1, tm, width), lambda i, idx_ref: (idx_ref[0], i, 0)))
    return pl.pallas_call(
        body, name=name, grid_spec=grid_spec, out_shape=jax.ShapeDtypeStruct(buf.shape, buf.dtype), input_output_aliases={1: 0},
        compiler_params=pltpu.CompilerParams(dimension_semantics=("arbitrary",)),
    )(jnp.asarray(idx, jnp.int32).reshape(1), buf, val)


def _allgather_weights(locals_):
    nb, nq = len(locals_), COPY_CHUNKS
    chunks = [_row_chunks(a.shape[1]) for a in locals_]

    def body(*refs):
        w_refs, out_refs = refs[:nb], refs[nb:2 * nb]
        send_sems, recv_sems = refs[2 * nb:]
        x, y, c, chips = _position()
        me = 2 * x + y
        sibling = (x, y, 1 - c)

        def copy(b, kind, q, src, dst, to):
            k = (b * 6 + kind) * nq + q
            return pltpu.make_async_remote_copy(src_ref=src, dst_ref=dst, send_sem=send_sems.at[k], recv_sem=recv_sems.at[k],
                                                device_id=to, device_id_type=MESH)

        first = [copy(b, j, q, w_refs[b].at[c, chunks[b][q]], out_refs[b].at[me, c, chunks[b][q]], (*chip, c))
                 for q in range(nq) for b in range(nb) for j, chip in enumerate(chips)]
        for cp in first:
            cp.start()
        passed = []
        for q in range(nq):
            for b in range(nb):
                for j, (px, py) in enumerate(chips):
                    landed = out_refs[b].at[2 * px + py, c, chunks[b][q]]
                    copy(b, j, q, landed, landed, (px, py, c)).wait_recv()
                    fwd = copy(b, 3 + j, q, landed, landed, sibling)
                    fwd.start()
                    passed.append(fwd)
        for q in range(nq):
            for b in range(nb):
                for j, (px, py) in enumerate(chips):
                    landed = out_refs[b].at[2 * px + py, 1 - c, chunks[b][q]]
                    copy(b, 3 + j, q, landed, landed, sibling).wait_recv()
        for cp in first + passed:
            cp.wait_send()

    return pl.pallas_call(
        body, name="allgather_weights", in_specs=[HBM_SPEC] * nb, out_specs=[HBM_SPEC] * nb,
        out_shape=[jax.ShapeDtypeStruct((N_SHARD,) + a.shape, a.dtype) for a in locals_],
        scratch_shapes=[pltpu.SemaphoreType.DMA((nb * 6 * nq,)), pltpu.SemaphoreType.DMA((nb * 6 * nq,))],
    )(*locals_)


def _sibling_exchange(bufs):
    nb, nq, ns = len(bufs), COPY_CHUNKS, N_SHARD
    chunks = [_row_chunks(a.shape[2]) for a in bufs]

    def body(*refs):
        g_refs, got_refs = refs[:nb], refs[nb:2 * nb]
        send_sems, recv_sems = refs[2 * nb:]
        x, y, c, _ = _position()
        swaps = [pltpu.make_async_remote_copy(src_ref=g_refs[b].at[s, 1 - c, chunks[b][q]], dst_ref=got_refs[b].at[s, chunks[b][q]],
                                              send_sem=send_sems.at[(b * ns + s) * nq + q], recv_sem=recv_sems.at[(b * ns + s) * nq + q],
                                              device_id=(x, y, 1 - c), device_id_type=MESH)
                 for b in range(nb) for s in range(ns) for q in range(nq)]
        for cp in swaps:
            cp.start()
        for cp in swaps:
            cp.wait()

    return pl.pallas_call(
        body, name="grad_sibling_exchange", in_specs=[HBM_SPEC] * nb, out_specs=[HBM_SPEC] * nb,
        out_shape=[jax.ShapeDtypeStruct((ns,) + a.shape[2:], a.dtype) for a in bufs],
        scratch_shapes=[pltpu.SemaphoreType.DMA((nb * ns * nq,)), pltpu.SemaphoreType.DMA((nb * ns * nq,))],
    )(*bufs)


def _chip_exchange(parts):
    nb, nq = len(parts), COPY_CHUNKS
    chunks = [_row_chunks(a.shape[1]) for a in parts]

    def body(*refs):
        b_refs, got_refs = refs[:nb], refs[nb:2 * nb]
        send_sems, recv_sems = refs[2 * nb:]
        x, y, c, chips = _position()
        sends = [pltpu.make_async_remote_copy(src_ref=b_refs[b].at[2 * px + py, chunks[b][q]], dst_ref=got_refs[b].at[j, chunks[b][q]],
                                              send_sem=send_sems.at[(b * 3 + j) * nq + q], recv_sem=recv_sems.at[(b * 3 + j) * nq + q],
                                              device_id=(px, py, c), device_id_type=MESH)
                 for q in range(nq) for b in range(nb) for j, (px, py) in enumerate(chips)]
        for cp in sends:
            cp.start()
        for cp in sends:
            cp.wait()

    return pl.pallas_call(
        body, name="grad_chip_exchange", in_specs=[HBM_SPEC] * nb, out_specs=[HBM_SPEC] * nb,
        out_shape=[jax.ShapeDtypeStruct((N_SHARD - 1,) + a.shape[1:], a.dtype) for a in parts],
        scratch_shapes=[pltpu.SemaphoreType.DMA((nb * 3 * nq,)), pltpu.SemaphoreType.DMA((nb * 3 * nq,))],
    )(*parts)


def _sibling_gather(halves):
    nb, nq = len(halves), 2 * COPY_CHUNKS
    chunks = [_row_chunks(a.shape[0], nq) for a in halves]

    def body(*refs):
        h_refs, out_refs = refs[:nb], refs[nb:2 * nb]
        send_sems, recv_sems = refs[2 * nb:]
        x, y, c, _ = _position()

        def chunk(b, q, half_idx):
            return pltpu.make_async_remote_copy(src_ref=h_refs[b].at[chunks[b][q]], dst_ref=out_refs[b].at[half_idx, chunks[b][q]],
                                                send_sem=send_sems.at[b * nq + q], recv_sem=recv_sems.at[b * nq + q],
                                                device_id=(x, y, 1 - c), device_id_type=MESH)

        pushes = [chunk(b, q, c) for b in range(nb) for q in range(nq)]
        for cp in pushes:
            cp.start()
        for b in range(nb):
            for q in range(nq):
                chunk(b, q, 1 - c).wait_recv()
        for cp in pushes:
            cp.wait_send()

    return pl.pallas_call(
        body, name="grad_sibling_gather", in_specs=[HBM_SPEC] * nb, out_specs=[HBM_SPEC] * nb,
        out_shape=[jax.ShapeDtypeStruct((2,) + a.shape, a.dtype) for a in halves],
        scratch_shapes=[pltpu.SemaphoreType.DMA((nb * nq,)), pltpu.SemaphoreType.DMA((nb * nq,))],
    )(*halves)


def _gather_partials(part):
    r, lanes = part.shape

    def body(p_ref, out_ref, send_sems, recv_sems):
        x, y, c, _ = _position()
        flips = [(fx, fy, fc) for fx in (0, 1) for fy in (0, 1) for fc in (0, 1)][1:]
        sends = []
        for k, (fx, fy, fc) in enumerate(flips):
            cp = pltpu.make_async_remote_copy(src_ref=p_ref, dst_ref=out_ref.at[4 * x + 2 * y + c], send_sem=send_sems.at[k],
                                              recv_sem=recv_sems.at[k], device_id=(x ^ fx, y ^ fy, c ^ fc), device_id_type=MESH)
            cp.start()
            sends.append(cp)
        for k, (fx, fy, fc) in enumerate(flips):
            src = out_ref.at[4 * (x ^ fx) + 2 * (y ^ fy) + (c ^ fc)]
            pltpu.make_async_remote_copy(src_ref=src, dst_ref=src, send_sem=send_sems.at[k], recv_sem=recv_sems.at[k],
                                         device_id=(x ^ fx, y ^ fy, c ^ fc), device_id_type=MESH).wait_recv()
        for cp in sends:
            cp.wait_send()

    return pl.pallas_call(
        body, name="small_gather_partials", in_specs=[HBM_SPEC], out_specs=HBM_SPEC,
        out_shape=jax.ShapeDtypeStruct((8, r, lanes), part.dtype),
        scratch_shapes=[pltpu.SemaphoreType.DMA((7,)), pltpu.SemaphoreType.DMA((7,))],
    )(part)


def _sum_leading(stack, name, also_bf16=False):
    k, r, lanes = stack.shape
    tm = _pick(r, 256, 16)
    outs = [jax.ShapeDtypeStruct((r, lanes), F32)] + ([jax.ShapeDtypeStruct((r, lanes), BF16)] if also_bf16 else [])

    def body(s_ref, *o_refs):
        acc = s_ref[0].astype(F32)
        for i in range(1, k):
            acc = acc + s_ref[i].astype(F32)
        for o in o_refs:
            o[...] = acc.astype(o.dtype)

    spec = pl.BlockSpec((tm, lanes), lambda i: (i, 0))
    return pl.pallas_call(
        body, name=name, grid=(r // tm,), in_specs=[pl.BlockSpec((k, tm, lanes), lambda i: (0, i, 0))],
        out_specs=[spec] * len(outs), out_shape=outs,
        compiler_params=pltpu.CompilerParams(dimension_semantics=("parallel",)),
    )(stack)


def _add_pair(g2, got, half, name):
    ns, _, rows, width = g2.shape
    tm = _pick(rows, 256, 16)
    spec = pl.BlockSpec((1, tm, width), lambda s, i, h_ref: (s, i, 0))

    def body(h_ref, a_ref, b_ref, f_ref, o_ref):
        acc = a_ref[0] + b_ref[...]
        f_ref[...] = acc
        o_ref[...] = acc.astype(BF16)

    grid_spec = pltpu.PrefetchScalarGridSpec(
        num_scalar_prefetch=1, grid=(ns, rows // tm),
        in_specs=[pl.BlockSpec((1, 1, tm, width), lambda s, i, h_ref: (s, h_ref[0], i, 0)), spec], out_specs=[spec, spec])
    return pl.pallas_call(
        body, name=name, grid_spec=grid_spec,
        out_shape=[jax.ShapeDtypeStruct(got.shape, F32), jax.ShapeDtypeStruct(got.shape, BF16)],
        compiler_params=pltpu.CompilerParams(dimension_semantics=("parallel", "parallel")),
    )(jnp.asarray(half, jnp.int32).reshape(1), g2, got)


def _add_own(parts, got, mine, name):
    _, rows, width = parts.shape
    tm = _pick(rows, 256, 16)

    def body(m_ref, p_ref, g_ref, out_ref):
        acc = p_ref[0]
        for j in range(g_ref.shape[0]):
            acc = acc + g_ref[j].astype(F32)
        out_ref[...] = acc

    grid_spec = pltpu.PrefetchScalarGridSpec(
        num_scalar_prefetch=1, grid=(rows // tm,),
        in_specs=[pl.BlockSpec((1, tm, width), lambda i, m_ref: (m_ref[0], i, 0)),
                  pl.BlockSpec((got.shape[0], tm, width), lambda i, m_ref: (0, i, 0))],
        out_specs=pl.BlockSpec((tm, width), lambda i, m_ref: (i, 0)))
    return pl.pallas_call(
        body, name=name, grid_spec=grid_spec, out_shape=jax.ShapeDtypeStruct((rows, width), F32),
        compiler_params=pltpu.CompilerParams(dimension_semantics=("parallel",)),
    )(jnp.asarray(mine, jnp.int32).reshape(1), parts, got)


def _local_shape(shape, axis):
    return tuple(d // N_SHARD if a == axis else d for a, d in enumerate(shape))


def _big_sizes():
    return [DEPTH * int(np.prod(_local_shape(shape, axis))) for _, shape, axis in FLAT_BIG]


FLAT_BIG = tuple(entry for entry in BIG if entry[0] != "w_in")
W_IN_SHARD = IN_WIDTH // N_SHARD
FLAT_ROW_TILE = 256
ELEMENTWISE_BIG = ("conv_w",)


def _flat_rows(n):
    rows = -(-n // (2 * LANES))
    return -(-rows // FLAT_ROW_TILE) * FLAT_ROW_TILE


def _three_bf16(w):
    hi = w.astype(BF16)
    r1 = w - hi.astype(F32)
    mid = r1.astype(BF16)
    lo = (r1 - mid.astype(F32)).astype(BF16)
    return jnp.stack([hi, mid, lo], axis=-1)


def _pack_local_weights(wl):
    parts = [(_three_bf16(wl[name]) if name in ELEMENTWISE_BIG else wl[name].astype(BF16)).reshape(-1) for name, _, _ in FLAT_BIG]
    flat = jnp.concatenate(parts)
    r = _flat_rows(flat.shape[0])
    flat = jnp.pad(flat, (0, 2 * r * LANES - flat.shape[0]))
    return flat.reshape(2, r, LANES)


def _unpack_local(flat):
    flat = flat.reshape(-1)
    out, off = {}, 0
    for (name, shape, axis), size in zip(FLAT_BIG, _big_sizes(), strict=True):
        out[name] = flat[off:off + size].reshape((DEPTH,) + _local_shape(shape, axis))
        off += size
    return out


def _unpack_gathered(gathered):
    flat = gathered.reshape(N_SHARD, -1)
    out, off = {}, 0
    for (name, shape, axis), size in zip(FLAT_BIG, _big_sizes(), strict=True):
        local = (N_SHARD, DEPTH) + _local_shape(shape, axis)
        if name in ELEMENTWISE_BIG:
            parts = flat[:, off:off + 3 * size].reshape(local + (3,)).astype(F32)
            seg = (parts[..., 0] + parts[..., 1]) + parts[..., 2]
            off += 3 * size
        else:
            seg = flat[:, off:off + size].reshape(local)
            off += size
        out[name] = jnp.moveaxis(seg, 0, 1 + axis).reshape((DEPTH,) + shape)
    return out


def _join_col_shards(shards):
    ns, depth, rows, c = shards.shape
    tm = _pick(rows, 256, 16)

    def body(i_ref, o_ref):
        for s in range(ns):
            o_ref[0, :, c * s:c * (s + 1)] = i_ref[s, 0]

    return pl.pallas_call(
        body, name="join_col_shards", grid=(depth, rows // tm),
        in_specs=[pl.BlockSpec((ns, 1, tm, c), lambda l, i: (0, l, i, 0))],
        out_specs=pl.BlockSpec((1, tm, ns * c), lambda l, i: (l, i, 0)),
        out_shape=jax.ShapeDtypeStruct((depth, rows, ns * c), shards.dtype),
        compiler_params=pltpu.CompilerParams(dimension_semantics=("parallel", "parallel")),
    )(shards)


def _split_col_shards(full, name):
    rows, width = full.shape
    c = width // N_SHARD
    tm = _pick(rows, 256, 16)

    def body(i_ref, o_ref):
        for s in range(N_SHARD):
            o_ref[s] = i_ref[:, c * s:c * (s + 1)]

    return pl.pallas_call(
        body, name=name, grid=(rows // tm,), in_specs=[pl.BlockSpec((tm, width), lambda i: (i, 0))],
        out_specs=pl.BlockSpec((N_SHARD, tm, c), lambda i: (0, i, 0)),
        out_shape=jax.ShapeDtypeStruct((N_SHARD, rows, c), full.dtype),
        compiler_params=pltpu.CompilerParams(dimension_semantics=("parallel",)),
    )(full)


def _pack_full_grads(big_grads):
    rows = []
    for name, shape, axis in FLAT_BIG:
        for gfull in big_grads[name]:
            split = gfull.reshape(shape[:axis] + (N_SHARD, shape[axis] // N_SHARD) + shape[axis + 1:])
            rows.append(jnp.moveaxis(split, axis, 0).reshape(N_SHARD, -1))
    flat = jnp.concatenate(rows, axis=1)
    r = _flat_rows(flat.shape[1])
    flat = jnp.pad(flat, ((0, 0), (0, 2 * r * LANES - flat.shape[1])))
    return flat.reshape(N_SHARD, 2, r, LANES)


def _pack_small(grads):
    flat = jnp.concatenate([grads[name].reshape(-1) for name in SMALL])
    r = -(-flat.shape[0] // (8 * LANES)) * 8
    return jnp.pad(flat, (0, r * LANES - flat.shape[0])).reshape(r, LANES)


def _unpack_small(flat, like):
    flat = flat.reshape(-1)
    out, off = {}, 0
    for name in SMALL:
        size = int(np.prod(like[name].shape))
        out[name] = flat[off:off + size].reshape(like[name].shape)
        off += size
    return out


def _adamw(w, g, m, v, name):
    shape = w.shape
    cols = shape[-1]
    rows = int(np.prod(shape[:-1])) if len(shape) > 1 else 1
    w2, g2, m2, v2 = (a.reshape(rows, cols) for a in (w, g, m, v))

    def fn(wv, gv, mv, vv):
        mn = ADAM_B1 * mv + (1.0 - ADAM_B1) * gv
        vn = ADAM_B2 * vv + (1.0 - ADAM_B2) * jnp.square(gv)
        m_hat = mn / (1.0 - ADAM_B1 ** ADAM_STEP)
        v_hat = vn / (1.0 - ADAM_B2 ** ADAM_STEP)
        delta = -ADAM_LR * (m_hat / (jnp.sqrt(v_hat) + ADAM_EPS) + ADAM_WD * wv)
        return [delta, mn, vn], []

    tm = 256 if rows % 8 == 0 and rows > 256 else rows
    res = _rw(fn, [(a, None, _c0) for a in (w2, g2, m2, v2)], [(cols, None, F32, _c0)] * 3, tm=tm, name=name)
    return [r.reshape(shape) for r in res]


def _step(x, p, target, weights, moments_m, moments_v):
    xi, yi, ci = lax.axis_index("x"), lax.axis_index("y"), lax.axis_index("c")
    chip = 2 * xi + yi
    half_rows = DEPTH // 2 * D_MODEL
    flat_local = _pack_local_weights(weights)
    win_local = weights["w_in"].astype(BF16).reshape(2, half_rows, W_IN_SHARD)
    flat_all, win_all = _allgather_weights([flat_local, win_local])
    flat_all = _place(flat_all.reshape(N_SHARD, -1, LANES), flat_local.reshape(-1, LANES), chip, "place_own_flat_weights")
    win_all = _place(win_all.reshape(N_SHARD, 2 * half_rows, W_IN_SHARD), win_local.reshape(2 * half_rows, W_IN_SHARD), chip,
                     "place_own_w_in")
    wt = dict(_unpack_gathered(flat_all))
    wt["w_in"] = _join_col_shards(win_all.reshape(N_SHARD, DEPTH, D_MODEL, W_IN_SHARD))
    for name in SMALL:
        wt[name] = weights[name]
    loss, dx, small_grads, big_grads = _local_step(x[0], p[:, 0], target[0], wt)
    loss = lax.psum(loss, ("x", "y", "c"))
    win_g = jnp.stack([_split_col_shards(g, f"split_col_shards_w_in_l{li}") for li, g in enumerate(big_grads["w_in"])], axis=1)
    bufs = [_pack_full_grads(big_grads), win_g.reshape(N_SHARD, 2, half_rows, W_IN_SHARD)]
    gots = _sibling_exchange(bufs)
    sums = [_add_pair(b, g, ci, f"grad_add_sibling_{k}") for k, (b, g) in enumerate(zip(bufs, gots, strict=True))]
    others = _chip_exchange([s_bf16 for _, s_bf16 in sums])
    halves = [_add_own(s_f32, o, chip, f"grad_add_chips_{k}") for k, ((s_f32, _), o) in enumerate(zip(sums, others, strict=True))]
    both = [_place(b, h, ci, f"place_own_half_{k}") for k, (b, h) in enumerate(zip(_sibling_gather(halves), halves, strict=True))]
    reduced = _unpack_local(both[0])
    reduced["w_in"] = both[1].reshape(DEPTH, D_MODEL, W_IN_SHARD)
    small_part = _pack_small(small_grads)
    small_all = _place(_gather_partials(small_part), small_part, 4 * xi + 2 * yi + ci, "place_own_small")
    reduced.update(_unpack_small(_sum_leading(small_all, "small_sum")[0], {k: weights[k] for k in SMALL}))
    outs_g, outs_d, outs_m, outs_v = [], [], [], []
    for name in WEIGHTS:
        d, mn, vn = _adamw(weights[name], reduced[name], moments_m[name], moments_v[name], f"adamw_{name}")
        outs_g.append(reduced[name])
        outs_d.append(d)
        outs_m.append(mn)
        outs_v.append(vn)
    return (loss, dx[None], *outs_g, *outs_d, *outs_m, *outs_v)


def kernel(x, p, rel_bias, norm_mix, w_in, ssm_lambda_re, ssm_lambda_im, ssm_b_re, ssm_b_im, ssm_c_re, ssm_c_im, ssm_d, ssm_log_dt, ssm_w_glu, conv_w, attn_sinks, w_branch, w_out, norm_ffn, w_ffn_in, w_ffn_out, norm_ple, w_ple_gate, w_ple_proj, norm_final, loss_target, m_rel_bias, m_norm_mix, m_w_in, m_ssm_lambda_re, m_ssm_lambda_im, m_ssm_b_re, m_ssm_b_im, m_ssm_c_re, m_ssm_c_im, m_ssm_d, m_ssm_log_dt, m_ssm_w_glu, m_conv_w, m_attn_sinks, m_w_branch, m_w_out, m_norm_ffn, m_w_ffn_in, m_w_ffn_out, m_norm_ple, m_w_ple_gate, m_w_ple_proj, m_norm_final, v_rel_bias, v_norm_mix, v_w_in, v_ssm_lambda_re, v_ssm_lambda_im, v_ssm_b_re, v_ssm_b_im, v_ssm_c_re, v_ssm_c_im, v_ssm_d, v_ssm_log_dt, v_ssm_w_glu, v_conv_w, v_attn_sinks, v_w_branch, v_w_out, v_norm_ffn, v_w_ffn_in, v_w_ffn_out, v_norm_ple, v_w_ple_gate, v_w_ple_proj, v_norm_final):
    weights = dict(rel_bias=rel_bias, norm_mix=norm_mix, w_in=w_in, ssm_lambda_re=ssm_lambda_re, ssm_lambda_im=ssm_lambda_im,
                   ssm_b_re=ssm_b_re, ssm_b_im=ssm_b_im, ssm_c_re=ssm_c_re, ssm_c_im=ssm_c_im, ssm_d=ssm_d, ssm_log_dt=ssm_log_dt,
                   ssm_w_glu=ssm_w_glu, conv_w=conv_w, attn_sinks=attn_sinks, w_branch=w_branch, w_out=w_out, norm_ffn=norm_ffn,
                   w_ffn_in=w_ffn_in, w_ffn_out=w_ffn_out, norm_ple=norm_ple, w_ple_gate=w_ple_gate, w_ple_proj=w_ple_proj,
                   norm_final=norm_final)
    moments_m = dict(rel_bias=m_rel_bias, norm_mix=m_norm_mix, w_in=m_w_in, ssm_lambda_re=m_ssm_lambda_re, ssm_lambda_im=m_ssm_lambda_im,
                     ssm_b_re=m_ssm_b_re, ssm_b_im=m_ssm_b_im, ssm_c_re=m_ssm_c_re, ssm_c_im=m_ssm_c_im, ssm_d=m_ssm_d,
                     ssm_log_dt=m_ssm_log_dt, ssm_w_glu=m_ssm_w_glu, conv_w=m_conv_w, attn_sinks=m_attn_sinks, w_branch=m_w_branch,
                     w_out=m_w_out, norm_ffn=m_norm_ffn, w_ffn_in=m_w_ffn_in, w_ffn_out=m_w_ffn_out, norm_ple=m_norm_ple,
                     w_ple_gate=m_w_ple_gate, w_ple_proj=m_w_ple_proj, norm_final=m_norm_final)
    moments_v = dict(rel_bias=v_rel_bias, norm_mix=v_norm_mix, w_in=v_w_in, ssm_lambda_re=v_ssm_lambda_re, ssm_lambda_im=v_ssm_lambda_im,
                     ssm_b_re=v_ssm_b_re, ssm_b_im=v_ssm_b_im, ssm_c_re=v_ssm_c_re, ssm_c_im=v_ssm_c_im, ssm_d=v_ssm_d,
                     ssm_log_dt=v_ssm_log_dt, ssm_w_glu=v_ssm_w_glu, conv_w=v_conv_w, attn_sinks=v_attn_sinks, w_branch=v_w_branch,
                     w_out=v_w_out, norm_ffn=v_norm_ffn, w_ffn_in=v_w_ffn_in, w_ffn_out=v_w_ffn_out, norm_ple=v_norm_ple,
                     w_ple_gate=v_w_ple_gate, w_ple_proj=v_w_ple_proj, norm_final=v_norm_final)
    return _step(x, p, loss_target, weights, moments_m, moments_v)
```

```python
import functools
import math

import numpy as np

import jax
import jax.numpy as jnp
from jax import lax
from jax.experimental import pallas as pl
from jax.experimental.pallas import tpu as pltpu

F32, BF16 = jnp.float32, jnp.bfloat16
MESH = pl.DeviceIdType.MESH

D_MODEL = 1024
DEPTH = 4
PLE_DIM = 256
BRANCH = 512
N_GROUPS = 32
GROUP_CH = 16
N_STATE = 64
SSM_STATES = N_GROUPS * N_STATE
SSM_BLOCKS = 4
HEAD_DIM = 64
N_Q = 8
N_KV = 2
GQA = N_Q // N_KV
WINDOW = 128
ATTN_SCALE = 1.0 / math.sqrt(HEAD_DIM)
REL_BUCKETS = 32
REL_MAX_DIST = 128
FFN_HIDDEN = 2816
FFN_COLS = 1408
FFN_NCOL = FFN_HIDDEN // FFN_COLS
IN_WIDTH = 5888
RMS_EPS = 1e-6
NEG = -1e30

ADAM_LR, ADAM_B1, ADAM_B2, ADAM_EPS, ADAM_WD, ADAM_STEP = 0.001, 0.9, 0.999, 1e-08, 0.01, 10

N_SHARD = 4
LANES = 1024
COPY_CHUNKS = 4

OFF_U, OFF_CB, OFF_CC, OFF_CX, OFF_Q, OFF_K, OFF_V, OFF_G = 0, 512, 1024, 1536, 2048, 2560, 2688, 2816

BIG = (
    ("w_in", (D_MODEL, IN_WIDTH), 1),
    ("ssm_w_glu", (BRANCH, BRANCH), 0),
    ("conv_w", (3, BRANCH), 1),
    ("w_branch", (3, BRANCH, D_MODEL), 2),
    ("w_out", (D_MODEL, D_MODEL), 0),
    ("w_ffn_in", (D_MODEL, 2 * FFN_HIDDEN), 1),
    ("w_ffn_out", (FFN_HIDDEN, D_MODEL), 0),
    ("w_ple_gate", (D_MODEL, D_MODEL), 0),
    ("w_ple_proj", (PLE_DIM, D_MODEL), 1),
)
SMALL = ("rel_bias", "norm_mix", "ssm_lambda_re", "ssm_lambda_im", "ssm_b_re", "ssm_b_im", "ssm_c_re", "ssm_c_im",
         "ssm_d", "ssm_log_dt", "attn_sinks", "norm_ffn", "norm_ple", "norm_final")
WEIGHTS = ("rel_bias", "norm_mix", "w_in", "ssm_lambda_re", "ssm_lambda_im", "ssm_b_re", "ssm_b_im", "ssm_c_re",
           "ssm_c_im", "ssm_d", "ssm_log_dt", "ssm_w_glu", "conv_w", "attn_sinks", "w_branch", "w_out", "norm_ffn",
           "w_ffn_in", "w_ffn_out", "norm_ple", "w_ple_gate", "w_ple_proj", "norm_final")


def _c0(j):
    return 0


def _pick(n, cap, unit=128):
    if n <= cap:
        return n
    best = None
    for t in range(unit, cap + 1, unit):
        if n % t == 0:
            best = t
    assert best is not None, (n, cap, unit)
    return best


_DIMS = {"nn": ((1,), (0,)), "nt": ((1,), (1,)), "tn": ((0,), (0,))}


def _mm(a, b, mode, *, name, out_dtype=F32, add=None, tm=1024, tn=1024, tk=1024, b_k0=0):
    if mode == "nn":
        (m, k), (k2, n) = a.shape, b.shape
    elif mode == "nt":
        (m, k), (n, k2) = a.shape, b.shape
    else:
        (k, m), (k2, n) = a.shape, b.shape
    assert k == k2 or (mode == "nt" and b_k0 + k <= k2), (a.shape, b.shape, mode)
    tm, tn, tk = _pick(m, tm, 128 if mode == "tn" else 8), _pick(n, tn), _pick(k, tk, 128 if mode != "tn" else 8)
    nk = k // tk
    assert b_k0 % tk == 0
    kb0 = b_k0 // tk
    a_spec = pl.BlockSpec((tk, tm), lambda i, j, kk: (kk, i)) if mode == "tn" else pl.BlockSpec((tm, tk), lambda i, j, kk: (i, kk))
    b_spec = pl.BlockSpec((tn, tk), lambda i, j, kk: (j, kb0 + kk)) if mode == "nt" else pl.BlockSpec((tk, tn), lambda i, j, kk: (kk, j))
    o_spec = pl.BlockSpec((tm, tn), lambda i, j, kk: (i, j))
    dims = (_DIMS[mode], ((), ()))
    has_add = add is not None

    def body(*refs):
        a_ref, b_ref = refs[0], refs[1]
        add_ref = refs[2] if has_add else None
        o_ref, acc_ref = refs[-2], refs[-1]
        part = lax.dot_general(a_ref[...].astype(BF16), b_ref[...].astype(BF16), dims, preferred_element_type=F32)

        def finish(acc):
            if has_add:
                acc = acc + add_ref[...]
            o_ref[...] = acc.astype(o_ref.dtype)

        if nk == 1:
            finish(part)
        else:
            kk = pl.program_id(2)

            @pl.when(kk == 0)
            def _():
                acc_ref[...] = part

            @pl.when(kk > 0)
            def _():
                acc_ref[...] += part

            @pl.when(kk == nk - 1)
            def _():
                finish(acc_ref[...])

    operands = [a, b] + ([add] if has_add else [])
    in_specs = [a_spec, b_spec] + ([o_spec] if has_add else [])
    return pl.pallas_call(
        body, name=name, grid=(m // tm, n // tn, nk), in_specs=in_specs, out_specs=o_spec,
        out_shape=jax.ShapeDtypeStruct((m, n), out_dtype),
        scratch_shapes=[pltpu.VMEM((tm, tn) if nk > 1 else (8, 128), F32)],
        compiler_params=pltpu.CompilerParams(dimension_semantics=("parallel", "parallel", "arbitrary")),
    )(*operands)


def _rw(fn, ins, outs, *, name, params=(), reds=(), tm=256, ncol=1, with_j=False):
    t = ins[0][0].shape[0]
    tm = _pick(t, tm, 8)
    nrow = t // tm
    n_in, n_p, n_out = len(ins), len(params), len(outs)

    in_specs = [pl.BlockSpec((tm, bw or arr.shape[1]), lambda j, i, cf=cf: (i, cf(j))) for arr, bw, cf in ins]
    in_specs += [pl.BlockSpec(p.shape, lambda j, i: (0, 0)) for p in params]
    out_specs = [pl.BlockSpec((tm, bw or w), lambda j, i, cf=cf: (i, cf(j))) for w, bw, _, cf in outs]
    out_specs += [pl.BlockSpec((shp[0], bw or shp[1]), lambda j, i, cf=cf: (0, cf(j))) for shp, bw, cf in reds]
    out_shape = [jax.ShapeDtypeStruct((t, w), dt) for w, _, dt, _ in outs]
    out_shape += [jax.ShapeDtypeStruct(shp, F32) for shp, _, _ in reds]

    def body(*refs):
        in_refs, p_refs = refs[:n_in], refs[n_in:n_in + n_p]
        o_refs, r_refs = refs[n_in + n_p:n_in + n_p + n_out], refs[n_in + n_p + n_out:]
        args = [r[...] for r in in_refs] + [r[...] for r in p_refs]
        if with_j:
            args = [pl.program_id(0)] + args
        o_vals, r_vals = fn(*args)
        for r, v in zip(o_refs, o_vals, strict=True):
            r[...] = v.astype(r.dtype)
        if r_refs:
            i = pl.program_id(1)
            for r, v in zip(r_refs, r_vals, strict=True):
                @pl.when(i == 0)
                def _(r=r, v=v):
                    r[...] = v

                @pl.when(i > 0)
                def _(r=r, v=v):
                    r[...] += v

    res = pl.pallas_call(
        body, name=name, grid=(ncol, nrow), in_specs=in_specs, out_specs=out_specs, out_shape=out_shape,
        compiler_params=pltpu.CompilerParams(dimension_semantics=("parallel", "arbitrary" if reds else "parallel")),
    )(*[a for a, _, _ in ins], *params)
    return res


def _rms(x, g):
    return x * lax.rsqrt(jnp.mean(x * x, axis=-1, keepdims=True) + RMS_EPS) * g


def _rms_fwd(x, g, name):
    return _rw(lambda xv, gv: ([_rms(xv, gv)], []), [(x, None, _c0)], [(D_MODEL, None, BF16, _c0)], params=[g], name=name)[0]


def _rms_bwd(x, dh, dres, g, name):
    def fn(xv, dhv, drv, gv):
        _, vjp = jax.vjp(_rms, xv, gv)
        dx, dg = vjp(dhv)
        return [drv + dx], [dg]

    return _rw(fn, [(x, None, _c0), (dh, None, _c0), (dres, None, _c0)], [(D_MODEL, None, F32, _c0)], params=[g],
               reds=[((1, D_MODEL), None, _c0)], name=name)


def _bd_apply(acts, mats, combos, mode, *, name, tm=512):
    t = acts[0].shape[0]
    tm = _pick(t, tm, 8)
    nb, r, c = mats[0].shape
    win, wout = (r, c) if mode == "nn" else (c, r)
    dims = (_DIMS[mode], ((), ()))
    n_a, n_m = len(acts), len(mats)

    def body(*refs):
        a_vals = [ar[...].astype(BF16) for ar in refs[:n_a]]
        m_refs, o_refs = refs[n_a:n_a + n_m], refs[n_a + n_m:]
        for o_ref, terms in zip(o_refs, combos, strict=True):
            for j in range(nb):
                acc = None
                for ai, mi in terms:
                    part = lax.dot_general(a_vals[ai][:, j * win:(j + 1) * win], m_refs[mi][j], dims, preferred_element_type=F32)
                    acc = part if acc is None else acc + part
                o_ref[:, j * wout:(j + 1) * wout] = acc

    return pl.pallas_call(
        body, name=name, grid=(t // tm,),
        in_specs=[pl.BlockSpec((tm, nb * win), lambda i: (i, 0))] * n_a + [pl.BlockSpec(m.shape, lambda i: (0, 0, 0)) for m in mats],
        out_specs=[pl.BlockSpec((tm, nb * wout), lambda i: (i, 0))] * len(combos),
        out_shape=[jax.ShapeDtypeStruct((t, nb * wout), F32)] * len(combos),
        compiler_params=pltpu.CompilerParams(dimension_semantics=("parallel",)),
    )(*acts, *mats)


def _bd_grads(arrs, widths, pairs, *, name, tk=512):
    t = arrs[0].shape[0]
    tk = _pick(t, tk, 8)
    n_a = len(arrs)
    dims = (_DIMS["tn"], ((), ()))

    def body(*refs):
        vals = [ar[...].astype(BF16) for ar in refs[:n_a]]
        o_refs = refs[n_a:]
        @pl.when(pl.program_id(0) == 0)
        def _():
            for o_ref in o_refs:
                o_ref[...] = jnp.zeros_like(o_ref)

        for o_ref, (ai, bi) in zip(o_refs, pairs, strict=True):
            wa, wb = widths[ai], widths[bi]
            for j in range(SSM_BLOCKS):
                o_ref[j] += lax.dot_general(vals[ai][:, j * wa:(j + 1) * wa], vals[bi][:, j * wb:(j + 1) * wb], dims,
                                            preferred_element_type=F32)

    return pl.pallas_call(
        body, name=name, grid=(t // tk,),
        in_specs=[pl.BlockSpec((tk, SSM_BLOCKS * w), lambda k: (k, 0)) for w in widths],
        out_specs=[pl.BlockSpec((SSM_BLOCKS, widths[ai], widths[bi]), lambda k: (0, 0, 0)) for ai, bi in pairs],
        out_shape=[jax.ShapeDtypeStruct((SSM_BLOCKS, widths[ai], widths[bi]), F32) for ai, bi in pairs],
        compiler_params=pltpu.CompilerParams(dimension_semantics=("arbitrary",)),
    )(*arrs)


SCAN_LW = 512
SCAN_ROWS = 512
_DOUBLING = ((1, 0), (2, 1), (4, 2))


def _scan(xr, xi, pr, pi, dr, di, *, reverse, name, hr=None, hi=None):
    t, s = xr.shape
    lc = _pick(t, SCAN_ROWS, 8)
    nt, ngroups = t // lc, lc // 8
    with_da = hr is not None

    def tmap(l, tt):
        return ((nt - 1 - tt) if reverse else tt, l)

    x_spec = pl.BlockSpec((lc, SCAN_LW), tmap)
    tab_spec = pl.BlockSpec((8, SCAN_LW), lambda l, tt: (0, l))

    def body(*refs):
        xr_ref, xi_ref, pr_ref, pi_ref, dr_ref, di_ref = refs[:6]
        if with_da:
            hr_ref, hi_ref, or_ref, oi_ref, ar_ref, ai_ref, cr_ref, ci_ref = refs[6:]
        else:
            or_ref, oi_ref, cr_ref, ci_ref = refs[6:]
        tt = pl.program_id(1)

        @pl.when(tt == 0)
        def _():
            cr_ref[...] = jnp.zeros_like(cr_ref)
            ci_ref[...] = jnp.zeros_like(ci_ref)
            if with_da:
                ar_ref[...] = jnp.zeros_like(ar_ref)
                ai_ref[...] = jnp.zeros_like(ai_ref)

        sub = lax.broadcasted_iota(jnp.int32, (8, SCAN_LW), 0)
        pw_r, pw_i = pr_ref[...], pi_ref[...]

        def step(g, carry):
            g = (ngroups - 1 - g) if reverse else g
            r0 = pl.multiple_of(g * 8, 8)
            vr, vi = xr_ref[pl.ds(r0, 8), :], xi_ref[pl.ds(r0, 8), :]
            for shift, row in _DOUBLING:
                a_r, a_i = dr_ref[row:row + 1, :], di_ref[row:row + 1, :]
                if reverse:
                    keep = sub < 8 - shift
                    sr, si = pltpu.roll(vr, 8 - shift, 0), pltpu.roll(vi, 8 - shift, 0)
                else:
                    keep = sub >= shift
                    sr, si = pltpu.roll(vr, shift, 0), pltpu.roll(vi, shift, 0)
                sr, si = jnp.where(keep, sr, 0.0), jnp.where(keep, si, 0.0)
                vr, vi = vr + a_r * sr - a_i * si, vi + a_r * si + a_i * sr
            if with_da:
                cr, ci, acc_r, acc_i = carry
            else:
                cr, ci = carry
            vr, vi = vr + pw_r * cr - pw_i * ci, vi + pw_r * ci + pw_i * cr
            or_ref[pl.ds(r0, 8), :] = vr
            oi_ref[pl.ds(r0, 8), :] = vi
            if with_da:
                nr = jnp.where(sub < 7, pltpu.roll(vr, 7, 0), cr)
                ni = jnp.where(sub < 7, pltpu.roll(vi, 7, 0), ci)
                h_r, h_i = hr_ref[pl.ds(r0, 8), :], hi_ref[pl.ds(r0, 8), :]
                acc_r = acc_r + h_r * nr + h_i * ni
                acc_i = acc_i + h_r * ni - h_i * nr
            edge = 0 if reverse else 7
            cr = jnp.broadcast_to(vr[edge:edge + 1, :], vr.shape)
            ci = jnp.broadcast_to(vi[edge:edge + 1, :], vi.shape)
            return (cr, ci, acc_r, acc_i) if with_da else (cr, ci)

        zero = jnp.zeros((8, SCAN_LW), F32)
        init = (cr_ref[...], ci_ref[...]) + ((zero, zero) if with_da else ())
        fin = lax.fori_loop(0, ngroups, step, init, unroll=2)
        cr_ref[...] = fin[0]
        ci_ref[...] = fin[1]
        if with_da:
            ar_ref[...] += fin[2]
            ai_ref[...] += fin[3]

    n_x = 4 if with_da else 2
    out_specs = [x_spec, x_spec] + ([tab_spec, tab_spec] if with_da else [])
    out_shape = [jax.ShapeDtypeStruct((t, s), F32)] * 2 + ([jax.ShapeDtypeStruct((8, s), F32)] * 2 if with_da else [])
    operands = [xr, xi, pr, pi, dr, di] + ([hr, hi] if with_da else [])
    return pl.pallas_call(
        body, name=name, grid=(s // SCAN_LW, nt),
        in_specs=[x_spec, x_spec] + [tab_spec] * 4 + [x_spec] * (n_x - 2),
        out_specs=out_specs, out_shape=out_shape,
        scratch_shapes=[pltpu.VMEM((8, SCAN_LW), F32), pltpu.VMEM((8, SCAN_LW), F32)],
        compiler_params=pltpu.CompilerParams(dimension_semantics=("parallel", "arbitrary")),
    )(*operands)


CONV_TM = 256


def _conv_specs(t, tm):
    nrow = t // tm
    hb = tm // 8

    def col(cidx):
        return pl.BlockSpec((tm, BRANCH), lambda i: (i, cidx))

    def prev(cidx):
        return pl.BlockSpec((8, BRANCH), lambda i: (jnp.maximum(i * hb - 1, 0), cidx))

    def nxt(cidx):
        return pl.BlockSpec((8, BRANCH), lambda i: (jnp.minimum((i + 1) * hb, nrow * hb - 1), cidx))

    return nrow, col, prev, nxt


def _conv_taps(cc, cx, cc_prev, cx_prev, first):
    tm = cc.shape[0]
    v = cc * cx
    halo = cc_prev * cx_prev * jnp.where(first, 0.0, 1.0)
    ext = jnp.concatenate([halo, v], axis=0)
    return v, pltpu.roll(ext, 1, 0)[8:8 + tm], pltpu.roll(ext, 2, 0)[8:8 + tm]


def _conv_fwd(z, conv_w, name):
    t = z.shape[0]
    tm = _pick(t, CONV_TM, 8)
    nrow, col, prev, _ = _conv_specs(t, tm)

    def body(cb_ref, cc_ref, cx_ref, ccp_ref, cxp_ref, w_ref, o_ref):
        first = pl.program_id(0) == 0
        v, v1, v2 = _conv_taps(cc_ref[...], cx_ref[...], ccp_ref[...], cxp_ref[...], first)
        y = w_ref[0:1, :] * v2 + w_ref[1:2, :] * v1 + w_ref[2:3, :] * v
        o_ref[...] = (cb_ref[...] * y).astype(o_ref.dtype)

    return pl.pallas_call(
        body, name=name, grid=(nrow,),
        in_specs=[col(1), col(2), col(3), prev(2), prev(3), pl.BlockSpec((3, BRANCH), lambda i: (0, 0))],
        out_specs=pl.BlockSpec((tm, BRANCH), lambda i: (i, 0)), out_shape=jax.ShapeDtypeStruct((t, BRANCH), BF16),
        compiler_params=pltpu.CompilerParams(dimension_semantics=("parallel",)),
    )(z, z, z, z, z, conv_w)


def _conv_bwd(dyc, z, conv_w, name):
    t = z.shape[0]
    tm = _pick(t, CONV_TM, 8)
    nrow, col, prev, nxt = _conv_specs(t, tm)
    d_cur = pl.BlockSpec((tm, BRANCH), lambda i: (i, 0))
    d_nxt = pl.BlockSpec((8, BRANCH), lambda i: (jnp.minimum((i + 1) * (tm // 8), nrow * (tm // 8) - 1), 0))

    def body(dy_ref, dyn_ref, cb_ref, cbn_ref, cc_ref, cx_ref, ccp_ref, cxp_ref, w_ref, dcb_ref, dcc_ref, dcx_ref, dw_ref):
        i = pl.program_id(0)
        cc, cx, cb = cc_ref[...], cx_ref[...], cb_ref[...]
        v, v1, v2 = _conv_taps(cc, cx, ccp_ref[...], cxp_ref[...], i == 0)
        w0, w1, w2 = w_ref[0:1, :], w_ref[1:2, :], w_ref[2:3, :]
        y = w0 * v2 + w1 * v1 + w2 * v
        dyc_v = dy_ref[...]
        dcb_ref[...] = (dyc_v * y).astype(dcb_ref.dtype)
        dy = dyc_v * cb
        halo = dyn_ref[...] * cbn_ref[...] * jnp.where(i == nrow - 1, 0.0, 1.0)
        ext = jnp.concatenate([dy, halo], axis=0)
        dy1 = pltpu.roll(ext, tm + 8 - 1, 0)[0:tm]
        dy2 = pltpu.roll(ext, tm + 8 - 2, 0)[0:tm]
        dv = w2 * dy + w1 * dy1 + w0 * dy2
        dcc_ref[...] = (dv * cx).astype(dcc_ref.dtype)
        dcx_ref[...] = (dv * cc).astype(dcx_ref.dtype)
        dw = jnp.concatenate([jnp.sum(dy * v2, axis=0, keepdims=True), jnp.sum(dy * v1, axis=0, keepdims=True),
                              jnp.sum(dy * v, axis=0, keepdims=True), jnp.zeros((5, BRANCH), F32)], axis=0)

        @pl.when(i == 0)
        def _():
            dw_ref[...] = dw

        @pl.when(i > 0)
        def _():
            dw_ref[...] += dw

    o_spec = pl.BlockSpec((tm, BRANCH), lambda i: (i, 0))
    return pl.pallas_call(
        body, name=name, grid=(nrow,),
        in_specs=[d_cur, d_nxt, col(1), nxt(1), col(2), col(3), prev(2), prev(3), pl.BlockSpec((3, BRANCH), lambda i: (0, 0))],
        out_specs=[o_spec, o_spec, o_spec, pl.BlockSpec((8, BRANCH), lambda i: (0, 0))],
        out_shape=[jax.ShapeDtypeStruct((t, BRANCH), BF16)] * 3 + [jax.ShapeDtypeStruct((8, BRANCH), F32)],
        compiler_params=pltpu.CompilerParams(dimension_semantics=("arbitrary",)),
    )(dyc, dyc, z, z, z, z, z, z, conv_w)


ATTN_BLOCKS = 4
GROUP_ROWS = GQA * WINDOW


def _attn_specs(nblk):
    rows = nblk * WINDOW
    q_spec = pl.BlockSpec((N_Q, rows, HEAD_DIM), lambda n: (0, n, 0))
    kv_cur = pl.BlockSpec((N_KV, rows, HEAD_DIM), lambda n: (0, n, 0))
    kv_prev = pl.BlockSpec((N_KV, WINDOW, HEAD_DIM), lambda n: (0, jnp.maximum(n * nblk - 1, 0), 0))
    bias_spec = pl.BlockSpec((N_Q, WINDOW, 2 * WINDOW), lambda n: (0, 0, 0))
    sink_spec = pl.BlockSpec((N_Q, 1), lambda n: (0, 0))
    return q_spec, kv_cur, kv_prev, bias_spec, sink_spec


def _attn_valid(first_key):
    qi = lax.broadcasted_iota(jnp.int32, (GROUP_ROWS, 2 * WINDOW), 0) & (WINDOW - 1)
    kj = lax.broadcasted_iota(jnp.int32, (GROUP_ROWS, 2 * WINDOW), 1)
    dist = qi + WINDOW - kj
    return (dist >= 0) & (dist < WINDOW) & (kj >= first_key)


def _attn_masks(n):
    return _attn_valid(jnp.where(n > 0, 0, WINDOW)), _attn_valid(0)


def _blk(b):
    return slice(b * WINDOW, (b + 1) * WINDOW)


def _group(ref, h, b, width):
    return ref[GQA * h:GQA * (h + 1), _blk(b)].reshape(GROUP_ROWS, width)


def _keys(prev_ref, cur_ref, h, b):
    prev = prev_ref[h] if b == 0 else cur_ref[h, _blk(b - 1)]
    return jnp.concatenate([prev, cur_ref[h, _blk(b)]], axis=0)


def _group_sinks(s_ref, h):
    return jnp.concatenate([jnp.broadcast_to(s_ref[GQA * h + g:GQA * h + g + 1, :], (WINDOW, 1)) for g in range(GQA)], axis=0)


def _attn_probs(q, kc, bias, sink, valid):
    s = lax.dot_general(q, kc, (_DIMS["nt"], ((), ())), preferred_element_type=F32) * ATTN_SCALE + bias
    s = jnp.where(valid, s, NEG)
    m = jnp.maximum(jnp.max(s, axis=1, keepdims=True), sink)
    p = jnp.exp(s - m)
    e_sink = jnp.exp(sink - m)
    inv = 1.0 / (jnp.sum(p, axis=1, keepdims=True) + e_sink)
    return p * inv, e_sink * inv


def _attn_fwd(qh, kh, vh, bias, sinks, name):
    t = qh.shape[1]
    nblk = min(ATTN_BLOCKS, t // WINDOW)
    q_spec, kv_cur, kv_prev, bias_spec, sink_spec = _attn_specs(nblk)

    def body(q_ref, kp_ref, kc_ref, vp_ref, vc_ref, b_ref, s_ref, o_ref):
        masks = _attn_masks(pl.program_id(0))
        for b in range(nblk):
            for h in range(N_KV):
                kc, vc = _keys(kp_ref, kc_ref, h, b), _keys(vp_ref, vc_ref, h, b)
                w, _ = _attn_probs(_group(q_ref, h, b, HEAD_DIM), kc, _group(b_ref, h, 0, 2 * WINDOW), _group_sinks(s_ref, h),
                                   masks[min(b, 1)])
                o = jnp.dot(w.astype(BF16), vc, preferred_element_type=F32)
                o_ref[GQA * h:GQA * (h + 1), _blk(b)] = o.reshape(GQA, WINDOW, HEAD_DIM).astype(o_ref.dtype)

    return pl.pallas_call(
        body, name=name, grid=(t // (nblk * WINDOW),),
        in_specs=[q_spec, kv_prev, kv_cur, kv_prev, kv_cur, bias_spec, sink_spec],
        out_specs=q_spec, out_shape=jax.ShapeDtypeStruct((N_Q, t, HEAD_DIM), BF16),
        compiler_params=pltpu.CompilerParams(dimension_semantics=("parallel",)),
    )(qh, kh, kh, vh, vh, bias, sinks)


def _attn_bwd(qh, kh, vh, doh, bias, sinks, name):
    t = qh.shape[1]
    nblk = min(ATTN_BLOCKS, t // WINDOW)
    nsteps = t // (nblk * WINDOW)
    q_spec, kv_cur, kv_prev, bias_spec, sink_spec = _attn_specs(nblk)

    def body(q_ref, kp_ref, kc_ref, vp_ref, vc_ref, do_ref, b_ref, s_ref,
             dq_ref, dkc_ref, dkp_ref, dvc_ref, dvp_ref, db_ref, ds_ref):
        n = pl.program_id(0)
        masks = _attn_masks(n)

        @pl.when(n == 0)
        def _():
            db_ref[...] = jnp.zeros_like(db_ref)
            ds_ref[...] = jnp.zeros_like(ds_ref)

        dks = [[None] * nblk for _ in range(N_KV)]
        dvs = [[None] * nblk for _ in range(N_KV)]
        for b in range(nblk):
            for h in range(N_KV):
                kc, vc = _keys(kp_ref, kc_ref, h, b), _keys(vp_ref, vc_ref, h, b)
                heads = slice(GQA * h, GQA * (h + 1))
                q, do = _group(q_ref, h, b, HEAD_DIM), _group(do_ref, h, b, HEAD_DIM)
                w, w_sink = _attn_probs(q, kc, _group(b_ref, h, 0, 2 * WINDOW), _group_sinks(s_ref, h), masks[min(b, 1)])
                dw = lax.dot_general(do, vc, (_DIMS["nt"], ((), ())), preferred_element_type=F32)
                delta = jnp.sum(w * dw, axis=1, keepdims=True)
                dscore = w * (dw - delta)
                ds_ref[heads] += (-w_sink * delta).reshape(GQA, WINDOW, 1)
                db_ref[heads] += dscore.reshape(GQA, WINDOW, 2 * WINDOW)
                dsb = dscore.astype(BF16)
                dq_ref[heads, _blk(b)] = (jnp.dot(dsb, kc, preferred_element_type=F32) * ATTN_SCALE).reshape(GQA, WINDOW, HEAD_DIM)
                dks[h][b] = lax.dot_general(dsb, q, (_DIMS["tn"], ((), ())), preferred_element_type=F32) * ATTN_SCALE
                dvs[h][b] = lax.dot_general(w.astype(BF16), do, (_DIMS["tn"], ((), ())), preferred_element_type=F32)
        for parts, cur_ref, prev_ref in ((dks, dkc_ref, dkp_ref), (dvs, dvc_ref, dvp_ref)):
            for h in range(N_KV):
                prev_ref[h] = parts[h][0][0:WINDOW]
                for b in range(nblk):
                    own = parts[h][b][WINDOW:2 * WINDOW]
                    cur_ref[h, _blk(b)] = own + parts[h][b + 1][0:WINDOW] if b + 1 < nblk else own

    kv_shape = jax.ShapeDtypeStruct((N_KV, t, HEAD_DIM), F32)
    kv_prev_out = pl.BlockSpec((N_KV, WINDOW, HEAD_DIM), lambda n: (0, n, 0))
    kv_prev_shape = jax.ShapeDtypeStruct((N_KV, nsteps * WINDOW, HEAD_DIM), F32)
    return pl.pallas_call(
        body, name=name, grid=(nsteps,),
        in_specs=[q_spec, kv_prev, kv_cur, kv_prev, kv_cur, q_spec, bias_spec, sink_spec],
        out_specs=[q_spec, kv_cur, kv_prev_out, kv_cur, kv_prev_out, bias_spec, pl.BlockSpec((N_Q, WINDOW, 1), lambda n: (0, 0, 0))],
        out_shape=[jax.ShapeDtypeStruct((N_Q, t, HEAD_DIM), F32), kv_shape, kv_prev_shape, kv_shape, kv_prev_shape,
                   jax.ShapeDtypeStruct((N_Q, WINDOW, 2 * WINDOW), F32), jax.ShapeDtypeStruct((N_Q, WINDOW, 1), F32)],
        compiler_params=pltpu.CompilerParams(dimension_semantics=("arbitrary",)),
    )(qh, kh, kh, vh, vh, doh, bias, sinks)


def _heads(a, n_heads):
    t = a.shape[0]
    return a.astype(BF16).reshape(t, n_heads, HEAD_DIM).transpose(1, 0, 2)


def _unheads(a):
    n_heads, t, _ = a.shape
    return a.transpose(1, 0, 2).reshape(t, n_heads * HEAD_DIM)


def _shift_blocks(cur, prev):
    n_kv, t, d = cur.shape
    nsteps = prev.shape[1] // WINDOW
    nblk = t // (nsteps * WINDOW)
    late = jnp.concatenate([prev.reshape(n_kv, nsteps, WINDOW, d)[:, 1:], jnp.zeros((n_kv, 1, WINDOW, d), cur.dtype)], axis=1)
    delta = jnp.concatenate([jnp.zeros((n_kv, nsteps, nblk - 1, WINDOW, d), cur.dtype), late[:, :, None]], axis=2)
    return (cur.reshape(n_kv, nsteps, nblk, WINDOW, d) + delta).reshape(n_kv, t, d)


def _t5_bucket_table():
    qi = np.arange(WINDOW)[:, None]
    kj = np.arange(2 * WINDOW)[None, :]
    dist = np.clip(qi + WINDOW - kj, 0, REL_MAX_DIST - 1)
    exact = REL_BUCKETS // 2
    df = np.maximum(dist, 1).astype(np.float32)
    large = exact + (np.log(df / np.float32(exact)) / np.float32(math.log(REL_MAX_DIST / exact)) * (REL_BUCKETS - exact)).astype(np.int32)
    large = np.minimum(large, REL_BUCKETS - 1)
    bucket = np.where(dist < exact, dist, large)
    onehot = np.zeros((WINDOW * 2 * WINDOW, REL_BUCKETS), np.float32)
    onehot[np.arange(WINDOW * 2 * WINDOW), bucket.reshape(-1)] = 1.0
    return onehot


def _band_bias(rel_bias):
    onehot = jnp.asarray(_t5_bucket_table())
    sel = jnp.sum(onehot[:, :, None] * rel_bias[None, :, :], axis=1)
    return sel.T.reshape(N_Q, WINDOW, 2 * WINDOW)


def _block_diag(a):
    g, r, c = a.shape
    a4 = a.reshape(SSM_BLOCKS, g // SSM_BLOCKS, r, c)
    eye = jnp.eye(g // SSM_BLOCKS, dtype=a.dtype)
    full = a4[:, :, :, None, :] * eye[None, :, None, :, None]
    return full.reshape(SSM_BLOCKS, (g // SSM_BLOCKS) * r, (g // SSM_BLOCKS) * c)


def _ssm_disc(lam_re, lam_im, b_re, b_im, c_re, c_im, log_dt):
    dt = jnp.exp(log_dt)[:, None]
    mag = jnp.exp(lam_re * dt)
    ang = lam_im * dt
    a_re = mag * jnp.cos(ang)
    a_im = mag * jnp.sin(ang)
    den = lam_re * lam_re + lam_im * lam_im
    nr = a_re - 1.0
    coef_re = (nr * lam_re + a_im * lam_im) / den
    coef_im = (a_im * lam_re - nr * lam_im) / den
    bb_re = coef_re[..., None] * b_re - coef_im[..., None] * b_im
    bb_im = coef_re[..., None] * b_im + coef_im[..., None] * b_re
    wb_re = _block_diag(jnp.swapaxes(bb_re, 1, 2))
    wb_im = _block_diag(jnp.swapaxes(bb_im, 1, 2))
    cm_re = _block_diag(jnp.swapaxes(c_re, 1, 2))
    cm_imn = _block_diag(-jnp.swapaxes(c_im, 1, 2))
    return a_re.reshape(-1), a_im.reshape(-1), wb_re, wb_im, cm_re, cm_imn


def _scan_tables(a_re, a_im):
    pr, pi = [a_re], [a_im]
    for _ in range(7):
        pr, pi = pr + [pr[-1] * a_re - pi[-1] * a_im], pi + [pr[-1] * a_im + pi[-1] * a_re]
    pr, pi = jnp.stack(pr), jnp.stack(pi)
    pad = jnp.zeros((5,) + a_re.shape, F32)
    dr = jnp.concatenate([jnp.stack([pr[0], pr[1], pr[3]]), pad])
    di = jnp.concatenate([jnp.stack([pi[0], pi[1], pi[3]]), pad])
    fwd = (pr, pi, dr, di)
    rev = (pr[::-1], -pi[::-1], dr, -di)
    return jax.tree.map(lax.stop_gradient, (fwd, rev))


def _gate_col(r):
    return lambda j: (OFF_G + r * D_MODEL) // 256 + j


def _layer_fwd(x, p_i, w, bias, li):
    nm = lambda s: f"{s}_l{li}"
    h = _rms_fwd(x, w["norm_mix"], nm("rms_mix"))
    z = _mm(h, w["w_in"], "nn", tm=512, tn=2944, name=nm("mm_in"))
    bu_re, bu_im = _bd_apply([z], [w["wb_re"], w["wb_im"]], [[(0, 0)], [(0, 1)]], "nn", name=nm("ssm_bu"))
    h_re, h_im = _scan(bu_re, bu_im, *w["scan_fwd"], reverse=False, name=nm("ssm_scan"))
    (y0,) = _bd_apply([h_re, h_im], [w["cm_re"], w["cm_imn"]], [[(0, 0), (1, 1)]], "nn", name=nm("ssm_c"))
    (y1,) = _rw(lambda a, u, d: ([jax.nn.gelu(a + d * u)], []),
                [(y0, None, _c0), (z, BRANCH, _c0)], [(BRANCH, None, F32, _c0)],
                params=[w["ssm_d"]], name=nm("ssm_gelu"))
    gl = _mm(y1, w["ssm_w_glu"], "nn", name=nm("mm_glu"))
    (y_ssm,) = _rw(lambda a, b: ([a * jax.nn.sigmoid(b)], []), [(y1, None, _c0), (gl, None, _c0)],
                   [(BRANCH, None, BF16, _c0)], name=nm("ssm_glu"))
    y_conv = _conv_fwd(z, w["conv_w"], nm("conv_fwd"))
    kv_w = N_KV * HEAD_DIM
    q2, k2, v2 = _rw(lambda q, k, v: ([q, k, v], []),
                     [(z, BRANCH, lambda j: OFF_Q // BRANCH), (z, kv_w, lambda j: OFF_K // kv_w), (z, kv_w, lambda j: OFF_V // kv_w)],
                     [(BRANCH, None, BF16, _c0), (kv_w, None, BF16, _c0), (kv_w, None, BF16, _c0)], name=nm("qkv_bf16"))
    qh, kh, vh = _heads(q2, N_Q), _heads(k2, N_KV), _heads(v2, N_KV)
    y_attn = _unheads(_attn_fwd(qh, kh, vh, bias, w["sinks"], nm("attn_fwd")))
    ys = (y_ssm, y_conv, y_attn)
    bs = [_mm(ys[r], w["w_branch"][r], "nn", name=nm(f"mm_branch{r}")) for r in range(3)]

    def merge(g0, g1, g2, b0, b1, b2):
        return [jax.nn.sigmoid(g0) * b0 + jax.nn.sigmoid(g1) * b1 + jax.nn.sigmoid(g2) * b2], []

    (merged,) = _rw(merge, [(z, 256, _gate_col(r)) for r in range(3)] + [(b, 256, lambda j: j) for b in bs],
                    [(D_MODEL, 256, BF16, lambda j: j)], ncol=4, name=nm("merge"))
    x1 = _mm(merged, w["w_out"], "nn", add=x, name=nm("mm_out"))
    hf_in = _rms_fwd(x1, w["norm_ffn"], nm("rms_ffn"))
    hf = _mm(hf_in, w["w_ffn_in"], "nn", tn=1408, name=nm("mm_ffn_in"))
    (act,) = _rw(lambda a, b: ([jax.nn.silu(a) * b], []), [(hf, FFN_COLS, lambda j: j), (hf, FFN_COLS, lambda j: FFN_NCOL + j)],
                 [(FFN_HIDDEN, FFN_COLS, BF16, lambda j: j)], ncol=FFN_NCOL, name=nm("swiglu"))
    x2 = _mm(act, w["w_ffn_out"], "nn", add=x1, tk=1408, name=nm("mm_ffn_out"))
    hp = _rms_fwd(x2, w["norm_ple"], nm("rms_ple"))
    pgl = _mm(hp, w["w_ple_gate"], "nn", name=nm("mm_ple_gate"))
    pp = _mm(p_i, w["w_ple_proj"], "nn", name=nm("mm_ple_proj"))
    (x3,) = _rw(lambda xv, a, b: ([xv + jax.nn.sigmoid(a) * b], []), [(x2, None, _c0), (pgl, None, _c0), (pp, None, _c0)],
                [(D_MODEL, None, F32, _c0)], name=nm("ple_add"))
    saved = dict(x=x, p=p_i, h=h, z=z, h_re=h_re, h_im=h_im, y0=y0, y1=y1, gl=gl, ys=ys, qh=qh, kh=kh, vh=vh,
                 bs=bs, merged=merged, x1=x1, hf_in=hf_in, hf=hf, act=act, x2=x2, hp=hp, pgl=pgl, pp=pp)
    return x3, saved


def _layer_bwd(dx3, s, w, bias, li):
    nm = lambda n: f"{n}_l{li}"
    g = {}
    z = s["z"]
    def ple_b(d, a, b):
        _, vjp = jax.vjp(lambda a_, b_: jax.nn.sigmoid(a_) * b_, a, b)
        return list(vjp(d)), []

    dpgl, dpp = _rw(ple_b, [(dx3, None, _c0), (s["pgl"], None, _c0), (s["pp"], None, _c0)],
                    [(D_MODEL, None, BF16, _c0)] * 2, name=nm("ple_bwd"))
    g["w_ple_proj"] = _mm(s["p"], dpp, "tn", name=nm("mmg_ple_proj"))
    g["w_ple_gate"] = _mm(s["hp"], dpgl, "tn", name=nm("mmg_ple_gate"))
    dhp = _mm(dpgl, w["w_ple_gate"], "nt", name=nm("mmb_ple_gate"))
    dx2, g["norm_ple"] = _rms_bwd(s["x2"], dhp, dx3, w["norm_ple"], nm("rmsb_ple"))
    dact = _mm(dx2, w["w_ffn_out"], "nt", tn=1408, name=nm("mmb_ffn_out"))
    g["w_ffn_out"] = _mm(s["act"], dx2, "tn", tm=1408, name=nm("mmg_ffn_out"))

    def swiglu_b(a, b, d):
        _, vjp = jax.vjp(lambda a_, b_: jax.nn.silu(a_) * b_, a, b)
        return list(vjp(d)), []

    dhf_a, dhf_b = _rw(swiglu_b, [(s["hf"], FFN_COLS, lambda j: j), (s["hf"], FFN_COLS, lambda j: FFN_NCOL + j), (dact, FFN_COLS, lambda j: j)],
                       [(FFN_HIDDEN, FFN_COLS, BF16, lambda j: j)] * 2, ncol=FFN_NCOL, name=nm("swiglu_bwd"))
    g["w_ffn_in"] = jnp.concatenate([_mm(s["hf_in"], dhf_a, "tn", tn=1408, name=nm("mmg_ffn_in_a")),
                                     _mm(s["hf_in"], dhf_b, "tn", tn=1408, name=nm("mmg_ffn_in_b"))], axis=1)
    dhf_in = _mm(dhf_a, w["w_ffn_in"], "nt", tk=1408, name=nm("mmb_ffn_in_a"))
    dhf_in = _mm(dhf_b, w["w_ffn_in"], "nt", tk=1408, b_k0=FFN_HIDDEN, add=dhf_in, name=nm("mmb_ffn_in_b"))
    dx1, g["norm_ffn"] = _rms_bwd(s["x1"], dhf_in, dx2, w["norm_ffn"], nm("rmsb_ffn"))
    dmerged = _mm(dx1, w["w_out"], "nt", name=nm("mmb_out"))
    g["w_out"] = _mm(s["merged"], dx1, "tn", name=nm("mmg_out"))

    def merge_b(d, g0, g1, g2, b0, b1, b2):
        outs_g, outs_b = [], []
        for gate, br in ((g0, b0), (g1, b1), (g2, b2)):
            sg = jax.nn.sigmoid(gate)
            outs_g.append(d * br * sg * (1.0 - sg))
            outs_b.append(d * sg)
        return outs_g + outs_b, []

    res = _rw(merge_b, [(dmerged, 256, lambda j: j)] + [(z, 256, _gate_col(r)) for r in range(3)] + [(b, 256, lambda j: j) for b in s["bs"]],
              [(D_MODEL, 256, BF16, lambda j: j)] * 6, ncol=4, name=nm("merge_bwd"))
    dgates, dbs = res[:3], res[3:]
    dys = [_mm(dbs[r], w["w_branch"][r], "nt", name=nm(f"mmb_branch{r}")) for r in range(3)]
    g["w_branch"] = jnp.stack([_mm(s["ys"][r], dbs[r], "tn", name=nm(f"mmg_branch{r}")) for r in range(3)])
    doh = _heads(dys[2], N_Q)
    dqh, dkc, dkp, dvc, dvp, dbias, dsink = _attn_bwd(s["qh"], s["kh"], s["vh"], doh, bias, w["sinks"], nm("attn_bwd"))
    dq, dk, dv = _unheads(dqh), _unheads(_shift_blocks(dkc, dkp)), _unheads(_shift_blocks(dvc, dvp))
    g["sinks"] = jnp.sum(dsink, axis=(1, 2))
    dcb, dcc, dcx, dconv = _conv_bwd(dys[1], z, w["conv_w"], nm("conv_bwd"))
    g["conv_w"] = dconv[0:3]
    def glu_b(d, y1, gl):
        sg = jax.nn.sigmoid(gl)
        return [d * y1 * sg * (1.0 - sg), d * sg], []

    dgl, dy1a = _rw(glu_b, [(dys[0], None, _c0), (s["y1"], None, _c0), (s["gl"], None, _c0)],
                    [(BRANCH, None, BF16, _c0), (BRANCH, None, F32, _c0)], name=nm("ssm_glu_bwd"))
    g["ssm_w_glu"] = _mm(s["y1"], dgl, "tn", name=nm("mmg_glu"))
    dy1b = _mm(dgl, w["ssm_w_glu"], "nt", name=nm("mmb_glu"))

    def gelu_b(da, db, a, u, d):
        _, vjp = jax.vjp(lambda pre: jax.nn.gelu(pre), a + d * u)
        (dy0,) = vjp(da + db)
        return [dy0, dy0 * d], [jnp.sum(dy0 * u, axis=0, keepdims=True)]

    dy0, du_a, g["ssm_d"] = _rw(gelu_b, [(dy1a, None, _c0), (dy1b, None, _c0), (s["y0"], None, _c0), (z, BRANCH, _c0)],
                                [(BRANCH, None, BF16, _c0), (BRANCH, None, F32, _c0)], params=[w["ssm_d"]],
                                reds=[((1, BRANCH), None, _c0)], name=nm("ssm_gelu_bwd"))
    dh_re, dh_im = _bd_apply([dy0], [w["cm_re"], w["cm_imn"]], [[(0, 0)], [(0, 1)]], "nt", name=nm("ssmb_c"))
    sb, cb = SSM_STATES // SSM_BLOCKS, BRANCH // SSM_BLOCKS
    g["cm_re"], g["cm_imn"] = _bd_grads([s["h_re"], s["h_im"], dy0], [sb, sb, cb], [(0, 2), (1, 2)], name=nm("ssmg_c"))
    l_re, l_im, da_re, da_im = _scan(dh_re, dh_im, *w["scan_rev"], reverse=True, hr=s["h_re"], hi=s["h_im"], name=nm("ssm_scan_bwd"))
    g["a_re"], g["a_im"] = jnp.sum(da_re, axis=0), jnp.sum(da_im, axis=0)
    (du_b,) = _bd_apply([l_re, l_im], [w["wb_re"], w["wb_im"]], [[(0, 0), (1, 1)]], "nt", name=nm("ssmb_bu"))
    g["wb_re"], g["wb_im"] = _bd_grads([z, l_re, l_im], [cb, sb, sb], [(0, 1), (0, 2)], name=nm("ssmg_bu"))
    dz = jnp.concatenate([(du_a + du_b).astype(BF16), dcb, dcc, dcx, dq.astype(BF16), dk.astype(BF16), dv.astype(BF16)] + list(dgates), axis=1)
    g["w_in"] = _mm(s["h"], dz, "tn", tm=512, tn=2944, name=nm("mmg_in"))
    dh = _mm(dz, w["w_in"], "nt", tk=2944, name=nm("mmb_in"))
    dx, g["norm_mix"] = _rms_bwd(s["x"], dh, dx1, w["norm_mix"], nm("rmsb_mix"))
    return dx, g, dbias


def _loss_and_seed(x, target, g_final):
    def fn(xv, tv, gv):
        y, vjp = jax.vjp(_rms, xv, gv)
        err = y - tv
        dx, dg = vjp(err * (1.0 / D_MODEL))
        return [dx], [jnp.sum(err * err, axis=0, keepdims=True) * (0.5 / D_MODEL), dg]

    return _rw(fn, [(x, None, _c0), (target, None, _c0)], [(D_MODEL, None, F32, _c0)], params=[g_final],
               reds=[((1, D_MODEL), None, _c0)] * 2, name="loss_head")


def _local_step(x, p, target, wt):
    bias, bias_vjp = jax.vjp(_band_bias, wt["rel_bias"])
    layers, disc_vjps = [], []
    for i in range(DEPTH):
        ssm_p = [wt[k][i] for k in ("ssm_lambda_re", "ssm_lambda_im", "ssm_b_re", "ssm_b_im", "ssm_c_re", "ssm_c_im", "ssm_log_dt")]
        (a_re, a_im, wb_re, wb_im, cm_re, cm_imn), disc_vjp = jax.vjp(_ssm_disc, *ssm_p)
        scan_fwd, scan_rev = _scan_tables(a_re, a_im)
        layers.append(dict(
            norm_mix=wt["norm_mix"][i][None], w_in=wt["w_in"][i], wb_re=wb_re.astype(BF16), wb_im=wb_im.astype(BF16),
            cm_re=cm_re.astype(BF16), cm_imn=cm_imn.astype(BF16), scan_fwd=scan_fwd, scan_rev=scan_rev,
            ssm_d=wt["ssm_d"][i][None], ssm_w_glu=wt["ssm_w_glu"][i], conv_w=wt["conv_w"][i],
            sinks=wt["attn_sinks"][i][:, None], w_branch=wt["w_branch"][i], w_out=wt["w_out"][i],
            norm_ffn=wt["norm_ffn"][i][None], w_ffn_in=wt["w_ffn_in"][i], w_ffn_out=wt["w_ffn_out"][i],
            norm_ple=wt["norm_ple"][i][None], w_ple_gate=wt["w_ple_gate"][i], w_ple_proj=wt["w_ple_proj"][i]))
        disc_vjps.append(disc_vjp)

    saved = []
    for i in range(DEPTH):
        x, s = _layer_fwd(x, p[i], layers[i], bias, i)
        saved.append(s)
    dx, loss_cols, g_final = _loss_and_seed(x, target, wt["norm_final"][None])
    loss = jnp.sum(loss_cols)

    per_layer = [None] * DEPTH
    dbias = None
    for i in reversed(range(DEPTH)):
        dx, g, db = _layer_bwd(dx, saved[i], layers[i], bias, i)
        dbias = db if dbias is None else dbias + db
        (g["ssm_lambda_re"], g["ssm_lambda_im"], g["ssm_b_re"], g["ssm_b_im"], g["ssm_c_re"], g["ssm_c_im"], g["ssm_log_dt"]) = \
            disc_vjps[i]((g.pop("a_re"), g.pop("a_im"), g.pop("wb_re"), g.pop("wb_im"), g.pop("cm_re"), g.pop("cm_imn")))
        g["attn_sinks"] = g.pop("sinks")
        for k in ("norm_mix", "norm_ffn", "norm_ple", "ssm_d"):
            g[k] = g[k][0]
        per_layer[i] = g
    big_names = [name for name, _, _ in BIG]
    big = {k: [per_layer[i][k] for i in range(DEPTH)] for k in big_names}
    small = {k: jnp.stack([per_layer[i][k] for i in range(DEPTH)]) for k in per_layer[0] if k not in big_names}
    (small["rel_bias"],) = bias_vjp(dbias)
    small["norm_final"] = g_final[0]
    return loss, dx, small, big


HBM_SPEC = pl.BlockSpec(memory_space=pltpu.HBM)


def _position():
    x, y, c = lax.axis_index("x"), lax.axis_index("y"), lax.axis_index("c")
    other_chips = [(1 - x, y), (x, 1 - y), (1 - x, 1 - y)]
    return x, y, c, other_chips


def _row_chunks(rows, n=COPY_CHUNKS):
    rq = rows // n
    assert rq * n == rows and rq % 16 == 0, rows
    return [pl.ds(q * rq, rq) for q in range(n)]


def _place(buf, val, idx, name):
    n, rows, width = buf.shape
    tm = _pick(rows, 512, 16)

    def body(idx_ref, buf_ref, v_ref, o_ref):
        o_ref[0] = v_ref[...]

    grid_spec = pltpu.PrefetchScalarGridSpec(
        num_scalar_prefetch=1, grid=(rows // tm,),
        in_specs=[pl.BlockSpec(memory_space=pl.ANY), pl.BlockSpec((tm, width), lambda i, idx_ref: (i, 0))],
        out_specs=pl.BlockSpec((1, tm, width), lambda i, idx_ref: (idx_ref[0], i, 0)))
    return pl.pallas_call(
        body, name=name, grid_spec=grid_spec, out_shape=jax.ShapeDtypeStruct(buf.shape, buf.dtype), input_output_aliases={1: 0},
        compiler_params=pltpu.CompilerParams(dimension_semantics=("arbitrary",)),
    )(jnp.asarray(idx, jnp.int32).reshape(1), buf, val)


def _allgather_weights(locals_):
    nb, nq = len(locals_), COPY_CHUNKS
    chunks = [_row_chunks(a.shape[1]) for a in locals_]

    def body(*refs):
        w_refs, out_refs = refs[:nb], refs[nb:2 * nb]
        send_sems, recv_sems = refs[2 * nb:]
        x, y, c, chips = _position()
        me = 2 * x + y
        sibling = (x, y, 1 - c)

        def copy(b, kind, q, src, dst, to):
            k = (b * 6 + kind) * nq + q
            return pltpu.make_async_remote_copy(src_ref=src, dst_ref=dst, send_sem=send_sems.at[k], recv_sem=recv_sems.at[k],
                                                device_id=to, device_id_type=MESH)

        first = [copy(b, j, q, w_refs[b].at[c, chunks[b][q]], out_refs[b].at[me, c, chunks[b][q]], (*chip, c))
                 for q in range(nq) for b in range(nb) for j, chip in enumerate(chips)]
        for cp in first:
            cp.start()
        passed = []
        for q in range(nq):
            for b in range(nb):
                for j, (px, py) in enumerate(chips):
                    landed = out_refs[b].at[2 * px + py, c, chunks[b][q]]
                    copy(b, j, q, landed, landed, (px, py, c)).wait_recv()
                    fwd = copy(b, 3 + j, q, landed, landed, sibling)
                    fwd.start()
                    passed.append(fwd)
        for q in range(nq):
            for b in range(nb):
                for j, (px, py) in enumerate(chips):
                    landed = out_refs[b].at[2 * px + py, 1 - c, chunks[b][q]]
                    copy(b, 3 + j, q, landed, landed, sibling).wait_recv()
        for cp in first + passed:
            cp.wait_send()

    return pl.pallas_call(
        body, name="allgather_weights", in_specs=[HBM_SPEC] * nb, out_specs=[HBM_SPEC] * nb,
        out_shape=[jax.ShapeDtypeStruct((N_SHARD,) + a.shape, a.dtype) for a in locals_],
        scratch_shapes=[pltpu.SemaphoreType.DMA((nb * 6 * nq,)), pltpu.SemaphoreType.DMA((nb * 6 * nq,))],
    )(*locals_)


def _sibling_exchange(bufs):
    nb, nq, ns = len(bufs), COPY_CHUNKS, N_SHARD
    chunks = [_row_chunks(a.shape[2]) for a in bufs]

    def body(*refs):
        g_refs, got_refs = refs[:nb], refs[nb:2 * nb]
        send_sems, recv_sems = refs[2 * nb:]
        x, y, c, _ = _position()
        swaps = [pltpu.make_async_remote_copy(src_ref=g_refs[b].at[s, 1 - c, chunks[b][q]], dst_ref=got_refs[b].at[s, chunks[b][q]],
                                              send_sem=send_sems.at[(b * ns + s) * nq + q], recv_sem=recv_sems.at[(b * ns + s) * nq + q],
                                              device_id=(x, y, 1 - c), device_id_type=MESH)
                 for b in range(nb) for s in range(ns) for q in range(nq)]
        for cp in swaps:
            cp.start()
        for cp in swaps:
            cp.wait()

    return pl.pallas_call(
        body, name="grad_sibling_exchange", in_specs=[HBM_SPEC] * nb, out_specs=[HBM_SPEC] * nb,
        out_shape=[jax.ShapeDtypeStruct((ns,) + a.shape[2:], a.dtype) for a in bufs],
        scratch_shapes=[pltpu.SemaphoreType.DMA((nb * ns * nq,)), pltpu.SemaphoreType.DMA((nb * ns * nq,))],
    )(*bufs)


def _chip_exchange(parts):
    nb, nq = len(parts), COPY_CHUNKS
    chunks = [_row_chunks(a.shape[1]) for a in parts]

    def body(*refs):
        b_refs, got_refs = refs[:nb], refs[nb:2 * nb]
        send_sems, recv_sems = refs[2 * nb:]
        x, y, c, chips = _position()
        sends = [pltpu.make_async_remote_copy(src_ref=b_refs[b].at[2 * px + py, chunks[b][q]], dst_ref=got_refs[b].at[j, chunks[b][q]],
                                              send_sem=send_sems.at[(b * 3 + j) * nq + q], recv_sem=recv_sems.at[(b * 3 + j) * nq + q],
                                              device_id=(px, py, c), device_id_type=MESH)
                 for q in range(nq) for b in range(nb) for j, (px, py) in enumerate(chips)]
        for cp in sends:
            cp.start()
        for cp in sends:
            cp.wait()

    return pl.pallas_call(
        body, name="grad_chip_exchange", in_specs=[HBM_SPEC] * nb, out_specs=[HBM_SPEC] * nb,
        out_shape=[jax.ShapeDtypeStruct((N_SHARD - 1,) + a.shape[1:], a.dtype) for a in parts],
        scratch_shapes=[pltpu.SemaphoreType.DMA((nb * 3 * nq,)), pltpu.SemaphoreType.DMA((nb * 3 * nq,))],
    )(*parts)


def _sibling_gather(halves):
    nb, nq = len(halves), 2 * COPY_CHUNKS
    chunks = [_row_chunks(a.shape[0], nq) for a in halves]

    def body(*refs):
        h_refs, out_refs = refs[:nb], refs[nb:2 * nb]
        send_sems, recv_sems = refs[2 * nb:]
        x, y, c, _ = _position()

        def chunk(b, q, half_idx):
            return pltpu.make_async_remote_copy(src_ref=h_refs[b].at[chunks[b][q]], dst_ref=out_refs[b].at[half_idx, chunks[b][q]],
                                                send_sem=send_sems.at[b * nq + q], recv_sem=recv_sems.at[b * nq + q],
                                                device_id=(x, y, 1 - c), device_id_type=MESH)

        pushes = [chunk(b, q, c) for b in range(nb) for q in range(nq)]
        for cp in pushes:
            cp.start()
        for b in range(nb):
            for q in range(nq):
                chunk(b, q, 1 - c).wait_recv()
        for cp in pushes:
            cp.wait_send()

    return pl.pallas_call(
        body, name="grad_sibling_gather", in_specs=[HBM_SPEC] * nb, out_specs=[HBM_SPEC] * nb,
        out_shape=[jax.ShapeDtypeStruct((2,) + a.shape, a.dtype) for a in halves],
        scratch_shapes=[pltpu.SemaphoreType.DMA((nb * nq,)), pltpu.SemaphoreType.DMA((nb * nq,))],
    )(*halves)


def _gather_partials(part):
    r, lanes = part.shape

    def body(p_ref, out_ref, send_sems, recv_sems):
        x, y, c, _ = _position()
        flips = [(fx, fy, fc) for fx in (0, 1) for fy in (0, 1) for fc in (0, 1)][1:]
        sends = []
        for k, (fx, fy, fc) in enumerate(flips):
            cp = pltpu.make_async_remote_copy(src_ref=p_ref, dst_ref=out_ref.at[4 * x + 2 * y + c], send_sem=send_sems.at[k],
                                              recv_sem=recv_sems.at[k], device_id=(x ^ fx, y ^ fy, c ^ fc), device_id_type=MESH)
            cp.start()
            sends.append(cp)
        for k, (fx, fy, fc) in enumerate(flips):
            src = out_ref.at[4 * (x ^ fx) + 2 * (y ^ fy) + (c ^ fc)]
            pltpu.make_async_remote_copy(src_ref=src, dst_ref=src, send_sem=send_sems.at[k], recv_sem=recv_sems.at[k],
                                         device_id=(x ^ fx, y ^ fy, c ^ fc), device_id_type=MESH).wait_recv()
        for cp in sends:
            cp.wait_send()

    return pl.pallas_call(
        body, name="small_gather_partials", in_specs=[HBM_SPEC], out_specs=HBM_SPEC,
        out_shape=jax.ShapeDtypeStruct((8, r, lanes), part.dtype),
        scratch_shapes=[pltpu.SemaphoreType.DMA((7,)), pltpu.SemaphoreType.DMA((7,))],
    )(part)


def _sum_leading(stack, name, also_bf16=False):
    k, r, lanes = stack.shape
    tm = _pick(r, 256, 16)
    outs = [jax.ShapeDtypeStruct((r, lanes), F32)] + ([jax.ShapeDtypeStruct((r, lanes), BF16)] if also_bf16 else [])

    def body(s_ref, *o_refs):
        acc = s_ref[0].astype(F32)
        for i in range(1, k):
            acc = acc + s_ref[i].astype(F32)
        for o in o_refs:
            o[...] = acc.astype(o.dtype)

    spec = pl.BlockSpec((tm, lanes), lambda i: (i, 0))
    return pl.pallas_call(
        body, name=name, grid=(r // tm,), in_specs=[pl.BlockSpec((k, tm, lanes), lambda i: (0, i, 0))],
        out_specs=[spec] * len(outs), out_shape=outs,
        compiler_params=pltpu.CompilerParams(dimension_semantics=("parallel",)),
    )(stack)


def _add_pair(g2, got, half, name):
    ns, _, rows, width = g2.shape
    tm = _pick(rows, 256, 16)
    spec = pl.BlockSpec((1, tm, width), lambda s, i, h_ref: (s, i, 0))

    def body(h_ref, a_ref, b_ref, f_ref, o_ref):
        acc = a_ref[0] + b_ref[...]
        f_ref[...] = acc
        o_ref[...] = acc.astype(BF16)

    grid_spec = pltpu.PrefetchScalarGridSpec(
        num_scalar_prefetch=1, grid=(ns, rows // tm),
        in_specs=[pl.BlockSpec((1, 1, tm, width), lambda s, i, h_ref: (s, h_ref[0], i, 0)), spec], out_specs=[spec, spec])
    return pl.pallas_call(
        body, name=name, grid_spec=grid_spec,
        out_shape=[jax.ShapeDtypeStruct(got.shape, F32), jax.ShapeDtypeStruct(got.shape, BF16)],
        compiler_params=pltpu.CompilerParams(dimension_semantics=("parallel", "parallel")),
    )(jnp.asarray(half, jnp.int32).reshape(1), g2, got)


def _add_own(parts, got, mine, name):
    _, rows, width = parts.shape
    tm = _pick(rows, 256, 16)

    def body(m_ref, p_ref, g_ref, out_ref):
        acc = p_ref[0]
        for j in range(g_ref.shape[0]):
            acc = acc + g_ref[j].astype(F32)
        out_ref[...] = acc

    grid_spec = pltpu.PrefetchScalarGridSpec(
        num_scalar_prefetch=1, grid=(rows // tm,),
        in_specs=[pl.BlockSpec((1, tm, width), lambda i, m_ref: (m_ref[0], i, 0)),
                  pl.BlockSpec((got.shape[0], tm, width), lambda i, m_ref: (0, i, 0))],
        out_specs=pl.BlockSpec((tm, width), lambda i, m_ref: (i, 0)))
    return pl.pallas_call(
        body, name=name, grid_spec=grid_spec, out_shape=jax.ShapeDtypeStruct((rows, width), F32),
        compiler_params=pltpu.CompilerParams(dimension_semantics=("parallel",)),
    )(jnp.asarray(mine, jnp.int32).reshape(1), parts, got)


def _local_shape(shape, axis):
    return tuple(d // N_SHARD if a == axis else d for a, d in enumerate(shape))


def _big_sizes():
    return [DEPTH * int(np.prod(_local_shape(shape, axis))) for _, shape, axis in FLAT_BIG]


COL_SHARDED = (("w_in", IN_WIDTH // N_SHARD), ("w_ffn_in", 2 * FFN_HIDDEN // N_SHARD))
FLAT_BIG = tuple(entry for entry in BIG if entry[0] not in [name for name, _ in COL_SHARDED])
FLAT_ROW_TILE = 256
ELEMENTWISE_BIG = ("conv_w",)


def _three_bf16(w):
    hi = w.astype(BF16)
    r1 = w - hi.astype(F32)
    mid = r1.astype(BF16)
    lo = (r1 - mid.astype(F32)).astype(BF16)
    return jnp.stack([hi, mid, lo], axis=-1)


PIECE_ROWS = 16


def _flat_layout(for_weights):
    pieces = []
    for (name, shape, axis), size in zip(FLAT_BIG, _big_sizes(), strict=True):
        n = size * (3 if for_weights and name in ELEMENTWISE_BIG else 1)
        rows = -(-n // (LANES * PIECE_ROWS)) * PIECE_ROWS
        pieces.append((name, shape, axis, n, rows))
    total = sum(p[-1] for p in pieces)
    half = -(-total // (2 * FLAT_ROW_TILE)) * FLAT_ROW_TILE
    return pieces, half


def _to_rows(flat, rows):
    lead, n = flat.shape[:-1], flat.shape[-1]
    fill = jnp.zeros(lead + (rows * LANES - n,), flat.dtype)
    return jnp.concatenate([flat, fill], axis=-1).reshape(lead + (rows, LANES))


def _pack_local_weights(wl):
    pieces, half = _flat_layout(True)
    parts = [_to_rows((_three_bf16(wl[name]) if name in ELEMENTWISE_BIG else wl[name].astype(BF16)).reshape(-1), rows)
             for name, _, _, _, rows in pieces]
    parts.append(jnp.zeros((2 * half - sum(p[-1] for p in pieces), LANES), BF16))
    return jnp.concatenate(parts, axis=0).reshape(2, half, LANES)


def _unpack_local(flat):
    pieces, _ = _flat_layout(False)
    flat = flat.reshape(-1, LANES)
    out, off = {}, 0
    for name, shape, axis, n, rows in pieces:
        out[name] = flat[off:off + rows].reshape(-1)[:n].reshape((DEPTH,) + _local_shape(shape, axis))
        off += rows
    return out


def _unpack_gathered(gathered):
    pieces, _ = _flat_layout(True)
    out, off = {}, 0
    for name, shape, axis, n, rows in pieces:
        local = (N_SHARD, DEPTH) + _local_shape(shape, axis)
        seg = gathered[:, off:off + rows].reshape(N_SHARD, -1)[:, :n]
        off += rows
        if name in ELEMENTWISE_BIG:
            parts = seg.reshape(local + (3,)).astype(F32)
            seg = (parts[..., 0] + parts[..., 1]) + parts[..., 2]
        else:
            seg = seg.reshape(local)
        out[name] = jnp.moveaxis(seg, 0, 1 + axis).reshape((DEPTH,) + shape)
    return out


def _join_col_shards(shards, name):
    ns, depth, rows, c = shards.shape
    tm = _pick(rows, 256, 16)

    def body(i_ref, o_ref):
        for s in range(ns):
            o_ref[0, :, c * s:c * (s + 1)] = i_ref[s, 0]

    return pl.pallas_call(
        body, name=name, grid=(depth, rows // tm),
        in_specs=[pl.BlockSpec((ns, 1, tm, c), lambda l, i: (0, l, i, 0))],
        out_specs=pl.BlockSpec((1, tm, ns * c), lambda l, i: (l, i, 0)),
        out_shape=jax.ShapeDtypeStruct((depth, rows, ns * c), shards.dtype),
        compiler_params=pltpu.CompilerParams(dimension_semantics=("parallel", "parallel")),
    )(shards)


def _split_col_shards(full, name):
    rows, width = full.shape
    c = width // N_SHARD
    tm = _pick(rows, 256, 16)

    def body(i_ref, o_ref):
        for s in range(N_SHARD):
            o_ref[s] = i_ref[:, c * s:c * (s + 1)]

    return pl.pallas_call(
        body, name=name, grid=(rows // tm,), in_specs=[pl.BlockSpec((tm, width), lambda i: (i, 0))],
        out_specs=pl.BlockSpec((N_SHARD, tm, c), lambda i: (0, i, 0)),
        out_shape=jax.ShapeDtypeStruct((N_SHARD, rows, c), full.dtype),
        compiler_params=pltpu.CompilerParams(dimension_semantics=("parallel",)),
    )(full)


def _pack_full_grads(big_grads):
    pieces, half = _flat_layout(False)
    parts = []
    for name, shape, axis, _, rows in pieces:
        per_layer = []
        for gfull in big_grads[name]:
            split = gfull.reshape(shape[:axis] + (N_SHARD, shape[axis] // N_SHARD) + shape[axis + 1:])
            per_layer.append(jnp.moveaxis(split, axis, 0).reshape(N_SHARD, -1))
        parts.append(_to_rows(jnp.concatenate(per_layer, axis=1), rows))
    parts.append(jnp.zeros((N_SHARD, 2 * half - sum(p[-1] for p in pieces), LANES), F32))
    return jnp.concatenate(parts, axis=1).reshape(N_SHARD, 2, half, LANES)


def _pack_small(grads):
    flat = jnp.concatenate([grads[name].reshape(-1) for name in SMALL])
    r = -(-flat.shape[0] // (8 * LANES)) * 8
    return jnp.pad(flat, (0, r * LANES - flat.shape[0])).reshape(r, LANES)


def _unpack_small(flat, like):
    flat = flat.reshape(-1)
    out, off = {}, 0
    for name in SMALL:
        size = int(np.prod(like[name].shape))
        out[name] = flat[off:off + size].reshape(like[name].shape)
        off += size
    return out


def _adamw(w, g, m, v, name):
    shape = w.shape
    cols = shape[-1]
    rows = int(np.prod(shape[:-1])) if len(shape) > 1 else 1
    w2, g2, m2, v2 = (a.reshape(rows, cols) for a in (w, g, m, v))

    def fn(wv, gv, mv, vv):
        mn = ADAM_B1 * mv + (1.0 - ADAM_B1) * gv
        vn = ADAM_B2 * vv + (1.0 - ADAM_B2) * jnp.square(gv)
        m_hat = mn / (1.0 - ADAM_B1 ** ADAM_STEP)
        v_hat = vn / (1.0 - ADAM_B2 ** ADAM_STEP)
        delta = -ADAM_LR * (m_hat / (jnp.sqrt(v_hat) + ADAM_EPS) + ADAM_WD * wv)
        return [delta, mn, vn], []

    tm = 256 if rows % 8 == 0 and rows > 256 else rows
    res = _rw(fn, [(a, None, _c0) for a in (w2, g2, m2, v2)], [(cols, None, F32, _c0)] * 3, tm=tm, name=name)
    return [r.reshape(shape) for r in res]


def _step(x, p, target, weights, moments_m, moments_v):
    xi, yi, ci = lax.axis_index("x"), lax.axis_index("y"), lax.axis_index("c")
    chip = 2 * xi + yi
    half_rows = DEPTH // 2 * D_MODEL
    locals_ = [_pack_local_weights(weights)] + [weights[name].astype(BF16).reshape(2, half_rows, c) for name, c in COL_SHARDED]
    gathered = _allgather_weights(locals_)
    gathered = [_place(g.reshape(N_SHARD, 2 * a.shape[1], a.shape[2]), a.reshape(2 * a.shape[1], a.shape[2]), chip, f"place_own_weights_{k}")
                for k, (g, a) in enumerate(zip(gathered, locals_, strict=True))]
    wt = dict(_unpack_gathered(gathered[0]))
    for (name, c), g in zip(COL_SHARDED, gathered[1:], strict=True):
        wt[name] = _join_col_shards(g.reshape(N_SHARD, DEPTH, D_MODEL, c), f"join_col_shards_{name}")
    for name in SMALL:
        wt[name] = weights[name]
    loss, dx, small_grads, big_grads = _local_step(x[0], p[:, 0], target[0], wt)
    loss = lax.psum(loss, ("x", "y", "c"))
    bufs = [_pack_full_grads(big_grads)]
    for name, c in COL_SHARDED:
        shards = [_split_col_shards(g, f"split_col_shards_{name}_l{li}") for li, g in enumerate(big_grads[name])]
        bufs.append(jnp.stack(shards, axis=1).reshape(N_SHARD, 2, half_rows, c))
    gots = _sibling_exchange(bufs)
    sums = [_add_pair(b, g, ci, f"grad_add_sibling_{k}") for k, (b, g) in enumerate(zip(bufs, gots, strict=True))]
    others = _chip_exchange([s_bf16 for _, s_bf16 in sums])
    halves = [_add_own(s_f32, o, chip, f"grad_add_chips_{k}") for k, ((s_f32, _), o) in enumerate(zip(sums, others, strict=True))]
    both = [_place(b, h, ci, f"place_own_half_{k}") for k, (b, h) in enumerate(zip(_sibling_gather(halves), halves, strict=True))]
    reduced = _unpack_local(both[0])
    for (name, c), b in zip(COL_SHARDED, both[1:], strict=True):
        reduced[name] = b.reshape(DEPTH, D_MODEL, c)
    small_part = _pack_small(small_grads)
    small_all = _place(_gather_partials(small_part), small_part, 4 * xi + 2 * yi + ci, "place_own_small")
    reduced.update(_unpack_small(_sum_leading(small_all, "small_sum")[0], {k: weights[k] for k in SMALL}))
    outs_g, outs_d, outs_m, outs_v = [], [], [], []
    for name in WEIGHTS:
        d, mn, vn = _adamw(weights[name], reduced[name], moments_m[name], moments_v[name], f"adamw_{name}")
        outs_g.append(reduced[name])
        outs_d.append(d)
        outs_m.append(mn)
        outs_v.append(vn)
    return (loss, dx[None], *outs_g, *outs_d, *outs_m, *outs_v)


def kernel(x, p, rel_bias, norm_mix, w_in, ssm_lambda_re, ssm_lambda_im, ssm_b_re, ssm_b_im, ssm_c_re, ssm_c_im, ssm_d, ssm_log_dt, ssm_w_glu, conv_w, attn_sinks, w_branch, w_out, norm_ffn, w_ffn_in, w_ffn_out, norm_ple, w_ple_gate, w_ple_proj, norm_final, loss_target, m_rel_bias, m_norm_mix, m_w_in, m_ssm_lambda_re, m_ssm_lambda_im, m_ssm_b_re, m_ssm_b_im, m_ssm_c_re, m_ssm_c_im, m_ssm_d, m_ssm_log_dt, m_ssm_w_glu, m_conv_w, m_attn_sinks, m_w_branch, m_w_out, m_norm_ffn, m_w_ffn_in, m_w_ffn_out, m_norm_ple, m_w_ple_gate, m_w_ple_proj, m_norm_final, v_rel_bias, v_norm_mix, v_w_in, v_ssm_lambda_re, v_ssm_lambda_im, v_ssm_b_re, v_ssm_b_im, v_ssm_c_re, v_ssm_c_im, v_ssm_d, v_ssm_log_dt, v_ssm_w_glu, v_conv_w, v_attn_sinks, v_w_branch, v_w_out, v_norm_ffn, v_w_ffn_in, v_w_ffn_out, v_norm_ple, v_w_ple_gate, v_w_ple_proj, v_norm_final):
    weights = dict(rel_bias=rel_bias, norm_mix=norm_mix, w_in=w_in, ssm_lambda_re=ssm_lambda_re, ssm_lambda_im=ssm_lambda_im,
                   ssm_b_re=ssm_b_re, ssm_b_im=ssm_b_im, ssm_c_re=ssm_c_re, ssm_c_im=ssm_c_im, ssm_d=ssm_d, ssm_log_dt=ssm_log_dt,
                   ssm_w_glu=ssm_w_glu, conv_w=conv_w, attn_sinks=attn_sinks, w_branch=w_branch, w_out=w_out, norm_ffn=norm_ffn,
                   w_ffn_in=w_ffn_in, w_ffn_out=w_ffn_out, norm_ple=norm_ple, w_ple_gate=w_ple_gate, w_ple_proj=w_ple_proj,
                   norm_final=norm_final)
    moments_m = dict(rel_bias=m_rel_bias, norm_mix=m_norm_mix, w_in=m_w_in, ssm_lambda_re=m_ssm_lambda_re, ssm_lambda_im=m_ssm_lambda_im,
                     ssm_b_re=m_ssm_b_re, ssm_b_im=m_ssm_b_im, ssm_c_re=m_ssm_c_re, ssm_c_im=m_ssm_c_im, ssm_d=m_ssm_d,
                     ssm_log_dt=m_ssm_log_dt, ssm_w_glu=m_ssm_w_glu, conv_w=m_conv_w, attn_sinks=m_attn_sinks, w_branch=m_w_branch,
                     w_out=m_w_out, norm_ffn=m_norm_ffn, w_ffn_in=m_w_ffn_in, w_ffn_out=m_w_ffn_out, norm_ple=m_norm_ple,
                     w_ple_gate=m_w_ple_gate, w_ple_proj=m_w_ple_proj, norm_final=m_norm_final)
    moments_v = dict(rel_bias=v_rel_bias, norm_mix=v_norm_mix, w_in=v_w_in, ssm_lambda_re=v_ssm_lambda_re, ssm_lambda_im=v_ssm_lambda_im,
                     ssm_b_re=v_ssm_b_re, ssm_b_im=v_ssm_b_im, ssm_c_re=v_ssm_c_re, ssm_c_im=v_ssm_c_im, ssm_d=v_ssm_d,
                     ssm_log_dt=v_ssm_log_dt, ssm_w_glu=v_ssm_w_glu, conv_w=v_conv_w, attn_sinks=v_attn_sinks, w_branch=v_w_branch,
                     w_out=v_w_out, norm_ffn=v_norm_ffn, w_ffn_in=v_w_ffn_in, w_ffn_out=v_w_ffn_out, norm_ple=v_norm_ple,
                     w_ple_gate=v_w_ple_gate, w_ple_proj=v_w_ple_proj, norm_final=v_norm_final)
    return _step(x, p, loss_target, weights, moments_m, moments_v)
```

```python
import functools
import math

import numpy as np

import jax
import jax.numpy as jnp
from jax import lax
from jax.experimental import pallas as pl
from jax.experimental.pallas import tpu as pltpu

F32, BF16 = jnp.float32, jnp.bfloat16
MESH = pl.DeviceIdType.MESH

D_MODEL = 1024
DEPTH = 4
PLE_DIM = 256
BRANCH = 512
N_GROUPS = 32
GROUP_CH = 16
N_STATE = 64
SSM_STATES = N_GROUPS * N_STATE
SSM_BLOCKS = 4
HEAD_DIM = 64
N_Q = 8
N_KV = 2
GQA = N_Q // N_KV
WINDOW = 128
ATTN_SCALE = 1.0 / math.sqrt(HEAD_DIM)
REL_BUCKETS = 32
REL_MAX_DIST = 128
FFN_HIDDEN = 2816
FFN_COLS = 1408
FFN_NCOL = FFN_HIDDEN // FFN_COLS
IN_WIDTH = 5888
RMS_EPS = 1e-6
NEG = -1e30

ADAM_LR, ADAM_B1, ADAM_B2, ADAM_EPS, ADAM_WD, ADAM_STEP = 0.001, 0.9, 0.999, 1e-08, 0.01, 10

N_SHARD = 4
LANES = 1024
COPY_CHUNKS = 4

GATES_WIDTH = 3 * D_MODEL
MAIN_WIDTH = IN_WIDTH - GATES_WIDTH
OFF_G, OFF_U, OFF_CB, OFF_CC, OFF_CX, OFF_Q, OFF_K, OFF_V = 0, 3072, 3584, 4096, 4608, 5120, 5632, 5760
U_BLK, CB_BLK, CC_BLK, CX_BLK = OFF_U // BRANCH, OFF_CB // BRANCH, OFF_CC // BRANCH, OFF_CX // BRANCH

BIG = (
    ("w_in", (D_MODEL, IN_WIDTH), 1),
    ("ssm_w_glu", (BRANCH, BRANCH), 0),
    ("conv_w", (3, BRANCH), 1),
    ("w_branch", (3, BRANCH, D_MODEL), 2),
    ("w_out", (D_MODEL, D_MODEL), 0),
    ("w_ffn_in", (D_MODEL, 2 * FFN_HIDDEN), 1),
    ("w_ffn_out", (FFN_HIDDEN, D_MODEL), 0),
    ("w_ple_gate", (D_MODEL, D_MODEL), 0),
    ("w_ple_proj", (PLE_DIM, D_MODEL), 1),
)
SMALL = ("rel_bias", "norm_mix", "ssm_lambda_re", "ssm_lambda_im", "ssm_b_re", "ssm_b_im", "ssm_c_re", "ssm_c_im",
         "ssm_d", "ssm_log_dt", "attn_sinks", "norm_ffn", "norm_ple", "norm_final")
WEIGHTS = ("rel_bias", "norm_mix", "w_in", "ssm_lambda_re", "ssm_lambda_im", "ssm_b_re", "ssm_b_im", "ssm_c_re",
           "ssm_c_im", "ssm_d", "ssm_log_dt", "ssm_w_glu", "conv_w", "attn_sinks", "w_branch", "w_out", "norm_ffn",
           "w_ffn_in", "w_ffn_out", "norm_ple", "w_ple_gate", "w_ple_proj", "norm_final")


def _c0(j):
    return 0


def _pick(n, cap, unit=128):
    if n <= cap:
        return n
    best = None
    for t in range(unit, cap + 1, unit):
        if n % t == 0:
            best = t
    assert best is not None, (n, cap, unit)
    return best


_DIMS = {"nn": ((1,), (0,)), "nt": ((1,), (1,)), "tn": ((0,), (0,))}


def _mm(a, b, mode, *, name, out_dtype=F32, add=None, tm=1024, tn=1024, tk=1024, b_k0=0, n_outer=False):
    if mode == "nn":
        (m, k), (k2, n) = a.shape, b.shape
    elif mode == "nt":
        (m, k), (n, k2) = a.shape, b.shape
    else:
        (k, m), (k2, n) = a.shape, b.shape
    assert k == k2 or (mode == "nt" and b_k0 + k <= k2), (a.shape, b.shape, mode)
    tm, tn, tk = _pick(m, tm, 128 if mode == "tn" else 8), _pick(n, tn), _pick(k, tk, 128 if mode != "tn" else 8)
    nk = k // tk
    assert b_k0 % tk == 0
    kb0 = b_k0 // tk
    def at(f):
        return (lambda j, i, kk: f(i, j, kk)) if n_outer else f

    a_spec = pl.BlockSpec((tk, tm), at(lambda i, j, kk: (kk, i))) if mode == "tn" else pl.BlockSpec((tm, tk), at(lambda i, j, kk: (i, kk)))
    b_spec = (pl.BlockSpec((tn, tk), at(lambda i, j, kk: (j, kb0 + kk))) if mode == "nt"
              else pl.BlockSpec((tk, tn), at(lambda i, j, kk: (kk, j))))
    o_spec = pl.BlockSpec((tm, tn), at(lambda i, j, kk: (i, j)))
    dims = (_DIMS[mode], ((), ()))
    has_add = add is not None

    def body(*refs):
        a_ref, b_ref = refs[0], refs[1]
        add_ref = refs[2] if has_add else None
        o_ref, acc_ref = refs[-2], refs[-1]
        part = lax.dot_general(a_ref[...].astype(BF16), b_ref[...].astype(BF16), dims, preferred_element_type=F32)

        def finish(acc):
            if has_add:
                acc = acc + add_ref[...]
            o_ref[...] = acc.astype(o_ref.dtype)

        if nk == 1:
            finish(part)
        else:
            kk = pl.program_id(2)

            @pl.when(kk == 0)
            def _():
                acc_ref[...] = part

            @pl.when(kk > 0)
            def _():
                acc_ref[...] += part

            @pl.when(kk == nk - 1)
            def _():
                finish(acc_ref[...])

    operands = [a, b] + ([add] if has_add else [])
    in_specs = [a_spec, b_spec] + ([o_spec] if has_add else [])
    return pl.pallas_call(
        body, name=name, grid=(n // tn, m // tm, nk) if n_outer else (m // tm, n // tn, nk), in_specs=in_specs, out_specs=o_spec,
        out_shape=jax.ShapeDtypeStruct((m, n), out_dtype),
        scratch_shapes=[pltpu.VMEM((tm, tn) if nk > 1 else (8, 128), F32)],
        compiler_params=pltpu.CompilerParams(dimension_semantics=("parallel", "parallel", "arbitrary")),
    )(*operands)


def _rw(fn, ins, outs, *, name, params=(), reds=(), tm=256, ncol=1, with_j=False):
    t = ins[0][0].shape[0]
    tm = _pick(t, tm, 8)
    nrow = t // tm
    n_in, n_p, n_out = len(ins), len(params), len(outs)

    in_specs = [pl.BlockSpec((tm, bw or arr.shape[1]), lambda j, i, cf=cf: (i, cf(j))) for arr, bw, cf in ins]
    in_specs += [pl.BlockSpec(p.shape, lambda j, i: (0, 0)) for p in params]
    out_specs = [pl.BlockSpec((tm, bw or w), lambda j, i, cf=cf: (i, cf(j))) for w, bw, _, cf in outs]
    out_specs += [pl.BlockSpec((shp[0], bw or shp[1]), lambda j, i, cf=cf: (0, cf(j))) for shp, bw, cf in reds]
    out_shape = [jax.ShapeDtypeStruct((t, w), dt) for w, _, dt, _ in outs]
    out_shape += [jax.ShapeDtypeStruct(shp, F32) for shp, _, _ in reds]

    def body(*refs):
        in_refs, p_refs = refs[:n_in], refs[n_in:n_in + n_p]
        o_refs, r_refs = refs[n_in + n_p:n_in + n_p + n_out], refs[n_in + n_p + n_out:]
        args = [r[...].astype(F32) for r in in_refs] + [r[...] for r in p_refs]
        if with_j:
            args = [pl.program_id(0)] + args
        o_vals, r_vals = fn(*args)
        for r, v in zip(o_refs, o_vals, strict=True):
            r[...] = v.astype(r.dtype)
        if r_refs:
            i = pl.program_id(1)
            for r, v in zip(r_refs, r_vals, strict=True):
                @pl.when(i == 0)
                def _(r=r, v=v):
                    r[...] = v

                @pl.when(i > 0)
                def _(r=r, v=v):
                    r[...] += v

    res = pl.pallas_call(
        body, name=name, grid=(ncol, nrow), in_specs=in_specs, out_specs=out_specs, out_shape=out_shape,
        compiler_params=pltpu.CompilerParams(dimension_semantics=("parallel", "arbitrary" if reds else "parallel")),
    )(*[a for a, _, _ in ins], *params)
    return res


def _rms(x, g):
    return x * lax.rsqrt(jnp.mean(x * x, axis=-1, keepdims=True) + RMS_EPS) * g


def _rms_fwd(x, g, name):
    return _rw(lambda xv, gv: ([_rms(xv, gv)], []), [(x, None, _c0)], [(D_MODEL, None, BF16, _c0)], params=[g], name=name)[0]


def _rms_bwd(x, dh, dres, g, name):
    def fn(xv, dhv, drv, gv):
        _, vjp = jax.vjp(_rms, xv, gv)
        dx, dg = vjp(dhv)
        return [drv + dx], [dg]

    return _rw(fn, [(x, None, _c0), (dh, None, _c0), (dres, None, _c0)], [(D_MODEL, None, F32, _c0)], params=[g],
               reds=[((1, D_MODEL), None, _c0)], name=name)


def _bd_apply(acts, mats, combos, mode, *, name, tm=512, col_blocks=None):
    t = acts[0].shape[0]
    tm = _pick(t, tm, 8)
    nb, r, c = mats[0].shape
    win, wout = (r, c) if mode == "nn" else (c, r)
    dims = (_DIMS[mode], ((), ()))
    n_a, n_m = len(acts), len(mats)

    def body(*refs):
        a_vals = [ar[...].astype(BF16) for ar in refs[:n_a]]
        m_refs, o_refs = refs[n_a:n_a + n_m], refs[n_a + n_m:]
        for o_ref, terms in zip(o_refs, combos, strict=True):
            for j in range(nb):
                acc = None
                for ai, mi in terms:
                    part = lax.dot_general(a_vals[ai][:, j * win:(j + 1) * win], m_refs[mi][j], dims, preferred_element_type=F32)
                    acc = part if acc is None else acc + part
                o_ref[:, j * wout:(j + 1) * wout] = acc

    return pl.pallas_call(
        body, name=name, grid=(t // tm,),
        in_specs=[pl.BlockSpec((tm, nb * win), lambda i, cb=cb: (i, cb)) for cb in (col_blocks or [0] * n_a)]
        + [pl.BlockSpec(m.shape, lambda i: (0, 0, 0)) for m in mats],
        out_specs=[pl.BlockSpec((tm, nb * wout), lambda i: (i, 0))] * len(combos),
        out_shape=[jax.ShapeDtypeStruct((t, nb * wout), F32)] * len(combos),
        compiler_params=pltpu.CompilerParams(dimension_semantics=("parallel",)),
    )(*acts, *mats)


def _bd_grads(arrs, widths, pairs, *, name, tk=512, col_blocks=None):
    t = arrs[0].shape[0]
    tk = _pick(t, tk, 8)
    n_a = len(arrs)
    dims = (_DIMS["tn"], ((), ()))

    def body(*refs):
        vals = [ar[...].astype(BF16) for ar in refs[:n_a]]
        o_refs = refs[n_a:]
        @pl.when(pl.program_id(0) == 0)
        def _():
            for o_ref in o_refs:
                o_ref[...] = jnp.zeros_like(o_ref)

        for o_ref, (ai, bi) in zip(o_refs, pairs, strict=True):
            wa, wb = widths[ai], widths[bi]
            for j in range(SSM_BLOCKS):
                o_ref[j] += lax.dot_general(vals[ai][:, j * wa:(j + 1) * wa], vals[bi][:, j * wb:(j + 1) * wb], dims,
                                            preferred_element_type=F32)

    return pl.pallas_call(
        body, name=name, grid=(t // tk,),
        in_specs=[pl.BlockSpec((tk, SSM_BLOCKS * w), lambda k, cb=cb: (k, cb)) for w, cb in zip(widths, col_blocks or [0] * n_a, strict=True)],
        out_specs=[pl.BlockSpec((SSM_BLOCKS, widths[ai], widths[bi]), lambda k: (0, 0, 0)) for ai, bi in pairs],
        out_shape=[jax.ShapeDtypeStruct((SSM_BLOCKS, widths[ai], widths[bi]), F32) for ai, bi in pairs],
        compiler_params=pltpu.CompilerParams(dimension_semantics=("arbitrary",)),
    )(*arrs)


SCAN_LW = 512
SCAN_ROWS = 512
_DOUBLING = ((1, 0), (2, 1), (4, 2))


def _scan(xr, xi, pr, pi, dr, di, *, reverse, name, hr=None, hi=None):
    t, s = xr.shape
    lc = _pick(t, SCAN_ROWS, 8)
    nt, ngroups = t // lc, lc // 8
    with_da = hr is not None

    def tmap(l, tt):
        return ((nt - 1 - tt) if reverse else tt, l)

    x_spec = pl.BlockSpec((lc, SCAN_LW), tmap)
    tab_spec = pl.BlockSpec((8, SCAN_LW), lambda l, tt: (0, l))

    def body(*refs):
        xr_ref, xi_ref, pr_ref, pi_ref, dr_ref, di_ref = refs[:6]
        if with_da:
            hr_ref, hi_ref, or_ref, oi_ref, ar_ref, ai_ref, cr_ref, ci_ref = refs[6:]
        else:
            or_ref, oi_ref, cr_ref, ci_ref = refs[6:]
        tt = pl.program_id(1)

        @pl.when(tt == 0)
        def _():
            cr_ref[...] = jnp.zeros_like(cr_ref)
            ci_ref[...] = jnp.zeros_like(ci_ref)
            if with_da:
                ar_ref[...] = jnp.zeros_like(ar_ref)
                ai_ref[...] = jnp.zeros_like(ai_ref)

        sub = lax.broadcasted_iota(jnp.int32, (8, SCAN_LW), 0)
        pw_r, pw_i = pr_ref[...], pi_ref[...]

        def step(g, carry):
            g = (ngroups - 1 - g) if reverse else g
            r0 = pl.multiple_of(g * 8, 8)
            vr, vi = xr_ref[pl.ds(r0, 8), :], xi_ref[pl.ds(r0, 8), :]
            for shift, row in _DOUBLING:
                a_r, a_i = dr_ref[row:row + 1, :], di_ref[row:row + 1, :]
                if reverse:
                    keep = sub < 8 - shift
                    sr, si = pltpu.roll(vr, 8 - shift, 0), pltpu.roll(vi, 8 - shift, 0)
                else:
                    keep = sub >= shift
                    sr, si = pltpu.roll(vr, shift, 0), pltpu.roll(vi, shift, 0)
                sr, si = jnp.where(keep, sr, 0.0), jnp.where(keep, si, 0.0)
                vr, vi = vr + a_r * sr - a_i * si, vi + a_r * si + a_i * sr
            if with_da:
                cr, ci, acc_r, acc_i = carry
            else:
                cr, ci = carry
            vr, vi = vr + pw_r * cr - pw_i * ci, vi + pw_r * ci + pw_i * cr
            or_ref[pl.ds(r0, 8), :] = vr
            oi_ref[pl.ds(r0, 8), :] = vi
            if with_da:
                nr = jnp.where(sub < 7, pltpu.roll(vr, 7, 0), cr)
                ni = jnp.where(sub < 7, pltpu.roll(vi, 7, 0), ci)
                h_r, h_i = hr_ref[pl.ds(r0, 8), :], hi_ref[pl.ds(r0, 8), :]
                acc_r = acc_r + h_r * nr + h_i * ni
                acc_i = acc_i + h_r * ni - h_i * nr
            edge = 0 if reverse else 7
            cr = jnp.broadcast_to(vr[edge:edge + 1, :], vr.shape)
            ci = jnp.broadcast_to(vi[edge:edge + 1, :], vi.shape)
            return (cr, ci, acc_r, acc_i) if with_da else (cr, ci)

        zero = jnp.zeros((8, SCAN_LW), F32)
        init = (cr_ref[...], ci_ref[...]) + ((zero, zero) if with_da else ())
        fin = lax.fori_loop(0, ngroups, step, init, unroll=2)
        cr_ref[...] = fin[0]
        ci_ref[...] = fin[1]
        if with_da:
            ar_ref[...] += fin[2]
            ai_ref[...] += fin[3]

    n_x = 4 if with_da else 2
    out_specs = [x_spec, x_spec] + ([tab_spec, tab_spec] if with_da else [])
    out_shape = [jax.ShapeDtypeStruct((t, s), F32)] * 2 + ([jax.ShapeDtypeStruct((8, s), F32)] * 2 if with_da else [])
    operands = [xr, xi, pr, pi, dr, di] + ([hr, hi] if with_da else [])
    return pl.pallas_call(
        body, name=name, grid=(s // SCAN_LW, nt),
        in_specs=[x_spec, x_spec] + [tab_spec] * 4 + [x_spec] * (n_x - 2),
        out_specs=out_specs, out_shape=out_shape,
        scratch_shapes=[pltpu.VMEM((8, SCAN_LW), F32), pltpu.VMEM((8, SCAN_LW), F32)],
        compiler_params=pltpu.CompilerParams(dimension_semantics=("parallel", "arbitrary")),
    )(*operands)


CONV_TM = 256
HALO = 16


def _conv_specs(t, tm):
    nrow = t // tm
    hb = tm // HALO

    def col(cidx):
        return pl.BlockSpec((tm, BRANCH), lambda i: (i, cidx))

    def prev(cidx):
        return pl.BlockSpec((HALO, BRANCH), lambda i: (jnp.maximum(i * hb - 1, 0), cidx))

    def nxt(cidx):
        return pl.BlockSpec((HALO, BRANCH), lambda i: (jnp.minimum((i + 1) * hb, nrow * hb - 1), cidx))

    return nrow, col, prev, nxt


def _conv_taps(cc, cx, cc_prev, cx_prev, first):
    tm = cc.shape[0]
    v = cc * cx
    halo = cc_prev * cx_prev * jnp.where(first, 0.0, 1.0)
    ext = jnp.concatenate([halo, v], axis=0)
    return v, pltpu.roll(ext, 1, 0)[HALO:HALO + tm], pltpu.roll(ext, 2, 0)[HALO:HALO + tm]


def _conv_fwd(z, conv_w, name):
    t = z.shape[0]
    tm = _pick(t, CONV_TM, HALO)
    nrow, col, prev, _ = _conv_specs(t, tm)

    def body(cb_ref, cc_ref, cx_ref, ccp_ref, cxp_ref, w_ref, o_ref):
        first = pl.program_id(0) == 0
        v, v1, v2 = _conv_taps(*(r[...].astype(F32) for r in (cc_ref, cx_ref, ccp_ref, cxp_ref)), first)
        y = w_ref[0:1, :] * v2 + w_ref[1:2, :] * v1 + w_ref[2:3, :] * v
        o_ref[...] = (cb_ref[...].astype(F32) * y).astype(o_ref.dtype)

    return pl.pallas_call(
        body, name=name, grid=(nrow,),
        in_specs=[col(CB_BLK), col(CC_BLK), col(CX_BLK), prev(CC_BLK), prev(CX_BLK), pl.BlockSpec((3, BRANCH), lambda i: (0, 0))],
        out_specs=pl.BlockSpec((tm, BRANCH), lambda i: (i, 0)), out_shape=jax.ShapeDtypeStruct((t, BRANCH), BF16),
        compiler_params=pltpu.CompilerParams(dimension_semantics=("parallel",)),
    )(z, z, z, z, z, conv_w)


def _conv_bwd(dyc, z, conv_w, name):
    t = z.shape[0]
    tm = _pick(t, CONV_TM, HALO)
    nrow, col, prev, nxt = _conv_specs(t, tm)
    d_cur = pl.BlockSpec((tm, BRANCH), lambda i: (i, 0))
    d_nxt = pl.BlockSpec((HALO, BRANCH), lambda i: (jnp.minimum((i + 1) * (tm // HALO), nrow * (tm // HALO) - 1), 0))

    def body(dy_ref, dyn_ref, cb_ref, cbn_ref, cc_ref, cx_ref, ccp_ref, cxp_ref, w_ref, dcb_ref, dcc_ref, dcx_ref, dw_ref):
        i = pl.program_id(0)
        cc, cx, cb = cc_ref[...].astype(F32), cx_ref[...].astype(F32), cb_ref[...].astype(F32)
        v, v1, v2 = _conv_taps(cc, cx, ccp_ref[...].astype(F32), cxp_ref[...].astype(F32), i == 0)
        w0, w1, w2 = w_ref[0:1, :], w_ref[1:2, :], w_ref[2:3, :]
        y = w0 * v2 + w1 * v1 + w2 * v
        dyc_v = dy_ref[...]
        dcb_ref[...] = (dyc_v * y).astype(dcb_ref.dtype)
        dy = dyc_v * cb
        halo = dyn_ref[...] * cbn_ref[...].astype(F32) * jnp.where(i == nrow - 1, 0.0, 1.0)
        ext = jnp.concatenate([dy, halo], axis=0)
        dy1 = pltpu.roll(ext, tm + HALO - 1, 0)[0:tm]
        dy2 = pltpu.roll(ext, tm + HALO - 2, 0)[0:tm]
        dv = w2 * dy + w1 * dy1 + w0 * dy2
        dcc_ref[...] = (dv * cx).astype(dcc_ref.dtype)
        dcx_ref[...] = (dv * cc).astype(dcx_ref.dtype)
        dw = jnp.concatenate([jnp.sum(dy * v2, axis=0, keepdims=True), jnp.sum(dy * v1, axis=0, keepdims=True),
                              jnp.sum(dy * v, axis=0, keepdims=True), jnp.zeros((5, BRANCH), F32)], axis=0)

        @pl.when(i == 0)
        def _():
            dw_ref[...] = dw

        @pl.when(i > 0)
        def _():
            dw_ref[...] += dw

    o_spec = pl.BlockSpec((tm, BRANCH), lambda i: (i, 0))
    return pl.pallas_call(
        body, name=name, grid=(nrow,),
        in_specs=[d_cur, d_nxt, col(CB_BLK), nxt(CB_BLK), col(CC_BLK), col(CX_BLK), prev(CC_BLK), prev(CX_BLK),
                  pl.BlockSpec((3, BRANCH), lambda i: (0, 0))],
        out_specs=[o_spec, o_spec, o_spec, pl.BlockSpec((8, BRANCH), lambda i: (0, 0))],
        out_shape=[jax.ShapeDtypeStruct((t, BRANCH), BF16)] * 3 + [jax.ShapeDtypeStruct((8, BRANCH), F32)],
        compiler_params=pltpu.CompilerParams(dimension_semantics=("arbitrary",)),
    )(dyc, dyc, z, z, z, z, z, z, conv_w)


ATTN_BLOCKS = 4
ATTN_BLOCKS_BWD = 2
GROUP_ROWS = GQA * WINDOW


def _attn_specs(nblk):
    rows = nblk * WINDOW
    q_spec = pl.BlockSpec((N_Q, rows, HEAD_DIM), lambda n: (0, n, 0))
    kv_cur = pl.BlockSpec((N_KV, rows, HEAD_DIM), lambda n: (0, n, 0))
    kv_prev = pl.BlockSpec((N_KV, WINDOW, HEAD_DIM), lambda n: (0, jnp.maximum(n * nblk - 1, 0), 0))
    bias_spec = pl.BlockSpec((N_Q, WINDOW, 2 * WINDOW), lambda n: (0, 0, 0))
    sink_spec = pl.BlockSpec((N_Q, 1), lambda n: (0, 0))
    return q_spec, kv_cur, kv_prev, bias_spec, sink_spec


def _attn_valid(first_key):
    qi = lax.broadcasted_iota(jnp.int32, (GROUP_ROWS, 2 * WINDOW), 0) & (WINDOW - 1)
    kj = lax.broadcasted_iota(jnp.int32, (GROUP_ROWS, 2 * WINDOW), 1)
    dist = qi + WINDOW - kj
    return (dist >= 0) & (dist < WINDOW) & (kj >= first_key)


def _attn_masks(n):
    return _attn_valid(jnp.where(n > 0, 0, WINDOW)), _attn_valid(0)


def _blk(b):
    return slice(b * WINDOW, (b + 1) * WINDOW)


def _group(ref, h, b, width):
    return ref[GQA * h:GQA * (h + 1), _blk(b)].reshape(GROUP_ROWS, width)


def _keys(prev_ref, cur_ref, h, b):
    prev = prev_ref[h] if b == 0 else cur_ref[h, _blk(b - 1)]
    return jnp.concatenate([prev, cur_ref[h, _blk(b)]], axis=0)


def _group_sinks(s_ref, h):
    return jnp.concatenate([jnp.broadcast_to(s_ref[GQA * h + g:GQA * h + g + 1, :], (WINDOW, 1)) for g in range(GQA)], axis=0)


def _attn_probs(q, kc, bias, sink, valid):
    s = lax.dot_general(q, kc, (_DIMS["nt"], ((), ())), preferred_element_type=F32) * ATTN_SCALE + bias
    s = jnp.where(valid, s, NEG)
    m = jnp.maximum(jnp.max(s, axis=1, keepdims=True), sink)
    p = jnp.exp(s - m)
    e_sink = jnp.exp(sink - m)
    inv = 1.0 / (jnp.sum(p, axis=1, keepdims=True) + e_sink)
    return p * inv, e_sink * inv


def _attn_fwd(qh, kh, vh, bias, sinks, name):
    t = qh.shape[1]
    nblk = min(ATTN_BLOCKS, t // WINDOW)
    q_spec, kv_cur, kv_prev, bias_spec, sink_spec = _attn_specs(nblk)

    def body(q_ref, kp_ref, kc_ref, vp_ref, vc_ref, b_ref, s_ref, o_ref):
        masks = _attn_masks(pl.program_id(0))
        for b in range(nblk):
            for h in range(N_KV):
                kc, vc = _keys(kp_ref, kc_ref, h, b), _keys(vp_ref, vc_ref, h, b)
                w, _ = _attn_probs(_group(q_ref, h, b, HEAD_DIM), kc, _group(b_ref, h, 0, 2 * WINDOW), _group_sinks(s_ref, h),
                                   masks[min(b, 1)])
                o = jnp.dot(w.astype(BF16), vc, preferred_element_type=F32)
                o_ref[GQA * h:GQA * (h + 1), _blk(b)] = o.reshape(GQA, WINDOW, HEAD_DIM).astype(o_ref.dtype)

    return pl.pallas_call(
        body, name=name, grid=(t // (nblk * WINDOW),),
        in_specs=[q_spec, kv_prev, kv_cur, kv_prev, kv_cur, bias_spec, sink_spec],
        out_specs=q_spec, out_shape=jax.ShapeDtypeStruct((N_Q, t, HEAD_DIM), BF16),
        compiler_params=pltpu.CompilerParams(dimension_semantics=("parallel",)),
    )(qh, kh, kh, vh, vh, bias, sinks)


def _attn_bwd(qh, kh, vh, doh, bias, sinks, name):
    t = qh.shape[1]
    nblk = min(ATTN_BLOCKS_BWD, t // WINDOW)
    nsteps = t // (nblk * WINDOW)
    q_spec, kv_cur, kv_prev, bias_spec, sink_spec = _attn_specs(nblk)

    def body(q_ref, kp_ref, kc_ref, vp_ref, vc_ref, do_ref, b_ref, s_ref,
             dq_ref, dkc_ref, dkp_ref, dvc_ref, dvp_ref, db_ref, ds_ref):
        n = pl.program_id(0)
        masks = _attn_masks(n)

        @pl.when(n == 0)
        def _():
            db_ref[...] = jnp.zeros_like(db_ref)
            ds_ref[...] = jnp.zeros_like(ds_ref)

        def scatter(part, h, b, cur_ref, prev_ref):
            if b == 0:
                prev_ref[h] = part[0:WINDOW]
            else:
                cur_ref[h, _blk(b - 1)] += part[0:WINDOW]
            cur_ref[h, _blk(b)] = part[WINDOW:2 * WINDOW]

        for b in range(nblk):
            for h in range(N_KV):
                kc, vc = _keys(kp_ref, kc_ref, h, b), _keys(vp_ref, vc_ref, h, b)
                heads = slice(GQA * h, GQA * (h + 1))
                q, do = _group(q_ref, h, b, HEAD_DIM), _group(do_ref, h, b, HEAD_DIM)
                w, w_sink = _attn_probs(q, kc, _group(b_ref, h, 0, 2 * WINDOW), _group_sinks(s_ref, h), masks[min(b, 1)])
                dw = lax.dot_general(do, vc, (_DIMS["nt"], ((), ())), preferred_element_type=F32)
                delta = jnp.sum(w * dw, axis=1, keepdims=True)
                dscore = w * (dw - delta)
                ds_ref[heads] += (-w_sink * delta).reshape(GQA, WINDOW, 1)
                db_ref[heads] += dscore.reshape(GQA, WINDOW, 2 * WINDOW)
                dsb = dscore.astype(BF16)
                dq_ref[heads, _blk(b)] = (jnp.dot(dsb, kc, preferred_element_type=F32) * ATTN_SCALE).reshape(GQA, WINDOW, HEAD_DIM)
                scatter(lax.dot_general(dsb, q, (_DIMS["tn"], ((), ())), preferred_element_type=F32) * ATTN_SCALE, h, b, dkc_ref, dkp_ref)
                scatter(lax.dot_general(w.astype(BF16), do, (_DIMS["tn"], ((), ())), preferred_element_type=F32), h, b, dvc_ref, dvp_ref)

    kv_shape = jax.ShapeDtypeStruct((N_KV, t, HEAD_DIM), F32)
    kv_prev_out = pl.BlockSpec((N_KV, WINDOW, HEAD_DIM), lambda n: (0, n, 0))
    kv_prev_shape = jax.ShapeDtypeStruct((N_KV, nsteps * WINDOW, HEAD_DIM), F32)
    return pl.pallas_call(
        body, name=name, grid=(nsteps,),
        in_specs=[q_spec, kv_prev, kv_cur, kv_prev, kv_cur, q_spec, bias_spec, sink_spec],
        out_specs=[q_spec, kv_cur, kv_prev_out, kv_cur, kv_prev_out, bias_spec, pl.BlockSpec((N_Q, WINDOW, 1), lambda n: (0, 0, 0))],
        out_shape=[jax.ShapeDtypeStruct((N_Q, t, HEAD_DIM), F32), kv_shape, kv_prev_shape, kv_shape, kv_prev_shape,
                   jax.ShapeDtypeStruct((N_Q, WINDOW, 2 * WINDOW), F32), jax.ShapeDtypeStruct((N_Q, WINDOW, 1), F32)],
        compiler_params=pltpu.CompilerParams(dimension_semantics=("arbitrary",)),
    )(qh, kh, kh, vh, vh, doh, bias, sinks)


def _heads(a, n_heads):
    t = a.shape[0]
    return a.astype(BF16).reshape(t, n_heads, HEAD_DIM).transpose(1, 0, 2)


def _unheads(a):
    n_heads, t, _ = a.shape
    return a.transpose(1, 0, 2).reshape(t, n_heads * HEAD_DIM)


def _shift_blocks(cur, prev):
    n_kv, t, d = cur.shape
    nsteps = prev.shape[1] // WINDOW
    nblk = t // (nsteps * WINDOW)
    late = jnp.concatenate([prev.reshape(n_kv, nsteps, WINDOW, d)[:, 1:], jnp.zeros((n_kv, 1, WINDOW, d), cur.dtype)], axis=1)
    delta = jnp.concatenate([jnp.zeros((n_kv, nsteps, nblk - 1, WINDOW, d), cur.dtype), late[:, :, None]], axis=2)
    return (cur.reshape(n_kv, nsteps, nblk, WINDOW, d) + delta).reshape(n_kv, t, d)


def _t5_bucket_table():
    qi = np.arange(WINDOW)[:, None]
    kj = np.arange(2 * WINDOW)[None, :]
    dist = np.clip(qi + WINDOW - kj, 0, REL_MAX_DIST - 1)
    exact = REL_BUCKETS // 2
    df = np.maximum(dist, 1).astype(np.float32)
    large = exact + (np.log(df / np.float32(exact)) / np.float32(math.log(REL_MAX_DIST / exact)) * (REL_BUCKETS - exact)).astype(np.int32)
    large = np.minimum(large, REL_BUCKETS - 1)
    bucket = np.where(dist < exact, dist, large)
    onehot = np.zeros((WINDOW * 2 * WINDOW, REL_BUCKETS), np.float32)
    onehot[np.arange(WINDOW * 2 * WINDOW), bucket.reshape(-1)] = 1.0
    return onehot


def _band_bias(rel_bias):
    onehot = jnp.asarray(_t5_bucket_table())
    sel = jnp.sum(onehot[:, :, None] * rel_bias[None, :, :], axis=1)
    return sel.T.reshape(N_Q, WINDOW, 2 * WINDOW)


def _block_diag(a):
    g, r, c = a.shape
    a4 = a.reshape(SSM_BLOCKS, g // SSM_BLOCKS, r, c)
    eye = jnp.eye(g // SSM_BLOCKS, dtype=a.dtype)
    full = a4[:, :, :, None, :] * eye[None, :, None, :, None]
    return full.reshape(SSM_BLOCKS, (g // SSM_BLOCKS) * r, (g // SSM_BLOCKS) * c)


def _ssm_disc(lam_re, lam_im, b_re, b_im, c_re, c_im, log_dt):
    dt = jnp.exp(log_dt)[:, None]
    mag = jnp.exp(lam_re * dt)
    ang = lam_im * dt
    a_re = mag * jnp.cos(ang)
    a_im = mag * jnp.sin(ang)
    den = lam_re * lam_re + lam_im * lam_im
    nr = a_re - 1.0
    coef_re = (nr * lam_re + a_im * lam_im) / den
    coef_im = (a_im * lam_re - nr * lam_im) / den
    bb_re = coef_re[..., None] * b_re - coef_im[..., None] * b_im
    bb_im = coef_re[..., None] * b_im + coef_im[..., None] * b_re
    wb_re = _block_diag(jnp.swapaxes(bb_re, 1, 2))
    wb_im = _block_diag(jnp.swapaxes(bb_im, 1, 2))
    cm_re = _block_diag(jnp.swapaxes(c_re, 1, 2))
    cm_imn = _block_diag(-jnp.swapaxes(c_im, 1, 2))
    return a_re.reshape(-1), a_im.reshape(-1), wb_re, wb_im, cm_re, cm_imn


def _scan_tables(a_re, a_im):
    pr, pi = [a_re], [a_im]
    for _ in range(7):
        pr, pi = pr + [pr[-1] * a_re - pi[-1] * a_im], pi + [pr[-1] * a_im + pi[-1] * a_re]
    pr, pi = jnp.stack(pr), jnp.stack(pi)
    pad = jnp.zeros((5,) + a_re.shape, F32)
    dr = jnp.concatenate([jnp.stack([pr[0], pr[1], pr[3]]), pad])
    di = jnp.concatenate([jnp.stack([pi[0], pi[1], pi[3]]), pad])
    fwd = (pr, pi, dr, di)
    rev = (pr[::-1], -pi[::-1], dr, -di)
    return jax.tree.map(lax.stop_gradient, (fwd, rev))


def _gate_col(r):
    return lambda j: OFF_G // D_MODEL + r


def _gates_first(w_in):
    return jnp.concatenate([w_in[:, MAIN_WIDTH:], w_in[:, :MAIN_WIDTH]], axis=1)


def _gates_last(g_w_in):
    return jnp.concatenate([g_w_in[:, GATES_WIDTH:], g_w_in[:, :GATES_WIDTH]], axis=1)


def _layer_fwd(x, p_i, w, bias, li):
    nm = lambda s: f"{s}_l{li}"
    h = _rms_fwd(x, w["norm_mix"], nm("rms_mix"))
    z = _mm(h, w["w_in"], "nn", tm=1024, tn=2944, n_outer=True, out_dtype=BF16, name=nm("mm_in"))
    bu_re, bu_im = _bd_apply([z], [w["wb_re"], w["wb_im"]], [[(0, 0)], [(0, 1)]], "nn", col_blocks=[U_BLK], name=nm("ssm_bu"))
    h_re, h_im = _scan(bu_re, bu_im, *w["scan_fwd"], reverse=False, name=nm("ssm_scan"))
    (y0,) = _bd_apply([h_re, h_im], [w["cm_re"], w["cm_imn"]], [[(0, 0), (1, 1)]], "nn", name=nm("ssm_c"))
    (y1,) = _rw(lambda a, u, d: ([jax.nn.gelu(a + d * u)], []),
                [(y0, None, _c0), (z, BRANCH, lambda j: U_BLK)], [(BRANCH, None, F32, _c0)],
                params=[w["ssm_d"]], name=nm("ssm_gelu"))
    gl = _mm(y1, w["ssm_w_glu"], "nn", name=nm("mm_glu"))
    (y_ssm,) = _rw(lambda a, b: ([a * jax.nn.sigmoid(b)], []), [(y1, None, _c0), (gl, None, _c0)],
                   [(BRANCH, None, BF16, _c0)], name=nm("ssm_glu"))
    y_conv = _conv_fwd(z, w["conv_w"], nm("conv_fwd"))
    kv_w = N_KV * HEAD_DIM
    q2, k2, v2 = _rw(lambda q, k, v: ([q, k, v], []),
                     [(z, BRANCH, lambda j: OFF_Q // BRANCH), (z, kv_w, lambda j: OFF_K // kv_w), (z, kv_w, lambda j: OFF_V // kv_w)],
                     [(BRANCH, None, BF16, _c0), (kv_w, None, BF16, _c0), (kv_w, None, BF16, _c0)], name=nm("qkv_bf16"))
    qh, kh, vh = _heads(q2, N_Q), _heads(k2, N_KV), _heads(v2, N_KV)
    y_attn = _unheads(_attn_fwd(qh, kh, vh, bias, w["sinks"], nm("attn_fwd")))
    ys = (y_ssm, y_conv, y_attn)
    bs = [_mm(ys[r], w["w_branch"][r], "nn", name=nm(f"mm_branch{r}")) for r in range(3)]

    def merge(g0, g1, g2, b0, b1, b2):
        return [jax.nn.sigmoid(g0) * b0 + jax.nn.sigmoid(g1) * b1 + jax.nn.sigmoid(g2) * b2], []

    (merged,) = _rw(merge, [(z, D_MODEL, _gate_col(r)) for r in range(3)] + [(b, None, _c0) for b in bs],
                    [(D_MODEL, None, BF16, _c0)], name=nm("merge"))
    x1 = _mm(merged, w["w_out"], "nn", add=x, name=nm("mm_out"))
    hf_in = _rms_fwd(x1, w["norm_ffn"], nm("rms_ffn"))
    hf = _mm(hf_in, w["w_ffn_in"], "nn", tn=1408, n_outer=True, out_dtype=BF16, name=nm("mm_ffn_in"))
    (act,) = _rw(lambda a, b: ([jax.nn.silu(a) * b], []), [(hf, FFN_COLS, lambda j: j), (hf, FFN_COLS, lambda j: FFN_NCOL + j)],
                 [(FFN_HIDDEN, FFN_COLS, BF16, lambda j: j)], ncol=FFN_NCOL, name=nm("swiglu"))
    x2 = _mm(act, w["w_ffn_out"], "nn", add=x1, tk=1408, name=nm("mm_ffn_out"))
    hp = _rms_fwd(x2, w["norm_ple"], nm("rms_ple"))
    pgl = _mm(hp, w["w_ple_gate"], "nn", name=nm("mm_ple_gate"))
    pp = _mm(p_i, w["w_ple_proj"], "nn", name=nm("mm_ple_proj"))
    (x3,) = _rw(lambda xv, a, b: ([xv + jax.nn.sigmoid(a) * b], []), [(x2, None, _c0), (pgl, None, _c0), (pp, None, _c0)],
                [(D_MODEL, None, F32, _c0)], name=nm("ple_add"))
    saved = dict(x=x, p=p_i, h=h, z=z, h_re=h_re, h_im=h_im, y0=y0, y1=y1, gl=gl, ys=ys, qh=qh, kh=kh, vh=vh,
                 bs=bs, merged=merged, x1=x1, hf_in=hf_in, hf=hf, act=act, x2=x2, hp=hp, pgl=pgl, pp=pp)
    return x3, saved


def _layer_bwd(dx3, s, w, bias, li):
    nm = lambda n: f"{n}_l{li}"
    g = {}
    z = s["z"]
    def ple_b(d, a, b):
        _, vjp = jax.vjp(lambda a_, b_: jax.nn.sigmoid(a_) * b_, a, b)
        return list(vjp(d)), []

    dpgl, dpp = _rw(ple_b, [(dx3, None, _c0), (s["pgl"], None, _c0), (s["pp"], None, _c0)],
                    [(D_MODEL, None, BF16, _c0)] * 2, name=nm("ple_bwd"))
    g["w_ple_proj"] = _mm(s["p"], dpp, "tn", name=nm("mmg_ple_proj"))
    g["w_ple_gate"] = _mm(s["hp"], dpgl, "tn", name=nm("mmg_ple_gate"))
    dhp = _mm(dpgl, w["w_ple_gate"], "nt", name=nm("mmb_ple_gate"))
    dx2, g["norm_ple"] = _rms_bwd(s["x2"], dhp, dx3, w["norm_ple"], nm("rmsb_ple"))
    dact = _mm(dx2, w["w_ffn_out"], "nt", tn=1408, name=nm("mmb_ffn_out"))
    g["w_ffn_out"] = _mm(s["act"], dx2, "tn", tm=1408, name=nm("mmg_ffn_out"))

    def swiglu_b(a, b, d):
        _, vjp = jax.vjp(lambda a_, b_: jax.nn.silu(a_) * b_, a, b)
        return list(vjp(d)), []

    dhf_a, dhf_b = _rw(swiglu_b, [(s["hf"], FFN_COLS, lambda j: j), (s["hf"], FFN_COLS, lambda j: FFN_NCOL + j), (dact, FFN_COLS, lambda j: j)],
                       [(FFN_HIDDEN, FFN_COLS, BF16, lambda j: j)] * 2, ncol=FFN_NCOL, name=nm("swiglu_bwd"))
    g["w_ffn_in"] = jnp.concatenate([_mm(s["hf_in"], dhf_a, "tn", tn=1408, name=nm("mmg_ffn_in_a")),
                                     _mm(s["hf_in"], dhf_b, "tn", tn=1408, name=nm("mmg_ffn_in_b"))], axis=1)
    dhf_in = _mm(dhf_a, w["w_ffn_in"], "nt", tk=1408, name=nm("mmb_ffn_in_a"))
    dhf_in = _mm(dhf_b, w["w_ffn_in"], "nt", tk=1408, b_k0=FFN_HIDDEN, add=dhf_in, name=nm("mmb_ffn_in_b"))
    dx1, g["norm_ffn"] = _rms_bwd(s["x1"], dhf_in, dx2, w["norm_ffn"], nm("rmsb_ffn"))
    dmerged = _mm(dx1, w["w_out"], "nt", name=nm("mmb_out"))
    g["w_out"] = _mm(s["merged"], dx1, "tn", name=nm("mmg_out"))

    def merge_b(d, g0, g1, g2, b0, b1, b2):
        outs_g, outs_b = [], []
        for gate, br in ((g0, b0), (g1, b1), (g2, b2)):
            sg = jax.nn.sigmoid(gate)
            outs_g.append(d * br * sg * (1.0 - sg))
            outs_b.append(d * sg)
        return outs_g + outs_b, []

    res = _rw(merge_b, [(dmerged, None, _c0)] + [(z, D_MODEL, _gate_col(r)) for r in range(3)] + [(b, None, _c0) for b in s["bs"]],
              [(D_MODEL, None, BF16, _c0)] * 6, name=nm("merge_bwd"))
    dgates, dbs = res[:3], res[3:]
    dys = [_mm(dbs[r], w["w_branch"][r], "nt", name=nm(f"mmb_branch{r}")) for r in range(3)]
    g["w_branch"] = jnp.stack([_mm(s["ys"][r], dbs[r], "tn", name=nm(f"mmg_branch{r}")) for r in range(3)])
    doh = _heads(dys[2], N_Q)
    dqh, dkc, dkp, dvc, dvp, dbias, dsink = _attn_bwd(s["qh"], s["kh"], s["vh"], doh, bias, w["sinks"], nm("attn_bwd"))
    dq, dk, dv = _unheads(dqh), _unheads(_shift_blocks(dkc, dkp)), _unheads(_shift_blocks(dvc, dvp))
    g["sinks"] = jnp.sum(dsink, axis=(1, 2))
    dcb, dcc, dcx, dconv = _conv_bwd(dys[1], z, w["conv_w"], nm("conv_bwd"))
    g["conv_w"] = dconv[0:3]
    def glu_b(d, y1, gl):
        sg = jax.nn.sigmoid(gl)
        return [d * y1 * sg * (1.0 - sg), d * sg], []

    dgl, dy1a = _rw(glu_b, [(dys[0], None, _c0), (s["y1"], None, _c0), (s["gl"], None, _c0)],
                    [(BRANCH, None, BF16, _c0), (BRANCH, None, F32, _c0)], name=nm("ssm_glu_bwd"))
    g["ssm_w_glu"] = _mm(s["y1"], dgl, "tn", name=nm("mmg_glu"))
    dy1b = _mm(dgl, w["ssm_w_glu"], "nt", name=nm("mmb_glu"))

    def gelu_b(da, db, a, u, d):
        _, vjp = jax.vjp(lambda pre: jax.nn.gelu(pre), a + d * u)
        (dy0,) = vjp(da + db)
        return [dy0, dy0 * d], [jnp.sum(dy0 * u, axis=0, keepdims=True)]

    dy0, du_a, g["ssm_d"] = _rw(gelu_b, [(dy1a, None, _c0), (dy1b, None, _c0), (s["y0"], None, _c0), (z, BRANCH, lambda j: U_BLK)],
                                [(BRANCH, None, BF16, _c0), (BRANCH, None, F32, _c0)], params=[w["ssm_d"]],
                                reds=[((1, BRANCH), None, _c0)], name=nm("ssm_gelu_bwd"))
    dh_re, dh_im = _bd_apply([dy0], [w["cm_re"], w["cm_imn"]], [[(0, 0)], [(0, 1)]], "nt", name=nm("ssmb_c"))
    sb, cb = SSM_STATES // SSM_BLOCKS, BRANCH // SSM_BLOCKS
    g["cm_re"], g["cm_imn"] = _bd_grads([s["h_re"], s["h_im"], dy0], [sb, sb, cb], [(0, 2), (1, 2)], name=nm("ssmg_c"))
    l_re, l_im, da_re, da_im = _scan(dh_re, dh_im, *w["scan_rev"], reverse=True, hr=s["h_re"], hi=s["h_im"], name=nm("ssm_scan_bwd"))
    g["a_re"], g["a_im"] = jnp.sum(da_re, axis=0), jnp.sum(da_im, axis=0)
    (du_b,) = _bd_apply([l_re, l_im], [w["wb_re"], w["wb_im"]], [[(0, 0), (1, 1)]], "nt", name=nm("ssmb_bu"))
    g["wb_re"], g["wb_im"] = _bd_grads([z, l_re, l_im], [cb, sb, sb], [(0, 1), (0, 2)], col_blocks=[U_BLK, 0, 0], name=nm("ssmg_bu"))
    dz = jnp.concatenate(list(dgates) + [(du_a + du_b).astype(BF16), dcb, dcc, dcx, dq.astype(BF16), dk.astype(BF16), dv.astype(BF16)], axis=1)
    g["w_in"] = _gates_last(_mm(s["h"], dz, "tn", tm=512, tn=2944, name=nm("mmg_in")))
    dh = _mm(dz, w["w_in"], "nt", tk=2944, name=nm("mmb_in"))
    dx, g["norm_mix"] = _rms_bwd(s["x"], dh, dx1, w["norm_mix"], nm("rmsb_mix"))
    return dx, g, dbias


def _loss_and_seed(x, target, g_final):
    def fn(xv, tv, gv):
        y, vjp = jax.vjp(_rms, xv, gv)
        err = y - tv
        dx, dg = vjp(err * (1.0 / D_MODEL))
        return [dx], [jnp.sum(err * err, axis=0, keepdims=True) * (0.5 / D_MODEL), dg]

    return _rw(fn, [(x, None, _c0), (target, None, _c0)], [(D_MODEL, None, F32, _c0)], params=[g_final],
               reds=[((1, D_MODEL), None, _c0)] * 2, name="loss_head")


def _local_step(x, p, target, wt):
    bias, bias_vjp = jax.vjp(_band_bias, wt["rel_bias"])
    layers, disc_vjps = [], []
    for i in range(DEPTH):
        ssm_p = [wt[k][i] for k in ("ssm_lambda_re", "ssm_lambda_im", "ssm_b_re", "ssm_b_im", "ssm_c_re", "ssm_c_im", "ssm_log_dt")]
        (a_re, a_im, wb_re, wb_im, cm_re, cm_imn), disc_vjp = jax.vjp(_ssm_disc, *ssm_p)
        scan_fwd, scan_rev = _scan_tables(a_re, a_im)
        layers.append(dict(
            norm_mix=wt["norm_mix"][i][None], w_in=_gates_first(wt["w_in"][i]), wb_re=wb_re.astype(BF16), wb_im=wb_im.astype(BF16),
            cm_re=cm_re.astype(BF16), cm_imn=cm_imn.astype(BF16), scan_fwd=scan_fwd, scan_rev=scan_rev,
            ssm_d=wt["ssm_d"][i][None], ssm_w_glu=wt["ssm_w_glu"][i], conv_w=wt["conv_w"][i],
            sinks=wt["attn_sinks"][i][:, None], w_branch=wt["w_branch"][i], w_out=wt["w_out"][i],
            norm_ffn=wt["norm_ffn"][i][None], w_ffn_in=wt["w_ffn_in"][i], w_ffn_out=wt["w_ffn_out"][i],
            norm_ple=wt["norm_ple"][i][None], w_ple_gate=wt["w_ple_gate"][i], w_ple_proj=wt["w_ple_proj"][i]))
        disc_vjps.append(disc_vjp)

    saved = []
    for i in range(DEPTH):
        x, s = _layer_fwd(x, p[i], layers[i], bias, i)
        saved.append(s)
    dx, loss_cols, g_final = _loss_and_seed(x, target, wt["norm_final"][None])
    loss = jnp.sum(loss_cols)

    per_layer = [None] * DEPTH
    dbias = None
    for i in reversed(range(DEPTH)):
        dx, g, db = _layer_bwd(dx, saved[i], layers[i], bias, i)
        dbias = db if dbias is None else dbias + db
        (g["ssm_lambda_re"], g["ssm_lambda_im"], g["ssm_b_re"], g["ssm_b_im"], g["ssm_c_re"], g["ssm_c_im"], g["ssm_log_dt"]) = \
            disc_vjps[i]((g.pop("a_re"), g.pop("a_im"), g.pop("wb_re"), g.pop("wb_im"), g.pop("cm_re"), g.pop("cm_imn")))
        g["attn_sinks"] = g.pop("sinks")
        for k in ("norm_mix", "norm_ffn", "norm_ple", "ssm_d"):
            g[k] = g[k][0]
        per_layer[i] = g
    big_names = [name for name, _, _ in BIG]
    big = {k: [per_layer[i][k] for i in range(DEPTH)] for k in big_names}
    small = {k: jnp.stack([per_layer[i][k] for i in range(DEPTH)]) for k in per_layer[0] if k not in big_names}
    (small["rel_bias"],) = bias_vjp(dbias)
    small["norm_final"] = g_final[0]
    return loss, dx, small, big


HBM_SPEC = pl.BlockSpec(memory_space=pltpu.HBM)


def _position():
    x, y, c = lax.axis_index("x"), lax.axis_index("y"), lax.axis_index("c")
    other_chips = [(1 - x, y), (x, 1 - y), (1 - x, 1 - y)]
    return x, y, c, other_chips


def _row_chunks(rows, n=COPY_CHUNKS):
    rq = rows // n
    assert rq * n == rows and rq % 16 == 0, rows
    return [pl.ds(q * rq, rq) for q in range(n)]


def _place(buf, val, idx, name):
    n, rows, width = buf.shape
    tm = _pick(rows, 512, 16)

    def body(idx_ref, buf_ref, v_ref, o_ref):
        o_ref[0] = v_ref[...]

    grid_spec = pltpu.PrefetchScalarGridSpec(
        num_scalar_prefetch=1, grid=(rows // tm,),
        in_specs=[pl.BlockSpec(memory_space=pl.ANY), pl.BlockSpec((tm, width), lambda i, idx_ref: (i, 0))],
        out_specs=pl.BlockSpec((1, tm, width), lambda i, idx_ref: (idx_ref[0], i, 0)))
    return pl.pallas_call(
        body, name=name, grid_spec=grid_spec, out_shape=jax.ShapeDtypeStruct(buf.shape, buf.dtype), input_output_aliases={1: 0},
        compiler_params=pltpu.CompilerParams(dimension_semantics=("arbitrary",)),
    )(jnp.asarray(idx, jnp.int32).reshape(1), buf, val)


def _allgather_weights(locals_):
    nb, nq = len(locals_), COPY_CHUNKS
    chunks = [_row_chunks(a.shape[1]) for a in locals_]

    def body(*refs):
        w_refs, out_refs = refs[:nb], refs[nb:2 * nb]
        send_sems, recv_sems = refs[2 * nb:]
        x, y, c, chips = _position()
        me = 2 * x + y
        sibling = (x, y, 1 - c)

        def copy(b, kind, q, src, dst, to):
            k = (b * 6 + kind) * nq + q
            return pltpu.make_async_remote_copy(src_ref=src, dst_ref=dst, send_sem=send_sems.at[k], recv_sem=recv_sems.at[k],
                                                device_id=to, device_id_type=MESH)

        first = [copy(b, j, q, w_refs[b].at[c, chunks[b][q]], out_refs[b].at[me, c, chunks[b][q]], (*chip, c))
                 for q in range(nq) for b in range(nb) for j, chip in enumerate(chips)]
        for cp in first:
            cp.start()
        passed = []
        for q in range(nq):
            for b in range(nb):
                for j, (px, py) in enumerate(chips):
                    landed = out_refs[b].at[2 * px + py, c, chunks[b][q]]
                    copy(b, j, q, landed, landed, (px, py, c)).wait_recv()
                    fwd = copy(b, 3 + j, q, landed, landed, sibling)
                    fwd.start()
                    passed.append(fwd)
        for q in range(nq):
            for b in range(nb):
                for j, (px, py) in enumerate(chips):
                    landed = out_refs[b].at[2 * px + py, 1 - c, chunks[b][q]]
                    copy(b, 3 + j, q, landed, landed, sibling).wait_recv()
        for cp in first + passed:
            cp.wait_send()

    return pl.pallas_call(
        body, name="allgather_weights", in_specs=[HBM_SPEC] * nb, out_specs=[HBM_SPEC] * nb,
        out_shape=[jax.ShapeDtypeStruct((N_SHARD,) + a.shape, a.dtype) for a in locals_],
        scratch_shapes=[pltpu.SemaphoreType.DMA((nb * 6 * nq,)), pltpu.SemaphoreType.DMA((nb * 6 * nq,))],
    )(*locals_)


def _sibling_exchange(bufs):
    nb, nq, ns = len(bufs), COPY_CHUNKS, N_SHARD
    chunks = [_row_chunks(a.shape[2]) for a in bufs]

    def body(*refs):
        g_refs, got_refs = refs[:nb], refs[nb:2 * nb]
        send_sems, recv_sems = refs[2 * nb:]
        x, y, c, _ = _position()
        swaps = [pltpu.make_async_remote_copy(src_ref=g_refs[b].at[s, 1 - c, chunks[b][q]], dst_ref=got_refs[b].at[s, chunks[b][q]],
                                              send_sem=send_sems.at[(b * ns + s) * nq + q], recv_sem=recv_sems.at[(b * ns + s) * nq + q],
                                              device_id=(x, y, 1 - c), device_id_type=MESH)
                 for b in range(nb) for s in range(ns) for q in range(nq)]
        for cp in swaps:
            cp.start()
        for cp in swaps:
            cp.wait()

    return pl.pallas_call(
        body, name="grad_sibling_exchange", in_specs=[HBM_SPEC] * nb, out_specs=[HBM_SPEC] * nb,
        out_shape=[jax.ShapeDtypeStruct((ns,) + a.shape[2:], a.dtype) for a in bufs],
        scratch_shapes=[pltpu.SemaphoreType.DMA((nb * ns * nq,)), pltpu.SemaphoreType.DMA((nb * ns * nq,))],
    )(*bufs)


def _chip_exchange(parts):
    nb, nq = len(parts), COPY_CHUNKS
    chunks = [_row_chunks(a.shape[1]) for a in parts]

    def body(*refs):
        b_refs, got_refs = refs[:nb], refs[nb:2 * nb]
        send_sems, recv_sems = refs[2 * nb:]
        x, y, c, chips = _position()
        sends = [pltpu.make_async_remote_copy(src_ref=b_refs[b].at[2 * px + py, chunks[b][q]], dst_ref=got_refs[b].at[j, chunks[b][q]],
                                              send_sem=send_sems.at[(b * 3 + j) * nq + q], recv_sem=recv_sems.at[(b * 3 + j) * nq + q],
                                              device_id=(px, py, c), device_id_type=MESH)
                 for q in range(nq) for b in range(nb) for j, (px, py) in enumerate(chips)]
        for cp in sends:
            cp.start()
        for cp in sends:
            cp.wait()

    return pl.pallas_call(
        body, name="grad_chip_exchange", in_specs=[HBM_SPEC] * nb, out_specs=[HBM_SPEC] * nb,
        out_shape=[jax.ShapeDtypeStruct((N_SHARD - 1,) + a.shape[1:], a.dtype) for a in parts],
        scratch_shapes=[pltpu.SemaphoreType.DMA((nb * 3 * nq,)), pltpu.SemaphoreType.DMA((nb * 3 * nq,))],
    )(*parts)


def _sibling_gather(halves):
    nb, nq = len(halves), 2 * COPY_CHUNKS
    chunks = [_row_chunks(a.shape[0], nq) for a in halves]

    def body(*refs):
        h_refs, out_refs = refs[:nb], refs[nb:2 * nb]
        send_sems, recv_sems = refs[2 * nb:]
        x, y, c, _ = _position()

        def chunk(b, q, half_idx):
            return pltpu.make_async_remote_copy(src_ref=h_refs[b].at[chunks[b][q]], dst_ref=out_refs[b].at[half_idx, chunks[b][q]],
                                                send_sem=send_sems.at[b * nq + q], recv_sem=recv_sems.at[b * nq + q],
                                                device_id=(x, y, 1 - c), device_id_type=MESH)

        pushes = [chunk(b, q, c) for b in range(nb) for q in range(nq)]
        for cp in pushes:
            cp.start()
        for b in range(nb):
            for q in range(nq):
                chunk(b, q, 1 - c).wait_recv()
        for cp in pushes:
            cp.wait_send()

    return pl.pallas_call(
        body, name="grad_sibling_gather", in_specs=[HBM_SPEC] * nb, out_specs=[HBM_SPEC] * nb,
        out_shape=[jax.ShapeDtypeStruct((2,) + a.shape, a.dtype) for a in halves],
        scratch_shapes=[pltpu.SemaphoreType.DMA((nb * nq,)), pltpu.SemaphoreType.DMA((nb * nq,))],
    )(*halves)


def _gather_partials(part):
    r, lanes = part.shape

    def body(p_ref, out_ref, send_sems, recv_sems):
        x, y, c, _ = _position()
        flips = [(fx, fy, fc) for fx in (0, 1) for fy in (0, 1) for fc in (0, 1)][1:]
        sends = []
        for k, (fx, fy, fc) in enumerate(flips):
            cp = pltpu.make_async_remote_copy(src_ref=p_ref, dst_ref=out_ref.at[4 * x + 2 * y + c], send_sem=send_sems.at[k],
                                              recv_sem=recv_sems.at[k], device_id=(x ^ fx, y ^ fy, c ^ fc), device_id_type=MESH)
            cp.start()
            sends.append(cp)
        for k, (fx, fy, fc) in enumerate(flips):
            src = out_ref.at[4 * (x ^ fx) + 2 * (y ^ fy) + (c ^ fc)]
            pltpu.make_async_remote_copy(src_ref=src, dst_ref=src, send_sem=send_sems.at[k], recv_sem=recv_sems.at[k],
                                         device_id=(x ^ fx, y ^ fy, c ^ fc), device_id_type=MESH).wait_recv()
        for cp in sends:
            cp.wait_send()

    return pl.pallas_call(
        body, name="small_gather_partials", in_specs=[HBM_SPEC], out_specs=HBM_SPEC,
        out_shape=jax.ShapeDtypeStruct((8, r, lanes), part.dtype),
        scratch_shapes=[pltpu.SemaphoreType.DMA((7,)), pltpu.SemaphoreType.DMA((7,))],
    )(part)


def _sum_leading(stack, name, also_bf16=False):
    k, r, lanes = stack.shape
    tm = _pick(r, 256, 16)
    outs = [jax.ShapeDtypeStruct((r, lanes), F32)] + ([jax.ShapeDtypeStruct((r, lanes), BF16)] if also_bf16 else [])

    def body(s_ref, *o_refs):
        acc = s_ref[0].astype(F32)
        for i in range(1, k):
            acc = acc + s_ref[i].astype(F32)
        for o in o_refs:
            o[...] = acc.astype(o.dtype)

    spec = pl.BlockSpec((tm, lanes), lambda i: (i, 0))
    return pl.pallas_call(
        body, name=name, grid=(r // tm,), in_specs=[pl.BlockSpec((k, tm, lanes), lambda i: (0, i, 0))],
        out_specs=[spec] * len(outs), out_shape=outs,
        compiler_params=pltpu.CompilerParams(dimension_semantics=("parallel",)),
    )(stack)


def _add_pair(g2, got, half, name):
    ns, _, rows, width = g2.shape
    tm = _pick(rows, 256, 16)
    spec = pl.BlockSpec((1, tm, width), lambda s, i, h_ref: (s, i, 0))

    def body(h_ref, a_ref, b_ref, f_ref, o_ref):
        acc = a_ref[0] + b_ref[...]
        f_ref[...] = acc
        o_ref[...] = acc.astype(BF16)

    grid_spec = pltpu.PrefetchScalarGridSpec(
        num_scalar_prefetch=1, grid=(ns, rows // tm),
        in_specs=[pl.BlockSpec((1, 1, tm, width), lambda s, i, h_ref: (s, h_ref[0], i, 0)), spec], out_specs=[spec, spec])
    return pl.pallas_call(
        body, name=name, grid_spec=grid_spec,
        out_shape=[jax.ShapeDtypeStruct(got.shape, F32), jax.ShapeDtypeStruct(got.shape, BF16)],
        compiler_params=pltpu.CompilerParams(dimension_semantics=("parallel", "parallel")),
    )(jnp.asarray(half, jnp.int32).reshape(1), g2, got)


def _add_own(parts, got, mine, name):
    _, rows, width = parts.shape
    tm = _pick(rows, 256, 16)

    def body(m_ref, p_ref, g_ref, out_ref):
        acc = p_ref[0]
        for j in range(g_ref.shape[0]):
            acc = acc + g_ref[j].astype(F32)
        out_ref[...] = acc

    grid_spec = pltpu.PrefetchScalarGridSpec(
        num_scalar_prefetch=1, grid=(rows // tm,),
        in_specs=[pl.BlockSpec((1, tm, width), lambda i, m_ref: (m_ref[0], i, 0)),
                  pl.BlockSpec((got.shape[0], tm, width), lambda i, m_ref: (0, i, 0))],
        out_specs=pl.BlockSpec((tm, width), lambda i, m_ref: (i, 0)))
    return pl.pallas_call(
        body, name=name, grid_spec=grid_spec, out_shape=jax.ShapeDtypeStruct((rows, width), F32),
        compiler_params=pltpu.CompilerParams(dimension_semantics=("parallel",)),
    )(jnp.asarray(mine, jnp.int32).reshape(1), parts, got)


def _local_shape(shape, axis):
    return tuple(d // N_SHARD if a == axis else d for a, d in enumerate(shape))


def _big_sizes():
    return [DEPTH * int(np.prod(_local_shape(shape, axis))) for _, shape, axis in FLAT_BIG]


COL_SHARDED = (("w_in", IN_WIDTH // N_SHARD), ("w_ffn_in", 2 * FFN_HIDDEN // N_SHARD))
FLAT_BIG = tuple(entry for entry in BIG if entry[0] not in [name for name, _ in COL_SHARDED])
FLAT_ROW_TILE = 256
ELEMENTWISE_BIG = ("conv_w",)


def _three_bf16(w):
    hi = w.astype(BF16)
    r1 = w - hi.astype(F32)
    mid = r1.astype(BF16)
    lo = (r1 - mid.astype(F32)).astype(BF16)
    return jnp.stack([hi, mid, lo], axis=-1)


PIECE_ROWS = 16


def _flat_layout(for_weights):
    pieces = []
    for (name, shape, axis), size in zip(FLAT_BIG, _big_sizes(), strict=True):
        n = size * (3 if for_weights and name in ELEMENTWISE_BIG else 1)
        rows = -(-n // (LANES * PIECE_ROWS)) * PIECE_ROWS
        pieces.append((name, shape, axis, n, rows))
    total = sum(p[-1] for p in pieces)
    half = -(-total // (2 * FLAT_ROW_TILE)) * FLAT_ROW_TILE
    return pieces, half


def _to_rows(flat, rows):
    lead, n = flat.shape[:-1], flat.shape[-1]
    fill = jnp.zeros(lead + (rows * LANES - n,), flat.dtype)
    return jnp.concatenate([flat, fill], axis=-1).reshape(lead + (rows, LANES))


def _pack_local_weights(wl):
    pieces, half = _flat_layout(True)
    parts = [_to_rows((_three_bf16(wl[name]) if name in ELEMENTWISE_BIG else wl[name].astype(BF16)).reshape(-1), rows)
             for name, _, _, _, rows in pieces]
    parts.append(jnp.zeros((2 * half - sum(p[-1] for p in pieces), LANES), BF16))
    return jnp.concatenate(parts, axis=0).reshape(2, half, LANES)


def _unpack_local(flat):
    pieces, _ = _flat_layout(False)
    flat = flat.reshape(-1, LANES)
    out, off = {}, 0
    for name, shape, axis, n, rows in pieces:
        out[name] = flat[off:off + rows].reshape(-1)[:n].reshape((DEPTH,) + _local_shape(shape, axis))
        off += rows
    return out


def _unpack_gathered(gathered):
    pieces, _ = _flat_layout(True)
    out, off = {}, 0
    for name, shape, axis, n, rows in pieces:
        local = (N_SHARD, DEPTH) + _local_shape(shape, axis)
        seg = gathered[:, off:off + rows].reshape(N_SHARD, -1)[:, :n]
        off += rows
        if name in ELEMENTWISE_BIG:
            parts = seg.reshape(local + (3,)).astype(F32)
            seg = (parts[..., 0] + parts[..., 1]) + parts[..., 2]
        else:
            seg = seg.reshape(local)
        out[name] = jnp.moveaxis(seg, 0, 1 + axis).reshape((DEPTH,) + shape)
    return out


def _join_col_shards(shards, name):
    ns, depth, rows, c = shards.shape
    tm = _pick(rows, 256, 16)

    def body(i_ref, o_ref):
        for s in range(ns):
            o_ref[0, :, c * s:c * (s + 1)] = i_ref[s, 0]

    return pl.pallas_call(
        body, name=name, grid=(depth, rows // tm),
        in_specs=[pl.BlockSpec((ns, 1, tm, c), lambda l, i: (0, l, i, 0))],
        out_specs=pl.BlockSpec((1, tm, ns * c), lambda l, i: (l, i, 0)),
        out_shape=jax.ShapeDtypeStruct((depth, rows, ns * c), shards.dtype),
        compiler_params=pltpu.CompilerParams(dimension_semantics=("parallel", "parallel")),
    )(shards)


def _split_col_shards(full, name):
    rows, width = full.shape
    c = width // N_SHARD
    tm = _pick(rows, 256, 16)

    def body(i_ref, o_ref):
        for s in range(N_SHARD):
            o_ref[s] = i_ref[:, c * s:c * (s + 1)]

    return pl.pallas_call(
        body, name=name, grid=(rows // tm,), in_specs=[pl.BlockSpec((tm, width), lambda i: (i, 0))],
        out_specs=pl.BlockSpec((N_SHARD, tm, c), lambda i: (0, i, 0)),
        out_shape=jax.ShapeDtypeStruct((N_SHARD, rows, c), full.dtype),
        compiler_params=pltpu.CompilerParams(dimension_semantics=("parallel",)),
    )(full)


def _pack_full_grads(big_grads):
    pieces, half = _flat_layout(False)
    parts = []
    for name, shape, axis, _, rows in pieces:
        per_layer = []
        for gfull in big_grads[name]:
            split = gfull.reshape(shape[:axis] + (N_SHARD, shape[axis] // N_SHARD) + shape[axis + 1:])
            per_layer.append(jnp.moveaxis(split, axis, 0).reshape(N_SHARD, -1))
        parts.append(_to_rows(jnp.concatenate(per_layer, axis=1), rows))
    parts.append(jnp.zeros((N_SHARD, 2 * half - sum(p[-1] for p in pieces), LANES), F32))
    return jnp.concatenate(parts, axis=1).reshape(N_SHARD, 2, half, LANES)


def _pack_small(grads):
    flat = jnp.concatenate([grads[name].reshape(-1) for name in SMALL])
    r = -(-flat.shape[0] // (8 * LANES)) * 8
    return jnp.pad(flat, (0, r * LANES - flat.shape[0])).reshape(r, LANES)


def _unpack_small(flat, like):
    flat = flat.reshape(-1)
    out, off = {}, 0
    for name in SMALL:
        size = int(np.prod(like[name].shape))
        out[name] = flat[off:off + size].reshape(like[name].shape)
        off += size
    return out


def _adamw(w, g, m, v, name):
    shape = w.shape
    cols = shape[-1]
    rows = int(np.prod(shape[:-1])) if len(shape) > 1 else 1
    w2, g2, m2, v2 = (a.reshape(rows, cols) for a in (w, g, m, v))

    def fn(wv, gv, mv, vv):
        mn = ADAM_B1 * mv + (1.0 - ADAM_B1) * gv
        vn = ADAM_B2 * vv + (1.0 - ADAM_B2) * jnp.square(gv)
        m_hat = mn / (1.0 - ADAM_B1 ** ADAM_STEP)
        v_hat = vn / (1.0 - ADAM_B2 ** ADAM_STEP)
        delta = -ADAM_LR * (m_hat / (jnp.sqrt(v_hat) + ADAM_EPS) + ADAM_WD * wv)
        return [delta, mn, vn], []

    tm = 256 if rows % 8 == 0 and rows > 256 else rows
    res = _rw(fn, [(a, None, _c0) for a in (w2, g2, m2, v2)], [(cols, None, F32, _c0)] * 3, tm=tm, name=name)
    return [r.reshape(shape) for r in res]


def _step(x, p, target, weights, moments_m, moments_v):
    xi, yi, ci = lax.axis_index("x"), lax.axis_index("y"), lax.axis_index("c")
    chip = 2 * xi + yi
    half_rows = DEPTH // 2 * D_MODEL
    locals_ = [_pack_local_weights(weights)] + [weights[name].astype(BF16).reshape(2, half_rows, c) for name, c in COL_SHARDED]
    gathered = _allgather_weights(locals_)
    gathered = [_place(g.reshape(N_SHARD, 2 * a.shape[1], a.shape[2]), a.reshape(2 * a.shape[1], a.shape[2]), chip, f"place_own_weights_{k}")
                for k, (g, a) in enumerate(zip(gathered, locals_, strict=True))]
    wt = dict(_unpack_gathered(gathered[0]))
    for (name, c), g in zip(COL_SHARDED, gathered[1:], strict=True):
        wt[name] = _join_col_shards(g.reshape(N_SHARD, DEPTH, D_MODEL, c), f"join_col_shards_{name}")
    for name in SMALL:
        wt[name] = weights[name]
    loss, dx, small_grads, big_grads = _local_step(x[0], p[:, 0], target[0], wt)
    loss = lax.psum(loss, ("x", "y", "c"))
    bufs = [_pack_full_grads(big_grads)]
    for name, c in COL_SHARDED:
        shards = [_split_col_shards(g, f"split_col_shards_{name}_l{li}") for li, g in enumerate(big_grads[name])]
        bufs.append(jnp.stack(shards, axis=1).reshape(N_SHARD, 2, half_rows, c))
    gots = _sibling_exchange(bufs)
    sums = [_add_pair(b, g, ci, f"grad_add_sibling_{k}") for k, (b, g) in enumerate(zip(bufs, gots, strict=True))]
    others = _chip_exchange([s_bf16 for _, s_bf16 in sums])
    halves = [_add_own(s_f32, o, chip, f"grad_add_chips_{k}") for k, ((s_f32, _), o) in enumerate(zip(sums, others, strict=True))]
    both = [_place(b, h, ci, f"place_own_half_{k}") for k, (b, h) in enumerate(zip(_sibling_gather(halves), halves, strict=True))]
    reduced = _unpack_local(both[0])
    for (name, c), b in zip(COL_SHARDED, both[1:], strict=True):
        reduced[name] = b.reshape(DEPTH, D_MODEL, c)
    small_part = _pack_small(small_grads)
    small_all = _place(_gather_partials(small_part), small_part, 4 * xi + 2 * yi + ci, "place_own_small")
    reduced.update(_unpack_small(_sum_leading(small_all, "small_sum")[0], {k: weights[k] for k in SMALL}))
    outs_g, outs_d, outs_m, outs_v = [], [], [], []
    for name in WEIGHTS:
        d, mn, vn = _adamw(weights[name], reduced[name], moments_m[name], moments_v[name], f"adamw_{name}")
        outs_g.append(reduced[name])
        outs_d.append(d)
        outs_m.append(mn)
        outs_v.append(vn)
    return (loss, dx[None], *outs_g, *outs_d, *outs_m, *outs_v)


def kernel(x, p, rel_bias, norm_mix, w_in, ssm_lambda_re, ssm_lambda_im, ssm_b_re, ssm_b_im, ssm_c_re, ssm_c_im, ssm_d, ssm_log_dt, ssm_w_glu, conv_w, attn_sinks, w_branch, w_out, norm_ffn, w_ffn_in, w_ffn_out, norm_ple, w_ple_gate, w_ple_proj, norm_final, loss_target, m_rel_bias, m_norm_mix, m_w_in, m_ssm_lambda_re, m_ssm_lambda_im, m_ssm_b_re, m_ssm_b_im, m_ssm_c_re, m_ssm_c_im, m_ssm_d, m_ssm_log_dt, m_ssm_w_glu, m_conv_w, m_attn_sinks, m_w_branch, m_w_out, m_norm_ffn, m_w_ffn_in, m_w_ffn_out, m_norm_ple, m_w_ple_gate, m_w_ple_proj, m_norm_final, v_rel_bias, v_norm_mix, v_w_in, v_ssm_lambda_re, v_ssm_lambda_im, v_ssm_b_re, v_ssm_b_im, v_ssm_c_re, v_ssm_c_im, v_ssm_d, v_ssm_log_dt, v_ssm_w_glu, v_conv_w, v_attn_sinks, v_w_branch, v_w_out, v_norm_ffn, v_w_ffn_in, v_w_ffn_out, v_norm_ple, v_w_ple_gate, v_w_ple_proj, v_norm_final):
    weights = dict(rel_bias=rel_bias, norm_mix=norm_mix, w_in=w_in, ssm_lambda_re=ssm_lambda_re, ssm_lambda_im=ssm_lambda_im,
                   ssm_b_re=ssm_b_re, ssm_b_im=ssm_b_im, ssm_c_re=ssm_c_re, ssm_c_im=ssm_c_im, ssm_d=ssm_d, ssm_log_dt=ssm_log_dt,
                   ssm_w_glu=ssm_w_glu, conv_w=conv_w, attn_sinks=attn_sinks, w_branch=w_branch, w_out=w_out, norm_ffn=norm_ffn,
                   w_ffn_in=w_ffn_in, w_ffn_out=w_ffn_out, norm_ple=norm_ple, w_ple_gate=w_ple_gate, w_ple_proj=w_ple_proj,
                   norm_final=norm_final)
    moments_m = dict(rel_bias=m_rel_bias, norm_mix=m_norm_mix, w_in=m_w_in, ssm_lambda_re=m_ssm_lambda_re, ssm_lambda_im=m_ssm_lambda_im,
                     ssm_b_re=m_ssm_b_re, ssm_b_im=m_ssm_b_im, ssm_c_re=m_ssm_c_re, ssm_c_im=m_ssm_c_im, ssm_d=m_ssm_d,
                     ssm_log_dt=m_ssm_log_dt, ssm_w_glu=m_ssm_w_glu, conv_w=m_conv_w, attn_sinks=m_attn_sinks, w_branch=m_w_branch,
                     w_out=m_w_out, norm_ffn=m_norm_ffn, w_ffn_in=m_w_ffn_in, w_ffn_out=m_w_ffn_out, norm_ple=m_norm_ple,
                     w_ple_gate=m_w_ple_gate, w_ple_proj=m_w_ple_proj, norm_final=m_norm_final)
    moments_v = dict(rel_bias=v_rel_bias, norm_mix=v_norm_mix, w_in=v_w_in, ssm_lambda_re=v_ssm_lambda_re, ssm_lambda_im=v_ssm_lambda_im,
                     ssm_b_re=v_ssm_b_re, ssm_b_im=v_ssm_b_im, ssm_c_re=v_ssm_c_re, ssm_c_im=v_ssm_c_im, ssm_d=v_ssm_d,
                     ssm_log_dt=v_ssm_log_dt, ssm_w_glu=v_ssm_w_glu, conv_w=v_conv_w, attn_sinks=v_attn_sinks, w_branch=v_w_branch,
                     w_out=v_w_out, norm_ffn=v_norm_ffn, w_ffn_in=v_w_ffn_in, w_ffn_out=v_w_ffn_out, norm_ple=v_norm_ple,
                     w_ple_gate=v_w_ple_gate, w_ple_proj=v_w_ple_proj, norm_final=v_norm_final)
    return _step(x, p, loss_target, weights, moments_m, moments_v)
```

```python
import functools
import math

import numpy as np

import jax
import jax.numpy as jnp
from jax import lax
from jax.experimental import pallas as pl
from jax.experimental.pallas import tpu as pltpu

F32, BF16 = jnp.float32, jnp.bfloat16
MESH = pl.DeviceIdType.MESH

D_MODEL = 1024
DEPTH = 4
PLE_DIM = 256
BRANCH = 512
N_GROUPS = 32
GROUP_CH = 16
N_STATE = 64
SSM_STATES = N_GROUPS * N_STATE
SSM_BLOCKS = 4
HEAD_DIM = 64
N_Q = 8
N_KV = 2
GQA = N_Q // N_KV
WINDOW = 128
ATTN_SCALE = 1.0 / math.sqrt(HEAD_DIM)
REL_BUCKETS = 32
REL_MAX_DIST = 128
FFN_HIDDEN = 2816
FFN_COLS = 1408
FFN_NCOL = FFN_HIDDEN // FFN_COLS
IN_WIDTH = 5888
RMS_EPS = 1e-6
NEG = -1e30

ADAM_LR, ADAM_B1, ADAM_B2, ADAM_EPS, ADAM_WD, ADAM_STEP = 0.001, 0.9, 0.999, 1e-08, 0.01, 10

N_SHARD = 4
LANES = 1024
COPY_CHUNKS = 4

GATES_WIDTH = 3 * D_MODEL
MAIN_WIDTH = IN_WIDTH - GATES_WIDTH
OFF_G, OFF_U, OFF_CB, OFF_CC, OFF_CX, OFF_Q, OFF_K, OFF_V = 0, 3072, 3584, 4096, 4608, 5120, 5632, 5760
U_BLK, CB_BLK, CC_BLK, CX_BLK = OFF_U // BRANCH, OFF_CB // BRANCH, OFF_CC // BRANCH, OFF_CX // BRANCH

BIG = (
    ("w_in", (D_MODEL, IN_WIDTH), 1),
    ("ssm_w_glu", (BRANCH, BRANCH), 0),
    ("conv_w", (3, BRANCH), 1),
    ("w_branch", (3, BRANCH, D_MODEL), 2),
    ("w_out", (D_MODEL, D_MODEL), 0),
    ("w_ffn_in", (D_MODEL, 2 * FFN_HIDDEN), 1),
    ("w_ffn_out", (FFN_HIDDEN, D_MODEL), 0),
    ("w_ple_gate", (D_MODEL, D_MODEL), 0),
    ("w_ple_proj", (PLE_DIM, D_MODEL), 1),
)
SMALL = ("rel_bias", "norm_mix", "ssm_lambda_re", "ssm_lambda_im", "ssm_b_re", "ssm_b_im", "ssm_c_re", "ssm_c_im",
         "ssm_d", "ssm_log_dt", "attn_sinks", "norm_ffn", "norm_ple", "norm_final")
WEIGHTS = ("rel_bias", "norm_mix", "w_in", "ssm_lambda_re", "ssm_lambda_im", "ssm_b_re", "ssm_b_im", "ssm_c_re",
           "ssm_c_im", "ssm_d", "ssm_log_dt", "ssm_w_glu", "conv_w", "attn_sinks", "w_branch", "w_out", "norm_ffn",
           "w_ffn_in", "w_ffn_out", "norm_ple", "w_ple_gate", "w_ple_proj", "norm_final")


def _c0(j):
    return 0


def _pick(n, cap, unit=128):
    if n <= cap:
        return n
    best = None
    for t in range(unit, cap + 1, unit):
        if n % t == 0:
            best = t
    assert best is not None, (n, cap, unit)
    return best


_DIMS = {"nn": ((1,), (0,)), "nt": ((1,), (1,)), "tn": ((0,), (0,))}


def _mm(a, b, mode, *, name, out_dtype=F32, add=None, tm=1024, tn=1024, tk=1024, b_k0=0, n_outer=False):
    if mode == "nn":
        (m, k), (k2, n) = a.shape, b.shape
    elif mode == "nt":
        (m, k), (n, k2) = a.shape, b.shape
    else:
        (k, m), (k2, n) = a.shape, b.shape
    assert k == k2 or (mode == "nt" and b_k0 + k <= k2), (a.shape, b.shape, mode)
    tm, tn, tk = _pick(m, tm, 128 if mode == "tn" else 8), _pick(n, tn), _pick(k, tk, 128 if mode != "tn" else 8)
    nk = k // tk
    assert b_k0 % tk == 0
    kb0 = b_k0 // tk
    def at(f):
        return (lambda j, i, kk: f(i, j, kk)) if n_outer else f

    a_spec = pl.BlockSpec((tk, tm), at(lambda i, j, kk: (kk, i))) if mode == "tn" else pl.BlockSpec((tm, tk), at(lambda i, j, kk: (i, kk)))
    b_spec = (pl.BlockSpec((tn, tk), at(lambda i, j, kk: (j, kb0 + kk))) if mode == "nt"
              else pl.BlockSpec((tk, tn), at(lambda i, j, kk: (kk, j))))
    o_spec = pl.BlockSpec((tm, tn), at(lambda i, j, kk: (i, j)))
    dims = (_DIMS[mode], ((), ()))
    has_add = add is not None

    def body(*refs):
        a_ref, b_ref = refs[0], refs[1]
        add_ref = refs[2] if has_add else None
        o_ref, acc_ref = refs[-2], refs[-1]
        part = lax.dot_general(a_ref[...].astype(BF16), b_ref[...].astype(BF16), dims, preferred_element_type=F32)

        def finish(acc):
            if has_add:
                acc = acc + add_ref[...]
            o_ref[...] = acc.astype(o_ref.dtype)

        if nk == 1:
            finish(part)
        else:
            kk = pl.program_id(2)

            @pl.when(kk == 0)
            def _():
                acc_ref[...] = part

            @pl.when(kk > 0)
            def _():
                acc_ref[...] += part

            @pl.when(kk == nk - 1)
            def _():
                finish(acc_ref[...])

    operands = [a, b] + ([add] if has_add else [])
    in_specs = [a_spec, b_spec] + ([o_spec] if has_add else [])
    return pl.pallas_call(
        body, name=name, grid=(n // tn, m // tm, nk) if n_outer else (m // tm, n // tn, nk), in_specs=in_specs, out_specs=o_spec,
        out_shape=jax.ShapeDtypeStruct((m, n), out_dtype),
        scratch_shapes=[pltpu.VMEM((tm, tn) if nk > 1 else (8, 128), F32)],
        compiler_params=pltpu.CompilerParams(dimension_semantics=("parallel", "parallel", "arbitrary")),
    )(*operands)


def _rw(fn, ins, outs, *, name, params=(), reds=(), tm=256, ncol=1, with_j=False):
    t = ins[0][0].shape[0]
    tm = _pick(t, tm, 8)
    nrow = t // tm
    n_in, n_p, n_out = len(ins), len(params), len(outs)

    in_specs = [pl.BlockSpec((tm, bw or arr.shape[1]), lambda j, i, cf=cf: (i, cf(j))) for arr, bw, cf in ins]
    in_specs += [pl.BlockSpec(p.shape, lambda j, i: (0, 0)) for p in params]
    out_specs = [pl.BlockSpec((tm, bw or w), lambda j, i, cf=cf: (i, cf(j))) for w, bw, _, cf in outs]
    out_specs += [pl.BlockSpec((shp[0], bw or shp[1]), lambda j, i, cf=cf: (0, cf(j))) for shp, bw, cf in reds]
    out_shape = [jax.ShapeDtypeStruct((t, w), dt) for w, _, dt, _ in outs]
    out_shape += [jax.ShapeDtypeStruct(shp, F32) for shp, _, _ in reds]

    def body(*refs):
        in_refs, p_refs = refs[:n_in], refs[n_in:n_in + n_p]
        o_refs, r_refs = refs[n_in + n_p:n_in + n_p + n_out], refs[n_in + n_p + n_out:]
        args = [r[...].astype(F32) for r in in_refs] + [r[...] for r in p_refs]
        if with_j:
            args = [pl.program_id(0)] + args
        o_vals, r_vals = fn(*args)
        for r, v in zip(o_refs, o_vals, strict=True):
            r[...] = v.astype(r.dtype)
        if r_refs:
            i = pl.program_id(1)
            for r, v in zip(r_refs, r_vals, strict=True):
                @pl.when(i == 0)
                def _(r=r, v=v):
                    r[...] = v

                @pl.when(i > 0)
                def _(r=r, v=v):
                    r[...] += v

    res = pl.pallas_call(
        body, name=name, grid=(ncol, nrow), in_specs=in_specs, out_specs=out_specs, out_shape=out_shape,
        compiler_params=pltpu.CompilerParams(dimension_semantics=("parallel", "arbitrary" if reds else "parallel")),
    )(*[a for a, _, _ in ins], *params)
    return res


def _rms(x, g):
    return x * lax.rsqrt(jnp.mean(x * x, axis=-1, keepdims=True) + RMS_EPS) * g


def _rms_fwd(x, g, name):
    return _rw(lambda xv, gv: ([_rms(xv, gv)], []), [(x, None, _c0)], [(D_MODEL, None, BF16, _c0)], params=[g], name=name)[0]


def _rms_bwd(x, dh, dres, g, name):
    def fn(xv, dhv, drv, gv):
        _, vjp = jax.vjp(_rms, xv, gv)
        dx, dg = vjp(dhv)
        return [drv + dx], [dg]

    return _rw(fn, [(x, None, _c0), (dh, None, _c0), (dres, None, _c0)], [(D_MODEL, None, F32, _c0)], params=[g],
               reds=[((1, D_MODEL), None, _c0)], name=name)


def _bd_apply(acts, mats, combos, mode, *, name, tm=512, col_blocks=None):
    t = acts[0].shape[0]
    tm = _pick(t, tm, 8)
    nb, r, c = mats[0].shape
    win, wout = (r, c) if mode == "nn" else (c, r)
    dims = (_DIMS[mode], ((), ()))
    n_a, n_m = len(acts), len(mats)

    def body(*refs):
        a_vals = [ar[...].astype(BF16) for ar in refs[:n_a]]
        m_refs, o_refs = refs[n_a:n_a + n_m], refs[n_a + n_m:]
        for o_ref, terms in zip(o_refs, combos, strict=True):
            for j in range(nb):
                acc = None
                for ai, mi in terms:
                    part = lax.dot_general(a_vals[ai][:, j * win:(j + 1) * win], m_refs[mi][j], dims, preferred_element_type=F32)
                    acc = part if acc is None else acc + part
                o_ref[:, j * wout:(j + 1) * wout] = acc

    return pl.pallas_call(
        body, name=name, grid=(t // tm,),
        in_specs=[pl.BlockSpec((tm, nb * win), lambda i, cb=cb: (i, cb)) for cb in (col_blocks or [0] * n_a)]
        + [pl.BlockSpec(m.shape, lambda i: (0, 0, 0)) for m in mats],
        out_specs=[pl.BlockSpec((tm, nb * wout), lambda i: (i, 0))] * len(combos),
        out_shape=[jax.ShapeDtypeStruct((t, nb * wout), F32)] * len(combos),
        compiler_params=pltpu.CompilerParams(dimension_semantics=("parallel",)),
    )(*acts, *mats)


def _bd_grads(arrs, widths, pairs, *, name, tk=512, col_blocks=None):
    t = arrs[0].shape[0]
    tk = _pick(t, tk, 8)
    n_a = len(arrs)
    dims = (_DIMS["tn"], ((), ()))

    def body(*refs):
        vals = [ar[...].astype(BF16) for ar in refs[:n_a]]
        o_refs = refs[n_a:]
        @pl.when(pl.program_id(0) == 0)
        def _():
            for o_ref in o_refs:
                o_ref[...] = jnp.zeros_like(o_ref)

        for o_ref, (ai, bi) in zip(o_refs, pairs, strict=True):
            wa, wb = widths[ai], widths[bi]
            for j in range(SSM_BLOCKS):
                o_ref[j] += lax.dot_general(vals[ai][:, j * wa:(j + 1) * wa], vals[bi][:, j * wb:(j + 1) * wb], dims,
                                            preferred_element_type=F32)

    return pl.pallas_call(
        body, name=name, grid=(t // tk,),
        in_specs=[pl.BlockSpec((tk, SSM_BLOCKS * w), lambda k, cb=cb: (k, cb)) for w, cb in zip(widths, col_blocks or [0] * n_a, strict=True)],
        out_specs=[pl.BlockSpec((SSM_BLOCKS, widths[ai], widths[bi]), lambda k: (0, 0, 0)) for ai, bi in pairs],
        out_shape=[jax.ShapeDtypeStruct((SSM_BLOCKS, widths[ai], widths[bi]), F32) for ai, bi in pairs],
        compiler_params=pltpu.CompilerParams(dimension_semantics=("arbitrary",)),
    )(*arrs)


SCAN_LW = 512
SCAN_ROWS = 512
_DOUBLING = ((1, 0), (2, 1), (4, 2))


def _scan(xr, xi, pr, pi, dr, di, *, reverse, name, hr=None, hi=None):
    t, s = xr.shape
    lc = _pick(t, SCAN_ROWS, 8)
    nt, ngroups = t // lc, lc // 8
    with_da = hr is not None

    def tmap(l, tt):
        return ((nt - 1 - tt) if reverse else tt, l)

    x_spec = pl.BlockSpec((lc, SCAN_LW), tmap)
    tab_spec = pl.BlockSpec((8, SCAN_LW), lambda l, tt: (0, l))

    def body(*refs):
        xr_ref, xi_ref, pr_ref, pi_ref, dr_ref, di_ref = refs[:6]
        if with_da:
            hr_ref, hi_ref, or_ref, oi_ref, ar_ref, ai_ref, cr_ref, ci_ref = refs[6:]
        else:
            or_ref, oi_ref, cr_ref, ci_ref = refs[6:]
        tt = pl.program_id(1)

        @pl.when(tt == 0)
        def _():
            cr_ref[...] = jnp.zeros_like(cr_ref)
            ci_ref[...] = jnp.zeros_like(ci_ref)
            if with_da:
                ar_ref[...] = jnp.zeros_like(ar_ref)
                ai_ref[...] = jnp.zeros_like(ai_ref)

        sub = lax.broadcasted_iota(jnp.int32, (8, SCAN_LW), 0)
        pw_r, pw_i = pr_ref[...], pi_ref[...]

        def step(g, carry):
            g = (ngroups - 1 - g) if reverse else g
            r0 = pl.multiple_of(g * 8, 8)
            vr, vi = xr_ref[pl.ds(r0, 8), :], xi_ref[pl.ds(r0, 8), :]
            for shift, row in _DOUBLING:
                a_r, a_i = dr_ref[row:row + 1, :], di_ref[row:row + 1, :]
                if reverse:
                    keep = sub < 8 - shift
                    sr, si = pltpu.roll(vr, 8 - shift, 0), pltpu.roll(vi, 8 - shift, 0)
                else:
                    keep = sub >= shift
                    sr, si = pltpu.roll(vr, shift, 0), pltpu.roll(vi, shift, 0)
                sr, si = jnp.where(keep, sr, 0.0), jnp.where(keep, si, 0.0)
                vr, vi = vr + a_r * sr - a_i * si, vi + a_r * si + a_i * sr
            if with_da:
                cr, ci, acc_r, acc_i = carry
            else:
                cr, ci = carry
            vr, vi = vr + pw_r * cr - pw_i * ci, vi + pw_r * ci + pw_i * cr
            or_ref[pl.ds(r0, 8), :] = vr
            oi_ref[pl.ds(r0, 8), :] = vi
            if with_da:
                nr = jnp.where(sub < 7, pltpu.roll(vr, 7, 0), cr)
                ni = jnp.where(sub < 7, pltpu.roll(vi, 7, 0), ci)
                h_r, h_i = hr_ref[pl.ds(r0, 8), :], hi_ref[pl.ds(r0, 8), :]
                acc_r = acc_r + h_r * nr + h_i * ni
                acc_i = acc_i + h_r * ni - h_i * nr
            edge = 0 if reverse else 7
            cr = jnp.broadcast_to(vr[edge:edge + 1, :], vr.shape)
            ci = jnp.broadcast_to(vi[edge:edge + 1, :], vi.shape)
            return (cr, ci, acc_r, acc_i) if with_da else (cr, ci)

        zero = jnp.zeros((8, SCAN_LW), F32)
        init = (cr_ref[...], ci_ref[...]) + ((zero, zero) if with_da else ())
        fin = lax.fori_loop(0, ngroups, step, init, unroll=2)
        cr_ref[...] = fin[0]
        ci_ref[...] = fin[1]
        if with_da:
            ar_ref[...] += fin[2]
            ai_ref[...] += fin[3]

    n_x = 4 if with_da else 2
    out_specs = [x_spec, x_spec] + ([tab_spec, tab_spec] if with_da else [])
    out_shape = [jax.ShapeDtypeStruct((t, s), F32)] * 2 + ([jax.ShapeDtypeStruct((8, s), F32)] * 2 if with_da else [])
    operands = [xr, xi, pr, pi, dr, di] + ([hr, hi] if with_da else [])
    return pl.pallas_call(
        body, name=name, grid=(s // SCAN_LW, nt),
        in_specs=[x_spec, x_spec] + [tab_spec] * 4 + [x_spec] * (n_x - 2),
        out_specs=out_specs, out_shape=out_shape,
        scratch_shapes=[pltpu.VMEM((8, SCAN_LW), F32), pltpu.VMEM((8, SCAN_LW), F32)],
        compiler_params=pltpu.CompilerParams(dimension_semantics=("parallel", "arbitrary")),
    )(*operands)


CONV_TM = 256
HALO = 16


def _conv_specs(t, tm):
    nrow = t // tm
    hb = tm // HALO

    def col(cidx):
        return pl.BlockSpec((tm, BRANCH), lambda i: (i, cidx))

    def prev(cidx):
        return pl.BlockSpec((HALO, BRANCH), lambda i: (jnp.maximum(i * hb - 1, 0), cidx))

    def nxt(cidx):
        return pl.BlockSpec((HALO, BRANCH), lambda i: (jnp.minimum((i + 1) * hb, nrow * hb - 1), cidx))

    return nrow, col, prev, nxt


def _conv_taps(cc, cx, cc_prev, cx_prev, first):
    tm = cc.shape[0]
    v = cc * cx
    halo = cc_prev * cx_prev * jnp.where(first, 0.0, 1.0)
    ext = jnp.concatenate([halo, v], axis=0)
    return v, pltpu.roll(ext, 1, 0)[HALO:HALO + tm], pltpu.roll(ext, 2, 0)[HALO:HALO + tm]


def _conv_fwd(z, conv_w, name):
    t = z.shape[0]
    tm = _pick(t, CONV_TM, HALO)
    nrow, col, prev, _ = _conv_specs(t, tm)

    def body(cb_ref, cc_ref, cx_ref, ccp_ref, cxp_ref, w_ref, o_ref):
        first = pl.program_id(0) == 0
        v, v1, v2 = _conv_taps(*(r[...].astype(F32) for r in (cc_ref, cx_ref, ccp_ref, cxp_ref)), first)
        y = w_ref[0:1, :] * v2 + w_ref[1:2, :] * v1 + w_ref[2:3, :] * v
        o_ref[...] = (cb_ref[...].astype(F32) * y).astype(o_ref.dtype)

    return pl.pallas_call(
        body, name=name, grid=(nrow,),
        in_specs=[col(CB_BLK), col(CC_BLK), col(CX_BLK), prev(CC_BLK), prev(CX_BLK), pl.BlockSpec((3, BRANCH), lambda i: (0, 0))],
        out_specs=pl.BlockSpec((tm, BRANCH), lambda i: (i, 0)), out_shape=jax.ShapeDtypeStruct((t, BRANCH), BF16),
        compiler_params=pltpu.CompilerParams(dimension_semantics=("parallel",)),
    )(z, z, z, z, z, conv_w)


def _conv_bwd(dyc, z, conv_w, name):
    t = z.shape[0]
    tm = _pick(t, CONV_TM, HALO)
    nrow, col, prev, nxt = _conv_specs(t, tm)
    d_cur = pl.BlockSpec((tm, BRANCH), lambda i: (i, 0))
    d_nxt = pl.BlockSpec((HALO, BRANCH), lambda i: (jnp.minimum((i + 1) * (tm // HALO), nrow * (tm // HALO) - 1), 0))

    def body(dy_ref, dyn_ref, cb_ref, cbn_ref, cc_ref, cx_ref, ccp_ref, cxp_ref, w_ref, dcb_ref, dcc_ref, dcx_ref, dw_ref):
        i = pl.program_id(0)
        cc, cx, cb = cc_ref[...].astype(F32), cx_ref[...].astype(F32), cb_ref[...].astype(F32)
        v, v1, v2 = _conv_taps(cc, cx, ccp_ref[...].astype(F32), cxp_ref[...].astype(F32), i == 0)
        w0, w1, w2 = w_ref[0:1, :], w_ref[1:2, :], w_ref[2:3, :]
        y = w0 * v2 + w1 * v1 + w2 * v
        dyc_v = dy_ref[...]
        dcb_ref[...] = (dyc_v * y).astype(dcb_ref.dtype)
        dy = dyc_v * cb
        halo = dyn_ref[...] * cbn_ref[...].astype(F32) * jnp.where(i == nrow - 1, 0.0, 1.0)
        ext = jnp.concatenate([dy, halo], axis=0)
        dy1 = pltpu.roll(ext, tm + HALO - 1, 0)[0:tm]
        dy2 = pltpu.roll(ext, tm + HALO - 2, 0)[0:tm]
        dv = w2 * dy + w1 * dy1 + w0 * dy2
        dcc_ref[...] = (dv * cx).astype(dcc_ref.dtype)
        dcx_ref[...] = (dv * cc).astype(dcx_ref.dtype)
        dw = jnp.concatenate([jnp.sum(dy * v2, axis=0, keepdims=True), jnp.sum(dy * v1, axis=0, keepdims=True),
                              jnp.sum(dy * v, axis=0, keepdims=True), jnp.zeros((5, BRANCH), F32)], axis=0)

        @pl.when(i == 0)
        def _():
            dw_ref[...] = dw

        @pl.when(i > 0)
        def _():
            dw_ref[...] += dw

    o_spec = pl.BlockSpec((tm, BRANCH), lambda i: (i, 0))
    return pl.pallas_call(
        body, name=name, grid=(nrow,),
        in_specs=[d_cur, d_nxt, col(CB_BLK), nxt(CB_BLK), col(CC_BLK), col(CX_BLK), prev(CC_BLK), prev(CX_BLK),
                  pl.BlockSpec((3, BRANCH), lambda i: (0, 0))],
        out_specs=[o_spec, o_spec, o_spec, pl.BlockSpec((8, BRANCH), lambda i: (0, 0))],
        out_shape=[jax.ShapeDtypeStruct((t, BRANCH), BF16)] * 3 + [jax.ShapeDtypeStruct((8, BRANCH), F32)],
        compiler_params=pltpu.CompilerParams(dimension_semantics=("arbitrary",)),
    )(dyc, dyc, z, z, z, z, z, z, conv_w)


ATTN_BLOCKS = 4
ATTN_BLOCKS_BWD = 1
GROUP_ROWS = GQA * WINDOW


def _attn_specs(nblk):
    rows = nblk * WINDOW
    q_spec = pl.BlockSpec((N_Q, rows, HEAD_DIM), lambda n: (0, n, 0))
    kv_cur = pl.BlockSpec((N_KV, rows, HEAD_DIM), lambda n: (0, n, 0))
    kv_prev = pl.BlockSpec((N_KV, WINDOW, HEAD_DIM), lambda n: (0, jnp.maximum(n * nblk - 1, 0), 0))
    bias_spec = pl.BlockSpec((N_Q, WINDOW, 2 * WINDOW), lambda n: (0, 0, 0))
    sink_spec = pl.BlockSpec((N_Q, 1), lambda n: (0, 0))
    return q_spec, kv_cur, kv_prev, bias_spec, sink_spec


def _attn_valid(first_key):
    qi = lax.broadcasted_iota(jnp.int32, (GROUP_ROWS, 2 * WINDOW), 0) & (WINDOW - 1)
    kj = lax.broadcasted_iota(jnp.int32, (GROUP_ROWS, 2 * WINDOW), 1)
    dist = qi + WINDOW - kj
    return (dist >= 0) & (dist < WINDOW) & (kj >= first_key)


def _attn_masks(n):
    return _attn_valid(jnp.where(n > 0, 0, WINDOW)), _attn_valid(0)


def _blk(b):
    return slice(b * WINDOW, (b + 1) * WINDOW)


def _group(ref, h, b, width):
    return ref[GQA * h:GQA * (h + 1), _blk(b)].reshape(GROUP_ROWS, width)


def _keys(prev_ref, cur_ref, h, b):
    prev = prev_ref[h] if b == 0 else cur_ref[h, _blk(b - 1)]
    return jnp.concatenate([prev, cur_ref[h, _blk(b)]], axis=0)


def _group_sinks(s_ref, h):
    return jnp.concatenate([jnp.broadcast_to(s_ref[GQA * h + g:GQA * h + g + 1, :], (WINDOW, 1)) for g in range(GQA)], axis=0)


def _attn_probs(q, kc, bias, sink, valid):
    s = lax.dot_general(q, kc, (_DIMS["nt"], ((), ())), preferred_element_type=F32) * ATTN_SCALE + bias
    s = jnp.where(valid, s, NEG)
    m = jnp.maximum(jnp.max(s, axis=1, keepdims=True), sink)
    p = jnp.exp(s - m)
    e_sink = jnp.exp(sink - m)
    inv = 1.0 / (jnp.sum(p, axis=1, keepdims=True) + e_sink)
    return p * inv, e_sink * inv


def _attn_fwd(qh, kh, vh, bias, sinks, name):
    t = qh.shape[1]
    nblk = min(ATTN_BLOCKS, t // WINDOW)
    q_spec, kv_cur, kv_prev, bias_spec, sink_spec = _attn_specs(nblk)

    def body(q_ref, kp_ref, kc_ref, vp_ref, vc_ref, b_ref, s_ref, o_ref):
        masks = _attn_masks(pl.program_id(0))
        for b in range(nblk):
            for h in range(N_KV):
                kc, vc = _keys(kp_ref, kc_ref, h, b), _keys(vp_ref, vc_ref, h, b)
                w, _ = _attn_probs(_group(q_ref, h, b, HEAD_DIM), kc, _group(b_ref, h, 0, 2 * WINDOW), _group_sinks(s_ref, h),
                                   masks[min(b, 1)])
                o = jnp.dot(w.astype(BF16), vc, preferred_element_type=F32)
                o_ref[GQA * h:GQA * (h + 1), _blk(b)] = o.reshape(GQA, WINDOW, HEAD_DIM).astype(o_ref.dtype)

    return pl.pallas_call(
        body, name=name, grid=(t // (nblk * WINDOW),),
        in_specs=[q_spec, kv_prev, kv_cur, kv_prev, kv_cur, bias_spec, sink_spec],
        out_specs=q_spec, out_shape=jax.ShapeDtypeStruct((N_Q, t, HEAD_DIM), BF16),
        compiler_params=pltpu.CompilerParams(dimension_semantics=("parallel",)),
    )(qh, kh, kh, vh, vh, bias, sinks)


def _attn_bwd(qh, kh, vh, doh, bias, sinks, name):
    t = qh.shape[1]
    nblk = min(ATTN_BLOCKS_BWD, t // WINDOW)
    nsteps = t // (nblk * WINDOW)
    q_spec, kv_cur, kv_prev, bias_spec, sink_spec = _attn_specs(nblk)

    def body(q_ref, kp_ref, kc_ref, vp_ref, vc_ref, do_ref, b_ref, s_ref,
             dq_ref, dkc_ref, dkp_ref, dvc_ref, dvp_ref, db_ref, ds_ref):
        n = pl.program_id(0)
        masks = _attn_masks(n)

        @pl.when(n == 0)
        def _():
            db_ref[...] = jnp.zeros_like(db_ref)
            ds_ref[...] = jnp.zeros_like(ds_ref)

        def scatter(part, h, b, cur_ref, prev_ref):
            if b == 0:
                prev_ref[h] = part[0:WINDOW]
            else:
                cur_ref[h, _blk(b - 1)] += part[0:WINDOW]
            cur_ref[h, _blk(b)] = part[WINDOW:2 * WINDOW]

        for b in range(nblk):
            for h in range(N_KV):
                kc, vc = _keys(kp_ref, kc_ref, h, b), _keys(vp_ref, vc_ref, h, b)
                heads = slice(GQA * h, GQA * (h + 1))
                q, do = _group(q_ref, h, b, HEAD_DIM), _group(do_ref, h, b, HEAD_DIM)
                w, w_sink = _attn_probs(q, kc, _group(b_ref, h, 0, 2 * WINDOW), _group_sinks(s_ref, h), masks[min(b, 1)])
                dw = lax.dot_general(do, vc, (_DIMS["nt"], ((), ())), preferred_element_type=F32)
                delta = jnp.sum(w * dw, axis=1, keepdims=True)
                dscore = w * (dw - delta)
                ds_ref[heads] += (-w_sink * delta).reshape(GQA, WINDOW, 1)
                db_ref[heads] += dscore.reshape(GQA, WINDOW, 2 * WINDOW)
                dsb = dscore.astype(BF16)
                dq_ref[heads, _blk(b)] = (jnp.dot(dsb, kc, preferred_element_type=F32) * ATTN_SCALE).reshape(GQA, WINDOW, HEAD_DIM)
                scatter(lax.dot_general(dsb, q, (_DIMS["tn"], ((), ())), preferred_element_type=F32) * ATTN_SCALE, h, b, dkc_ref, dkp_ref)
                scatter(lax.dot_general(w.astype(BF16), do, (_DIMS["tn"], ((), ())), preferred_element_type=F32), h, b, dvc_ref, dvp_ref)

    kv_shape = jax.ShapeDtypeStruct((N_KV, t, HEAD_DIM), F32)
    kv_prev_out = pl.BlockSpec((N_KV, WINDOW, HEAD_DIM), lambda n: (0, n, 0))
    kv_prev_shape = jax.ShapeDtypeStruct((N_KV, nsteps * WINDOW, HEAD_DIM), F32)
    return pl.pallas_call(
        body, name=name, grid=(nsteps,),
        in_specs=[q_spec, kv_prev, kv_cur, kv_prev, kv_cur, q_spec, bias_spec, sink_spec],
        out_specs=[q_spec, kv_cur, kv_prev_out, kv_cur, kv_prev_out, bias_spec, pl.BlockSpec((N_Q, WINDOW, 1), lambda n: (0, 0, 0))],
        out_shape=[jax.ShapeDtypeStruct((N_Q, t, HEAD_DIM), F32), kv_shape, kv_prev_shape, kv_shape, kv_prev_shape,
                   jax.ShapeDtypeStruct((N_Q, WINDOW, 2 * WINDOW), F32), jax.ShapeDtypeStruct((N_Q, WINDOW, 1), F32)],
        compiler_params=pltpu.CompilerParams(dimension_semantics=("arbitrary",)),
    )(qh, kh, kh, vh, vh, doh, bias, sinks)


def _heads(a, n_heads):
    t = a.shape[0]
    return a.astype(BF16).reshape(t, n_heads, HEAD_DIM).transpose(1, 0, 2)


def _unheads(a):
    n_heads, t, _ = a.shape
    return a.transpose(1, 0, 2).reshape(t, n_heads * HEAD_DIM)


def _shift_blocks(cur, prev):
    n_kv, t, d = cur.shape
    nsteps = prev.shape[1] // WINDOW
    nblk = t // (nsteps * WINDOW)
    late = jnp.concatenate([prev.reshape(n_kv, nsteps, WINDOW, d)[:, 1:], jnp.zeros((n_kv, 1, WINDOW, d), cur.dtype)], axis=1)
    delta = jnp.concatenate([jnp.zeros((n_kv, nsteps, nblk - 1, WINDOW, d), cur.dtype), late[:, :, None]], axis=2)
    return (cur.reshape(n_kv, nsteps, nblk, WINDOW, d) + delta).reshape(n_kv, t, d)


def _t5_bucket_table():
    qi = np.arange(WINDOW)[:, None]
    kj = np.arange(2 * WINDOW)[None, :]
    dist = np.clip(qi + WINDOW - kj, 0, REL_MAX_DIST - 1)
    exact = REL_BUCKETS // 2
    df = np.maximum(dist, 1).astype(np.float32)
    large = exact + (np.log(df / np.float32(exact)) / np.float32(math.log(REL_MAX_DIST / exact)) * (REL_BUCKETS - exact)).astype(np.int32)
    large = np.minimum(large, REL_BUCKETS - 1)
    bucket = np.where(dist < exact, dist, large)
    onehot = np.zeros((WINDOW * 2 * WINDOW, REL_BUCKETS), np.float32)
    onehot[np.arange(WINDOW * 2 * WINDOW), bucket.reshape(-1)] = 1.0
    return onehot


def _band_bias(rel_bias):
    onehot = jnp.asarray(_t5_bucket_table())
    sel = jnp.sum(onehot[:, :, None] * rel_bias[None, :, :], axis=1)
    return sel.T.reshape(N_Q, WINDOW, 2 * WINDOW)


def _block_diag(a):
    g, r, c = a.shape
    a4 = a.reshape(SSM_BLOCKS, g // SSM_BLOCKS, r, c)
    eye = jnp.eye(g // SSM_BLOCKS, dtype=a.dtype)
    full = a4[:, :, :, None, :] * eye[None, :, None, :, None]
    return full.reshape(SSM_BLOCKS, (g // SSM_BLOCKS) * r, (g // SSM_BLOCKS) * c)


def _ssm_disc(lam_re, lam_im, b_re, b_im, c_re, c_im, log_dt):
    dt = jnp.exp(log_dt)[:, None]
    mag = jnp.exp(lam_re * dt)
    ang = lam_im * dt
    a_re = mag * jnp.cos(ang)
    a_im = mag * jnp.sin(ang)
    den = lam_re * lam_re + lam_im * lam_im
    nr = a_re - 1.0
    coef_re = (nr * lam_re + a_im * lam_im) / den
    coef_im = (a_im * lam_re - nr * lam_im) / den
    bb_re = coef_re[..., None] * b_re - coef_im[..., None] * b_im
    bb_im = coef_re[..., None] * b_im + coef_im[..., None] * b_re
    wb_re = _block_diag(jnp.swapaxes(bb_re, 1, 2))
    wb_im = _block_diag(jnp.swapaxes(bb_im, 1, 2))
    cm_re = _block_diag(jnp.swapaxes(c_re, 1, 2))
    cm_imn = _block_diag(-jnp.swapaxes(c_im, 1, 2))
    return a_re.reshape(-1), a_im.reshape(-1), wb_re, wb_im, cm_re, cm_imn


def _scan_tables(a_re, a_im):
    pr, pi = [a_re], [a_im]
    for _ in range(7):
        pr, pi = pr + [pr[-1] * a_re - pi[-1] * a_im], pi + [pr[-1] * a_im + pi[-1] * a_re]
    pr, pi = jnp.stack(pr), jnp.stack(pi)
    pad = jnp.zeros((5,) + a_re.shape, F32)
    dr = jnp.concatenate([jnp.stack([pr[0], pr[1], pr[3]]), pad])
    di = jnp.concatenate([jnp.stack([pi[0], pi[1], pi[3]]), pad])
    fwd = (pr, pi, dr, di)
    rev = (pr[::-1], -pi[::-1], dr, -di)
    return jax.tree.map(lax.stop_gradient, (fwd, rev))


def _gate_col(r):
    return lambda j: OFF_G // D_MODEL + r


def _layer_fwd(x, p_i, w, bias, li):
    nm = lambda s: f"{s}_l{li}"
    h = _rms_fwd(x, w["norm_mix"], nm("rms_mix"))
    z = _mm(h, w["w_in"], "nn", tm=1024, tn=2944, n_outer=True, out_dtype=BF16, name=nm("mm_in"))
    bu_re, bu_im = _bd_apply([z], [w["wb_re"], w["wb_im"]], [[(0, 0)], [(0, 1)]], "nn", col_blocks=[U_BLK], name=nm("ssm_bu"))
    h_re, h_im = _scan(bu_re, bu_im, *w["scan_fwd"], reverse=False, name=nm("ssm_scan"))
    (y0,) = _bd_apply([h_re, h_im], [w["cm_re"], w["cm_imn"]], [[(0, 0), (1, 1)]], "nn", name=nm("ssm_c"))
    (y1,) = _rw(lambda a, u, d: ([jax.nn.gelu(a + d * u)], []),
                [(y0, None, _c0), (z, BRANCH, lambda j: U_BLK)], [(BRANCH, None, F32, _c0)],
                params=[w["ssm_d"]], name=nm("ssm_gelu"))
    gl = _mm(y1, w["ssm_w_glu"], "nn", name=nm("mm_glu"))
    (y_ssm,) = _rw(lambda a, b: ([a * jax.nn.sigmoid(b)], []), [(y1, None, _c0), (gl, None, _c0)],
                   [(BRANCH, None, BF16, _c0)], name=nm("ssm_glu"))
    y_conv = _conv_fwd(z, w["conv_w"], nm("conv_fwd"))
    kv_w = N_KV * HEAD_DIM
    q2, k2, v2 = _rw(lambda q, k, v: ([q, k, v], []),
                     [(z, BRANCH, lambda j: OFF_Q // BRANCH), (z, kv_w, lambda j: OFF_K // kv_w), (z, kv_w, lambda j: OFF_V // kv_w)],
                     [(BRANCH, None, BF16, _c0), (kv_w, None, BF16, _c0), (kv_w, None, BF16, _c0)], name=nm("qkv_bf16"))
    qh, kh, vh = _heads(q2, N_Q), _heads(k2, N_KV), _heads(v2, N_KV)
    y_attn = _unheads(_attn_fwd(qh, kh, vh, bias, w["sinks"], nm("attn_fwd")))
    ys = (y_ssm, y_conv, y_attn)
    bs = [_mm(ys[r], w["w_branch"][r], "nn", name=nm(f"mm_branch{r}")) for r in range(3)]

    def merge(g0, g1, g2, b0, b1, b2):
        return [jax.nn.sigmoid(g0) * b0 + jax.nn.sigmoid(g1) * b1 + jax.nn.sigmoid(g2) * b2], []

    (merged,) = _rw(merge, [(z, D_MODEL, _gate_col(r)) for r in range(3)] + [(b, None, _c0) for b in bs],
                    [(D_MODEL, None, BF16, _c0)], name=nm("merge"))
    x1 = _mm(merged, w["w_out"], "nn", add=x, name=nm("mm_out"))
    hf_in = _rms_fwd(x1, w["norm_ffn"], nm("rms_ffn"))
    hf = _mm(hf_in, w["w_ffn_in"], "nn", tn=1408, n_outer=True, out_dtype=BF16, name=nm("mm_ffn_in"))
    (act,) = _rw(lambda a, b: ([jax.nn.silu(a) * b], []), [(hf, FFN_COLS, lambda j: j), (hf, FFN_COLS, lambda j: FFN_NCOL + j)],
                 [(FFN_HIDDEN, FFN_COLS, BF16, lambda j: j)], ncol=FFN_NCOL, name=nm("swiglu"))
    x2 = _mm(act, w["w_ffn_out"], "nn", add=x1, tk=1408, name=nm("mm_ffn_out"))
    hp = _rms_fwd(x2, w["norm_ple"], nm("rms_ple"))
    pgl = _mm(hp, w["w_ple_gate"], "nn", name=nm("mm_ple_gate"))
    pp = _mm(p_i, w["w_ple_proj"], "nn", name=nm("mm_ple_proj"))
    (x3,) = _rw(lambda xv, a, b: ([xv + jax.nn.sigmoid(a) * b], []), [(x2, None, _c0), (pgl, None, _c0), (pp, None, _c0)],
                [(D_MODEL, None, F32, _c0)], name=nm("ple_add"))
    saved = dict(x=x, p=p_i, h=h, z=z, h_re=h_re, h_im=h_im, y0=y0, y1=y1, gl=gl, ys=ys, qh=qh, kh=kh, vh=vh,
                 bs=bs, merged=merged, x1=x1, hf_in=hf_in, hf=hf, act=act, x2=x2, hp=hp, pgl=pgl, pp=pp)
    return x3, saved


def _layer_bwd(dx3, s, w, bias, li):
    nm = lambda n: f"{n}_l{li}"
    g = {}
    z = s["z"]
    def ple_b(d, a, b):
        _, vjp = jax.vjp(lambda a_, b_: jax.nn.sigmoid(a_) * b_, a, b)
        return list(vjp(d)), []

    dpgl, dpp = _rw(ple_b, [(dx3, None, _c0), (s["pgl"], None, _c0), (s["pp"], None, _c0)],
                    [(D_MODEL, None, BF16, _c0)] * 2, name=nm("ple_bwd"))
    g["w_ple_proj"] = _mm(s["p"], dpp, "tn", name=nm("mmg_ple_proj"))
    g["w_ple_gate"] = _mm(s["hp"], dpgl, "tn", name=nm("mmg_ple_gate"))
    dhp = _mm(dpgl, w["w_ple_gate"], "nt", name=nm("mmb_ple_gate"))
    dx2, g["norm_ple"] = _rms_bwd(s["x2"], dhp, dx3, w["norm_ple"], nm("rmsb_ple"))
    dact = _mm(dx2, w["w_ffn_out"], "nt", tn=1408, name=nm("mmb_ffn_out"))
    g["w_ffn_out"] = _mm(s["act"], dx2, "tn", tm=1408, name=nm("mmg_ffn_out"))

    def swiglu_b(a, b, d):
        _, vjp = jax.vjp(lambda a_, b_: jax.nn.silu(a_) * b_, a, b)
        return list(vjp(d)), []

    dhf_a, dhf_b = _rw(swiglu_b, [(s["hf"], FFN_COLS, lambda j: j), (s["hf"], FFN_COLS, lambda j: FFN_NCOL + j), (dact, FFN_COLS, lambda j: j)],
                       [(FFN_HIDDEN, FFN_COLS, BF16, lambda j: j)] * 2, ncol=FFN_NCOL, name=nm("swiglu_bwd"))
    g["w_ffn_in"] = jnp.concatenate([_mm(s["hf_in"], dhf_a, "tn", tn=1408, name=nm("mmg_ffn_in_a")),
                                     _mm(s["hf_in"], dhf_b, "tn", tn=1408, name=nm("mmg_ffn_in_b"))], axis=1)
    dhf_in = _mm(dhf_a, w["w_ffn_in"], "nt", tk=1408, name=nm("mmb_ffn_in_a"))
    dhf_in = _mm(dhf_b, w["w_ffn_in"], "nt", tk=1408, b_k0=FFN_HIDDEN, add=dhf_in, name=nm("mmb_ffn_in_b"))
    dx1, g["norm_ffn"] = _rms_bwd(s["x1"], dhf_in, dx2, w["norm_ffn"], nm("rmsb_ffn"))
    dmerged = _mm(dx1, w["w_out"], "nt", name=nm("mmb_out"))
    g["w_out"] = _mm(s["merged"], dx1, "tn", name=nm("mmg_out"))

    def merge_b(d, g0, g1, g2, b0, b1, b2):
        outs_g, outs_b = [], []
        for gate, br in ((g0, b0), (g1, b1), (g2, b2)):
            sg = jax.nn.sigmoid(gate)
            outs_g.append(d * br * sg * (1.0 - sg))
            outs_b.append(d * sg)
        return outs_g + outs_b, []

    res = _rw(merge_b, [(dmerged, None, _c0)] + [(z, D_MODEL, _gate_col(r)) for r in range(3)] + [(b, None, _c0) for b in s["bs"]],
              [(D_MODEL, None, BF16, _c0)] * 6, name=nm("merge_bwd"))
    dgates, dbs = res[:3], res[3:]
    dys = [_mm(dbs[r], w["w_branch"][r], "nt", name=nm(f"mmb_branch{r}")) for r in range(3)]
    g["w_branch"] = jnp.stack([_mm(s["ys"][r], dbs[r], "tn", name=nm(f"mmg_branch{r}")) for r in range(3)])
    doh = _heads(dys[2], N_Q)
    dqh, dkc, dkp, dvc, dvp, dbias, dsink = _attn_bwd(s["qh"], s["kh"], s["vh"], doh, bias, w["sinks"], nm("attn_bwd"))
    dq, dk, dv = _unheads(dqh), _unheads(_shift_blocks(dkc, dkp)), _unheads(_shift_blocks(dvc, dvp))
    g["sinks"] = jnp.sum(dsink, axis=(1, 2))
    dcb, dcc, dcx, dconv = _conv_bwd(dys[1], z, w["conv_w"], nm("conv_bwd"))
    g["conv_w"] = dconv[0:3]
    def glu_b(d, y1, gl):
        sg = jax.nn.sigmoid(gl)
        return [d * y1 * sg * (1.0 - sg), d * sg], []

    dgl, dy1a = _rw(glu_b, [(dys[0], None, _c0), (s["y1"], None, _c0), (s["gl"], None, _c0)],
                    [(BRANCH, None, BF16, _c0), (BRANCH, None, F32, _c0)], name=nm("ssm_glu_bwd"))
    g["ssm_w_glu"] = _mm(s["y1"], dgl, "tn", name=nm("mmg_glu"))
    dy1b = _mm(dgl, w["ssm_w_glu"], "nt", name=nm("mmb_glu"))

    def gelu_b(da, db, a, u, d):
        _, vjp = jax.vjp(lambda pre: jax.nn.gelu(pre), a + d * u)
        (dy0,) = vjp(da + db)
        return [dy0, dy0 * d], [jnp.sum(dy0 * u, axis=0, keepdims=True)]

    dy0, du_a, g["ssm_d"] = _rw(gelu_b, [(dy1a, None, _c0), (dy1b, None, _c0), (s["y0"], None, _c0), (z, BRANCH, lambda j: U_BLK)],
                                [(BRANCH, None, BF16, _c0), (BRANCH, None, F32, _c0)], params=[w["ssm_d"]],
                                reds=[((1, BRANCH), None, _c0)], name=nm("ssm_gelu_bwd"))
    dh_re, dh_im = _bd_apply([dy0], [w["cm_re"], w["cm_imn"]], [[(0, 0)], [(0, 1)]], "nt", name=nm("ssmb_c"))
    sb, cb = SSM_STATES // SSM_BLOCKS, BRANCH // SSM_BLOCKS
    g["cm_re"], g["cm_imn"] = _bd_grads([s["h_re"], s["h_im"], dy0], [sb, sb, cb], [(0, 2), (1, 2)], name=nm("ssmg_c"))
    l_re, l_im, da_re, da_im = _scan(dh_re, dh_im, *w["scan_rev"], reverse=True, hr=s["h_re"], hi=s["h_im"], name=nm("ssm_scan_bwd"))
    g["a_re"], g["a_im"] = jnp.sum(da_re, axis=0), jnp.sum(da_im, axis=0)
    (du_b,) = _bd_apply([l_re, l_im], [w["wb_re"], w["wb_im"]], [[(0, 0), (1, 1)]], "nt", name=nm("ssmb_bu"))
    g["wb_re"], g["wb_im"] = _bd_grads([z, l_re, l_im], [cb, sb, sb], [(0, 1), (0, 2)], col_blocks=[U_BLK, 0, 0], name=nm("ssmg_bu"))
    dz = jnp.concatenate(list(dgates) + [(du_a + du_b).astype(BF16), dcb, dcc, dcx, dq.astype(BF16), dk.astype(BF16), dv.astype(BF16)], axis=1)
    g["w_in"] = _mm(s["h"], dz, "tn", tm=512, tn=2944, name=nm("mmg_in"))
    dh = _mm(dz, w["w_in"], "nt", tk=2944, name=nm("mmb_in"))
    dx, g["norm_mix"] = _rms_bwd(s["x"], dh, dx1, w["norm_mix"], nm("rmsb_mix"))
    return dx, g, dbias


def _loss_and_seed(x, target, g_final):
    def fn(xv, tv, gv):
        y, vjp = jax.vjp(_rms, xv, gv)
        err = y - tv
        dx, dg = vjp(err * (1.0 / D_MODEL))
        return [dx], [jnp.sum(err * err, axis=0, keepdims=True) * (0.5 / D_MODEL), dg]

    return _rw(fn, [(x, None, _c0), (target, None, _c0)], [(D_MODEL, None, F32, _c0)], params=[g_final],
               reds=[((1, D_MODEL), None, _c0)] * 2, name="loss_head")


def _local_step(x, p, target, wt):
    bias, bias_vjp = jax.vjp(_band_bias, wt["rel_bias"])
    layers, disc_vjps = [], []
    for i in range(DEPTH):
        ssm_p = [wt[k][i] for k in ("ssm_lambda_re", "ssm_lambda_im", "ssm_b_re", "ssm_b_im", "ssm_c_re", "ssm_c_im", "ssm_log_dt")]
        (a_re, a_im, wb_re, wb_im, cm_re, cm_imn), disc_vjp = jax.vjp(_ssm_disc, *ssm_p)
        scan_fwd, scan_rev = _scan_tables(a_re, a_im)
        layers.append(dict(
            norm_mix=wt["norm_mix"][i][None], w_in=wt["w_in"][i], wb_re=wb_re.astype(BF16), wb_im=wb_im.astype(BF16),
            cm_re=cm_re.astype(BF16), cm_imn=cm_imn.astype(BF16), scan_fwd=scan_fwd, scan_rev=scan_rev,
            ssm_d=wt["ssm_d"][i][None], ssm_w_glu=wt["ssm_w_glu"][i], conv_w=wt["conv_w"][i],
            sinks=wt["attn_sinks"][i][:, None], w_branch=wt["w_branch"][i], w_out=wt["w_out"][i],
            norm_ffn=wt["norm_ffn"][i][None], w_ffn_in=wt["w_ffn_in"][i], w_ffn_out=wt["w_ffn_out"][i],
            norm_ple=wt["norm_ple"][i][None], w_ple_gate=wt["w_ple_gate"][i], w_ple_proj=wt["w_ple_proj"][i]))
        disc_vjps.append(disc_vjp)

    saved = []
    for i in range(DEPTH):
        x, s = _layer_fwd(x, p[i], layers[i], bias, i)
        saved.append(s)
    dx, loss_cols, g_final = _loss_and_seed(x, target, wt["norm_final"][None])
    loss = jnp.sum(loss_cols)

    per_layer = [None] * DEPTH
    dbias = None
    for i in reversed(range(DEPTH)):
        dx, g, db = _layer_bwd(dx, saved[i], layers[i], bias, i)
        dbias = db if dbias is None else dbias + db
        (g["ssm_lambda_re"], g["ssm_lambda_im"], g["ssm_b_re"], g["ssm_b_im"], g["ssm_c_re"], g["ssm_c_im"], g["ssm_log_dt"]) = \
            disc_vjps[i]((g.pop("a_re"), g.pop("a_im"), g.pop("wb_re"), g.pop("wb_im"), g.pop("cm_re"), g.pop("cm_imn")))
        g["attn_sinks"] = g.pop("sinks")
        for k in ("norm_mix", "norm_ffn", "norm_ple", "ssm_d"):
            g[k] = g[k][0]
        per_layer[i] = g
    big_names = [name for name, _, _ in BIG]
    big = {k: [per_layer[i][k] for i in range(DEPTH)] for k in big_names}
    small = {k: jnp.stack([per_layer[i][k] for i in range(DEPTH)]) for k in per_layer[0] if k not in big_names}
    (small["rel_bias"],) = bias_vjp(dbias)
    small["norm_final"] = g_final[0]
    return loss, dx, small, big


HBM_SPEC = pl.BlockSpec(memory_space=pltpu.HBM)


def _position():
    x, y, c = lax.axis_index("x"), lax.axis_index("y"), lax.axis_index("c")
    other_chips = [(1 - x, y), (x, 1 - y), (1 - x, 1 - y)]
    return x, y, c, other_chips


def _row_chunks(rows, n=COPY_CHUNKS):
    rq = rows // n
    assert rq * n == rows and rq % 16 == 0, rows
    return [pl.ds(q * rq, rq) for q in range(n)]


def _place(buf, val, idx, name):
    n, rows, width = buf.shape
    tm = _pick(rows, 512, 16)

    def body(idx_ref, buf_ref, v_ref, o_ref):
        o_ref[0] = v_ref[...]

    grid_spec = pltpu.PrefetchScalarGridSpec(
        num_scalar_prefetch=1, grid=(rows // tm,),
        in_specs=[pl.BlockSpec(memory_space=pl.ANY), pl.BlockSpec((tm, width), lambda i, idx_ref: (i, 0))],
        out_specs=pl.BlockSpec((1, tm, width), lambda i, idx_ref: (idx_ref[0], i, 0)))
    return pl.pallas_call(
        body, name=name, grid_spec=grid_spec, out_shape=jax.ShapeDtypeStruct(buf.shape, buf.dtype), input_output_aliases={1: 0},
        compiler_params=pltpu.CompilerParams(dimension_semantics=("arbitrary",)),
    )(jnp.asarray(idx, jnp.int32).reshape(1), buf, val)


def _allgather_weights(locals_):
    nb, nq = len(locals_), COPY_CHUNKS
    chunks = [_row_chunks(a.shape[1]) for a in locals_]

    def body(*refs):
        w_refs, out_refs = refs[:nb], refs[nb:2 * nb]
        send_sems, recv_sems = refs[2 * nb:]
        x, y, c, chips = _position()
        me = 2 * x + y
        sibling = (x, y, 1 - c)

        def copy(b, kind, q, src, dst, to):
            k = (b * 6 + kind) * nq + q
            return pltpu.make_async_remote_copy(src_ref=src, dst_ref=dst, send_sem=send_sems.at[k], recv_sem=recv_sems.at[k],
                                                device_id=to, device_id_type=MESH)

        first = [copy(b, j, q, w_refs[b].at[c, chunks[b][q]], out_refs[b].at[me, c, chunks[b][q]], (*chip, c))
                 for q in range(nq) for b in range(nb) for j, chip in enumerate(chips)]
        for cp in first:
            cp.start()
        passed = []
        for q in range(nq):
            for b in range(nb):
                for j, (px, py) in enumerate(chips):
                    landed = out_refs[b].at[2 * px + py, c, chunks[b][q]]
                    copy(b, j, q, landed, landed, (px, py, c)).wait_recv()
                    fwd = copy(b, 3 + j, q, landed, landed, sibling)
                    fwd.start()
                    passed.append(fwd)
        for q in range(nq):
            for b in range(nb):
                for j, (px, py) in enumerate(chips):
                    landed = out_refs[b].at[2 * px + py, 1 - c, chunks[b][q]]
                    copy(b, 3 + j, q, landed, landed, sibling).wait_recv()
        for cp in first + passed:
            cp.wait_send()

    return pl.pallas_call(
        body, name="allgather_weights", in_specs=[HBM_SPEC] * nb, out_specs=[HBM_SPEC] * nb,
        out_shape=[jax.ShapeDtypeStruct((N_SHARD,) + a.shape, a.dtype) for a in locals_],
        scratch_shapes=[pltpu.SemaphoreType.DMA((nb * 6 * nq,)), pltpu.SemaphoreType.DMA((nb * 6 * nq,))],
    )(*locals_)


def _sibling_exchange(bufs):
    nb, nq, ns = len(bufs), COPY_CHUNKS, N_SHARD
    chunks = [_row_chunks(a.shape[2]) for a in bufs]

    def body(*refs):
        g_refs, got_refs = refs[:nb], refs[nb:2 * nb]
        send_sems, recv_sems = refs[2 * nb:]
        x, y, c, _ = _position()
        swaps = [pltpu.make_async_remote_copy(src_ref=g_refs[b].at[s, 1 - c, chunks[b][q]], dst_ref=got_refs[b].at[s, chunks[b][q]],
                                              send_sem=send_sems.at[(b * ns + s) * nq + q], recv_sem=recv_sems.at[(b * ns + s) * nq + q],
                                              device_id=(x, y, 1 - c), device_id_type=MESH)
                 for b in range(nb) for s in range(ns) for q in range(nq)]
        for cp in swaps:
            cp.start()
        for cp in swaps:
            cp.wait()

    return pl.pallas_call(
        body, name="grad_sibling_exchange", in_specs=[HBM_SPEC] * nb, out_specs=[HBM_SPEC] * nb,
        out_shape=[jax.ShapeDtypeStruct((ns,) + a.shape[2:], a.dtype) for a in bufs],
        scratch_shapes=[pltpu.SemaphoreType.DMA((nb * ns * nq,)), pltpu.SemaphoreType.DMA((nb * ns * nq,))],
    )(*bufs)


def _chip_exchange(parts):
    nb, nq = len(parts), COPY_CHUNKS
    chunks = [_row_chunks(a.shape[1]) for a in parts]

    def body(*refs):
        b_refs, got_refs = refs[:nb], refs[nb:2 * nb]
        send_sems, recv_sems = refs[2 * nb:]
        x, y, c, chips = _position()
        sends = [pltpu.make_async_remote_copy(src_ref=b_refs[b].at[2 * px + py, chunks[b][q]], dst_ref=got_refs[b].at[j, chunks[b][q]],
                                              send_sem=send_sems.at[(b * 3 + j) * nq + q], recv_sem=recv_sems.at[(b * 3 + j) * nq + q],
                                              device_id=(px, py, c), device_id_type=MESH)
                 for q in range(nq) for b in range(nb) for j, (px, py) in enumerate(chips)]
        for cp in sends:
            cp.start()
        for cp in sends:
            cp.wait()

    return pl.pallas_call(
        body, name="grad_chip_exchange", in_specs=[HBM_SPEC] * nb, out_specs=[HBM_SPEC] * nb,
        out_shape=[jax.ShapeDtypeStruct((N_SHARD - 1,) + a.shape[1:], a.dtype) for a in parts],
        scratch_shapes=[pltpu.SemaphoreType.DMA((nb * 3 * nq,)), pltpu.SemaphoreType.DMA((nb * 3 * nq,))],
    )(*parts)


def _sibling_gather(halves):
    nb, nq = len(halves), 2 * COPY_CHUNKS
    chunks = [_row_chunks(a.shape[0], nq) for a in halves]

    def body(*refs):
        h_refs, out_refs = refs[:nb], refs[nb:2 * nb]
        send_sems, recv_sems = refs[2 * nb:]
        x, y, c, _ = _position()

        def chunk(b, q, half_idx):
            return pltpu.make_async_remote_copy(src_ref=h_refs[b].at[chunks[b][q]], dst_ref=out_refs[b].at[half_idx, chunks[b][q]],
                                                send_sem=send_sems.at[b * nq + q], recv_sem=recv_sems.at[b * nq + q],
                                                device_id=(x, y, 1 - c), device_id_type=MESH)

        pushes = [chunk(b, q, c) for b in range(nb) for q in range(nq)]
        for cp in pushes:
            cp.start()
        for b in range(nb):
            for q in range(nq):
                chunk(b, q, 1 - c).wait_recv()
        for cp in pushes:
            cp.wait_send()

    return pl.pallas_call(
        body, name="grad_sibling_gather", in_specs=[HBM_SPEC] * nb, out_specs=[HBM_SPEC] * nb,
        out_shape=[jax.ShapeDtypeStruct((2,) + a.shape, a.dtype) for a in halves],
        scratch_shapes=[pltpu.SemaphoreType.DMA((nb * nq,)), pltpu.SemaphoreType.DMA((nb * nq,))],
    )(*halves)


def _gather_partials(part):
    r, lanes = part.shape

    def body(p_ref, out_ref, send_sems, recv_sems):
        x, y, c, _ = _position()
        flips = [(fx, fy, fc) for fx in (0, 1) for fy in (0, 1) for fc in (0, 1)][1:]
        sends = []
        for k, (fx, fy, fc) in enumerate(flips):
            cp = pltpu.make_async_remote_copy(src_ref=p_ref, dst_ref=out_ref.at[4 * x + 2 * y + c], send_sem=send_sems.at[k],
                                              recv_sem=recv_sems.at[k], device_id=(x ^ fx, y ^ fy, c ^ fc), device_id_type=MESH)
            cp.start()
            sends.append(cp)
        for k, (fx, fy, fc) in enumerate(flips):
            src = out_ref.at[4 * (x ^ fx) + 2 * (y ^ fy) + (c ^ fc)]
            pltpu.make_async_remote_copy(src_ref=src, dst_ref=src, send_sem=send_sems.at[k], recv_sem=recv_sems.at[k],
                                         device_id=(x ^ fx, y ^ fy, c ^ fc), device_id_type=MESH).wait_recv()
        for cp in sends:
            cp.wait_send()

    return pl.pallas_call(
        body, name="small_gather_partials", in_specs=[HBM_SPEC], out_specs=HBM_SPEC,
        out_shape=jax.ShapeDtypeStruct((8, r, lanes), part.dtype),
        scratch_shapes=[pltpu.SemaphoreType.DMA((7,)), pltpu.SemaphoreType.DMA((7,))],
    )(part)


def _sum_leading(stack, name, also_bf16=False):
    k, r, lanes = stack.shape
    tm = _pick(r, 256, 16)
    outs = [jax.ShapeDtypeStruct((r, lanes), F32)] + ([jax.ShapeDtypeStruct((r, lanes), BF16)] if also_bf16 else [])

    def body(s_ref, *o_refs):
        acc = s_ref[0].astype(F32)
        for i in range(1, k):
            acc = acc + s_ref[i].astype(F32)
        for o in o_refs:
            o[...] = acc.astype(o.dtype)

    spec = pl.BlockSpec((tm, lanes), lambda i: (i, 0))
    return pl.pallas_call(
        body, name=name, grid=(r // tm,), in_specs=[pl.BlockSpec((k, tm, lanes), lambda i: (0, i, 0))],
        out_specs=[spec] * len(outs), out_shape=outs,
        compiler_params=pltpu.CompilerParams(dimension_semantics=("parallel",)),
    )(stack)


def _add_pair(g2, got, half, name):
    ns, _, rows, width = g2.shape
    tm = _pick(rows, 256, 16)
    spec = pl.BlockSpec((1, tm, width), lambda s, i, h_ref: (s, i, 0))

    def body(h_ref, a_ref, b_ref, f_ref, o_ref):
        acc = a_ref[0] + b_ref[...]
        f_ref[...] = acc
        o_ref[...] = acc.astype(BF16)

    grid_spec = pltpu.PrefetchScalarGridSpec(
        num_scalar_prefetch=1, grid=(ns, rows // tm),
        in_specs=[pl.BlockSpec((1, 1, tm, width), lambda s, i, h_ref: (s, h_ref[0], i, 0)), spec], out_specs=[spec, spec])
    return pl.pallas_call(
        body, name=name, grid_spec=grid_spec,
        out_shape=[jax.ShapeDtypeStruct(got.shape, F32), jax.ShapeDtypeStruct(got.shape, BF16)],
        compiler_params=pltpu.CompilerParams(dimension_semantics=("parallel", "parallel")),
    )(jnp.asarray(half, jnp.int32).reshape(1), g2, got)


def _add_own(parts, got, mine, name):
    _, rows, width = parts.shape
    tm = _pick(rows, 256, 16)

    def body(m_ref, p_ref, g_ref, out_ref):
        acc = p_ref[0]
        for j in range(g_ref.shape[0]):
            acc = acc + g_ref[j].astype(F32)
        out_ref[...] = acc

    grid_spec = pltpu.PrefetchScalarGridSpec(
        num_scalar_prefetch=1, grid=(rows // tm,),
        in_specs=[pl.BlockSpec((1, tm, width), lambda i, m_ref: (m_ref[0], i, 0)),
                  pl.BlockSpec((got.shape[0], tm, width), lambda i, m_ref: (0, i, 0))],
        out_specs=pl.BlockSpec((tm, width), lambda i, m_ref: (i, 0)))
    return pl.pallas_call(
        body, name=name, grid_spec=grid_spec, out_shape=jax.ShapeDtypeStruct((rows, width), F32),
        compiler_params=pltpu.CompilerParams(dimension_semantics=("parallel",)),
    )(jnp.asarray(mine, jnp.int32).reshape(1), parts, got)


def _local_shape(shape, axis):
    return tuple(d // N_SHARD if a == axis else d for a, d in enumerate(shape))


def _big_sizes():
    return [DEPTH * int(np.prod(_local_shape(shape, axis))) for _, shape, axis in FLAT_BIG]


COL_SHARDED = (("w_in", IN_WIDTH // N_SHARD), ("w_ffn_in", 2 * FFN_HIDDEN // N_SHARD))
FLAT_BIG = tuple(entry for entry in BIG if entry[0] not in [name for name, _ in COL_SHARDED])
FLAT_ROW_TILE = 256
ELEMENTWISE_BIG = ("conv_w",)


def _three_bf16(w):
    hi = w.astype(BF16)
    r1 = w - hi.astype(F32)
    mid = r1.astype(BF16)
    lo = (r1 - mid.astype(F32)).astype(BF16)
    return jnp.stack([hi, mid, lo], axis=-1)


PIECE_ROWS = 16


def _flat_layout(for_weights):
    pieces = []
    for (name, shape, axis), size in zip(FLAT_BIG, _big_sizes(), strict=True):
        n = size * (3 if for_weights and name in ELEMENTWISE_BIG else 1)
        rows = -(-n // (LANES * PIECE_ROWS)) * PIECE_ROWS
        pieces.append((name, shape, axis, n, rows))
    total = sum(p[-1] for p in pieces)
    half = -(-total // (2 * FLAT_ROW_TILE)) * FLAT_ROW_TILE
    return pieces, half


def _to_rows(flat, rows):
    lead, n = flat.shape[:-1], flat.shape[-1]
    fill = jnp.zeros(lead + (rows * LANES - n,), flat.dtype)
    return jnp.concatenate([flat, fill], axis=-1).reshape(lead + (rows, LANES))


def _pack_local_weights(wl):
    pieces, half = _flat_layout(True)
    parts = [_to_rows((_three_bf16(wl[name]) if name in ELEMENTWISE_BIG else wl[name].astype(BF16)).reshape(-1), rows)
             for name, _, _, _, rows in pieces]
    parts.append(jnp.zeros((2 * half - sum(p[-1] for p in pieces), LANES), BF16))
    return jnp.concatenate(parts, axis=0).reshape(2, half, LANES)


def _unpack_local(flat):
    pieces, _ = _flat_layout(False)
    flat = flat.reshape(-1, LANES)
    out, off = {}, 0
    for name, shape, axis, n, rows in pieces:
        out[name] = flat[off:off + rows].reshape(-1)[:n].reshape((DEPTH,) + _local_shape(shape, axis))
        off += rows
    return out


def _unpack_gathered(gathered):
    pieces, _ = _flat_layout(True)
    out, off = {}, 0
    for name, shape, axis, n, rows in pieces:
        local = (N_SHARD, DEPTH) + _local_shape(shape, axis)
        seg = gathered[:, off:off + rows].reshape(N_SHARD, -1)[:, :n]
        off += rows
        if name in ELEMENTWISE_BIG:
            parts = seg.reshape(local + (3,)).astype(F32)
            seg = (parts[..., 0] + parts[..., 1]) + parts[..., 2]
        else:
            seg = seg.reshape(local)
        out[name] = jnp.moveaxis(seg, 0, 1 + axis).reshape((DEPTH,) + shape)
    return out


def _col_segments(name, c):
    if name != "w_in":
        return [(s, 0, c, c * s) for s in range(N_SHARD)]
    segs = []
    for s in range(N_SHARD):
        lo, hi = c * s, c * (s + 1)
        if lo < MAIN_WIDTH:
            segs.append((s, 0, min(hi, MAIN_WIDTH) - lo, GATES_WIDTH + lo))
        if hi > MAIN_WIDTH:
            first = max(lo, MAIN_WIDTH)
            segs.append((s, first - lo, c, first - MAIN_WIDTH))
    return segs


def _join_col_shards(shards, segs, name):
    ns, depth, rows, c = shards.shape
    tm = _pick(rows, 256, 16)

    def body(i_ref, o_ref):
        for s, lo, hi, start in segs:
            o_ref[0, :, start:start + hi - lo] = i_ref[s, 0, :, lo:hi]

    return pl.pallas_call(
        body, name=name, grid=(depth, rows // tm),
        in_specs=[pl.BlockSpec((ns, 1, tm, c), lambda l, i: (0, l, i, 0))],
        out_specs=pl.BlockSpec((1, tm, ns * c), lambda l, i: (l, i, 0)),
        out_shape=jax.ShapeDtypeStruct((depth, rows, ns * c), shards.dtype),
        compiler_params=pltpu.CompilerParams(dimension_semantics=("parallel", "parallel")),
    )(shards)


def _split_col_shards(full, segs, layer, stacked, name):
    rows, width = full.shape
    c = width // N_SHARD
    tm = _pick(rows, 256, 16)

    def body(*refs):
        i_ref, o_ref = refs[0], refs[-1]
        for s, lo, hi, start in segs:
            o_ref[s, 0, :, lo:hi] = i_ref[:, start:start + hi - lo]

    return pl.pallas_call(
        body, name=name, grid=(rows // tm,),
        in_specs=[pl.BlockSpec((tm, width), lambda i: (i, 0))] + ([] if stacked is None else [pl.BlockSpec(memory_space=pl.ANY)]),
        out_specs=pl.BlockSpec((N_SHARD, 1, tm, c), lambda i: (0, layer, i, 0)),
        out_shape=jax.ShapeDtypeStruct((N_SHARD, DEPTH, rows, c), full.dtype),
        input_output_aliases={} if stacked is None else {1: 0},
        compiler_params=pltpu.CompilerParams(dimension_semantics=("parallel",)),
    )(*([full] if stacked is None else [full, stacked]))


def _pack_full_grads(big_grads):
    pieces, half = _flat_layout(False)
    parts = []
    for name, shape, axis, _, rows in pieces:
        per_layer = []
        for gfull in big_grads[name]:
            split = gfull.reshape(shape[:axis] + (N_SHARD, shape[axis] // N_SHARD) + shape[axis + 1:])
            per_layer.append(jnp.moveaxis(split, axis, 0).reshape(N_SHARD, -1))
        parts.append(_to_rows(jnp.concatenate(per_layer, axis=1), rows))
    parts.append(jnp.zeros((N_SHARD, 2 * half - sum(p[-1] for p in pieces), LANES), F32))
    return jnp.concatenate(parts, axis=1).reshape(N_SHARD, 2, half, LANES)


def _pack_small(grads):
    flat = jnp.concatenate([grads[name].reshape(-1) for name in SMALL])
    r = -(-flat.shape[0] // (8 * LANES)) * 8
    return jnp.pad(flat, (0, r * LANES - flat.shape[0])).reshape(r, LANES)


def _unpack_small(flat, like):
    flat = flat.reshape(-1)
    out, off = {}, 0
    for name in SMALL:
        size = int(np.prod(like[name].shape))
        out[name] = flat[off:off + size].reshape(like[name].shape)
        off += size
    return out


def _adamw(w, g, m, v, name):
    shape = w.shape
    cols = shape[-1]
    rows = int(np.prod(shape[:-1])) if len(shape) > 1 else 1
    w2, g2, m2, v2 = (a.reshape(rows, cols) for a in (w, g, m, v))

    def fn(wv, gv, mv, vv):
        mn = ADAM_B1 * mv + (1.0 - ADAM_B1) * gv
        vn = ADAM_B2 * vv + (1.0 - ADAM_B2) * jnp.square(gv)
        m_hat = mn / (1.0 - ADAM_B1 ** ADAM_STEP)
        v_hat = vn / (1.0 - ADAM_B2 ** ADAM_STEP)
        delta = -ADAM_LR * (m_hat / (jnp.sqrt(v_hat) + ADAM_EPS) + ADAM_WD * wv)
        return [delta, mn, vn], []

    tm = 256 if rows % 8 == 0 and rows > 256 else rows
    res = _rw(fn, [(a, None, _c0) for a in (w2, g2, m2, v2)], [(cols, None, F32, _c0)] * 3, tm=tm, name=name)
    return [r.reshape(shape) for r in res]


def _step(x, p, target, weights, moments_m, moments_v):
    xi, yi, ci = lax.axis_index("x"), lax.axis_index("y"), lax.axis_index("c")
    chip = 2 * xi + yi
    half_rows = DEPTH // 2 * D_MODEL
    locals_ = [_pack_local_weights(weights)] + [weights[name].astype(BF16).reshape(2, half_rows, c) for name, c in COL_SHARDED]
    gathered = _allgather_weights(locals_)
    gathered = [_place(g.reshape(N_SHARD, 2 * a.shape[1], a.shape[2]), a.reshape(2 * a.shape[1], a.shape[2]), chip, f"place_own_weights_{k}")
                for k, (g, a) in enumerate(zip(gathered, locals_, strict=True))]
    wt = dict(_unpack_gathered(gathered[0]))
    for (name, c), g in zip(COL_SHARDED, gathered[1:], strict=True):
        wt[name] = _join_col_shards(g.reshape(N_SHARD, DEPTH, D_MODEL, c), _col_segments(name, c), f"join_col_shards_{name}")
    for name in SMALL:
        wt[name] = weights[name]
    loss, dx, small_grads, big_grads = _local_step(x[0], p[:, 0], target[0], wt)
    loss = lax.psum(loss, ("x", "y", "c"))
    bufs = [_pack_full_grads(big_grads)]
    for name, c in COL_SHARDED:
        stacked = None
        for li, g in enumerate(big_grads[name]):
            stacked = _split_col_shards(g, _col_segments(name, c), li, stacked, f"split_col_shards_{name}_l{li}")
        bufs.append(stacked.reshape(N_SHARD, 2, half_rows, c))
    gots = _sibling_exchange(bufs)
    sums = [_add_pair(b, g, ci, f"grad_add_sibling_{k}") for k, (b, g) in enumerate(zip(bufs, gots, strict=True))]
    others = _chip_exchange([s_bf16 for _, s_bf16 in sums])
    halves = [_add_own(s_f32, o, chip, f"grad_add_chips_{k}") for k, ((s_f32, _), o) in enumerate(zip(sums, others, strict=True))]
    both = [_place(b, h, ci, f"place_own_half_{k}") for k, (b, h) in enumerate(zip(_sibling_gather(halves), halves, strict=True))]
    reduced = _unpack_local(both[0])
    for (name, c), b in zip(COL_SHARDED, both[1:], strict=True):
        reduced[name] = b.reshape(DEPTH, D_MODEL, c)
    small_part = _pack_small(small_grads)
    small_all = _place(_gather_partials(small_part), small_part, 4 * xi + 2 * yi + ci, "place_own_small")
    reduced.update(_unpack_small(_sum_leading(small_all, "small_sum")[0], {k: weights[k] for k in SMALL}))
    outs_g, outs_d, outs_m, outs_v = [], [], [], []
    for name in WEIGHTS:
        d, mn, vn = _adamw(weights[name], reduced[name], moments_m[name], moments_v[name], f"adamw_{name}")
        outs_g.append(reduced[name])
        outs_d.append(d)
        outs_m.append(mn)
        outs_v.append(vn)
    return (loss, dx[None], *outs_g, *outs_d, *outs_m, *outs_v)


def kernel(x, p, rel_bias, norm_mix, w_in, ssm_lambda_re, ssm_lambda_im, ssm_b_re, ssm_b_im, ssm_c_re, ssm_c_im, ssm_d, ssm_log_dt, ssm_w_glu, conv_w, attn_sinks, w_branch, w_out, norm_ffn, w_ffn_in, w_ffn_out, norm_ple, w_ple_gate, w_ple_proj, norm_final, loss_target, m_rel_bias, m_norm_mix, m_w_in, m_ssm_lambda_re, m_ssm_lambda_im, m_ssm_b_re, m_ssm_b_im, m_ssm_c_re, m_ssm_c_im, m_ssm_d, m_ssm_log_dt, m_ssm_w_glu, m_conv_w, m_attn_sinks, m_w_branch, m_w_out, m_norm_ffn, m_w_ffn_in, m_w_ffn_out, m_norm_ple, m_w_ple_gate, m_w_ple_proj, m_norm_final, v_rel_bias, v_norm_mix, v_w_in, v_ssm_lambda_re, v_ssm_lambda_im, v_ssm_b_re, v_ssm_b_im, v_ssm_c_re, v_ssm_c_im, v_ssm_d, v_ssm_log_dt, v_ssm_w_glu, v_conv_w, v_attn_sinks, v_w_branch, v_w_out, v_norm_ffn, v_w_ffn_in, v_w_ffn_out, v_norm_ple, v_w_ple_gate, v_w_ple_proj, v_norm_final):
    weights = dict(rel_bias=rel_bias, norm_mix=norm_mix, w_in=w_in, ssm_lambda_re=ssm_lambda_re, ssm_lambda_im=ssm_lambda_im,
                   ssm_b_re=ssm_b_re, ssm_b_im=ssm_b_im, ssm_c_re=ssm_c_re, ssm_c_im=ssm_c_im, ssm_d=ssm_d, ssm_log_dt=ssm_log_dt,
                   ssm_w_glu=ssm_w_glu, conv_w=conv_w, attn_sinks=attn_sinks, w_branch=w_branch, w_out=w_out, norm_ffn=norm_ffn,
                   w_ffn_in=w_ffn_in, w_ffn_out=w_ffn_out, norm_ple=norm_ple, w_ple_gate=w_ple_gate, w_ple_proj=w_ple_proj,
                   norm_final=norm_final)
    moments_m = dict(rel_bias=m_rel_bias, norm_mix=m_norm_mix, w_in=m_w_in, ssm_lambda_re=m_ssm_lambda_re, ssm_lambda_im=m_ssm_lambda_im,
                     ssm_b_re=m_ssm_b_re, ssm_b_im=m_ssm_b_im, ssm_c_re=m_ssm_c_re, ssm_c_im=m_ssm_c_im, ssm_d=m_ssm_d,
                     ssm_log_dt=m_ssm_log_dt, ssm_w_glu=m_ssm_w_glu, conv_w=m_conv_w, attn_sinks=m_attn_sinks, w_branch=m_w_branch,
                     w_out=m_w_out, norm_ffn=m_norm_ffn, w_ffn_in=m_w_ffn_in, w_ffn_out=m_w_ffn_out, norm_ple=m_norm_ple,
                     w_ple_gate=m_w_ple_gate, w_ple_proj=m_w_ple_proj, norm_final=m_norm_final)
    moments_v = dict(rel_bias=v_rel_bias, norm_mix=v_norm_mix, w_in=v_w_in, ssm_lambda_re=v_ssm_lambda_re, ssm_lambda_im=v_ssm_lambda_im,
                     ssm_b_re=v_ssm_b_re, ssm_b_im=v_ssm_b_im, ssm_c_re=v_ssm_c_re, ssm_c_im=v_ssm_c_im, ssm_d=v_ssm_d,
                     ssm_log_dt=v_ssm_log_dt, ssm_w_glu=v_ssm_w_glu, conv_w=v_conv_w, attn_sinks=v_attn_sinks, w_branch=v_w_branch,
                     w_out=v_w_out, norm_ffn=v_norm_ffn, w_ffn_in=v_w_ffn_in, w_ffn_out=v_w_ffn_out, norm_ple=v_norm_ple,
                     w_ple_gate=v_w_ple_gate, w_ple_proj=v_w_ple_proj, norm_final=v_norm_final)
    return _step(x, p, loss_target, weights, moments_m, moments_v)
```

```python
import functools
import math

import numpy as np

import jax
import jax.numpy as jnp
from jax import lax
from jax.experimental import pallas as pl
from jax.experimental.pallas import tpu as pltpu

F32, BF16 = jnp.float32, jnp.bfloat16
MESH = pl.DeviceIdType.MESH

D_MODEL = 1024
DEPTH = 4
PLE_DIM = 256
BRANCH = 512
N_GROUPS = 32
GROUP_CH = 16
N_STATE = 64
SSM_STATES = N_GROUPS * N_STATE
SSM_BLOCKS = 4
HEAD_DIM = 64
N_Q = 8
N_KV = 2
GQA = N_Q // N_KV
WINDOW = 128
ATTN_SCALE = 1.0 / math.sqrt(HEAD_DIM)
REL_BUCKETS = 32
REL_MAX_DIST = 128
FFN_HIDDEN = 2816
FFN_COLS = 1408
FFN_NCOL = FFN_HIDDEN // FFN_COLS
IN_WIDTH = 5888
RMS_EPS = 1e-6
NEG = -1e30

ADAM_LR, ADAM_B1, ADAM_B2, ADAM_EPS, ADAM_WD, ADAM_STEP = 0.001, 0.9, 0.999, 1e-08, 0.01, 10

N_SHARD = 4
LANES = 1024
COPY_CHUNKS = 4

GATES_WIDTH = 3 * D_MODEL
MAIN_WIDTH = IN_WIDTH - GATES_WIDTH
OFF_G, OFF_U, OFF_CB, OFF_CC, OFF_CX, OFF_Q, OFF_K, OFF_V = 0, 3072, 3584, 4096, 4608, 5120, 5632, 5760
U_BLK, CB_BLK, CC_BLK, CX_BLK = OFF_U // BRANCH, OFF_CB // BRANCH, OFF_CC // BRANCH, OFF_CX // BRANCH

BIG = (
    ("w_in", (D_MODEL, IN_WIDTH), 1),
    ("ssm_w_glu", (BRANCH, BRANCH), 0),
    ("conv_w", (3, BRANCH), 1),
    ("w_branch", (3, BRANCH, D_MODEL), 2),
    ("w_out", (D_MODEL, D_MODEL), 0),
    ("w_ffn_in", (D_MODEL, 2 * FFN_HIDDEN), 1),
    ("w_ffn_out", (FFN_HIDDEN, D_MODEL), 0),
    ("w_ple_gate", (D_MODEL, D_MODEL), 0),
    ("w_ple_proj", (PLE_DIM, D_MODEL), 1),
)
SMALL = ("rel_bias", "norm_mix", "ssm_lambda_re", "ssm_lambda_im", "ssm_b_re", "ssm_b_im", "ssm_c_re", "ssm_c_im",
         "ssm_d", "ssm_log_dt", "attn_sinks", "norm_ffn", "norm_ple", "norm_final")
WEIGHTS = ("rel_bias", "norm_mix", "w_in", "ssm_lambda_re", "ssm_lambda_im", "ssm_b_re", "ssm_b_im", "ssm_c_re",
           "ssm_c_im", "ssm_d", "ssm_log_dt", "ssm_w_glu", "conv_w", "attn_sinks", "w_branch", "w_out", "norm_ffn",
           "w_ffn_in", "w_ffn_out", "norm_ple", "w_ple_gate", "w_ple_proj", "norm_final")


def _c0(j):
    return 0


def _pick(n, cap, unit=128):
    if n <= cap:
        return n
    best = None
    for t in range(unit, cap + 1, unit):
        if n % t == 0:
            best = t
    assert best is not None, (n, cap, unit)
    return best


_DIMS = {"nn": ((1,), (0,)), "nt": ((1,), (1,)), "tn": ((0,), (0,))}


def _mm(a, b, mode, *, name, out_dtype=F32, add=None, tm=1024, tn=1024, tk=1024, b_k0=0, n_outer=False, rms_out=None, rms_back=None):
    if mode == "nn":
        (m, k), (k2, n) = a.shape, b.shape
    elif mode == "nt":
        (m, k), (n, k2) = a.shape, b.shape
    else:
        (k, m), (k2, n) = a.shape, b.shape
    assert k == k2 or (mode == "nt" and b_k0 + k <= k2), (a.shape, b.shape, mode)
    tm, tn, tk = _pick(m, tm, 128 if mode == "tn" else 8), _pick(n, tn), _pick(k, tk, 128 if mode != "tn" else 8)
    nk = k // tk
    assert b_k0 % tk == 0
    kb0 = b_k0 // tk
    def at(f):
        return (lambda j, i, kk: f(i, j, kk)) if n_outer else f

    a_spec = pl.BlockSpec((tk, tm), at(lambda i, j, kk: (kk, i))) if mode == "tn" else pl.BlockSpec((tm, tk), at(lambda i, j, kk: (i, kk)))
    b_spec = (pl.BlockSpec((tn, tk), at(lambda i, j, kk: (j, kb0 + kk))) if mode == "nt"
              else pl.BlockSpec((tk, tn), at(lambda i, j, kk: (kk, j))))
    o_spec = pl.BlockSpec((tm, tn), at(lambda i, j, kk: (i, j)))
    dims = (_DIMS[mode], ((), ()))
    has_add = add is not None
    fused_rows = rms_out is not None or rms_back is not None
    assert not fused_rows or (tn == n and not n_outer), "a fused RMSNorm needs whole rows in a tile"
    row_spec = pl.BlockSpec((1, n), at(lambda i, j, kk: (0, 0)))
    extra = [rms_out] if rms_out is not None else (list(rms_back) if rms_back is not None else [])
    extra_specs = [row_spec] if rms_out is not None else ([o_spec, o_spec, row_spec] if rms_back is not None else [])
    n_in = 2 + has_add + len(extra)

    def body(*refs):
        a_ref, b_ref = refs[0], refs[1]
        add_ref = refs[2] if has_add else None
        x_refs = refs[2 + has_add:n_in]
        o_refs, acc_ref = refs[n_in:-1], refs[-1]
        part = lax.dot_general(a_ref[...].astype(BF16), b_ref[...].astype(BF16), dims, preferred_element_type=F32)

        def finish(acc):
            if has_add:
                acc = acc + add_ref[...]
            if rms_back is not None:
                _, vjp = jax.vjp(_rms, x_refs[0][...], x_refs[2][...])
                dx, dg = vjp(acc)
                o_refs[0][...] = x_refs[1][...] + dx
                first = pl.program_id(0) == 0

                @pl.when(first)
                def _():
                    o_refs[1][...] = dg

                @pl.when(jnp.logical_not(first))
                def _():
                    o_refs[1][...] += dg
                return
            o_refs[0][...] = acc.astype(o_refs[0].dtype)
            if rms_out is not None:
                o_refs[1][...] = _rms(acc, x_refs[0][...]).astype(BF16)

        if nk == 1:
            finish(part)
        else:
            kk = pl.program_id(2)

            @pl.when(kk == 0)
            def _():
                acc_ref[...] = part

            @pl.when(kk > 0)
            def _():
                acc_ref[...] += part

            @pl.when(kk == nk - 1)
            def _():
                finish(acc_ref[...])

    operands = [a, b] + ([add] if has_add else []) + extra
    in_specs = [a_spec, b_spec] + ([o_spec] if has_add else []) + extra_specs
    out_specs, out_shape = [o_spec], [jax.ShapeDtypeStruct((m, n), out_dtype)]
    if rms_out is not None:
        out_specs, out_shape = out_specs + [o_spec], out_shape + [jax.ShapeDtypeStruct((m, n), BF16)]
    if rms_back is not None:
        out_specs, out_shape = out_specs + [row_spec], out_shape + [jax.ShapeDtypeStruct((1, n), F32)]
    res = pl.pallas_call(
        body, name=name, grid=(n // tn, m // tm, nk) if n_outer else (m // tm, n // tn, nk), in_specs=in_specs, out_specs=out_specs,
        out_shape=out_shape, scratch_shapes=[pltpu.VMEM((tm, tn) if nk > 1 else (8, 128), F32)],
        compiler_params=pltpu.CompilerParams(
            dimension_semantics=("arbitrary",) * 3 if rms_back is not None else ("parallel", "parallel", "arbitrary")),
    )(*operands)
    return res if fused_rows else res[0]


def _rw(fn, ins, outs, *, name, params=(), reds=(), tm=256, ncol=1, with_j=False):
    t = ins[0][0].shape[0]
    tm = _pick(t, tm, 8)
    nrow = t // tm
    n_in, n_p, n_out = len(ins), len(params), len(outs)

    in_specs = [pl.BlockSpec((tm, bw or arr.shape[1]), lambda j, i, cf=cf: (i, cf(j))) for arr, bw, cf in ins]
    in_specs += [pl.BlockSpec(p.shape, lambda j, i: (0, 0)) for p in params]
    out_specs = [pl.BlockSpec((tm, bw or w), lambda j, i, cf=cf: (i, cf(j))) for w, bw, _, cf in outs]
    out_specs += [pl.BlockSpec((shp[0], bw or shp[1]), lambda j, i, cf=cf: (0, cf(j))) for shp, bw, cf in reds]
    out_shape = [jax.ShapeDtypeStruct((t, w), dt) for w, _, dt, _ in outs]
    out_shape += [jax.ShapeDtypeStruct(shp, F32) for shp, _, _ in reds]

    def body(*refs):
        in_refs, p_refs = refs[:n_in], refs[n_in:n_in + n_p]
        o_refs, r_refs = refs[n_in + n_p:n_in + n_p + n_out], refs[n_in + n_p + n_out:]
        args = [r[...].astype(F32) for r in in_refs] + [r[...] for r in p_refs]
        if with_j:
            args = [pl.program_id(0)] + args
        o_vals, r_vals = fn(*args)
        for r, v in zip(o_refs, o_vals, strict=True):
            r[...] = v.astype(r.dtype)
        if r_refs:
            i = pl.program_id(1)
            for r, v in zip(r_refs, r_vals, strict=True):
                @pl.when(i == 0)
                def _(r=r, v=v):
                    r[...] = v

                @pl.when(i > 0)
                def _(r=r, v=v):
                    r[...] += v

    res = pl.pallas_call(
        body, name=name, grid=(ncol, nrow), in_specs=in_specs, out_specs=out_specs, out_shape=out_shape,
        compiler_params=pltpu.CompilerParams(dimension_semantics=("parallel", "arbitrary" if reds else "parallel")),
    )(*[a for a, _, _ in ins], *params)
    return res


def _rms(x, g):
    return x * lax.rsqrt(jnp.mean(x * x, axis=-1, keepdims=True) + RMS_EPS) * g


def _rms_fwd(x, g, name):
    return _rw(lambda xv, gv: ([_rms(xv, gv)], []), [(x, None, _c0)], [(D_MODEL, None, BF16, _c0)], params=[g], name=name)[0]


def _bd_apply(acts, mats, combos, mode, *, name, tm=512, col_blocks=None):
    t = acts[0].shape[0]
    tm = _pick(t, tm, 8)
    nb, r, c = mats[0].shape
    win, wout = (r, c) if mode == "nn" else (c, r)
    dims = (_DIMS[mode], ((), ()))
    n_a, n_m = len(acts), len(mats)

    def body(*refs):
        a_vals = [ar[...].astype(BF16) for ar in refs[:n_a]]
        m_refs, o_refs = refs[n_a:n_a + n_m], refs[n_a + n_m:]
        for o_ref, terms in zip(o_refs, combos, strict=True):
            for j in range(nb):
                acc = None
                for ai, mi in terms:
                    part = lax.dot_general(a_vals[ai][:, j * win:(j + 1) * win], m_refs[mi][j], dims, preferred_element_type=F32)
                    acc = part if acc is None else acc + part
                o_ref[:, j * wout:(j + 1) * wout] = acc

    return pl.pallas_call(
        body, name=name, grid=(t // tm,),
        in_specs=[pl.BlockSpec((tm, nb * win), lambda i, cb=cb: (i, cb)) for cb in (col_blocks or [0] * n_a)]
        + [pl.BlockSpec(m.shape, lambda i: (0, 0, 0)) for m in mats],
        out_specs=[pl.BlockSpec((tm, nb * wout), lambda i: (i, 0))] * len(combos),
        out_shape=[jax.ShapeDtypeStruct((t, nb * wout), F32)] * len(combos),
        compiler_params=pltpu.CompilerParams(dimension_semantics=("parallel",)),
    )(*acts, *mats)


def _bd_grads(arrs, widths, pairs, *, name, tk=512, col_blocks=None):
    t = arrs[0].shape[0]
    tk = _pick(t, tk, 8)
    n_a = len(arrs)
    dims = (_DIMS["tn"], ((), ()))

    def body(*refs):
        vals = [ar[...].astype(BF16) for ar in refs[:n_a]]
        o_refs = refs[n_a:]
        @pl.when(pl.program_id(0) == 0)
        def _():
            for o_ref in o_refs:
                o_ref[...] = jnp.zeros_like(o_ref)

        for o_ref, (ai, bi) in zip(o_refs, pairs, strict=True):
            wa, wb = widths[ai], widths[bi]
            for j in range(SSM_BLOCKS):
                o_ref[j] += lax.dot_general(vals[ai][:, j * wa:(j + 1) * wa], vals[bi][:, j * wb:(j + 1) * wb], dims,
                                            preferred_element_type=F32)

    return pl.pallas_call(
        body, name=name, grid=(t // tk,),
        in_specs=[pl.BlockSpec((tk, SSM_BLOCKS * w), lambda k, cb=cb: (k, cb)) for w, cb in zip(widths, col_blocks or [0] * n_a, strict=True)],
        out_specs=[pl.BlockSpec((SSM_BLOCKS, widths[ai], widths[bi]), lambda k: (0, 0, 0)) for ai, bi in pairs],
        out_shape=[jax.ShapeDtypeStruct((SSM_BLOCKS, widths[ai], widths[bi]), F32) for ai, bi in pairs],
        compiler_params=pltpu.CompilerParams(dimension_semantics=("arbitrary",)),
    )(*arrs)


SCAN_LW = 512
SCAN_ROWS = 512
_DOUBLING = ((1, 0), (2, 1), (4, 2))


def _scan(xr, xi, pr, pi, dr, di, *, reverse, name, hr=None, hi=None):
    t, s = xr.shape
    lc = _pick(t, SCAN_ROWS, 8)
    nt, ngroups = t // lc, lc // 8
    with_da = hr is not None

    def tmap(l, tt):
        return ((nt - 1 - tt) if reverse else tt, l)

    x_spec = pl.BlockSpec((lc, SCAN_LW), tmap)
    tab_spec = pl.BlockSpec((8, SCAN_LW), lambda l, tt: (0, l))

    def body(*refs):
        xr_ref, xi_ref, pr_ref, pi_ref, dr_ref, di_ref = refs[:6]
        if with_da:
            hr_ref, hi_ref, or_ref, oi_ref, ar_ref, ai_ref, cr_ref, ci_ref = refs[6:]
        else:
            or_ref, oi_ref, cr_ref, ci_ref = refs[6:]
        tt = pl.program_id(1)

        @pl.when(tt == 0)
        def _():
            cr_ref[...] = jnp.zeros_like(cr_ref)
            ci_ref[...] = jnp.zeros_like(ci_ref)
            if with_da:
                ar_ref[...] = jnp.zeros_like(ar_ref)
                ai_ref[...] = jnp.zeros_like(ai_ref)

        sub = lax.broadcasted_iota(jnp.int32, (8, SCAN_LW), 0)
        pw_r, pw_i = pr_ref[...], pi_ref[...]

        def step(g, carry):
            g = (ngroups - 1 - g) if reverse else g
            r0 = pl.multiple_of(g * 8, 8)
            vr, vi = xr_ref[pl.ds(r0, 8), :], xi_ref[pl.ds(r0, 8), :]
            for shift, row in _DOUBLING:
                a_r, a_i = dr_ref[row:row + 1, :], di_ref[row:row + 1, :]
                if reverse:
                    keep = sub < 8 - shift
                    sr, si = pltpu.roll(vr, 8 - shift, 0), pltpu.roll(vi, 8 - shift, 0)
                else:
                    keep = sub >= shift
                    sr, si = pltpu.roll(vr, shift, 0), pltpu.roll(vi, shift, 0)
                sr, si = jnp.where(keep, sr, 0.0), jnp.where(keep, si, 0.0)
                vr, vi = vr + a_r * sr - a_i * si, vi + a_r * si + a_i * sr
            if with_da:
                cr, ci, acc_r, acc_i = carry
            else:
                cr, ci = carry
            vr, vi = vr + pw_r * cr - pw_i * ci, vi + pw_r * ci + pw_i * cr
            or_ref[pl.ds(r0, 8), :] = vr
            oi_ref[pl.ds(r0, 8), :] = vi
            if with_da:
                nr = jnp.where(sub < 7, pltpu.roll(vr, 7, 0), cr)
                ni = jnp.where(sub < 7, pltpu.roll(vi, 7, 0), ci)
                h_r, h_i = hr_ref[pl.ds(r0, 8), :], hi_ref[pl.ds(r0, 8), :]
                acc_r = acc_r + h_r * nr + h_i * ni
                acc_i = acc_i + h_r * ni - h_i * nr
            edge = 0 if reverse else 7
            cr = jnp.broadcast_to(vr[edge:edge + 1, :], vr.shape)
            ci = jnp.broadcast_to(vi[edge:edge + 1, :], vi.shape)
            return (cr, ci, acc_r, acc_i) if with_da else (cr, ci)

        zero = jnp.zeros((8, SCAN_LW), F32)
        init = (cr_ref[...], ci_ref[...]) + ((zero, zero) if with_da else ())
        fin = lax.fori_loop(0, ngroups, step, init, unroll=2)
        cr_ref[...] = fin[0]
        ci_ref[...] = fin[1]
        if with_da:
            ar_ref[...] += fin[2]
            ai_ref[...] += fin[3]

    n_x = 4 if with_da else 2
    out_specs = [x_spec, x_spec] + ([tab_spec, tab_spec] if with_da else [])
    out_shape = [jax.ShapeDtypeStruct((t, s), F32)] * 2 + ([jax.ShapeDtypeStruct((8, s), F32)] * 2 if with_da else [])
    operands = [xr, xi, pr, pi, dr, di] + ([hr, hi] if with_da else [])
    return pl.pallas_call(
        body, name=name, grid=(s // SCAN_LW, nt),
        in_specs=[x_spec, x_spec] + [tab_spec] * 4 + [x_spec] * (n_x - 2),
        out_specs=out_specs, out_shape=out_shape,
        scratch_shapes=[pltpu.VMEM((8, SCAN_LW), F32), pltpu.VMEM((8, SCAN_LW), F32)],
        compiler_params=pltpu.CompilerParams(dimension_semantics=("parallel", "arbitrary")),
    )(*operands)


CONV_TM = 256
HALO = 16


def _conv_specs(t, tm):
    nrow = t // tm
    hb = tm // HALO

    def col(cidx):
        return pl.BlockSpec((tm, BRANCH), lambda i: (i, cidx))

    def prev(cidx):
        return pl.BlockSpec((HALO, BRANCH), lambda i: (jnp.maximum(i * hb - 1, 0), cidx))

    def nxt(cidx):
        return pl.BlockSpec((HALO, BRANCH), lambda i: (jnp.minimum((i + 1) * hb, nrow * hb - 1), cidx))

    return nrow, col, prev, nxt


def _conv_taps(cc, cx, cc_prev, cx_prev, first):
    tm = cc.shape[0]
    v = cc * cx
    halo = cc_prev * cx_prev * jnp.where(first, 0.0, 1.0)
    ext = jnp.concatenate([halo, v], axis=0)
    return v, pltpu.roll(ext, 1, 0)[HALO:HALO + tm], pltpu.roll(ext, 2, 0)[HALO:HALO + tm]


def _conv_fwd(z, conv_w, name):
    t = z.shape[0]
    tm = _pick(t, CONV_TM, HALO)
    nrow, col, prev, _ = _conv_specs(t, tm)

    def body(cb_ref, cc_ref, cx_ref, ccp_ref, cxp_ref, w_ref, o_ref):
        first = pl.program_id(0) == 0
        v, v1, v2 = _conv_taps(*(r[...].astype(F32) for r in (cc_ref, cx_ref, ccp_ref, cxp_ref)), first)
        y = w_ref[0:1, :] * v2 + w_ref[1:2, :] * v1 + w_ref[2:3, :] * v
        o_ref[...] = (cb_ref[...].astype(F32) * y).astype(o_ref.dtype)

    return pl.pallas_call(
        body, name=name, grid=(nrow,),
        in_specs=[col(CB_BLK), col(CC_BLK), col(CX_BLK), prev(CC_BLK), prev(CX_BLK), pl.BlockSpec((3, BRANCH), lambda i: (0, 0))],
        out_specs=pl.BlockSpec((tm, BRANCH), lambda i: (i, 0)), out_shape=jax.ShapeDtypeStruct((t, BRANCH), BF16),
        compiler_params=pltpu.CompilerParams(dimension_semantics=("parallel",)),
    )(z, z, z, z, z, conv_w)


def _conv_bwd(dyc, z, conv_w, name):
    t = z.shape[0]
    tm = _pick(t, CONV_TM, HALO)
    nrow, col, prev, nxt = _conv_specs(t, tm)
    d_cur = pl.BlockSpec((tm, BRANCH), lambda i: (i, 0))
    d_nxt = pl.BlockSpec((HALO, BRANCH), lambda i: (jnp.minimum((i + 1) * (tm // HALO), nrow * (tm // HALO) - 1), 0))

    def body(dy_ref, dyn_ref, cb_ref, cbn_ref, cc_ref, cx_ref, ccp_ref, cxp_ref, w_ref, dcb_ref, dcc_ref, dcx_ref, dw_ref):
        i = pl.program_id(0)
        cc, cx, cb = cc_ref[...].astype(F32), cx_ref[...].astype(F32), cb_ref[...].astype(F32)
        v, v1, v2 = _conv_taps(cc, cx, ccp_ref[...].astype(F32), cxp_ref[...].astype(F32), i == 0)
        w0, w1, w2 = w_ref[0:1, :], w_ref[1:2, :], w_ref[2:3, :]
        y = w0 * v2 + w1 * v1 + w2 * v
        dyc_v = dy_ref[...]
        dcb_ref[...] = (dyc_v * y).astype(dcb_ref.dtype)
        dy = dyc_v * cb
        halo = dyn_ref[...] * cbn_ref[...].astype(F32) * jnp.where(i == nrow - 1, 0.0, 1.0)
        ext = jnp.concatenate([dy, halo], axis=0)
        dy1 = pltpu.roll(ext, tm + HALO - 1, 0)[0:tm]
        dy2 = pltpu.roll(ext, tm + HALO - 2, 0)[0:tm]
        dv = w2 * dy + w1 * dy1 + w0 * dy2
        dcc_ref[...] = (dv * cx).astype(dcc_ref.dtype)
        dcx_ref[...] = (dv * cc).astype(dcx_ref.dtype)
        dw = jnp.concatenate([jnp.sum(dy * v2, axis=0, keepdims=True), jnp.sum(dy * v1, axis=0, keepdims=True),
                              jnp.sum(dy * v, axis=0, keepdims=True), jnp.zeros((5, BRANCH), F32)], axis=0)

        @pl.when(i == 0)
        def _():
            dw_ref[...] = dw

        @pl.when(i > 0)
        def _():
            dw_ref[...] += dw

    o_spec = pl.BlockSpec((tm, BRANCH), lambda i: (i, 0))
    return pl.pallas_call(
        body, name=name, grid=(nrow,),
        in_specs=[d_cur, d_nxt, col(CB_BLK), nxt(CB_BLK), col(CC_BLK), col(CX_BLK), prev(CC_BLK), prev(CX_BLK),
                  pl.BlockSpec((3, BRANCH), lambda i: (0, 0))],
        out_specs=[o_spec, o_spec, o_spec, pl.BlockSpec((8, BRANCH), lambda i: (0, 0))],
        out_shape=[jax.ShapeDtypeStruct((t, BRANCH), BF16)] * 3 + [jax.ShapeDtypeStruct((8, BRANCH), F32)],
        compiler_params=pltpu.CompilerParams(dimension_semantics=("arbitrary",)),
    )(dyc, dyc, z, z, z, z, z, z, conv_w)


ATTN_BLOCKS = 4
ATTN_BLOCKS_BWD = 1
GROUP_ROWS = GQA * WINDOW


def _attn_specs(nblk):
    rows = nblk * WINDOW
    q_spec = pl.BlockSpec((N_Q, rows, HEAD_DIM), lambda n: (0, n, 0))
    kv_cur = pl.BlockSpec((N_KV, rows, HEAD_DIM), lambda n: (0, n, 0))
    kv_prev = pl.BlockSpec((N_KV, WINDOW, HEAD_DIM), lambda n: (0, jnp.maximum(n * nblk - 1, 0), 0))
    bias_spec = pl.BlockSpec((N_Q, WINDOW, 2 * WINDOW), lambda n: (0, 0, 0))
    sink_spec = pl.BlockSpec((N_Q, 1), lambda n: (0, 0))
    return q_spec, kv_cur, kv_prev, bias_spec, sink_spec


def _attn_valid(first_key):
    qi = lax.broadcasted_iota(jnp.int32, (GROUP_ROWS, 2 * WINDOW), 0) & (WINDOW - 1)
    kj = lax.broadcasted_iota(jnp.int32, (GROUP_ROWS, 2 * WINDOW), 1)
    dist = qi + WINDOW - kj
    return (dist >= 0) & (dist < WINDOW) & (kj >= first_key)


def _attn_masks(n):
    return _attn_valid(jnp.where(n > 0, 0, WINDOW)), _attn_valid(0)


def _blk(b):
    return slice(b * WINDOW, (b + 1) * WINDOW)


def _group(ref, h, b, width):
    return ref[GQA * h:GQA * (h + 1), _blk(b)].reshape(GROUP_ROWS, width)


def _keys(prev_ref, cur_ref, h, b):
    prev = prev_ref[h] if b == 0 else cur_ref[h, _blk(b - 1)]
    return jnp.concatenate([prev, cur_ref[h, _blk(b)]], axis=0)


def _group_sinks(s_ref, h):
    return jnp.concatenate([jnp.broadcast_to(s_ref[GQA * h + g:GQA * h + g + 1, :], (WINDOW, 1)) for g in range(GQA)], axis=0)


def _attn_probs(q, kc, bias, sink, valid):
    s = lax.dot_general(q, kc, (_DIMS["nt"], ((), ())), preferred_element_type=F32) * ATTN_SCALE + bias
    s = jnp.where(valid, s, NEG)
    m = jnp.maximum(jnp.max(s, axis=1, keepdims=True), sink)
    p = jnp.exp(s - m)
    e_sink = jnp.exp(sink - m)
    inv = 1.0 / (jnp.sum(p, axis=1, keepdims=True) + e_sink)
    return p * inv, e_sink * inv


def _attn_fwd(qh, kh, vh, bias, sinks, name):
    t = qh.shape[1]
    nblk = min(ATTN_BLOCKS, t // WINDOW)
    q_spec, kv_cur, kv_prev, bias_spec, sink_spec = _attn_specs(nblk)

    def body(q_ref, kp_ref, kc_ref, vp_ref, vc_ref, b_ref, s_ref, o_ref):
        masks = _attn_masks(pl.program_id(0))
        for b in range(nblk):
            for h in range(N_KV):
                kc, vc = _keys(kp_ref, kc_ref, h, b), _keys(vp_ref, vc_ref, h, b)
                w, _ = _attn_probs(_group(q_ref, h, b, HEAD_DIM), kc, _group(b_ref, h, 0, 2 * WINDOW), _group_sinks(s_ref, h),
                                   masks[min(b, 1)])
                o = jnp.dot(w.astype(BF16), vc, preferred_element_type=F32)
                o_ref[GQA * h:GQA * (h + 1), _blk(b)] = o.reshape(GQA, WINDOW, HEAD_DIM).astype(o_ref.dtype)

    return pl.pallas_call(
        body, name=name, grid=(t // (nblk * WINDOW),),
        in_specs=[q_spec, kv_prev, kv_cur, kv_prev, kv_cur, bias_spec, sink_spec],
        out_specs=q_spec, out_shape=jax.ShapeDtypeStruct((N_Q, t, HEAD_DIM), BF16),
        compiler_params=pltpu.CompilerParams(dimension_semantics=("parallel",)),
    )(qh, kh, kh, vh, vh, bias, sinks)


def _attn_bwd(qh, kh, vh, doh, bias, sinks, name):
    t = qh.shape[1]
    nblk = min(ATTN_BLOCKS_BWD, t // WINDOW)
    nsteps = t // (nblk * WINDOW)
    q_spec, kv_cur, kv_prev, bias_spec, sink_spec = _attn_specs(nblk)

    def body(q_ref, kp_ref, kc_ref, vp_ref, vc_ref, do_ref, b_ref, s_ref,
             dq_ref, dkc_ref, dkp_ref, dvc_ref, dvp_ref, db_ref, ds_ref):
        n = pl.program_id(0)
        masks = _attn_masks(n)

        @pl.when(n == 0)
        def _():
            db_ref[...] = jnp.zeros_like(db_ref)
            ds_ref[...] = jnp.zeros_like(ds_ref)

        def scatter(part, h, b, cur_ref, prev_ref):
            if b == 0:
                prev_ref[h] = part[0:WINDOW]
            else:
                cur_ref[h, _blk(b - 1)] += part[0:WINDOW]
            cur_ref[h, _blk(b)] = part[WINDOW:2 * WINDOW]

        d_bias, d_sink = [None] * N_KV, [None] * N_KV
        for b in range(nblk):
            for h in range(N_KV):
                kc, vc = _keys(kp_ref, kc_ref, h, b), _keys(vp_ref, vc_ref, h, b)
                heads = slice(GQA * h, GQA * (h + 1))
                q, do = _group(q_ref, h, b, HEAD_DIM), _group(do_ref, h, b, HEAD_DIM)
                w, w_sink = _attn_probs(q, kc, _group(b_ref, h, 0, 2 * WINDOW), _group_sinks(s_ref, h), masks[min(b, 1)])
                dw = lax.dot_general(do, vc, (_DIMS["nt"], ((), ())), preferred_element_type=F32)
                delta = jnp.sum(w * dw, axis=1, keepdims=True)
                dscore = w * (dw - delta)
                d_sink[h] = -w_sink * delta if b == 0 else d_sink[h] - w_sink * delta
                d_bias[h] = dscore if b == 0 else d_bias[h] + dscore
                dsb = dscore.astype(BF16)
                dq_ref[heads, _blk(b)] = (jnp.dot(dsb, kc, preferred_element_type=F32) * ATTN_SCALE).reshape(GQA, WINDOW, HEAD_DIM)
                scatter(lax.dot_general(dsb, q, (_DIMS["tn"], ((), ())), preferred_element_type=F32) * ATTN_SCALE, h, b, dkc_ref, dkp_ref)
                scatter(lax.dot_general(w.astype(BF16), do, (_DIMS["tn"], ((), ())), preferred_element_type=F32), h, b, dvc_ref, dvp_ref)
        for h in range(N_KV):
            heads = slice(GQA * h, GQA * (h + 1))
            ds_ref[heads] += d_sink[h].reshape(GQA, WINDOW, 1)
            db_ref[heads] += d_bias[h].reshape(GQA, WINDOW, 2 * WINDOW)

    kv_shape = jax.ShapeDtypeStruct((N_KV, t, HEAD_DIM), F32)
    kv_prev_out = pl.BlockSpec((N_KV, WINDOW, HEAD_DIM), lambda n: (0, n, 0))
    kv_prev_shape = jax.ShapeDtypeStruct((N_KV, nsteps * WINDOW, HEAD_DIM), F32)
    return pl.pallas_call(
        body, name=name, grid=(nsteps,),
        in_specs=[q_spec, kv_prev, kv_cur, kv_prev, kv_cur, q_spec, bias_spec, sink_spec],
        out_specs=[q_spec, kv_cur, kv_prev_out, kv_cur, kv_prev_out, bias_spec, pl.BlockSpec((N_Q, WINDOW, 1), lambda n: (0, 0, 0))],
        out_shape=[jax.ShapeDtypeStruct((N_Q, t, HEAD_DIM), F32), kv_shape, kv_prev_shape, kv_shape, kv_prev_shape,
                   jax.ShapeDtypeStruct((N_Q, WINDOW, 2 * WINDOW), F32), jax.ShapeDtypeStruct((N_Q, WINDOW, 1), F32)],
        compiler_params=pltpu.CompilerParams(dimension_semantics=("arbitrary",)),
    )(qh, kh, kh, vh, vh, doh, bias, sinks)


def _heads(a, n_heads):
    t = a.shape[0]
    return a.astype(BF16).reshape(t, n_heads, HEAD_DIM).transpose(1, 0, 2)


def _unheads(a):
    n_heads, t, _ = a.shape
    return a.transpose(1, 0, 2).reshape(t, n_heads * HEAD_DIM)


def _shift_blocks(cur, prev):
    n_kv, t, d = cur.shape
    nsteps = prev.shape[1] // WINDOW
    nblk = t // (nsteps * WINDOW)
    late = jnp.concatenate([prev.reshape(n_kv, nsteps, WINDOW, d)[:, 1:], jnp.zeros((n_kv, 1, WINDOW, d), cur.dtype)], axis=1)
    delta = jnp.concatenate([jnp.zeros((n_kv, nsteps, nblk - 1, WINDOW, d), cur.dtype), late[:, :, None]], axis=2)
    return (cur.reshape(n_kv, nsteps, nblk, WINDOW, d) + delta).reshape(n_kv, t, d)


def _t5_bucket_table():
    qi = np.arange(WINDOW)[:, None]
    kj = np.arange(2 * WINDOW)[None, :]
    dist = np.clip(qi + WINDOW - kj, 0, REL_MAX_DIST - 1)
    exact = REL_BUCKETS // 2
    df = np.maximum(dist, 1).astype(np.float32)
    large = exact + (np.log(df / np.float32(exact)) / np.float32(math.log(REL_MAX_DIST / exact)) * (REL_BUCKETS - exact)).astype(np.int32)
    large = np.minimum(large, REL_BUCKETS - 1)
    bucket = np.where(dist < exact, dist, large)
    onehot = np.zeros((WINDOW * 2 * WINDOW, REL_BUCKETS), np.float32)
    onehot[np.arange(WINDOW * 2 * WINDOW), bucket.reshape(-1)] = 1.0
    return onehot


def _band_bias(rel_bias):
    onehot = jnp.asarray(_t5_bucket_table())
    sel = jnp.sum(onehot[:, :, None] * rel_bias[None, :, :], axis=1)
    return sel.T.reshape(N_Q, WINDOW, 2 * WINDOW)


def _block_diag(a):
    g, r, c = a.shape
    a4 = a.reshape(SSM_BLOCKS, g // SSM_BLOCKS, r, c)
    eye = jnp.eye(g // SSM_BLOCKS, dtype=a.dtype)
    full = a4[:, :, :, None, :] * eye[None, :, None, :, None]
    return full.reshape(SSM_BLOCKS, (g // SSM_BLOCKS) * r, (g // SSM_BLOCKS) * c)


def _ssm_disc(lam_re, lam_im, b_re, b_im, c_re, c_im, log_dt):
    dt = jnp.exp(log_dt)[:, None]
    mag = jnp.exp(lam_re * dt)
    ang = lam_im * dt
    a_re = mag * jnp.cos(ang)
    a_im = mag * jnp.sin(ang)
    den = lam_re * lam_re + lam_im * lam_im
    nr = a_re - 1.0
    coef_re = (nr * lam_re + a_im * lam_im) / den
    coef_im = (a_im * lam_re - nr * lam_im) / den
    bb_re = coef_re[..., None] * b_re - coef_im[..., None] * b_im
    bb_im = coef_re[..., None] * b_im + coef_im[..., None] * b_re
    wb_re = _block_diag(jnp.swapaxes(bb_re, 1, 2))
    wb_im = _block_diag(jnp.swapaxes(bb_im, 1, 2))
    cm_re = _block_diag(jnp.swapaxes(c_re, 1, 2))
    cm_imn = _block_diag(-jnp.swapaxes(c_im, 1, 2))
    return a_re.reshape(-1), a_im.reshape(-1), wb_re, wb_im, cm_re, cm_imn


def _scan_tables(a_re, a_im):
    pr, pi = [a_re], [a_im]
    for _ in range(7):
        pr, pi = pr + [pr[-1] * a_re - pi[-1] * a_im], pi + [pr[-1] * a_im + pi[-1] * a_re]
    pr, pi = jnp.stack(pr), jnp.stack(pi)
    pad = jnp.zeros((5,) + a_re.shape, F32)
    dr = jnp.concatenate([jnp.stack([pr[0], pr[1], pr[3]]), pad])
    di = jnp.concatenate([jnp.stack([pi[0], pi[1], pi[3]]), pad])
    fwd = (pr, pi, dr, di)
    rev = (pr[::-1], -pi[::-1], dr, -di)
    return jax.tree.map(lax.stop_gradient, (fwd, rev))


def _gate_col(r):
    return lambda j: OFF_G // D_MODEL + r


def _layer_fwd(x, h, p_i, w, bias, li, next_gain):
    nm = lambda s: f"{s}_l{li}"
    z = _mm(h, w["w_in"], "nn", tm=1024, tn=2944, n_outer=True, out_dtype=BF16, name=nm("mm_in"))
    bu_re, bu_im = _bd_apply([z], [w["wb_re"], w["wb_im"]], [[(0, 0)], [(0, 1)]], "nn", col_blocks=[U_BLK], name=nm("ssm_bu"))
    h_re, h_im = _scan(bu_re, bu_im, *w["scan_fwd"], reverse=False, name=nm("ssm_scan"))
    (y0,) = _bd_apply([h_re, h_im], [w["cm_re"], w["cm_imn"]], [[(0, 0), (1, 1)]], "nn", name=nm("ssm_c"))
    (y1,) = _rw(lambda a, u, d: ([jax.nn.gelu(a + d * u)], []),
                [(y0, None, _c0), (z, BRANCH, lambda j: U_BLK)], [(BRANCH, None, F32, _c0)],
                params=[w["ssm_d"]], name=nm("ssm_gelu"))
    gl = _mm(y1, w["ssm_w_glu"], "nn", name=nm("mm_glu"))
    (y_ssm,) = _rw(lambda a, b: ([a * jax.nn.sigmoid(b)], []), [(y1, None, _c0), (gl, None, _c0)],
                   [(BRANCH, None, BF16, _c0)], name=nm("ssm_glu"))
    y_conv = _conv_fwd(z, w["conv_w"], nm("conv_fwd"))
    kv_w = N_KV * HEAD_DIM
    q2, k2, v2 = _rw(lambda q, k, v: ([q, k, v], []),
                     [(z, BRANCH, lambda j: OFF_Q // BRANCH), (z, kv_w, lambda j: OFF_K // kv_w), (z, kv_w, lambda j: OFF_V // kv_w)],
                     [(BRANCH, None, BF16, _c0), (kv_w, None, BF16, _c0), (kv_w, None, BF16, _c0)], name=nm("qkv_bf16"))
    qh, kh, vh = _heads(q2, N_Q), _heads(k2, N_KV), _heads(v2, N_KV)
    y_attn = _unheads(_attn_fwd(qh, kh, vh, bias, w["sinks"], nm("attn_fwd")))
    ys = (y_ssm, y_conv, y_attn)
    bs = [_mm(ys[r], w["w_branch"][r], "nn", name=nm(f"mm_branch{r}")) for r in range(3)]

    def merge(g0, g1, g2, b0, b1, b2):
        return [jax.nn.sigmoid(g0) * b0 + jax.nn.sigmoid(g1) * b1 + jax.nn.sigmoid(g2) * b2], []

    (merged,) = _rw(merge, [(z, D_MODEL, _gate_col(r)) for r in range(3)] + [(b, None, _c0) for b in bs],
                    [(D_MODEL, None, BF16, _c0)], name=nm("merge"))
    x1, hf_in = _mm(merged, w["w_out"], "nn", add=x, rms_out=w["norm_ffn"], name=nm("mm_out"))
    hf = _mm(hf_in, w["w_ffn_in"], "nn", tn=1408, n_outer=True, out_dtype=BF16, name=nm("mm_ffn_in"))
    (act,) = _rw(lambda a, b: ([jax.nn.silu(a) * b], []), [(hf, FFN_COLS, lambda j: j), (hf, FFN_COLS, lambda j: FFN_NCOL + j)],
                 [(FFN_HIDDEN, FFN_COLS, BF16, lambda j: j)], ncol=FFN_NCOL, name=nm("swiglu"))
    x2, hp = _mm(act, w["w_ffn_out"], "nn", add=x1, tk=1408, rms_out=w["norm_ple"], name=nm("mm_ffn_out"))
    pgl = _mm(hp, w["w_ple_gate"], "nn", name=nm("mm_ple_gate"))
    pp = _mm(p_i, w["w_ple_proj"], "nn", name=nm("mm_ple_proj"))

    def ple_add(xv, a, b, *gain):
        x3v = xv + jax.nn.sigmoid(a) * b
        return [x3v] + [_rms(x3v, g) for g in gain], []

    has_next = next_gain is not None
    res = _rw(ple_add, [(x2, None, _c0), (pgl, None, _c0), (pp, None, _c0)],
              [(D_MODEL, None, F32, _c0)] + [(D_MODEL, None, BF16, _c0)] * has_next, params=[next_gain] * has_next, name=nm("ple_add"))
    x3, h_next = res[0], (res[1] if has_next else None)
    saved = dict(x=x, p=p_i, h=h, z=z, h_re=h_re, h_im=h_im, y0=y0, y1=y1, gl=gl, ys=ys, qh=qh, kh=kh, vh=vh,
                 bs=bs, merged=merged, x1=x1, hf_in=hf_in, hf=hf, act=act, x2=x2, hp=hp, pgl=pgl, pp=pp)
    return x3, h_next, saved


def _layer_bwd(dx3, s, w, bias, li):
    nm = lambda n: f"{n}_l{li}"
    g = {}
    z = s["z"]
    def ple_b(d, a, b):
        _, vjp = jax.vjp(lambda a_, b_: jax.nn.sigmoid(a_) * b_, a, b)
        return list(vjp(d)), []

    dpgl, dpp = _rw(ple_b, [(dx3, None, _c0), (s["pgl"], None, _c0), (s["pp"], None, _c0)],
                    [(D_MODEL, None, BF16, _c0)] * 2, name=nm("ple_bwd"))
    g["w_ple_proj"] = _mm(s["p"], dpp, "tn", name=nm("mmg_ple_proj"))
    g["w_ple_gate"] = _mm(s["hp"], dpgl, "tn", name=nm("mmg_ple_gate"))
    dx2, g["norm_ple"] = _mm(dpgl, w["w_ple_gate"], "nt", tm=512, rms_back=(s["x2"], dx3, w["norm_ple"]), name=nm("mmb_ple_gate"))
    dact = _mm(dx2, w["w_ffn_out"], "nt", tn=1408, name=nm("mmb_ffn_out"))
    g["w_ffn_out"] = _mm(s["act"], dx2, "tn", tm=1408, name=nm("mmg_ffn_out"))

    def swiglu_b(a, b, d):
        _, vjp = jax.vjp(lambda a_, b_: jax.nn.silu(a_) * b_, a, b)
        return list(vjp(d)), []

    dhf_a, dhf_b = _rw(swiglu_b, [(s["hf"], FFN_COLS, lambda j: j), (s["hf"], FFN_COLS, lambda j: FFN_NCOL + j), (dact, FFN_COLS, lambda j: j)],
                       [(FFN_HIDDEN, FFN_COLS, BF16, lambda j: j)] * 2, ncol=FFN_NCOL, name=nm("swiglu_bwd"))
    g["w_ffn_in"] = jnp.concatenate([_mm(s["hf_in"], dhf_a, "tn", tn=1408, name=nm("mmg_ffn_in_a")),
                                     _mm(s["hf_in"], dhf_b, "tn", tn=1408, name=nm("mmg_ffn_in_b"))], axis=1)
    dhf_in = _mm(dhf_a, w["w_ffn_in"], "nt", tk=1408, name=nm("mmb_ffn_in_a"))
    dx1, g["norm_ffn"] = _mm(dhf_b, w["w_ffn_in"], "nt", tm=512, tk=1408, b_k0=FFN_HIDDEN, add=dhf_in,
                             rms_back=(s["x1"], dx2, w["norm_ffn"]), name=nm("mmb_ffn_in_b"))
    dmerged = _mm(dx1, w["w_out"], "nt", name=nm("mmb_out"))
    g["w_out"] = _mm(s["merged"], dx1, "tn", name=nm("mmg_out"))

    def merge_b(d, g0, g1, g2, b0, b1, b2):
        outs_g, outs_b = [], []
        for gate, br in ((g0, b0), (g1, b1), (g2, b2)):
            sg = jax.nn.sigmoid(gate)
            outs_g.append(d * br * sg * (1.0 - sg))
            outs_b.append(d * sg)
        return outs_g + outs_b, []

    res = _rw(merge_b, [(dmerged, None, _c0)] + [(z, D_MODEL, _gate_col(r)) for r in range(3)] + [(b, None, _c0) for b in s["bs"]],
              [(D_MODEL, None, BF16, _c0)] * 6, name=nm("merge_bwd"))
    dgates, dbs = res[:3], res[3:]
    dys = [_mm(dbs[r], w["w_branch"][r], "nt", name=nm(f"mmb_branch{r}")) for r in range(3)]
    g["w_branch"] = jnp.stack([_mm(s["ys"][r], dbs[r], "tn", name=nm(f"mmg_branch{r}")) for r in range(3)])
    doh = _heads(dys[2], N_Q)
    dqh, dkc, dkp, dvc, dvp, dbias, dsink = _attn_bwd(s["qh"], s["kh"], s["vh"], doh, bias, w["sinks"], nm("attn_bwd"))
    dq, dk, dv = _unheads(dqh), _unheads(_shift_blocks(dkc, dkp)), _unheads(_shift_blocks(dvc, dvp))
    g["sinks"] = jnp.sum(dsink, axis=(1, 2))
    dcb, dcc, dcx, dconv = _conv_bwd(dys[1], z, w["conv_w"], nm("conv_bwd"))
    g["conv_w"] = dconv[0:3]
    def glu_b(d, y1, gl):
        sg = jax.nn.sigmoid(gl)
        return [d * y1 * sg * (1.0 - sg), d * sg], []

    dgl, dy1a = _rw(glu_b, [(dys[0], None, _c0), (s["y1"], None, _c0), (s["gl"], None, _c0)],
                    [(BRANCH, None, BF16, _c0), (BRANCH, None, F32, _c0)], name=nm("ssm_glu_bwd"))
    g["ssm_w_glu"] = _mm(s["y1"], dgl, "tn", name=nm("mmg_glu"))
    dy1b = _mm(dgl, w["ssm_w_glu"], "nt", name=nm("mmb_glu"))

    def gelu_b(da, db, a, u, d):
        _, vjp = jax.vjp(lambda pre: jax.nn.gelu(pre), a + d * u)
        (dy0,) = vjp(da + db)
        return [dy0, dy0 * d], [jnp.sum(dy0 * u, axis=0, keepdims=True)]

    dy0, du_a, g["ssm_d"] = _rw(gelu_b, [(dy1a, None, _c0), (dy1b, None, _c0), (s["y0"], None, _c0), (z, BRANCH, lambda j: U_BLK)],
                                [(BRANCH, None, BF16, _c0), (BRANCH, None, F32, _c0)], params=[w["ssm_d"]],
                                reds=[((1, BRANCH), None, _c0)], name=nm("ssm_gelu_bwd"))
    dh_re, dh_im = _bd_apply([dy0], [w["cm_re"], w["cm_imn"]], [[(0, 0)], [(0, 1)]], "nt", name=nm("ssmb_c"))
    sb, cb = SSM_STATES // SSM_BLOCKS, BRANCH // SSM_BLOCKS
    g["cm_re"], g["cm_imn"] = _bd_grads([s["h_re"], s["h_im"], dy0], [sb, sb, cb], [(0, 2), (1, 2)], name=nm("ssmg_c"))
    l_re, l_im, da_re, da_im = _scan(dh_re, dh_im, *w["scan_rev"], reverse=True, hr=s["h_re"], hi=s["h_im"], name=nm("ssm_scan_bwd"))
    g["a_re"], g["a_im"] = jnp.sum(da_re, axis=0), jnp.sum(da_im, axis=0)
    (du_b,) = _bd_apply([l_re, l_im], [w["wb_re"], w["wb_im"]], [[(0, 0), (1, 1)]], "nt", name=nm("ssmb_bu"))
    g["wb_re"], g["wb_im"] = _bd_grads([z, l_re, l_im], [cb, sb, sb], [(0, 1), (0, 2)], col_blocks=[U_BLK, 0, 0], name=nm("ssmg_bu"))
    dz = jnp.concatenate(list(dgates) + [(du_a + du_b).astype(BF16), dcb, dcc, dcx, dq.astype(BF16), dk.astype(BF16), dv.astype(BF16)], axis=1)
    g["w_in"] = _mm(s["h"], dz, "tn", tm=512, tn=2944, name=nm("mmg_in"))
    dx, g["norm_mix"] = _mm(dz, w["w_in"], "nt", tm=512, tk=2944, rms_back=(s["x"], dx1, w["norm_mix"]), name=nm("mmb_in"))
    return dx, g, dbias


def _loss_and_seed(x, target, g_final):
    def fn(xv, tv, gv):
        y, vjp = jax.vjp(_rms, xv, gv)
        err = y - tv
        dx, dg = vjp(err * (1.0 / D_MODEL))
        return [dx], [jnp.sum(err * err, axis=0, keepdims=True) * (0.5 / D_MODEL), dg]

    return _rw(fn, [(x, None, _c0), (target, None, _c0)], [(D_MODEL, None, F32, _c0)], params=[g_final],
               reds=[((1, D_MODEL), None, _c0)] * 2, name="loss_head")


def _local_step(x, p, target, wt):
    bias, bias_vjp = jax.vjp(_band_bias, wt["rel_bias"])
    layers, disc_vjps = [], []
    for i in range(DEPTH):
        ssm_p = [wt[k][i] for k in ("ssm_lambda_re", "ssm_lambda_im", "ssm_b_re", "ssm_b_im", "ssm_c_re", "ssm_c_im", "ssm_log_dt")]
        (a_re, a_im, wb_re, wb_im, cm_re, cm_imn), disc_vjp = jax.vjp(_ssm_disc, *ssm_p)
        scan_fwd, scan_rev = _scan_tables(a_re, a_im)
        layers.append(dict(
            norm_mix=wt["norm_mix"][i][None], w_in=wt["w_in"][i], wb_re=wb_re.astype(BF16), wb_im=wb_im.astype(BF16),
            cm_re=cm_re.astype(BF16), cm_imn=cm_imn.astype(BF16), scan_fwd=scan_fwd, scan_rev=scan_rev,
            ssm_d=wt["ssm_d"][i][None], ssm_w_glu=wt["ssm_w_glu"][i], conv_w=wt["conv_w"][i],
            sinks=wt["attn_sinks"][i][:, None], w_branch=wt["w_branch"][i], w_out=wt["w_out"][i],
            norm_ffn=wt["norm_ffn"][i][None], w_ffn_in=wt["w_ffn_in"][i], w_ffn_out=wt["w_ffn_out"][i],
            norm_ple=wt["norm_ple"][i][None], w_ple_gate=wt["w_ple_gate"][i], w_ple_proj=wt["w_ple_proj"][i]))
        disc_vjps.append(disc_vjp)

    saved = []
    for i in range(DEPTH):
        h = _rms_fwd(x, layers[0]["norm_mix"], "rms_mix_l0") if i == 0 else h
        x, h, s = _layer_fwd(x, h, p[i], layers[i], bias, i, layers[i + 1]["norm_mix"] if i + 1 < DEPTH else None)
        saved.append(s)
    dx, loss_cols, g_final = _loss_and_seed(x, target, wt["norm_final"][None])
    loss = jnp.sum(loss_cols)

    per_layer = [None] * DEPTH
    dbias = None
    for i in reversed(range(DEPTH)):
        dx, g, db = _layer_bwd(dx, saved[i], layers[i], bias, i)
        dbias = db if dbias is None else dbias + db
        (g["ssm_lambda_re"], g["ssm_lambda_im"], g["ssm_b_re"], g["ssm_b_im"], g["ssm_c_re"], g["ssm_c_im"], g["ssm_log_dt"]) = \
            disc_vjps[i]((g.pop("a_re"), g.pop("a_im"), g.pop("wb_re"), g.pop("wb_im"), g.pop("cm_re"), g.pop("cm_imn")))
        g["attn_sinks"] = g.pop("sinks")
        for k in ("norm_mix", "norm_ffn", "norm_ple", "ssm_d"):
            g[k] = g[k][0]
        per_layer[i] = g
    big_names = [name for name, _, _ in BIG]
    big = {k: [per_layer[i][k] for i in range(DEPTH)] for k in big_names}
    small = {k: jnp.stack([per_layer[i][k] for i in range(DEPTH)]) for k in per_layer[0] if k not in big_names}
    (small["rel_bias"],) = bias_vjp(dbias)
    small["norm_final"] = g_final[0]
    return loss, dx, small, big


HBM_SPEC = pl.BlockSpec(memory_space=pltpu.HBM)


def _position():
    x, y, c = lax.axis_index("x"), lax.axis_index("y"), lax.axis_index("c")
    other_chips = [(1 - x, y), (x, 1 - y), (1 - x, 1 - y)]
    return x, y, c, other_chips


def _row_chunks(rows, n=COPY_CHUNKS):
    rq = rows // n
    assert rq * n == rows and rq % 16 == 0, rows
    return [pl.ds(q * rq, rq) for q in range(n)]


def _place(buf, val, idx, name):
    n, rows, width = buf.shape
    tm = _pick(rows, 512, 16)

    def body(idx_ref, buf_ref, v_ref, o_ref):
        o_ref[0] = v_ref[...]

    grid_spec = pltpu.PrefetchScalarGridSpec(
        num_scalar_prefetch=1, grid=(rows // tm,),
        in_specs=[pl.BlockSpec(memory_space=pl.ANY), pl.BlockSpec((tm, width), lambda i, idx_ref: (i, 0))],
        out_specs=pl.BlockSpec((1, tm, width), lambda i, idx_ref: (idx_ref[0], i, 0)))
    return pl.pallas_call(
        body, name=name, grid_spec=grid_spec, out_shape=jax.ShapeDtypeStruct(buf.shape, buf.dtype), input_output_aliases={1: 0},
        compiler_params=pltpu.CompilerParams(dimension_semantics=("arbitrary",)),
    )(jnp.asarray(idx, jnp.int32).reshape(1), buf, val)


def _allgather_weights(locals_):
    nb, nq = len(locals_), COPY_CHUNKS
    chunks = [_row_chunks(a.shape[1]) for a in locals_]

    def body(*refs):
        w_refs, out_refs = refs[:nb], refs[nb:2 * nb]
        send_sems, recv_sems = refs[2 * nb:]
        x, y, c, chips = _position()
        me = 2 * x + y
        sibling = (x, y, 1 - c)

        def copy(b, kind, q, src, dst, to):
            k = (b * 6 + kind) * nq + q
            return pltpu.make_async_remote_copy(src_ref=src, dst_ref=dst, send_sem=send_sems.at[k], recv_sem=recv_sems.at[k],
                                                device_id=to, device_id_type=MESH)

        first = [copy(b, j, q, w_refs[b].at[c, chunks[b][q]], out_refs[b].at[me, c, chunks[b][q]], (*chip, c))
                 for q in range(nq) for b in range(nb) for j, chip in enumerate(chips)]
        for cp in first:
            cp.start()
        passed = []
        for q in range(nq):
            for b in range(nb):
                for j, (px, py) in enumerate(chips):
                    landed = out_refs[b].at[2 * px + py, c, chunks[b][q]]
                    copy(b, j, q, landed, landed, (px, py, c)).wait_recv()
                    fwd = copy(b, 3 + j, q, landed, landed, sibling)
                    fwd.start()
                    passed.append(fwd)
        for q in range(nq):
            for b in range(nb):
                for j, (px, py) in enumerate(chips):
                    landed = out_refs[b].at[2 * px + py, 1 - c, chunks[b][q]]
                    copy(b, 3 + j, q, landed, landed, sibling).wait_recv()
        for cp in first + passed:
            cp.wait_send()

    return pl.pallas_call(
        body, name="allgather_weights", in_specs=[HBM_SPEC] * nb, out_specs=[HBM_SPEC] * nb,
        out_shape=[jax.ShapeDtypeStruct((N_SHARD,) + a.shape, a.dtype) for a in locals_],
        scratch_shapes=[pltpu.SemaphoreType.DMA((nb * 6 * nq,)), pltpu.SemaphoreType.DMA((nb * 6 * nq,))],
    )(*locals_)


def _sibling_exchange(bufs):
    nb, nq, ns = len(bufs), COPY_CHUNKS, N_SHARD
    chunks = [_row_chunks(a.shape[2]) for a in bufs]

    def body(*refs):
        g_refs, got_refs = refs[:nb], refs[nb:2 * nb]
        send_sems, recv_sems = refs[2 * nb:]
        x, y, c, _ = _position()
        swaps = [pltpu.make_async_remote_copy(src_ref=g_refs[b].at[s, 1 - c, chunks[b][q]], dst_ref=got_refs[b].at[s, chunks[b][q]],
                                              send_sem=send_sems.at[(b * ns + s) * nq + q], recv_sem=recv_sems.at[(b * ns + s) * nq + q],
                                              device_id=(x, y, 1 - c), device_id_type=MESH)
                 for b in range(nb) for s in range(ns) for q in range(nq)]
        for cp in swaps:
            cp.start()
        for cp in swaps:
            cp.wait()

    return pl.pallas_call(
        body, name="grad_sibling_exchange", in_specs=[HBM_SPEC] * nb, out_specs=[HBM_SPEC] * nb,
        out_shape=[jax.ShapeDtypeStruct((ns,) + a.shape[2:], a.dtype) for a in bufs],
        scratch_shapes=[pltpu.SemaphoreType.DMA((nb * ns * nq,)), pltpu.SemaphoreType.DMA((nb * ns * nq,))],
    )(*bufs)


def _chip_exchange(parts):
    nb, nq = len(parts), COPY_CHUNKS
    chunks = [_row_chunks(a.shape[1]) for a in parts]

    def body(*refs):
        b_refs, got_refs = refs[:nb], refs[nb:2 * nb]
        send_sems, recv_sems = refs[2 * nb:]
        x, y, c, chips = _position()
        sends = [pltpu.make_async_remote_copy(src_ref=b_refs[b].at[2 * px + py, chunks[b][q]], dst_ref=got_refs[b].at[j, chunks[b][q]],
                                              send_sem=send_sems.at[(b * 3 + j) * nq + q], recv_sem=recv_sems.at[(b * 3 + j) * nq + q],
                                              device_id=(px, py, c), device_id_type=MESH)
                 for q in range(nq) for b in range(nb) for j, (px, py) in enumerate(chips)]
        for cp in sends:
            cp.start()
        for cp in sends:
            cp.wait()

    return pl.pallas_call(
        body, name="grad_chip_exchange", in_specs=[HBM_SPEC] * nb, out_specs=[HBM_SPEC] * nb,
        out_shape=[jax.ShapeDtypeStruct((N_SHARD - 1,) + a.shape[1:], a.dtype) for a in parts],
        scratch_shapes=[pltpu.SemaphoreType.DMA((nb * 3 * nq,)), pltpu.SemaphoreType.DMA((nb * 3 * nq,))],
    )(*parts)


def _sibling_gather(halves):
    nb, nq = len(halves), 2 * COPY_CHUNKS
    chunks = [_row_chunks(a.shape[0], nq) for a in halves]

    def body(*refs):
        h_refs, out_refs = refs[:nb], refs[nb:2 * nb]
        send_sems, recv_sems = refs[2 * nb:]
        x, y, c, _ = _position()

        def chunk(b, q, half_idx):
            return pltpu.make_async_remote_copy(src_ref=h_refs[b].at[chunks[b][q]], dst_ref=out_refs[b].at[half_idx, chunks[b][q]],
                                                send_sem=send_sems.at[b * nq + q], recv_sem=recv_sems.at[b * nq + q],
                                                device_id=(x, y, 1 - c), device_id_type=MESH)

        pushes = [chunk(b, q, c) for b in range(nb) for q in range(nq)]
        for cp in pushes:
            cp.start()
        for b in range(nb):
            for q in range(nq):
                chunk(b, q, 1 - c).wait_recv()
        for cp in pushes:
            cp.wait_send()

    return pl.pallas_call(
        body, name="grad_sibling_gather", in_specs=[HBM_SPEC] * nb, out_specs=[HBM_SPEC] * nb,
        out_shape=[jax.ShapeDtypeStruct((2,) + a.shape, a.dtype) for a in halves],
        scratch_shapes=[pltpu.SemaphoreType.DMA((nb * nq,)), pltpu.SemaphoreType.DMA((nb * nq,))],
    )(*halves)


def _gather_partials(part):
    r, lanes = part.shape

    def body(p_ref, out_ref, send_sems, recv_sems):
        x, y, c, _ = _position()
        flips = [(fx, fy, fc) for fx in (0, 1) for fy in (0, 1) for fc in (0, 1)][1:]
        sends = []
        for k, (fx, fy, fc) in enumerate(flips):
            cp = pltpu.make_async_remote_copy(src_ref=p_ref, dst_ref=out_ref.at[4 * x + 2 * y + c], send_sem=send_sems.at[k],
                                              recv_sem=recv_sems.at[k], device_id=(x ^ fx, y ^ fy, c ^ fc), device_id_type=MESH)
            cp.start()
            sends.append(cp)
        for k, (fx, fy, fc) in enumerate(flips):
            src = out_ref.at[4 * (x ^ fx) + 2 * (y ^ fy) + (c ^ fc)]
            pltpu.make_async_remote_copy(src_ref=src, dst_ref=src, send_sem=send_sems.at[k], recv_sem=recv_sems.at[k],
                                         device_id=(x ^ fx, y ^ fy, c ^ fc), device_id_type=MESH).wait_recv()
        for cp in sends:
            cp.wait_send()

    return pl.pallas_call(
        body, name="small_gather_partials", in_specs=[HBM_SPEC], out_specs=HBM_SPEC,
        out_shape=jax.ShapeDtypeStruct((8, r, lanes), part.dtype),
        scratch_shapes=[pltpu.SemaphoreType.DMA((7,)), pltpu.SemaphoreType.DMA((7,))],
    )(part)


def _sum_leading(stack, name, also_bf16=False):
    k, r, lanes = stack.shape
    tm = _pick(r, 256, 16)
    outs = [jax.ShapeDtypeStruct((r, lanes), F32)] + ([jax.ShapeDtypeStruct((r, lanes), BF16)] if also_bf16 else [])

    def body(s_ref, *o_refs):
        acc = s_ref[0].astype(F32)
        for i in range(1, k):
            acc = acc + s_ref[i].astype(F32)
        for o in o_refs:
            o[...] = acc.astype(o.dtype)

    spec = pl.BlockSpec((tm, lanes), lambda i: (i, 0))
    return pl.pallas_call(
        body, name=name, grid=(r // tm,), in_specs=[pl.BlockSpec((k, tm, lanes), lambda i: (0, i, 0))],
        out_specs=[spec] * len(outs), out_shape=outs,
        compiler_params=pltpu.CompilerParams(dimension_semantics=("parallel",)),
    )(stack)


def _add_pair(g2, got, half, name):
    ns, _, rows, width = g2.shape
    tm = _pick(rows, 256, 16)
    spec = pl.BlockSpec((1, tm, width), lambda s, i, h_ref: (s, i, 0))

    def body(h_ref, a_ref, b_ref, f_ref, o_ref):
        acc = a_ref[0] + b_ref[...]
        f_ref[...] = acc
        o_ref[...] = acc.astype(BF16)

    grid_spec = pltpu.PrefetchScalarGridSpec(
        num_scalar_prefetch=1, grid=(ns, rows // tm),
        in_specs=[pl.BlockSpec((1, 1, tm, width), lambda s, i, h_ref: (s, h_ref[0], i, 0)), spec], out_specs=[spec, spec])
    return pl.pallas_call(
        body, name=name, grid_spec=grid_spec,
        out_shape=[jax.ShapeDtypeStruct(got.shape, F32), jax.ShapeDtypeStruct(got.shape, BF16)],
        compiler_params=pltpu.CompilerParams(dimension_semantics=("parallel", "parallel")),
    )(jnp.asarray(half, jnp.int32).reshape(1), g2, got)


def _add_own(parts, got, mine, name):
    _, rows, width = parts.shape
    tm = _pick(rows, 256, 16)

    def body(m_ref, p_ref, g_ref, out_ref):
        acc = p_ref[0]
        for j in range(g_ref.shape[0]):
            acc = acc + g_ref[j].astype(F32)
        out_ref[...] = acc

    grid_spec = pltpu.PrefetchScalarGridSpec(
        num_scalar_prefetch=1, grid=(rows // tm,),
        in_specs=[pl.BlockSpec((1, tm, width), lambda i, m_ref: (m_ref[0], i, 0)),
                  pl.BlockSpec((got.shape[0], tm, width), lambda i, m_ref: (0, i, 0))],
        out_specs=pl.BlockSpec((tm, width), lambda i, m_ref: (i, 0)))
    return pl.pallas_call(
        body, name=name, grid_spec=grid_spec, out_shape=jax.ShapeDtypeStruct((rows, width), F32),
        compiler_params=pltpu.CompilerParams(dimension_semantics=("parallel",)),
    )(jnp.asarray(mine, jnp.int32).reshape(1), parts, got)


def _local_shape(shape, axis):
    return tuple(d // N_SHARD if a == axis else d for a, d in enumerate(shape))


def _big_sizes():
    return [DEPTH * int(np.prod(_local_shape(shape, axis))) for _, shape, axis in FLAT_BIG]


COL_SHARDED = (("w_in", IN_WIDTH // N_SHARD), ("w_ffn_in", 2 * FFN_HIDDEN // N_SHARD))
FLAT_BIG = tuple(entry for entry in BIG if entry[0] not in [name for name, _ in COL_SHARDED])
FLAT_ROW_TILE = 256
ELEMENTWISE_BIG = ("conv_w",)


def _three_bf16(w):
    hi = w.astype(BF16)
    r1 = w - hi.astype(F32)
    mid = r1.astype(BF16)
    lo = (r1 - mid.astype(F32)).astype(BF16)
    return jnp.stack([hi, mid, lo], axis=-1)


PIECE_ROWS = 16


def _flat_layout(for_weights):
    pieces = []
    for (name, shape, axis), size in zip(FLAT_BIG, _big_sizes(), strict=True):
        n = size * (3 if for_weights and name in ELEMENTWISE_BIG else 1)
        rows = -(-n // (LANES * PIECE_ROWS)) * PIECE_ROWS
        pieces.append((name, shape, axis, n, rows))
    total = sum(p[-1] for p in pieces)
    half = -(-total // (2 * FLAT_ROW_TILE)) * FLAT_ROW_TILE
    return pieces, half


def _to_rows(flat, rows):
    lead, n = flat.shape[:-1], flat.shape[-1]
    fill = jnp.zeros(lead + (rows * LANES - n,), flat.dtype)
    return jnp.concatenate([flat, fill], axis=-1).reshape(lead + (rows, LANES))


def _pack_local_weights(wl):
    pieces, half = _flat_layout(True)
    parts = [_to_rows((_three_bf16(wl[name]) if name in ELEMENTWISE_BIG else wl[name].astype(BF16)).reshape(-1), rows)
             for name, _, _, _, rows in pieces]
    parts.append(jnp.zeros((2 * half - sum(p[-1] for p in pieces), LANES), BF16))
    return jnp.concatenate(parts, axis=0).reshape(2, half, LANES)


def _unpack_local(flat):
    pieces, _ = _flat_layout(False)
    flat = flat.reshape(-1, LANES)
    out, off = {}, 0
    for name, shape, axis, n, rows in pieces:
        out[name] = flat[off:off + rows].reshape(-1)[:n].reshape((DEPTH,) + _local_shape(shape, axis))
        off += rows
    return out


def _unpack_gathered(gathered):
    pieces, _ = _flat_layout(True)
    out, off = {}, 0
    for name, shape, axis, n, rows in pieces:
        local = (N_SHARD, DEPTH) + _local_shape(shape, axis)
        seg = gathered[:, off:off + rows].reshape(N_SHARD, -1)[:, :n]
        off += rows
        if name in ELEMENTWISE_BIG:
            parts = seg.reshape(local + (3,)).astype(F32)
            seg = (parts[..., 0] + parts[..., 1]) + parts[..., 2]
        else:
            seg = seg.reshape(local)
        out[name] = jnp.moveaxis(seg, 0, 1 + axis).reshape((DEPTH,) + shape)
    return out


def _col_segments(name, c):
    if name != "w_in":
        return [(s, 0, c, c * s) for s in range(N_SHARD)]
    segs = []
    for s in range(N_SHARD):
        lo, hi = c * s, c * (s + 1)
        if lo < MAIN_WIDTH:
            segs.append((s, 0, min(hi, MAIN_WIDTH) - lo, GATES_WIDTH + lo))
        if hi > MAIN_WIDTH:
            first = max(lo, MAIN_WIDTH)
            segs.append((s, first - lo, c, first - MAIN_WIDTH))
    return segs


def _join_col_shards(shards, segs, name):
    ns, depth, rows, c = shards.shape
    tm = _pick(rows, 256, 16)

    def body(i_ref, o_ref):
        for s, lo, hi, start in segs:
            o_ref[0, :, start:start + hi - lo] = i_ref[s, 0, :, lo:hi]

    return pl.pallas_call(
        body, name=name, grid=(depth, rows // tm),
        in_specs=[pl.BlockSpec((ns, 1, tm, c), lambda l, i: (0, l, i, 0))],
        out_specs=pl.BlockSpec((1, tm, ns * c), lambda l, i: (l, i, 0)),
        out_shape=jax.ShapeDtypeStruct((depth, rows, ns * c), shards.dtype),
        compiler_params=pltpu.CompilerParams(dimension_semantics=("parallel", "parallel")),
    )(shards)


def _split_col_shards(full, segs, layer, stacked, name):
    rows, width = full.shape
    c = width // N_SHARD
    tm = _pick(rows, 256, 16)

    def body(*refs):
        i_ref, o_ref = refs[0], refs[-1]
        for s, lo, hi, start in segs:
            o_ref[s, 0, :, lo:hi] = i_ref[:, start:start + hi - lo]

    return pl.pallas_call(
        body, name=name, grid=(rows // tm,),
        in_specs=[pl.BlockSpec((tm, width), lambda i: (i, 0))] + ([] if stacked is None else [pl.BlockSpec(memory_space=pl.ANY)]),
        out_specs=pl.BlockSpec((N_SHARD, 1, tm, c), lambda i: (0, layer, i, 0)),
        out_shape=jax.ShapeDtypeStruct((N_SHARD, DEPTH, rows, c), full.dtype),
        input_output_aliases={} if stacked is None else {1: 0},
        compiler_params=pltpu.CompilerParams(dimension_semantics=("parallel",)),
    )(*([full] if stacked is None else [full, stacked]))


def _pack_full_grads(big_grads):
    pieces, half = _flat_layout(False)
    parts = []
    for name, shape, axis, _, rows in pieces:
        per_layer = []
        for gfull in big_grads[name]:
            split = gfull.reshape(shape[:axis] + (N_SHARD, shape[axis] // N_SHARD) + shape[axis + 1:])
            per_layer.append(jnp.moveaxis(split, axis, 0).reshape(N_SHARD, -1))
        parts.append(_to_rows(jnp.concatenate(per_layer, axis=1), rows))
    parts.append(jnp.zeros((N_SHARD, 2 * half - sum(p[-1] for p in pieces), LANES), F32))
    return jnp.concatenate(parts, axis=1).reshape(N_SHARD, 2, half, LANES)


def _pack_small(grads):
    flat = jnp.concatenate([grads[name].reshape(-1) for name in SMALL])
    r = -(-flat.shape[0] // (8 * LANES)) * 8
    return jnp.pad(flat, (0, r * LANES - flat.shape[0])).reshape(r, LANES)


def _unpack_small(flat, like):
    flat = flat.reshape(-1)
    out, off = {}, 0
    for name in SMALL:
        size = int(np.prod(like[name].shape))
        out[name] = flat[off:off + size].reshape(like[name].shape)
        off += size
    return out


def _adamw(w, g, m, v, name):
    shape = w.shape
    cols = shape[-1]
    rows = int(np.prod(shape[:-1])) if len(shape) > 1 else 1
    w2, g2, m2, v2 = (a.reshape(rows, cols) for a in (w, g, m, v))

    def fn(wv, gv, mv, vv):
        mn = ADAM_B1 * mv + (1.0 - ADAM_B1) * gv
        vn = ADAM_B2 * vv + (1.0 - ADAM_B2) * jnp.square(gv)
        m_hat = mn / (1.0 - ADAM_B1 ** ADAM_STEP)
        v_hat = vn / (1.0 - ADAM_B2 ** ADAM_STEP)
        delta = -ADAM_LR * (m_hat / (jnp.sqrt(v_hat) + ADAM_EPS) + ADAM_WD * wv)
        return [delta, mn, vn], []

    tm = 256 if rows % 8 == 0 and rows > 256 else rows
    res = _rw(fn, [(a, None, _c0) for a in (w2, g2, m2, v2)], [(cols, None, F32, _c0)] * 3, tm=tm, name=name)
    return [r.reshape(shape) for r in res]


def _step(x, p, target, weights, moments_m, moments_v):
    xi, yi, ci = lax.axis_index("x"), lax.axis_index("y"), lax.axis_index("c")
    chip = 2 * xi + yi
    half_rows = DEPTH // 2 * D_MODEL
    locals_ = [_pack_local_weights(weights)] + [weights[name].astype(BF16).reshape(2, half_rows, c) for name, c in COL_SHARDED]
    gathered = _allgather_weights(locals_)
    gathered = [_place(g.reshape(N_SHARD, 2 * a.shape[1], a.shape[2]), a.reshape(2 * a.shape[1], a.shape[2]), chip, f"place_own_weights_{k}")
                for k, (g, a) in enumerate(zip(gathered, locals_, strict=True))]
    wt = dict(_unpack_gathered(gathered[0]))
    for (name, c), g in zip(COL_SHARDED, gathered[1:], strict=True):
        wt[name] = _join_col_shards(g.reshape(N_SHARD, DEPTH, D_MODEL, c), _col_segments(name, c), f"join_col_shards_{name}")
    for name in SMALL:
        wt[name] = weights[name]
    loss, dx, small_grads, big_grads = _local_step(x[0], p[:, 0], target[0], wt)
    loss = lax.psum(loss, ("x", "y", "c"))
    bufs = [_pack_full_grads(big_grads)]
    for name, c in COL_SHARDED:
        stacked = None
        for li, g in enumerate(big_grads[name]):
            stacked = _split_col_shards(g, _col_segments(name, c), li, stacked, f"split_col_shards_{name}_l{li}")
        bufs.append(stacked.reshape(N_SHARD, 2, half_rows, c))
    gots = _sibling_exchange(bufs)
    sums = [_add_pair(b, g, ci, f"grad_add_sibling_{k}") for k, (b, g) in enumerate(zip(bufs, gots, strict=True))]
    others = _chip_exchange([s_bf16 for _, s_bf16 in sums])
    halves = [_add_own(s_f32, o, chip, f"grad_add_chips_{k}") for k, ((s_f32, _), o) in enumerate(zip(sums, others, strict=True))]
    both = [_place(b, h, ci, f"place_own_half_{k}") for k, (b, h) in enumerate(zip(_sibling_gather(halves), halves, strict=True))]
    reduced = _unpack_local(both[0])
    for (name, c), b in zip(COL_SHARDED, both[1:], strict=True):
        reduced[name] = b.reshape(DEPTH, D_MODEL, c)
    small_part = _pack_small(small_grads)
    small_all = _place(_gather_partials(small_part), small_part, 4 * xi + 2 * yi + ci, "place_own_small")
    reduced.update(_unpack_small(_sum_leading(small_all, "small_sum")[0], {k: weights[k] for k in SMALL}))
    outs_g, outs_d, outs_m, outs_v = [], [], [], []
    for name in WEIGHTS:
        d, mn, vn = _adamw(weights[name], reduced[name], moments_m[name], moments_v[name], f"adamw_{name}")
        outs_g.append(reduced[name])
        outs_d.append(d)
        outs_m.append(mn)
        outs_v.append(vn)
    return (loss, dx[None], *outs_g, *outs_d, *outs_m, *outs_v)


def kernel(x, p, rel_bias, norm_mix, w_in, ssm_lambda_re, ssm_lambda_im, ssm_b_re, ssm_b_im, ssm_c_re, ssm_c_im, ssm_d, ssm_log_dt, ssm_w_glu, conv_w, attn_sinks, w_branch, w_out, norm_ffn, w_ffn_in, w_ffn_out, norm_ple, w_ple_gate, w_ple_proj, norm_final, loss_target, m_rel_bias, m_norm_mix, m_w_in, m_ssm_lambda_re, m_ssm_lambda_im, m_ssm_b_re, m_ssm_b_im, m_ssm_c_re, m_ssm_c_im, m_ssm_d, m_ssm_log_dt, m_ssm_w_glu, m_conv_w, m_attn_sinks, m_w_branch, m_w_out, m_norm_ffn, m_w_ffn_in, m_w_ffn_out, m_norm_ple, m_w_ple_gate, m_w_ple_proj, m_norm_final, v_rel_bias, v_norm_mix, v_w_in, v_ssm_lambda_re, v_ssm_lambda_im, v_ssm_b_re, v_ssm_b_im, v_ssm_c_re, v_ssm_c_im, v_ssm_d, v_ssm_log_dt, v_ssm_w_glu, v_conv_w, v_attn_sinks, v_w_branch, v_w_out, v_norm_ffn, v_w_ffn_in, v_w_ffn_out, v_norm_ple, v_w_ple_gate, v_w_ple_proj, v_norm_final):
    weights = dict(rel_bias=rel_bias, norm_mix=norm_mix, w_in=w_in, ssm_lambda_re=ssm_lambda_re, ssm_lambda_im=ssm_lambda_im,
                   ssm_b_re=ssm_b_re, ssm_b_im=ssm_b_im, ssm_c_re=ssm_c_re, ssm_c_im=ssm_c_im, ssm_d=ssm_d, ssm_log_dt=ssm_log_dt,
                   ssm_w_glu=ssm_w_glu, conv_w=conv_w, attn_sinks=attn_sinks, w_branch=w_branch, w_out=w_out, norm_ffn=norm_ffn,
                   w_ffn_in=w_ffn_in, w_ffn_out=w_ffn_out, norm_ple=norm_ple, w_ple_gate=w_ple_gate, w_ple_proj=w_ple_proj,
                   norm_final=norm_final)
    moments_m = dict(rel_bias=m_rel_bias, norm_mix=m_norm_mix, w_in=m_w_in, ssm_lambda_re=m_ssm_lambda_re, ssm_lambda_im=m_ssm_lambda_im,
                     ssm_b_re=m_ssm_b_re, ssm_b_im=m_ssm_b_im, ssm_c_re=m_ssm_c_re, ssm_c_im=m_ssm_c_im, ssm_d=m_ssm_d,
                     ssm_log_dt=m_ssm_log_dt, ssm_w_glu=m_ssm_w_glu, conv_w=m_conv_w, attn_sinks=m_attn_sinks, w_branch=m_w_branch,
                     w_out=m_w_out, norm_ffn=m_norm_ffn, w_ffn_in=m_w_ffn_in, w_ffn_out=m_w_ffn_out, norm_ple=m_norm_ple,
                     w_ple_gate=m_w_ple_gate, w_ple_proj=m_w_ple_proj, norm_final=m_norm_final)
    moments_v = dict(rel_bias=v_rel_bias, norm_mix=v_norm_mix, w_in=v_w_in, ssm_lambda_re=v_ssm_lambda_re, ssm_lambda_im=v_ssm_lambda_im,
                     ssm_b_re=v_ssm_b_re, ssm_b_im=v_ssm_b_im, ssm_c_re=v_ssm_c_re, ssm_c_im=v_ssm_c_im, ssm_d=v_ssm_d,
                     ssm_log_dt=v_ssm_log_dt, ssm_w_glu=v_ssm_w_glu, conv_w=v_conv_w, attn_sinks=v_attn_sinks, w_branch=v_w_branch,
                     w_out=v_w_out, norm_ffn=v_norm_ffn, w_ffn_in=v_w_ffn_in, w_ffn_out=v_w_ffn_out, norm_ple=v_norm_ple,
                     w_ple_gate=v_w_ple_gate, w_ple_proj=v_w_ple_proj, norm_final=v_norm_final)
    return _step(x, p, loss_target, weights, moments_m, moments_v)
```

```python
import functools
import math

import numpy as np

import jax
import jax.numpy as jnp
from jax import lax
from jax.experimental import pallas as pl
from jax.experimental.pallas import tpu as pltpu

F32, BF16 = jnp.float32, jnp.bfloat16
MESH = pl.DeviceIdType.MESH

D_MODEL = 1024
DEPTH = 4
PLE_DIM = 256
BRANCH = 512
N_GROUPS = 32
GROUP_CH = 16
N_STATE = 64
SSM_STATES = N_GROUPS * N_STATE
SSM_BLOCKS = 4
HEAD_DIM = 64
N_Q = 8
N_KV = 2
GQA = N_Q // N_KV
WINDOW = 128
ATTN_SCALE = 1.0 / math.sqrt(HEAD_DIM)
REL_BUCKETS = 32
REL_MAX_DIST = 128
FFN_HIDDEN = 2816
FFN_COLS = 1408
FFN_NCOL = FFN_HIDDEN // FFN_COLS
IN_WIDTH = 5888
RMS_EPS = 1e-6
NEG = -1e30

ADAM_LR, ADAM_B1, ADAM_B2, ADAM_EPS, ADAM_WD, ADAM_STEP = 0.001, 0.9, 0.999, 1e-08, 0.01, 10

N_SHARD = 4
LANES = 1024
COPY_CHUNKS = 4

GATES_WIDTH = 3 * D_MODEL
MAIN_WIDTH = IN_WIDTH - GATES_WIDTH
OFF_G, OFF_U, OFF_CB, OFF_CC, OFF_CX, OFF_Q, OFF_K, OFF_V = 0, 3072, 3584, 4096, 4608, 5120, 5632, 5760
U_BLK, CB_BLK, CC_BLK, CX_BLK = OFF_U // BRANCH, OFF_CB // BRANCH, OFF_CC // BRANCH, OFF_CX // BRANCH

BIG = (
    ("w_in", (D_MODEL, IN_WIDTH), 1),
    ("ssm_w_glu", (BRANCH, BRANCH), 0),
    ("conv_w", (3, BRANCH), 1),
    ("w_branch", (3, BRANCH, D_MODEL), 2),
    ("w_out", (D_MODEL, D_MODEL), 0),
    ("w_ffn_in", (D_MODEL, 2 * FFN_HIDDEN), 1),
    ("w_ffn_out", (FFN_HIDDEN, D_MODEL), 0),
    ("w_ple_gate", (D_MODEL, D_MODEL), 0),
    ("w_ple_proj", (PLE_DIM, D_MODEL), 1),
)
SMALL = ("rel_bias", "norm_mix", "ssm_lambda_re", "ssm_lambda_im", "ssm_b_re", "ssm_b_im", "ssm_c_re", "ssm_c_im",
         "ssm_d", "ssm_log_dt", "attn_sinks", "norm_ffn", "norm_ple", "norm_final")
WEIGHTS = ("rel_bias", "norm_mix", "w_in", "ssm_lambda_re", "ssm_lambda_im", "ssm_b_re", "ssm_b_im", "ssm_c_re",
           "ssm_c_im", "ssm_d", "ssm_log_dt", "ssm_w_glu", "conv_w", "attn_sinks", "w_branch", "w_out", "norm_ffn",
           "w_ffn_in", "w_ffn_out", "norm_ple", "w_ple_gate", "w_ple_proj", "norm_final")


def _c0(j):
    return 0


def _pick(n, cap, unit=128):
    if n <= cap:
        return n
    best = None
    for t in range(unit, cap + 1, unit):
        if n % t == 0:
            best = t
    assert best is not None, (n, cap, unit)
    return best


_DIMS = {"nn": ((1,), (0,)), "nt": ((1,), (1,)), "tn": ((0,), (0,))}


def _mm(a, b, mode, *, name, out_dtype=F32, add=None, tm=1024, tn=1024, tk=1024, b_k0=0, n_outer=False, rms_out=None, rms_back=None):
    if mode == "nn":
        (m, k), (k2, n) = a.shape, b.shape
    elif mode == "nt":
        (m, k), (n, k2) = a.shape, b.shape
    else:
        (k, m), (k2, n) = a.shape, b.shape
    assert k == k2 or (mode == "nt" and b_k0 + k <= k2), (a.shape, b.shape, mode)
    tm, tn, tk = _pick(m, tm, 128 if mode == "tn" else 8), _pick(n, tn), _pick(k, tk, 128 if mode != "tn" else 8)
    nk = k // tk
    assert b_k0 % tk == 0
    kb0 = b_k0 // tk
    def at(f):
        return (lambda j, i, kk: f(i, j, kk)) if n_outer else f

    a_spec = pl.BlockSpec((tk, tm), at(lambda i, j, kk: (kk, i))) if mode == "tn" else pl.BlockSpec((tm, tk), at(lambda i, j, kk: (i, kk)))
    b_spec = (pl.BlockSpec((tn, tk), at(lambda i, j, kk: (j, kb0 + kk))) if mode == "nt"
              else pl.BlockSpec((tk, tn), at(lambda i, j, kk: (kk, j))))
    o_spec = pl.BlockSpec((tm, tn), at(lambda i, j, kk: (i, j)))
    dims = (_DIMS[mode], ((), ()))
    has_add = add is not None
    fused_rows = rms_out is not None or rms_back is not None
    assert not fused_rows or (tn == n and not n_outer), "a fused RMSNorm needs whole rows in a tile"
    row_spec = pl.BlockSpec((1, n), at(lambda i, j, kk: (0, 0)))
    extra = [rms_out] if rms_out is not None else (list(rms_back) if rms_back is not None else [])
    extra_specs = [row_spec] if rms_out is not None else ([o_spec, o_spec, row_spec] if rms_back is not None else [])
    n_in = 2 + has_add + len(extra)

    def body(*refs):
        a_ref, b_ref = refs[0], refs[1]
        add_ref = refs[2] if has_add else None
        x_refs = refs[2 + has_add:n_in]
        o_refs, acc_ref = refs[n_in:-1], refs[-1]
        part = lax.dot_general(a_ref[...].astype(BF16), b_ref[...].astype(BF16), dims, preferred_element_type=F32)

        def finish(acc):
            if has_add:
                acc = acc + add_ref[...]
            if rms_back is not None:
                _, vjp = jax.vjp(_rms, x_refs[0][...], x_refs[2][...])
                dx, dg = vjp(acc)
                o_refs[0][...] = x_refs[1][...] + dx
                first = pl.program_id(0) == 0

                @pl.when(first)
                def _():
                    o_refs[1][...] = dg

                @pl.when(jnp.logical_not(first))
                def _():
                    o_refs[1][...] += dg
                return
            o_refs[0][...] = acc.astype(o_refs[0].dtype)
            if rms_out is not None:
                o_refs[1][...] = _rms(acc, x_refs[0][...]).astype(BF16)

        if nk == 1:
            finish(part)
        else:
            kk = pl.program_id(2)

            @pl.when(kk == 0)
            def _():
                acc_ref[...] = part

            @pl.when(kk > 0)
            def _():
                acc_ref[...] += part

            @pl.when(kk == nk - 1)
            def _():
                finish(acc_ref[...])

    operands = [a, b] + ([add] if has_add else []) + extra
    in_specs = [a_spec, b_spec] + ([o_spec] if has_add else []) + extra_specs
    out_specs, out_shape = [o_spec], [jax.ShapeDtypeStruct((m, n), out_dtype)]
    if rms_out is not None:
        out_specs, out_shape = out_specs + [o_spec], out_shape + [jax.ShapeDtypeStruct((m, n), BF16)]
    if rms_back is not None:
        out_specs, out_shape = out_specs + [row_spec], out_shape + [jax.ShapeDtypeStruct((1, n), F32)]
    res = pl.pallas_call(
        body, name=name, grid=(n // tn, m // tm, nk) if n_outer else (m // tm, n // tn, nk), in_specs=in_specs, out_specs=out_specs,
        out_shape=out_shape, scratch_shapes=[pltpu.VMEM((tm, tn) if nk > 1 else (8, 128), F32)],
        compiler_params=pltpu.CompilerParams(
            dimension_semantics=("arbitrary",) * 3 if rms_back is not None else ("parallel", "parallel", "arbitrary")),
    )(*operands)
    return res if fused_rows else res[0]


def _rw(fn, ins, outs, *, name, params=(), reds=(), tm=256, ncol=1, with_j=False):
    t = ins[0][0].shape[0]
    tm = _pick(t, tm, 8)
    nrow = t // tm
    n_in, n_p, n_out = len(ins), len(params), len(outs)

    in_specs = [pl.BlockSpec((tm, bw or arr.shape[1]), lambda j, i, cf=cf: (i, cf(j))) for arr, bw, cf in ins]
    in_specs += [pl.BlockSpec(p.shape, lambda j, i: (0, 0)) for p in params]
    out_specs = [pl.BlockSpec((tm, bw or w), lambda j, i, cf=cf: (i, cf(j))) for w, bw, _, cf in outs]
    out_specs += [pl.BlockSpec((shp[0], bw or shp[1]), lambda j, i, cf=cf: (0, cf(j))) for shp, bw, cf in reds]
    out_shape = [jax.ShapeDtypeStruct((t, w), dt) for w, _, dt, _ in outs]
    out_shape += [jax.ShapeDtypeStruct(shp, F32) for shp, _, _ in reds]

    def body(*refs):
        in_refs, p_refs = refs[:n_in], refs[n_in:n_in + n_p]
        o_refs, r_refs = refs[n_in + n_p:n_in + n_p + n_out], refs[n_in + n_p + n_out:]
        args = [r[...].astype(F32) for r in in_refs] + [r[...] for r in p_refs]
        if with_j:
            args = [pl.program_id(0)] + args
        o_vals, r_vals = fn(*args)
        for r, v in zip(o_refs, o_vals, strict=True):
            r[...] = v.astype(r.dtype)
        if r_refs:
            i = pl.program_id(1)
            for r, v in zip(r_refs, r_vals, strict=True):
                @pl.when(i == 0)
                def _(r=r, v=v):
                    r[...] = v

                @pl.when(i > 0)
                def _(r=r, v=v):
                    r[...] += v

    res = pl.pallas_call(
        body, name=name, grid=(ncol, nrow), in_specs=in_specs, out_specs=out_specs, out_shape=out_shape,
        compiler_params=pltpu.CompilerParams(dimension_semantics=("parallel", "arbitrary" if reds else "parallel")),
    )(*[a for a, _, _ in ins], *params)
    return res


def _rms(x, g):
    return x * lax.rsqrt(jnp.mean(x * x, axis=-1, keepdims=True) + RMS_EPS) * g


def _rms_fwd(x, g, name):
    return _rw(lambda xv, gv: ([_rms(xv, gv)], []), [(x, None, _c0)], [(D_MODEL, None, BF16, _c0)], params=[g], name=name)[0]


def _bd_apply(acts, mats, combos, mode, *, name, tm=512, col_blocks=None):
    t = acts[0].shape[0]
    tm = _pick(t, tm, 8)
    nb, r, c = mats[0].shape
    win, wout = (r, c) if mode == "nn" else (c, r)
    dims = (_DIMS[mode], ((), ()))
    n_a, n_m = len(acts), len(mats)

    def body(*refs):
        a_vals = [ar[...].astype(BF16) for ar in refs[:n_a]]
        m_refs, o_refs = refs[n_a:n_a + n_m], refs[n_a + n_m:]
        for o_ref, terms in zip(o_refs, combos, strict=True):
            for j in range(nb):
                acc = None
                for ai, mi in terms:
                    part = lax.dot_general(a_vals[ai][:, j * win:(j + 1) * win], m_refs[mi][j], dims, preferred_element_type=F32)
                    acc = part if acc is None else acc + part
                o_ref[:, j * wout:(j + 1) * wout] = acc

    return pl.pallas_call(
        body, name=name, grid=(t // tm,),
        in_specs=[pl.BlockSpec((tm, nb * win), lambda i, cb=cb: (i, cb)) for cb in (col_blocks or [0] * n_a)]
        + [pl.BlockSpec(m.shape, lambda i: (0, 0, 0)) for m in mats],
        out_specs=[pl.BlockSpec((tm, nb * wout), lambda i: (i, 0))] * len(combos),
        out_shape=[jax.ShapeDtypeStruct((t, nb * wout), F32)] * len(combos),
        compiler_params=pltpu.CompilerParams(dimension_semantics=("parallel",)),
    )(*acts, *mats)


def _bd_grads(arrs, widths, pairs, *, name, tk=512, col_blocks=None):
    t = arrs[0].shape[0]
    tk = _pick(t, tk, 8)
    n_a = len(arrs)
    dims = (_DIMS["tn"], ((), ()))

    def body(*refs):
        vals = [ar[...].astype(BF16) for ar in refs[:n_a]]
        o_refs = refs[n_a:]
        @pl.when(pl.program_id(0) == 0)
        def _():
            for o_ref in o_refs:
                o_ref[...] = jnp.zeros_like(o_ref)

        for o_ref, (ai, bi) in zip(o_refs, pairs, strict=True):
            wa, wb = widths[ai], widths[bi]
            for j in range(SSM_BLOCKS):
                o_ref[j] += lax.dot_general(vals[ai][:, j * wa:(j + 1) * wa], vals[bi][:, j * wb:(j + 1) * wb], dims,
                                            preferred_element_type=F32)

    return pl.pallas_call(
        body, name=name, grid=(t // tk,),
        in_specs=[pl.BlockSpec((tk, SSM_BLOCKS * w), lambda k, cb=cb: (k, cb)) for w, cb in zip(widths, col_blocks or [0] * n_a, strict=True)],
        out_specs=[pl.BlockSpec((SSM_BLOCKS, widths[ai], widths[bi]), lambda k: (0, 0, 0)) for ai, bi in pairs],
        out_shape=[jax.ShapeDtypeStruct((SSM_BLOCKS, widths[ai], widths[bi]), F32) for ai, bi in pairs],
        compiler_params=pltpu.CompilerParams(dimension_semantics=("arbitrary",)),
    )(*arrs)


SCAN_LW = 512
SCAN_ROWS = 512
_DOUBLING = ((1, 0), (2, 1), (4, 2))


def _scan(xr, xi, pr, pi, dr, di, *, reverse, name, hr=None, hi=None):
    t, s = xr.shape
    lc = _pick(t, SCAN_ROWS, 8)
    nt, ngroups = t // lc, lc // 8
    with_da = hr is not None

    def tmap(l, tt):
        return ((nt - 1 - tt) if reverse else tt, l)

    x_spec = pl.BlockSpec((lc, SCAN_LW), tmap)
    tab_spec = pl.BlockSpec((8, SCAN_LW), lambda l, tt: (0, l))

    def body(*refs):
        xr_ref, xi_ref, pr_ref, pi_ref, dr_ref, di_ref = refs[:6]
        if with_da:
            hr_ref, hi_ref, or_ref, oi_ref, ar_ref, ai_ref, cr_ref, ci_ref = refs[6:]
        else:
            or_ref, oi_ref, cr_ref, ci_ref = refs[6:]
        tt = pl.program_id(1)

        @pl.when(tt == 0)
        def _():
            cr_ref[...] = jnp.zeros_like(cr_ref)
            ci_ref[...] = jnp.zeros_like(ci_ref)
            if with_da:
                ar_ref[...] = jnp.zeros_like(ar_ref)
                ai_ref[...] = jnp.zeros_like(ai_ref)

        sub = lax.broadcasted_iota(jnp.int32, (8, SCAN_LW), 0)
        pw_r, pw_i = pr_ref[...], pi_ref[...]

        def step(g, carry):
            g = (ngroups - 1 - g) if reverse else g
            r0 = pl.multiple_of(g * 8, 8)
            vr, vi = xr_ref[pl.ds(r0, 8), :], xi_ref[pl.ds(r0, 8), :]
            for shift, row in _DOUBLING:
                a_r, a_i = dr_ref[row:row + 1, :], di_ref[row:row + 1, :]
                if reverse:
                    keep = sub < 8 - shift
                    sr, si = pltpu.roll(vr, 8 - shift, 0), pltpu.roll(vi, 8 - shift, 0)
                else:
                    keep = sub >= shift
                    sr, si = pltpu.roll(vr, shift, 0), pltpu.roll(vi, shift, 0)
                sr, si = jnp.where(keep, sr, 0.0), jnp.where(keep, si, 0.0)
                vr, vi = vr + a_r * sr - a_i * si, vi + a_r * si + a_i * sr
            if with_da:
                cr, ci, acc_r, acc_i = carry
            else:
                cr, ci = carry
            vr, vi = vr + pw_r * cr - pw_i * ci, vi + pw_r * ci + pw_i * cr
            or_ref[pl.ds(r0, 8), :] = vr
            oi_ref[pl.ds(r0, 8), :] = vi
            if with_da:
                nr = jnp.where(sub < 7, pltpu.roll(vr, 7, 0), cr)
                ni = jnp.where(sub < 7, pltpu.roll(vi, 7, 0), ci)
                h_r, h_i = hr_ref[pl.ds(r0, 8), :], hi_ref[pl.ds(r0, 8), :]
                acc_r = acc_r + h_r * nr + h_i * ni
                acc_i = acc_i + h_r * ni - h_i * nr
            edge = 0 if reverse else 7
            cr = jnp.broadcast_to(vr[edge:edge + 1, :], vr.shape)
            ci = jnp.broadcast_to(vi[edge:edge + 1, :], vi.shape)
            return (cr, ci, acc_r, acc_i) if with_da else (cr, ci)

        zero = jnp.zeros((8, SCAN_LW), F32)
        init = (cr_ref[...], ci_ref[...]) + ((zero, zero) if with_da else ())
        fin = lax.fori_loop(0, ngroups, step, init, unroll=2)
        cr_ref[...] = fin[0]
        ci_ref[...] = fin[1]
        if with_da:
            ar_ref[...] += fin[2]
            ai_ref[...] += fin[3]

    n_x = 4 if with_da else 2
    out_specs = [x_spec, x_spec] + ([tab_spec, tab_spec] if with_da else [])
    out_shape = [jax.ShapeDtypeStruct((t, s), F32)] * 2 + ([jax.ShapeDtypeStruct((8, s), F32)] * 2 if with_da else [])
    operands = [xr, xi, pr, pi, dr, di] + ([hr, hi] if with_da else [])
    return pl.pallas_call(
        body, name=name, grid=(s // SCAN_LW, nt),
        in_specs=[x_spec, x_spec] + [tab_spec] * 4 + [x_spec] * (n_x - 2),
        out_specs=out_specs, out_shape=out_shape,
        scratch_shapes=[pltpu.VMEM((8, SCAN_LW), F32), pltpu.VMEM((8, SCAN_LW), F32)],
        compiler_params=pltpu.CompilerParams(dimension_semantics=("parallel", "arbitrary")),
    )(*operands)


CONV_TM = 256
HALO = 16


def _conv_specs(t, tm):
    nrow = t // tm
    hb = tm // HALO

    def col(cidx):
        return pl.BlockSpec((tm, BRANCH), lambda i: (i, cidx))

    def prev(cidx):
        return pl.BlockSpec((HALO, BRANCH), lambda i: (jnp.maximum(i * hb - 1, 0), cidx))

    def nxt(cidx):
        return pl.BlockSpec((HALO, BRANCH), lambda i: (jnp.minimum((i + 1) * hb, nrow * hb - 1), cidx))

    return nrow, col, prev, nxt


def _conv_taps(cc, cx, cc_prev, cx_prev, first):
    tm = cc.shape[0]
    v = cc * cx
    halo = cc_prev * cx_prev * jnp.where(first, 0.0, 1.0)
    ext = jnp.concatenate([halo, v], axis=0)
    return v, pltpu.roll(ext, 1, 0)[HALO:HALO + tm], pltpu.roll(ext, 2, 0)[HALO:HALO + tm]


def _conv_fwd(z, conv_w, name):
    t = z.shape[0]
    tm = _pick(t, CONV_TM, HALO)
    nrow, col, prev, _ = _conv_specs(t, tm)

    def body(cb_ref, cc_ref, cx_ref, ccp_ref, cxp_ref, w_ref, o_ref):
        first = pl.program_id(0) == 0
        v, v1, v2 = _conv_taps(*(r[...].astype(F32) for r in (cc_ref, cx_ref, ccp_ref, cxp_ref)), first)
        y = w_ref[0:1, :] * v2 + w_ref[1:2, :] * v1 + w_ref[2:3, :] * v
        o_ref[...] = (cb_ref[...].astype(F32) * y).astype(o_ref.dtype)

    return pl.pallas_call(
        body, name=name, grid=(nrow,),
        in_specs=[col(CB_BLK), col(CC_BLK), col(CX_BLK), prev(CC_BLK), prev(CX_BLK), pl.BlockSpec((3, BRANCH), lambda i: (0, 0))],
        out_specs=pl.BlockSpec((tm, BRANCH), lambda i: (i, 0)), out_shape=jax.ShapeDtypeStruct((t, BRANCH), BF16),
        compiler_params=pltpu.CompilerParams(dimension_semantics=("parallel",)),
    )(z, z, z, z, z, conv_w)


def _conv_bwd(dyc, z, conv_w, name):
    t = z.shape[0]
    tm = _pick(t, CONV_TM, HALO)
    nrow, col, prev, nxt = _conv_specs(t, tm)
    d_cur = pl.BlockSpec((tm, BRANCH), lambda i: (i, 0))
    d_nxt = pl.BlockSpec((HALO, BRANCH), lambda i: (jnp.minimum((i + 1) * (tm // HALO), nrow * (tm // HALO) - 1), 0))

    def body(dy_ref, dyn_ref, cb_ref, cbn_ref, cc_ref, cx_ref, ccp_ref, cxp_ref, w_ref, dcb_ref, dcc_ref, dcx_ref, dw_ref):
        i = pl.program_id(0)
        cc, cx, cb = cc_ref[...].astype(F32), cx_ref[...].astype(F32), cb_ref[...].astype(F32)
        v, v1, v2 = _conv_taps(cc, cx, ccp_ref[...].astype(F32), cxp_ref[...].astype(F32), i == 0)
        w0, w1, w2 = w_ref[0:1, :], w_ref[1:2, :], w_ref[2:3, :]
        y = w0 * v2 + w1 * v1 + w2 * v
        dyc_v = dy_ref[...]
        dcb_ref[...] = (dyc_v * y).astype(dcb_ref.dtype)
        dy = dyc_v * cb
        halo = dyn_ref[...] * cbn_ref[...].astype(F32) * jnp.where(i == nrow - 1, 0.0, 1.0)
        ext = jnp.concatenate([dy, halo], axis=0)
        dy1 = pltpu.roll(ext, tm + HALO - 1, 0)[0:tm]
        dy2 = pltpu.roll(ext, tm + HALO - 2, 0)[0:tm]
        dv = w2 * dy + w1 * dy1 + w0 * dy2
        dcc_ref[...] = (dv * cx).astype(dcc_ref.dtype)
        dcx_ref[...] = (dv * cc).astype(dcx_ref.dtype)
        dw = jnp.concatenate([jnp.sum(dy * v2, axis=0, keepdims=True), jnp.sum(dy * v1, axis=0, keepdims=True),
                              jnp.sum(dy * v, axis=0, keepdims=True), jnp.zeros((5, BRANCH), F32)], axis=0)

        @pl.when(i == 0)
        def _():
            dw_ref[...] = dw

        @pl.when(i > 0)
        def _():
            dw_ref[...] += dw

    o_spec = pl.BlockSpec((tm, BRANCH), lambda i: (i, 0))
    return pl.pallas_call(
        body, name=name, grid=(nrow,),
        in_specs=[d_cur, d_nxt, col(CB_BLK), nxt(CB_BLK), col(CC_BLK), col(CX_BLK), prev(CC_BLK), prev(CX_BLK),
                  pl.BlockSpec((3, BRANCH), lambda i: (0, 0))],
        out_specs=[o_spec, o_spec, o_spec, pl.BlockSpec((8, BRANCH), lambda i: (0, 0))],
        out_shape=[jax.ShapeDtypeStruct((t, BRANCH), BF16)] * 3 + [jax.ShapeDtypeStruct((8, BRANCH), F32)],
        compiler_params=pltpu.CompilerParams(dimension_semantics=("arbitrary",)),
    )(dyc, dyc, z, z, z, z, z, z, conv_w)


ATTN_BLOCKS = 4
ATTN_BLOCKS_BWD = 1
GROUP_ROWS = GQA * WINDOW


def _attn_specs(nblk):
    rows = nblk * WINDOW
    q_spec = pl.BlockSpec((N_Q, rows, HEAD_DIM), lambda n: (0, n, 0))
    kv_cur = pl.BlockSpec((N_KV, rows, HEAD_DIM), lambda n: (0, n, 0))
    kv_prev = pl.BlockSpec((N_KV, WINDOW, HEAD_DIM), lambda n: (0, jnp.maximum(n * nblk - 1, 0), 0))
    bias_spec = pl.BlockSpec((N_Q, WINDOW, 2 * WINDOW), lambda n: (0, 0, 0))
    sink_spec = pl.BlockSpec((N_Q, 1), lambda n: (0, 0))
    return q_spec, kv_cur, kv_prev, bias_spec, sink_spec


def _attn_valid(first_key):
    qi = lax.broadcasted_iota(jnp.int32, (GROUP_ROWS, 2 * WINDOW), 0) & (WINDOW - 1)
    kj = lax.broadcasted_iota(jnp.int32, (GROUP_ROWS, 2 * WINDOW), 1)
    dist = qi + WINDOW - kj
    return (dist >= 0) & (dist < WINDOW) & (kj >= first_key)


def _attn_masks(n):
    return _attn_valid(jnp.where(n > 0, 0, WINDOW)), _attn_valid(0)


def _blk(b):
    return slice(b * WINDOW, (b + 1) * WINDOW)


def _group(ref, h, b, width):
    return ref[GQA * h:GQA * (h + 1), _blk(b)].reshape(GROUP_ROWS, width)


def _keys(prev_ref, cur_ref, h, b):
    prev = prev_ref[h] if b == 0 else cur_ref[h, _blk(b - 1)]
    return jnp.concatenate([prev, cur_ref[h, _blk(b)]], axis=0)


def _group_sinks(s_ref, h):
    return jnp.concatenate([jnp.broadcast_to(s_ref[GQA * h + g:GQA * h + g + 1, :], (WINDOW, 1)) for g in range(GQA)], axis=0)


def _attn_probs(q, kc, bias, sink, valid):
    s = lax.dot_general(q, kc, (_DIMS["nt"], ((), ())), preferred_element_type=F32) * ATTN_SCALE + bias
    s = jnp.where(valid, s, NEG)
    m = jnp.maximum(jnp.max(s, axis=1, keepdims=True), sink)
    p = jnp.exp(s - m)
    e_sink = jnp.exp(sink - m)
    inv = 1.0 / (jnp.sum(p, axis=1, keepdims=True) + e_sink)
    return p * inv, e_sink * inv


def _attn_fwd(qh, kh, vh, bias, sinks, name):
    t = qh.shape[1]
    nblk = min(ATTN_BLOCKS, t // WINDOW)
    q_spec, kv_cur, kv_prev, bias_spec, sink_spec = _attn_specs(nblk)

    def body(q_ref, kp_ref, kc_ref, vp_ref, vc_ref, b_ref, s_ref, o_ref):
        masks = _attn_masks(pl.program_id(0))
        for b in range(nblk):
            for h in range(N_KV):
                kc, vc = _keys(kp_ref, kc_ref, h, b), _keys(vp_ref, vc_ref, h, b)
                w, _ = _attn_probs(_group(q_ref, h, b, HEAD_DIM), kc, _group(b_ref, h, 0, 2 * WINDOW), _group_sinks(s_ref, h),
                                   masks[min(b, 1)])
                o = jnp.dot(w.astype(BF16), vc, preferred_element_type=F32)
                o_ref[GQA * h:GQA * (h + 1), _blk(b)] = o.reshape(GQA, WINDOW, HEAD_DIM).astype(o_ref.dtype)

    return pl.pallas_call(
        body, name=name, grid=(t // (nblk * WINDOW),),
        in_specs=[q_spec, kv_prev, kv_cur, kv_prev, kv_cur, bias_spec, sink_spec],
        out_specs=q_spec, out_shape=jax.ShapeDtypeStruct((N_Q, t, HEAD_DIM), BF16),
        compiler_params=pltpu.CompilerParams(dimension_semantics=("parallel",)),
    )(qh, kh, kh, vh, vh, bias, sinks)


def _attn_bwd(qh, kh, vh, doh, bias, sinks, name):
    t = qh.shape[1]
    nblk = min(ATTN_BLOCKS_BWD, t // WINDOW)
    nsteps = t // (nblk * WINDOW)
    q_spec, kv_cur, kv_prev, bias_spec, sink_spec = _attn_specs(nblk)

    def body(q_ref, kp_ref, kc_ref, vp_ref, vc_ref, do_ref, b_ref, s_ref,
             dq_ref, dkc_ref, dkp_ref, dvc_ref, dvp_ref, db_ref, ds_ref):
        n = pl.program_id(0)
        masks = _attn_masks(n)

        @pl.when(n == 0)
        def _():
            db_ref[...] = jnp.zeros_like(db_ref)
            ds_ref[...] = jnp.zeros_like(ds_ref)

        def scatter(part, h, b, cur_ref, prev_ref):
            if b == 0:
                prev_ref[h] = part[0:WINDOW]
            else:
                cur_ref[h, _blk(b - 1)] += part[0:WINDOW]
            cur_ref[h, _blk(b)] = part[WINDOW:2 * WINDOW]

        d_bias, d_sink = [None] * N_KV, [None] * N_KV
        for b in range(nblk):
            for h in range(N_KV):
                kc, vc = _keys(kp_ref, kc_ref, h, b), _keys(vp_ref, vc_ref, h, b)
                heads = slice(GQA * h, GQA * (h + 1))
                q, do = _group(q_ref, h, b, HEAD_DIM), _group(do_ref, h, b, HEAD_DIM)
                w, w_sink = _attn_probs(q, kc, _group(b_ref, h, 0, 2 * WINDOW), _group_sinks(s_ref, h), masks[min(b, 1)])
                dw = lax.dot_general(do, vc, (_DIMS["nt"], ((), ())), preferred_element_type=F32)
                delta = jnp.sum(w * dw, axis=1, keepdims=True)
                dscore = w * (dw - delta)
                d_sink[h] = -w_sink * delta if b == 0 else d_sink[h] - w_sink * delta
                d_bias[h] = dscore if b == 0 else d_bias[h] + dscore
                dsb = dscore.astype(BF16)
                dq_ref[heads, _blk(b)] = (jnp.dot(dsb, kc, preferred_element_type=F32) * ATTN_SCALE).reshape(GQA, WINDOW, HEAD_DIM)
                scatter(lax.dot_general(dsb, q, (_DIMS["tn"], ((), ())), preferred_element_type=F32) * ATTN_SCALE, h, b, dkc_ref, dkp_ref)
                scatter(lax.dot_general(w.astype(BF16), do, (_DIMS["tn"], ((), ())), preferred_element_type=F32), h, b, dvc_ref, dvp_ref)
        for h in range(N_KV):
            heads = slice(GQA * h, GQA * (h + 1))
            ds_ref[heads] += d_sink[h].reshape(GQA, WINDOW, 1)
            db_ref[heads] += d_bias[h].reshape(GQA, WINDOW, 2 * WINDOW)

    kv_shape = jax.ShapeDtypeStruct((N_KV, t, HEAD_DIM), F32)
    kv_prev_out = pl.BlockSpec((N_KV, WINDOW, HEAD_DIM), lambda n: (0, n, 0))
    kv_prev_shape = jax.ShapeDtypeStruct((N_KV, nsteps * WINDOW, HEAD_DIM), F32)
    return pl.pallas_call(
        body, name=name, grid=(nsteps,),
        in_specs=[q_spec, kv_prev, kv_cur, kv_prev, kv_cur, q_spec, bias_spec, sink_spec],
        out_specs=[q_spec, kv_cur, kv_prev_out, kv_cur, kv_prev_out, bias_spec, pl.BlockSpec((N_Q, WINDOW, 1), lambda n: (0, 0, 0))],
        out_shape=[jax.ShapeDtypeStruct((N_Q, t, HEAD_DIM), F32), kv_shape, kv_prev_shape, kv_shape, kv_prev_shape,
                   jax.ShapeDtypeStruct((N_Q, WINDOW, 2 * WINDOW), F32), jax.ShapeDtypeStruct((N_Q, WINDOW, 1), F32)],
        compiler_params=pltpu.CompilerParams(dimension_semantics=("arbitrary",)),
    )(qh, kh, kh, vh, vh, doh, bias, sinks)


def _heads(a, n_heads):
    t = a.shape[0]
    return a.astype(BF16).reshape(t, n_heads, HEAD_DIM).transpose(1, 0, 2)


def _unheads(a):
    n_heads, t, _ = a.shape
    return a.transpose(1, 0, 2).reshape(t, n_heads * HEAD_DIM)


def _shift_blocks(cur, prev):
    n_kv, t, d = cur.shape
    nsteps = prev.shape[1] // WINDOW
    nblk = t // (nsteps * WINDOW)
    late = jnp.concatenate([prev.reshape(n_kv, nsteps, WINDOW, d)[:, 1:], jnp.zeros((n_kv, 1, WINDOW, d), cur.dtype)], axis=1)
    delta = jnp.concatenate([jnp.zeros((n_kv, nsteps, nblk - 1, WINDOW, d), cur.dtype), late[:, :, None]], axis=2)
    return (cur.reshape(n_kv, nsteps, nblk, WINDOW, d) + delta).reshape(n_kv, t, d)


def _t5_bucket_table():
    qi = np.arange(WINDOW)[:, None]
    kj = np.arange(2 * WINDOW)[None, :]
    dist = np.clip(qi + WINDOW - kj, 0, REL_MAX_DIST - 1)
    exact = REL_BUCKETS // 2
    df = np.maximum(dist, 1).astype(np.float32)
    large = exact + (np.log(df / np.float32(exact)) / np.float32(math.log(REL_MAX_DIST / exact)) * (REL_BUCKETS - exact)).astype(np.int32)
    large = np.minimum(large, REL_BUCKETS - 1)
    bucket = np.where(dist < exact, dist, large)
    onehot = np.zeros((WINDOW * 2 * WINDOW, REL_BUCKETS), np.float32)
    onehot[np.arange(WINDOW * 2 * WINDOW), bucket.reshape(-1)] = 1.0
    return onehot


def _band_bias(rel_bias):
    onehot = jnp.asarray(_t5_bucket_table())
    sel = jnp.sum(onehot[:, :, None] * rel_bias[None, :, :], axis=1)
    return sel.T.reshape(N_Q, WINDOW, 2 * WINDOW)


def _block_diag(a):
    g, r, c = a.shape
    a4 = a.reshape(SSM_BLOCKS, g // SSM_BLOCKS, r, c)
    eye = jnp.eye(g // SSM_BLOCKS, dtype=a.dtype)
    full = a4[:, :, :, None, :] * eye[None, :, None, :, None]
    return full.reshape(SSM_BLOCKS, (g // SSM_BLOCKS) * r, (g // SSM_BLOCKS) * c)


def _ssm_disc(lam_re, lam_im, b_re, b_im, c_re, c_im, log_dt):
    dt = jnp.exp(log_dt)[:, None]
    mag = jnp.exp(lam_re * dt)
    ang = lam_im * dt
    a_re = mag * jnp.cos(ang)
    a_im = mag * jnp.sin(ang)
    den = lam_re * lam_re + lam_im * lam_im
    nr = a_re - 1.0
    coef_re = (nr * lam_re + a_im * lam_im) / den
    coef_im = (a_im * lam_re - nr * lam_im) / den
    bb_re = coef_re[..., None] * b_re - coef_im[..., None] * b_im
    bb_im = coef_re[..., None] * b_im + coef_im[..., None] * b_re
    wb_re = _block_diag(jnp.swapaxes(bb_re, 1, 2))
    wb_im = _block_diag(jnp.swapaxes(bb_im, 1, 2))
    cm_re = _block_diag(jnp.swapaxes(c_re, 1, 2))
    cm_imn = _block_diag(-jnp.swapaxes(c_im, 1, 2))
    return a_re.reshape(-1), a_im.reshape(-1), wb_re, wb_im, cm_re, cm_imn


def _scan_tables(a_re, a_im):
    pr, pi = [a_re], [a_im]
    for _ in range(7):
        pr, pi = pr + [pr[-1] * a_re - pi[-1] * a_im], pi + [pr[-1] * a_im + pi[-1] * a_re]
    pr, pi = jnp.stack(pr), jnp.stack(pi)
    pad = jnp.zeros((5,) + a_re.shape, F32)
    dr = jnp.concatenate([jnp.stack([pr[0], pr[1], pr[3]]), pad])
    di = jnp.concatenate([jnp.stack([pi[0], pi[1], pi[3]]), pad])
    fwd = (pr, pi, dr, di)
    rev = (pr[::-1], -pi[::-1], dr, -di)
    return jax.tree.map(lax.stop_gradient, (fwd, rev))


def _gate_col(r):
    return lambda j: OFF_G // D_MODEL + r


def _layer_fwd(x, h, p_i, w, bias, li, next_gain):
    nm = lambda s: f"{s}_l{li}"
    z = _mm(h, w["w_in"], "nn", tm=1024, tn=2944, n_outer=True, out_dtype=BF16, name=nm("mm_in"))
    bu_re, bu_im = _bd_apply([z], [w["wb_re"], w["wb_im"]], [[(0, 0)], [(0, 1)]], "nn", col_blocks=[U_BLK], name=nm("ssm_bu"))
    h_re, h_im = _scan(bu_re, bu_im, *w["scan_fwd"], reverse=False, name=nm("ssm_scan"))
    (y0,) = _bd_apply([h_re, h_im], [w["cm_re"], w["cm_imn"]], [[(0, 0), (1, 1)]], "nn", name=nm("ssm_c"))
    (y1,) = _rw(lambda a, u, d: ([jax.nn.gelu(a + d * u)], []),
                [(y0, None, _c0), (z, BRANCH, lambda j: U_BLK)], [(BRANCH, None, F32, _c0)],
                params=[w["ssm_d"]], name=nm("ssm_gelu"))
    gl = _mm(y1, w["ssm_w_glu"], "nn", name=nm("mm_glu"))
    (y_ssm,) = _rw(lambda a, b: ([a * jax.nn.sigmoid(b)], []), [(y1, None, _c0), (gl, None, _c0)],
                   [(BRANCH, None, BF16, _c0)], name=nm("ssm_glu"))
    y_conv = _conv_fwd(z, w["conv_w"], nm("conv_fwd"))
    kv_w = N_KV * HEAD_DIM
    q2, k2, v2 = _rw(lambda q, k, v: ([q, k, v], []),
                     [(z, BRANCH, lambda j: OFF_Q // BRANCH), (z, kv_w, lambda j: OFF_K // kv_w), (z, kv_w, lambda j: OFF_V // kv_w)],
                     [(BRANCH, None, BF16, _c0), (kv_w, None, BF16, _c0), (kv_w, None, BF16, _c0)], name=nm("qkv_bf16"))
    qh, kh, vh = _heads(q2, N_Q), _heads(k2, N_KV), _heads(v2, N_KV)
    y_attn = _unheads(_attn_fwd(qh, kh, vh, bias, w["sinks"], nm("attn_fwd")))
    ys = (y_ssm, y_conv, y_attn)
    bs = [_mm(ys[r], w["w_branch"][r], "nn", out_dtype=BF16, name=nm(f"mm_branch{r}")) for r in range(3)]

    def merge(g0, g1, g2, b0, b1, b2):
        return [jax.nn.sigmoid(g0) * b0 + jax.nn.sigmoid(g1) * b1 + jax.nn.sigmoid(g2) * b2], []

    (merged,) = _rw(merge, [(z, D_MODEL, _gate_col(r)) for r in range(3)] + [(b, None, _c0) for b in bs],
                    [(D_MODEL, None, BF16, _c0)], name=nm("merge"))
    x1, hf_in = _mm(merged, w["w_out"], "nn", add=x, rms_out=w["norm_ffn"], name=nm("mm_out"))
    hf = _mm(hf_in, w["w_ffn_in"], "nn", tn=1408, n_outer=True, out_dtype=BF16, name=nm("mm_ffn_in"))
    (act,) = _rw(lambda a, b: ([jax.nn.silu(a) * b], []), [(hf, FFN_COLS, lambda j: j), (hf, FFN_COLS, lambda j: FFN_NCOL + j)],
                 [(FFN_HIDDEN, FFN_COLS, BF16, lambda j: j)], ncol=FFN_NCOL, name=nm("swiglu"))
    x2, hp = _mm(act, w["w_ffn_out"], "nn", add=x1, tk=1408, rms_out=w["norm_ple"], name=nm("mm_ffn_out"))
    pgl = _mm(hp, w["w_ple_gate"], "nn", name=nm("mm_ple_gate"))
    pp = _mm(p_i, w["w_ple_proj"], "nn", name=nm("mm_ple_proj"))

    def ple_add(xv, a, b, *gain):
        x3v = xv + jax.nn.sigmoid(a) * b
        return [x3v] + [_rms(x3v, g) for g in gain], []

    has_next = next_gain is not None
    res = _rw(ple_add, [(x2, None, _c0), (pgl, None, _c0), (pp, None, _c0)],
              [(D_MODEL, None, F32, _c0)] + [(D_MODEL, None, BF16, _c0)] * has_next, params=[next_gain] * has_next, name=nm("ple_add"))
    x3, h_next = res[0], (res[1] if has_next else None)
    saved = dict(x=x, p=p_i, h=h, z=z, h_re=h_re, h_im=h_im, y0=y0, y1=y1, gl=gl, ys=ys, qh=qh, kh=kh, vh=vh,
                 bs=bs, merged=merged, x1=x1, hf_in=hf_in, hf=hf, act=act, x2=x2, hp=hp, pgl=pgl, pp=pp)
    return x3, h_next, saved


def _layer_bwd(dx3, s, w, bias, li):
    nm = lambda n: f"{n}_l{li}"
    g = {}
    z = s["z"]
    def ple_b(d, a, b):
        _, vjp = jax.vjp(lambda a_, b_: jax.nn.sigmoid(a_) * b_, a, b)
        return list(vjp(d)), []

    dpgl, dpp = _rw(ple_b, [(dx3, None, _c0), (s["pgl"], None, _c0), (s["pp"], None, _c0)],
                    [(D_MODEL, None, BF16, _c0)] * 2, name=nm("ple_bwd"))
    g["w_ple_proj"] = _mm(s["p"], dpp, "tn", name=nm("mmg_ple_proj"))
    g["w_ple_gate"] = _mm(s["hp"], dpgl, "tn", name=nm("mmg_ple_gate"))
    dx2, g["norm_ple"] = _mm(dpgl, w["w_ple_gate"], "nt", tm=512, rms_back=(s["x2"], dx3, w["norm_ple"]), name=nm("mmb_ple_gate"))
    dact = _mm(dx2, w["w_ffn_out"], "nt", tn=1408, out_dtype=BF16, name=nm("mmb_ffn_out"))
    g["w_ffn_out"] = _mm(s["act"], dx2, "tn", tm=1408, name=nm("mmg_ffn_out"))

    def swiglu_b(a, b, d):
        _, vjp = jax.vjp(lambda a_, b_: jax.nn.silu(a_) * b_, a, b)
        return list(vjp(d)), []

    dhf_a, dhf_b = _rw(swiglu_b, [(s["hf"], FFN_COLS, lambda j: j), (s["hf"], FFN_COLS, lambda j: FFN_NCOL + j), (dact, FFN_COLS, lambda j: j)],
                       [(FFN_HIDDEN, FFN_COLS, BF16, lambda j: j)] * 2, ncol=FFN_NCOL, name=nm("swiglu_bwd"))
    g["w_ffn_in"] = jnp.concatenate([_mm(s["hf_in"], dhf_a, "tn", tn=1408, name=nm("mmg_ffn_in_a")),
                                     _mm(s["hf_in"], dhf_b, "tn", tn=1408, name=nm("mmg_ffn_in_b"))], axis=1)
    dhf_in = _mm(dhf_a, w["w_ffn_in"], "nt", tk=1408, name=nm("mmb_ffn_in_a"))
    dx1, g["norm_ffn"] = _mm(dhf_b, w["w_ffn_in"], "nt", tm=512, tk=1408, b_k0=FFN_HIDDEN, add=dhf_in,
                             rms_back=(s["x1"], dx2, w["norm_ffn"]), name=nm("mmb_ffn_in_b"))
    dmerged = _mm(dx1, w["w_out"], "nt", out_dtype=BF16, name=nm("mmb_out"))
    g["w_out"] = _mm(s["merged"], dx1, "tn", name=nm("mmg_out"))

    def merge_b(d, g0, g1, g2, b0, b1, b2):
        outs_g, outs_b = [], []
        for gate, br in ((g0, b0), (g1, b1), (g2, b2)):
            sg = jax.nn.sigmoid(gate)
            outs_g.append(d * br * sg * (1.0 - sg))
            outs_b.append(d * sg)
        return outs_g + outs_b, []

    res = _rw(merge_b, [(dmerged, None, _c0)] + [(z, D_MODEL, _gate_col(r)) for r in range(3)] + [(b, None, _c0) for b in s["bs"]],
              [(D_MODEL, None, BF16, _c0)] * 6, name=nm("merge_bwd"))
    dgates, dbs = res[:3], res[3:]
    dys = [_mm(dbs[r], w["w_branch"][r], "nt", name=nm(f"mmb_branch{r}")) for r in range(3)]
    g["w_branch"] = jnp.stack([_mm(s["ys"][r], dbs[r], "tn", name=nm(f"mmg_branch{r}")) for r in range(3)])
    doh = _heads(dys[2], N_Q)
    dqh, dkc, dkp, dvc, dvp, dbias, dsink = _attn_bwd(s["qh"], s["kh"], s["vh"], doh, bias, w["sinks"], nm("attn_bwd"))
    dq, dk, dv = _unheads(dqh), _unheads(_shift_blocks(dkc, dkp)), _unheads(_shift_blocks(dvc, dvp))
    g["sinks"] = jnp.sum(dsink, axis=(1, 2))
    dcb, dcc, dcx, dconv = _conv_bwd(dys[1], z, w["conv_w"], nm("conv_bwd"))
    g["conv_w"] = dconv[0:3]
    def glu_b(d, y1, gl):
        sg = jax.nn.sigmoid(gl)
        return [d * y1 * sg * (1.0 - sg), d * sg], []

    dgl, dy1a = _rw(glu_b, [(dys[0], None, _c0), (s["y1"], None, _c0), (s["gl"], None, _c0)],
                    [(BRANCH, None, BF16, _c0), (BRANCH, None, F32, _c0)], name=nm("ssm_glu_bwd"))
    g["ssm_w_glu"] = _mm(s["y1"], dgl, "tn", name=nm("mmg_glu"))
    dy1b = _mm(dgl, w["ssm_w_glu"], "nt", name=nm("mmb_glu"))

    def gelu_b(da, db, a, u, d):
        _, vjp = jax.vjp(lambda pre: jax.nn.gelu(pre), a + d * u)
        (dy0,) = vjp(da + db)
        return [dy0, dy0 * d], [jnp.sum(dy0 * u, axis=0, keepdims=True)]

    dy0, du_a, g["ssm_d"] = _rw(gelu_b, [(dy1a, None, _c0), (dy1b, None, _c0), (s["y0"], None, _c0), (z, BRANCH, lambda j: U_BLK)],
                                [(BRANCH, None, BF16, _c0), (BRANCH, None, F32, _c0)], params=[w["ssm_d"]],
                                reds=[((1, BRANCH), None, _c0)], name=nm("ssm_gelu_bwd"))
    dh_re, dh_im = _bd_apply([dy0], [w["cm_re"], w["cm_imn"]], [[(0, 0)], [(0, 1)]], "nt", name=nm("ssmb_c"))
    sb, cb = SSM_STATES // SSM_BLOCKS, BRANCH // SSM_BLOCKS
    g["cm_re"], g["cm_imn"] = _bd_grads([s["h_re"], s["h_im"], dy0], [sb, sb, cb], [(0, 2), (1, 2)], name=nm("ssmg_c"))
    l_re, l_im, da_re, da_im = _scan(dh_re, dh_im, *w["scan_rev"], reverse=True, hr=s["h_re"], hi=s["h_im"], name=nm("ssm_scan_bwd"))
    g["a_re"], g["a_im"] = jnp.sum(da_re, axis=0), jnp.sum(da_im, axis=0)
    (du_b,) = _bd_apply([l_re, l_im], [w["wb_re"], w["wb_im"]], [[(0, 0), (1, 1)]], "nt", name=nm("ssmb_bu"))
    g["wb_re"], g["wb_im"] = _bd_grads([z, l_re, l_im], [cb, sb, sb], [(0, 1), (0, 2)], col_blocks=[U_BLK, 0, 0], name=nm("ssmg_bu"))
    dz = jnp.concatenate(list(dgates) + [(du_a + du_b).astype(BF16), dcb, dcc, dcx, dq.astype(BF16), dk.astype(BF16), dv.astype(BF16)], axis=1)
    g["w_in"] = _mm(s["h"], dz, "tn", tm=512, tn=2944, name=nm("mmg_in"))
    dx, g["norm_mix"] = _mm(dz, w["w_in"], "nt", tm=512, tk=2944, rms_back=(s["x"], dx1, w["norm_mix"]), name=nm("mmb_in"))
    return dx, g, dbias


def _loss_and_seed(x, target, g_final):
    def fn(xv, tv, gv):
        y, vjp = jax.vjp(_rms, xv, gv)
        err = y - tv
        dx, dg = vjp(err * (1.0 / D_MODEL))
        return [dx], [jnp.sum(err * err, axis=0, keepdims=True) * (0.5 / D_MODEL), dg]

    return _rw(fn, [(x, None, _c0), (target, None, _c0)], [(D_MODEL, None, F32, _c0)], params=[g_final],
               reds=[((1, D_MODEL), None, _c0)] * 2, name="loss_head")


def _local_step(x, p, target, wt):
    bias, bias_vjp = jax.vjp(_band_bias, wt["rel_bias"])
    layers, disc_vjps = [], []
    for i in range(DEPTH):
        ssm_p = [wt[k][i] for k in ("ssm_lambda_re", "ssm_lambda_im", "ssm_b_re", "ssm_b_im", "ssm_c_re", "ssm_c_im", "ssm_log_dt")]
        (a_re, a_im, wb_re, wb_im, cm_re, cm_imn), disc_vjp = jax.vjp(_ssm_disc, *ssm_p)
        scan_fwd, scan_rev = _scan_tables(a_re, a_im)
        layers.append(dict(
            norm_mix=wt["norm_mix"][i][None], w_in=wt["w_in"][i], wb_re=wb_re.astype(BF16), wb_im=wb_im.astype(BF16),
            cm_re=cm_re.astype(BF16), cm_imn=cm_imn.astype(BF16), scan_fwd=scan_fwd, scan_rev=scan_rev,
            ssm_d=wt["ssm_d"][i][None], ssm_w_glu=wt["ssm_w_glu"][i], conv_w=wt["conv_w"][i],
            sinks=wt["attn_sinks"][i][:, None], w_branch=wt["w_branch"][i], w_out=wt["w_out"][i],
            norm_ffn=wt["norm_ffn"][i][None], w_ffn_in=wt["w_ffn_in"][i], w_ffn_out=wt["w_ffn_out"][i],
            norm_ple=wt["norm_ple"][i][None], w_ple_gate=wt["w_ple_gate"][i], w_ple_proj=wt["w_ple_proj"][i]))
        disc_vjps.append(disc_vjp)

    saved = []
    for i in range(DEPTH):
        h = _rms_fwd(x, layers[0]["norm_mix"], "rms_mix_l0") if i == 0 else h
        x, h, s = _layer_fwd(x, h, p[i], layers[i], bias, i, layers[i + 1]["norm_mix"] if i + 1 < DEPTH else None)
        saved.append(s)
    dx, loss_cols, g_final = _loss_and_seed(x, target, wt["norm_final"][None])
    loss = jnp.sum(loss_cols)

    per_layer = [None] * DEPTH
    dbias = None
    for i in reversed(range(DEPTH)):
        dx, g, db = _layer_bwd(dx, saved[i], layers[i], bias, i)
        dbias = db if dbias is None else dbias + db
        (g["ssm_lambda_re"], g["ssm_lambda_im"], g["ssm_b_re"], g["ssm_b_im"], g["ssm_c_re"], g["ssm_c_im"], g["ssm_log_dt"]) = \
            disc_vjps[i]((g.pop("a_re"), g.pop("a_im"), g.pop("wb_re"), g.pop("wb_im"), g.pop("cm_re"), g.pop("cm_imn")))
        g["attn_sinks"] = g.pop("sinks")
        for k in ("norm_mix", "norm_ffn", "norm_ple", "ssm_d"):
            g[k] = g[k][0]
        per_layer[i] = g
    big_names = [name for name, _, _ in BIG]
    big = {k: [per_layer[i][k] for i in range(DEPTH)] for k in big_names}
    small = {k: jnp.stack([per_layer[i][k] for i in range(DEPTH)]) for k in per_layer[0] if k not in big_names}
    (small["rel_bias"],) = bias_vjp(dbias)
    small["norm_final"] = g_final[0]
    return loss, dx, small, big


HBM_SPEC = pl.BlockSpec(memory_space=pltpu.HBM)


def _position():
    x, y, c = lax.axis_index("x"), lax.axis_index("y"), lax.axis_index("c")
    other_chips = [(1 - x, y), (x, 1 - y), (1 - x, 1 - y)]
    return x, y, c, other_chips


def _row_chunks(rows, n=COPY_CHUNKS):
    rq = rows // n
    assert rq * n == rows and rq % 16 == 0, rows
    return [pl.ds(q * rq, rq) for q in range(n)]


def _place(buf, val, idx, name):
    n, rows, width = buf.shape
    tm = _pick(rows, 512, 16)

    def body(idx_ref, buf_ref, v_ref, o_ref):
        o_ref[0] = v_ref[...]

    grid_spec = pltpu.PrefetchScalarGridSpec(
        num_scalar_prefetch=1, grid=(rows // tm,),
        in_specs=[pl.BlockSpec(memory_space=pl.ANY), pl.BlockSpec((tm, width), lambda i, idx_ref: (i, 0))],
        out_specs=pl.BlockSpec((1, tm, width), lambda i, idx_ref: (idx_ref[0], i, 0)))
    return pl.pallas_call(
        body, name=name, grid_spec=grid_spec, out_shape=jax.ShapeDtypeStruct(buf.shape, buf.dtype), input_output_aliases={1: 0},
        compiler_params=pltpu.CompilerParams(dimension_semantics=("arbitrary",)),
    )(jnp.asarray(idx, jnp.int32).reshape(1), buf, val)


def _allgather_weights(locals_):
    nb, nq = len(locals_), COPY_CHUNKS
    chunks = [_row_chunks(a.shape[1]) for a in locals_]

    def body(*refs):
        w_refs, out_refs = refs[:nb], refs[nb:2 * nb]
        send_sems, recv_sems = refs[2 * nb:]
        x, y, c, chips = _position()
        me = 2 * x + y
        sibling = (x, y, 1 - c)

        def copy(b, kind, q, src, dst, to):
            k = (b * 6 + kind) * nq + q
            return pltpu.make_async_remote_copy(src_ref=src, dst_ref=dst, send_sem=send_sems.at[k], recv_sem=recv_sems.at[k],
                                                device_id=to, device_id_type=MESH)

        first = [copy(b, j, q, w_refs[b].at[c, chunks[b][q]], out_refs[b].at[me, c, chunks[b][q]], (*chip, c))
                 for q in range(nq) for b in range(nb) for j, chip in enumerate(chips)]
        for cp in first:
            cp.start()
        passed = []
        for q in range(nq):
            for b in range(nb):
                for j, (px, py) in enumerate(chips):
                    landed = out_refs[b].at[2 * px + py, c, chunks[b][q]]
                    copy(b, j, q, landed, landed, (px, py, c)).wait_recv()
                    fwd = copy(b, 3 + j, q, landed, landed, sibling)
                    fwd.start()
                    passed.append(fwd)
        for q in range(nq):
            for b in range(nb):
                for j, (px, py) in enumerate(chips):
                    landed = out_refs[b].at[2 * px + py, 1 - c, chunks[b][q]]
                    copy(b, 3 + j, q, landed, landed, sibling).wait_recv()
        for cp in first + passed:
            cp.wait_send()

    return pl.pallas_call(
        body, name="allgather_weights", in_specs=[HBM_SPEC] * nb, out_specs=[HBM_SPEC] * nb,
        out_shape=[jax.ShapeDtypeStruct((N_SHARD,) + a.shape, a.dtype) for a in locals_],
        scratch_shapes=[pltpu.SemaphoreType.DMA((nb * 6 * nq,)), pltpu.SemaphoreType.DMA((nb * 6 * nq,))],
    )(*locals_)


def _sibling_exchange(bufs):
    nb, nq, ns = len(bufs), COPY_CHUNKS, N_SHARD
    chunks = [_row_chunks(a.shape[2]) for a in bufs]

    def body(*refs):
        g_refs, got_refs = refs[:nb], refs[nb:2 * nb]
        send_sems, recv_sems = refs[2 * nb:]
        x, y, c, _ = _position()
        swaps = [pltpu.make_async_remote_copy(src_ref=g_refs[b].at[s, 1 - c, chunks[b][q]], dst_ref=got_refs[b].at[s, chunks[b][q]],
                                              send_sem=send_sems.at[(b * ns + s) * nq + q], recv_sem=recv_sems.at[(b * ns + s) * nq + q],
                                              device_id=(x, y, 1 - c), device_id_type=MESH)
                 for b in range(nb) for s in range(ns) for q in range(nq)]
        for cp in swaps:
            cp.start()
        for cp in swaps:
            cp.wait()

    return pl.pallas_call(
        body, name="grad_sibling_exchange", in_specs=[HBM_SPEC] * nb, out_specs=[HBM_SPEC] * nb,
        out_shape=[jax.ShapeDtypeStruct((ns,) + a.shape[2:], a.dtype) for a in bufs],
        scratch_shapes=[pltpu.SemaphoreType.DMA((nb * ns * nq,)), pltpu.SemaphoreType.DMA((nb * ns * nq,))],
    )(*bufs)


def _chip_exchange(parts):
    nb, nq = len(parts), COPY_CHUNKS
    chunks = [_row_chunks(a.shape[1]) for a in parts]

    def body(*refs):
        b_refs, got_refs = refs[:nb], refs[nb:2 * nb]
        send_sems, recv_sems = refs[2 * nb:]
        x, y, c, chips = _position()
        sends = [pltpu.make_async_remote_copy(src_ref=b_refs[b].at[2 * px + py, chunks[b][q]], dst_ref=got_refs[b].at[j, chunks[b][q]],
                                              send_sem=send_sems.at[(b * 3 + j) * nq + q], recv_sem=recv_sems.at[(b * 3 + j) * nq + q],
                                              device_id=(px, py, c), device_id_type=MESH)
                 for q in range(nq) for b in range(nb) for j, (px, py) in enumerate(chips)]
        for cp in sends:
            cp.start()
        for cp in sends:
            cp.wait()

    return pl.pallas_call(
        body, name="grad_chip_exchange", in_specs=[HBM_SPEC] * nb, out_specs=[HBM_SPEC] * nb,
        out_shape=[jax.ShapeDtypeStruct((N_SHARD - 1,) + a.shape[1:], a.dtype) for a in parts],
        scratch_shapes=[pltpu.SemaphoreType.DMA((nb * 3 * nq,)), pltpu.SemaphoreType.DMA((nb * 3 * nq,))],
    )(*parts)


def _sibling_gather(halves):
    nb, nq = len(halves), 2 * COPY_CHUNKS
    chunks = [_row_chunks(a.shape[0], nq) for a in halves]

    def body(*refs):
        h_refs, out_refs = refs[:nb], refs[nb:2 * nb]
        send_sems, recv_sems = refs[2 * nb:]
        x, y, c, _ = _position()

        def chunk(b, q, half_idx):
            return pltpu.make_async_remote_copy(src_ref=h_refs[b].at[chunks[b][q]], dst_ref=out_refs[b].at[half_idx, chunks[b][q]],
                                                send_sem=send_sems.at[b * nq + q], recv_sem=recv_sems.at[b * nq + q],
                                                device_id=(x, y, 1 - c), device_id_type=MESH)

        pushes = [chunk(b, q, c) for b in range(nb) for q in range(nq)]
        for cp in pushes:
            cp.start()
        for b in range(nb):
            for q in range(nq):
                chunk(b, q, 1 - c).wait_recv()
        for cp in pushes:
            cp.wait_send()

    return pl.pallas_call(
        body, name="grad_sibling_gather", in_specs=[HBM_SPEC] * nb, out_specs=[HBM_SPEC] * nb,
        out_shape=[jax.ShapeDtypeStruct((2,) + a.shape, a.dtype) for a in halves],
        scratch_shapes=[pltpu.SemaphoreType.DMA((nb * nq,)), pltpu.SemaphoreType.DMA((nb * nq,))],
    )(*halves)


def _gather_partials(part):
    r, lanes = part.shape

    def body(p_ref, out_ref, send_sems, recv_sems):
        x, y, c, _ = _position()
        flips = [(fx, fy, fc) for fx in (0, 1) for fy in (0, 1) for fc in (0, 1)][1:]
        sends = []
        for k, (fx, fy, fc) in enumerate(flips):
            cp = pltpu.make_async_remote_copy(src_ref=p_ref, dst_ref=out_ref.at[4 * x + 2 * y + c], send_sem=send_sems.at[k],
                                              recv_sem=recv_sems.at[k], device_id=(x ^ fx, y ^ fy, c ^ fc), device_id_type=MESH)
            cp.start()
            sends.append(cp)
        for k, (fx, fy, fc) in enumerate(flips):
            src = out_ref.at[4 * (x ^ fx) + 2 * (y ^ fy) + (c ^ fc)]
            pltpu.make_async_remote_copy(src_ref=src, dst_ref=src, send_sem=send_sems.at[k], recv_sem=recv_sems.at[k],
                                         device_id=(x ^ fx, y ^ fy, c ^ fc), device_id_type=MESH).wait_recv()
        for cp in sends:
            cp.wait_send()

    return pl.pallas_call(
        body, name="small_gather_partials", in_specs=[HBM_SPEC], out_specs=HBM_SPEC,
        out_shape=jax.ShapeDtypeStruct((8, r, lanes), part.dtype),
        scratch_shapes=[pltpu.SemaphoreType.DMA((7,)), pltpu.SemaphoreType.DMA((7,))],
    )(part)


def _sum_leading(stack, name, also_bf16=False):
    k, r, lanes = stack.shape
    tm = _pick(r, 256, 16)
    outs = [jax.ShapeDtypeStruct((r, lanes), F32)] + ([jax.ShapeDtypeStruct((r, lanes), BF16)] if also_bf16 else [])

    def body(s_ref, *o_refs):
        acc = s_ref[0].astype(F32)
        for i in range(1, k):
            acc = acc + s_ref[i].astype(F32)
        for o in o_refs:
            o[...] = acc.astype(o.dtype)

    spec = pl.BlockSpec((tm, lanes), lambda i: (i, 0))
    return pl.pallas_call(
        body, name=name, grid=(r // tm,), in_specs=[pl.BlockSpec((k, tm, lanes), lambda i: (0, i, 0))],
        out_specs=[spec] * len(outs), out_shape=outs,
        compiler_params=pltpu.CompilerParams(dimension_semantics=("parallel",)),
    )(stack)


def _add_pair(g2, got, half, name):
    ns, _, rows, width = g2.shape
    tm = _pick(rows, 256, 16)
    spec = pl.BlockSpec((1, tm, width), lambda s, i, h_ref: (s, i, 0))

    def body(h_ref, a_ref, b_ref, f_ref, o_ref):
        acc = a_ref[0] + b_ref[...]
        f_ref[...] = acc
        o_ref[...] = acc.astype(BF16)

    grid_spec = pltpu.PrefetchScalarGridSpec(
        num_scalar_prefetch=1, grid=(ns, rows // tm),
        in_specs=[pl.BlockSpec((1, 1, tm, width), lambda s, i, h_ref: (s, h_ref[0], i, 0)), spec], out_specs=[spec, spec])
    return pl.pallas_call(
        body, name=name, grid_spec=grid_spec,
        out_shape=[jax.ShapeDtypeStruct(got.shape, F32), jax.ShapeDtypeStruct(got.shape, BF16)],
        compiler_params=pltpu.CompilerParams(dimension_semantics=("parallel", "parallel")),
    )(jnp.asarray(half, jnp.int32).reshape(1), g2, got)


def _add_own(parts, got, mine, name):
    _, rows, width = parts.shape
    tm = _pick(rows, 256, 16)

    def body(m_ref, p_ref, g_ref, out_ref):
        acc = p_ref[0]
        for j in range(g_ref.shape[0]):
            acc = acc + g_ref[j].astype(F32)
        out_ref[...] = acc

    grid_spec = pltpu.PrefetchScalarGridSpec(
        num_scalar_prefetch=1, grid=(rows // tm,),
        in_specs=[pl.BlockSpec((1, tm, width), lambda i, m_ref: (m_ref[0], i, 0)),
                  pl.BlockSpec((got.shape[0], tm, width), lambda i, m_ref: (0, i, 0))],
        out_specs=pl.BlockSpec((tm, width), lambda i, m_ref: (i, 0)))
    return pl.pallas_call(
        body, name=name, grid_spec=grid_spec, out_shape=jax.ShapeDtypeStruct((rows, width), F32),
        compiler_params=pltpu.CompilerParams(dimension_semantics=("parallel",)),
    )(jnp.asarray(mine, jnp.int32).reshape(1), parts, got)


def _local_shape(shape, axis):
    return tuple(d // N_SHARD if a == axis else d for a, d in enumerate(shape))


def _big_sizes():
    return [DEPTH * int(np.prod(_local_shape(shape, axis))) for _, shape, axis in FLAT_BIG]


COL_SHARDED = (("w_in", IN_WIDTH // N_SHARD), ("w_ffn_in", 2 * FFN_HIDDEN // N_SHARD))
FLAT_BIG = tuple(entry for entry in BIG if entry[0] not in [name for name, _ in COL_SHARDED])
FLAT_ROW_TILE = 256
ELEMENTWISE_BIG = ("conv_w",)


def _three_bf16(w):
    hi = w.astype(BF16)
    r1 = w - hi.astype(F32)
    mid = r1.astype(BF16)
    lo = (r1 - mid.astype(F32)).astype(BF16)
    return jnp.stack([hi, mid, lo], axis=-1)


PIECE_ROWS = 16


def _flat_layout(for_weights):
    pieces = []
    for (name, shape, axis), size in zip(FLAT_BIG, _big_sizes(), strict=True):
        n = size * (3 if for_weights and name in ELEMENTWISE_BIG else 1)
        rows = -(-n // (LANES * PIECE_ROWS)) * PIECE_ROWS
        pieces.append((name, shape, axis, n, rows))
    total = sum(p[-1] for p in pieces)
    half = -(-total // (2 * FLAT_ROW_TILE)) * FLAT_ROW_TILE
    return pieces, half


def _to_rows(flat, rows):
    lead, n = flat.shape[:-1], flat.shape[-1]
    fill = jnp.zeros(lead + (rows * LANES - n,), flat.dtype)
    return jnp.concatenate([flat, fill], axis=-1).reshape(lead + (rows, LANES))


def _pack_local_weights(wl):
    pieces, half = _flat_layout(True)
    parts = [_to_rows((_three_bf16(wl[name]) if name in ELEMENTWISE_BIG else wl[name].astype(BF16)).reshape(-1), rows)
             for name, _, _, _, rows in pieces]
    parts.append(jnp.zeros((2 * half - sum(p[-1] for p in pieces), LANES), BF16))
    return jnp.concatenate(parts, axis=0).reshape(2, half, LANES)


def _unpack_local(flat):
    pieces, _ = _flat_layout(False)
    flat = flat.reshape(-1, LANES)
    out, off = {}, 0
    for name, shape, axis, n, rows in pieces:
        out[name] = flat[off:off + rows].reshape(-1)[:n].reshape((DEPTH,) + _local_shape(shape, axis))
        off += rows
    return out


def _unpack_gathered(gathered):
    pieces, _ = _flat_layout(True)
    out, off = {}, 0
    for name, shape, axis, n, rows in pieces:
        local = (N_SHARD, DEPTH) + _local_shape(shape, axis)
        seg = gathered[:, off:off + rows].reshape(N_SHARD, -1)[:, :n]
        off += rows
        if name in ELEMENTWISE_BIG:
            parts = seg.reshape(local + (3,)).astype(F32)
            seg = (parts[..., 0] + parts[..., 1]) + parts[..., 2]
        else:
            seg = seg.reshape(local)
        out[name] = jnp.moveaxis(seg, 0, 1 + axis).reshape((DEPTH,) + shape)
    return out


def _col_segments(name, c):
    if name != "w_in":
        return [(s, 0, c, c * s) for s in range(N_SHARD)]
    segs = []
    for s in range(N_SHARD):
        lo, hi = c * s, c * (s + 1)
        if lo < MAIN_WIDTH:
            segs.append((s, 0, min(hi, MAIN_WIDTH) - lo, GATES_WIDTH + lo))
        if hi > MAIN_WIDTH:
            first = max(lo, MAIN_WIDTH)
            segs.append((s, first - lo, c, first - MAIN_WIDTH))
    return segs


def _join_col_shards(shards, segs, name):
    ns, depth, rows, c = shards.shape
    tm = _pick(rows, 256, 16)

    def body(i_ref, o_ref):
        for s, lo, hi, start in segs:
            o_ref[0, :, start:start + hi - lo] = i_ref[s, 0, :, lo:hi]

    return pl.pallas_call(
        body, name=name, grid=(depth, rows // tm),
        in_specs=[pl.BlockSpec((ns, 1, tm, c), lambda l, i: (0, l, i, 0))],
        out_specs=pl.BlockSpec((1, tm, ns * c), lambda l, i: (l, i, 0)),
        out_shape=jax.ShapeDtypeStruct((depth, rows, ns * c), shards.dtype),
        compiler_params=pltpu.CompilerParams(dimension_semantics=("parallel", "parallel")),
    )(shards)


def _split_col_shards(full, segs, layer, stacked, name):
    rows, width = full.shape
    c = width // N_SHARD
    tm = _pick(rows, 256, 16)

    def body(*refs):
        i_ref, o_ref = refs[0], refs[-1]
        for s, lo, hi, start in segs:
            o_ref[s, 0, :, lo:hi] = i_ref[:, start:start + hi - lo]

    return pl.pallas_call(
        body, name=name, grid=(rows // tm,),
        in_specs=[pl.BlockSpec((tm, width), lambda i: (i, 0))] + ([] if stacked is None else [pl.BlockSpec(memory_space=pl.ANY)]),
        out_specs=pl.BlockSpec((N_SHARD, 1, tm, c), lambda i: (0, layer, i, 0)),
        out_shape=jax.ShapeDtypeStruct((N_SHARD, DEPTH, rows, c), full.dtype),
        input_output_aliases={} if stacked is None else {1: 0},
        compiler_params=pltpu.CompilerParams(dimension_semantics=("parallel",)),
    )(*([full] if stacked is None else [full, stacked]))


def _pack_full_grads(big_grads):
    pieces, half = _flat_layout(False)
    parts = []
    for name, shape, axis, _, rows in pieces:
        per_layer = []
        for gfull in big_grads[name]:
            split = gfull.reshape(shape[:axis] + (N_SHARD, shape[axis] // N_SHARD) + shape[axis + 1:])
            per_layer.append(jnp.moveaxis(split, axis, 0).reshape(N_SHARD, -1))
        parts.append(_to_rows(jnp.concatenate(per_layer, axis=1), rows))
    parts.append(jnp.zeros((N_SHARD, 2 * half - sum(p[-1] for p in pieces), LANES), F32))
    return jnp.concatenate(parts, axis=1).reshape(N_SHARD, 2, half, LANES)


def _pack_small(grads):
    flat = jnp.concatenate([grads[name].reshape(-1) for name in SMALL])
    r = -(-flat.shape[0] // (8 * LANES)) * 8
    return jnp.pad(flat, (0, r * LANES - flat.shape[0])).reshape(r, LANES)


def _unpack_small(flat, like):
    flat = flat.reshape(-1)
    out, off = {}, 0
    for name in SMALL:
        size = int(np.prod(like[name].shape))
        out[name] = flat[off:off + size].reshape(like[name].shape)
        off += size
    return out


def _adamw(w, g, m, v, name):
    shape = w.shape
    cols = shape[-1]
    rows = int(np.prod(shape[:-1])) if len(shape) > 1 else 1
    w2, g2, m2, v2 = (a.reshape(rows, cols) for a in (w, g, m, v))

    def fn(wv, gv, mv, vv):
        mn = ADAM_B1 * mv + (1.0 - ADAM_B1) * gv
        vn = ADAM_B2 * vv + (1.0 - ADAM_B2) * jnp.square(gv)
        m_hat = mn / (1.0 - ADAM_B1 ** ADAM_STEP)
        v_hat = vn / (1.0 - ADAM_B2 ** ADAM_STEP)
        delta = -ADAM_LR * (m_hat / (jnp.sqrt(v_hat) + ADAM_EPS) + ADAM_WD * wv)
        return [delta, mn, vn], []

    tm = 256 if rows % 8 == 0 and rows > 256 else rows
    res = _rw(fn, [(a, None, _c0) for a in (w2, g2, m2, v2)], [(cols, None, F32, _c0)] * 3, tm=tm, name=name)
    return [r.reshape(shape) for r in res]


def _step(x, p, target, weights, moments_m, moments_v):
    xi, yi, ci = lax.axis_index("x"), lax.axis_index("y"), lax.axis_index("c")
    chip = 2 * xi + yi
    half_rows = DEPTH // 2 * D_MODEL
    locals_ = [_pack_local_weights(weights)] + [weights[name].astype(BF16).reshape(2, half_rows, c) for name, c in COL_SHARDED]
    gathered = _allgather_weights(locals_)
    gathered = [_place(g.reshape(N_SHARD, 2 * a.shape[1], a.shape[2]), a.reshape(2 * a.shape[1], a.shape[2]), chip, f"place_own_weights_{k}")
                for k, (g, a) in enumerate(zip(gathered, locals_, strict=True))]
    wt = dict(_unpack_gathered(gathered[0]))
    for (name, c), g in zip(COL_SHARDED, gathered[1:], strict=True):
        wt[name] = _join_col_shards(g.reshape(N_SHARD, DEPTH, D_MODEL, c), _col_segments(name, c), f"join_col_shards_{name}")
    for name in SMALL:
        wt[name] = weights[name]
    loss, dx, small_grads, big_grads = _local_step(x[0], p[:, 0], target[0], wt)
    loss = lax.psum(loss, ("x", "y", "c"))
    bufs = [_pack_full_grads(big_grads)]
    for name, c in COL_SHARDED:
        stacked = None
        for li, g in enumerate(big_grads[name]):
            stacked = _split_col_shards(g, _col_segments(name, c), li, stacked, f"split_col_shards_{name}_l{li}")
        bufs.append(stacked.reshape(N_SHARD, 2, half_rows, c))
    gots = _sibling_exchange(bufs)
    sums = [_add_pair(b, g, ci, f"grad_add_sibling_{k}") for k, (b, g) in enumerate(zip(bufs, gots, strict=True))]
    others = _chip_exchange([s_bf16 for _, s_bf16 in sums])
    halves = [_add_own(s_f32, o, chip, f"grad_add_chips_{k}") for k, ((s_f32, _), o) in enumerate(zip(sums, others, strict=True))]
    both = [_place(b, h, ci, f"place_own_half_{k}") for k, (b, h) in enumerate(zip(_sibling_gather(halves), halves, strict=True))]
    reduced = _unpack_local(both[0])
    for (name, c), b in zip(COL_SHARDED, both[1:], strict=True):
        reduced[name] = b.reshape(DEPTH, D_MODEL, c)
    small_part = _pack_small(small_grads)
    small_all = _place(_gather_partials(small_part), small_part, 4 * xi + 2 * yi + ci, "place_own_small")
    reduced.update(_unpack_small(_sum_leading(small_all, "small_sum")[0], {k: weights[k] for k in SMALL}))
    outs_g, outs_d, outs_m, outs_v = [], [], [], []
    for name in WEIGHTS:
        d, mn, vn = _adamw(weights[name], reduced[name], moments_m[name], moments_v[name], f"adamw_{name}")
        outs_g.append(reduced[name])
        outs_d.append(d)
        outs_m.append(mn)
        outs_v.append(vn)
    return (loss, dx[None], *outs_g, *outs_d, *outs_m, *outs_v)


def kernel(x, p, rel_bias, norm_mix, w_in, ssm_lambda_re, ssm_lambda_im, ssm_b_re, ssm_b_im, ssm_c_re, ssm_c_im, ssm_d, ssm_log_dt, ssm_w_glu, conv_w, attn_sinks, w_branch, w_out, norm_ffn, w_ffn_in, w_ffn_out, norm_ple, w_ple_gate, w_ple_proj, norm_final, loss_target, m_rel_bias, m_norm_mix, m_w_in, m_ssm_lambda_re, m_ssm_lambda_im, m_ssm_b_re, m_ssm_b_im, m_ssm_c_re, m_ssm_c_im, m_ssm_d, m_ssm_log_dt, m_ssm_w_glu, m_conv_w, m_attn_sinks, m_w_branch, m_w_out, m_norm_ffn, m_w_ffn_in, m_w_ffn_out, m_norm_ple, m_w_ple_gate, m_w_ple_proj, m_norm_final, v_rel_bias, v_norm_mix, v_w_in, v_ssm_lambda_re, v_ssm_lambda_im, v_ssm_b_re, v_ssm_b_im, v_ssm_c_re, v_ssm_c_im, v_ssm_d, v_ssm_log_dt, v_ssm_w_glu, v_conv_w, v_attn_sinks, v_w_branch, v_w_out, v_norm_ffn, v_w_ffn_in, v_w_ffn_out, v_norm_ple, v_w_ple_gate, v_w_ple_proj, v_norm_final):
    weights = dict(rel_bias=rel_bias, norm_mix=norm_mix, w_in=w_in, ssm_lambda_re=ssm_lambda_re, ssm_lambda_im=ssm_lambda_im,
                   ssm_b_re=ssm_b_re, ssm_b_im=ssm_b_im, ssm_c_re=ssm_c_re, ssm_c_im=ssm_c_im, ssm_d=ssm_d, ssm_log_dt=ssm_log_dt,
                   ssm_w_glu=ssm_w_glu, conv_w=conv_w, attn_sinks=attn_sinks, w_branch=w_branch, w_out=w_out, norm_ffn=norm_ffn,
                   w_ffn_in=w_ffn_in, w_ffn_out=w_ffn_out, norm_ple=norm_ple, w_ple_gate=w_ple_gate, w_ple_proj=w_ple_proj,
                   norm_final=norm_final)
    moments_m = dict(rel_bias=m_rel_bias, norm_mix=m_norm_mix, w_in=m_w_in, ssm_lambda_re=m_ssm_lambda_re, ssm_lambda_im=m_ssm_lambda_im,
                     ssm_b_re=m_ssm_b_re, ssm_b_im=m_ssm_b_im, ssm_c_re=m_ssm_c_re, ssm_c_im=m_ssm_c_im, ssm_d=m_ssm_d,
                     ssm_log_dt=m_ssm_log_dt, ssm_w_glu=m_ssm_w_glu, conv_w=m_conv_w, attn_sinks=m_attn_sinks, w_branch=m_w_branch,
                     w_out=m_w_out, norm_ffn=m_norm_ffn, w_ffn_in=m_w_ffn_in, w_ffn_out=m_w_ffn_out, norm_ple=m_norm_ple,
                     w_ple_gate=m_w_ple_gate, w_ple_proj=m_w_ple_proj, norm_final=m_norm_final)
    moments_v = dict(rel_bias=v_rel_bias, norm_mix=v_norm_mix, w_in=v_w_in, ssm_lambda_re=v_ssm_lambda_re, ssm_lambda_im=v_ssm_lambda_im,
                     ssm_b_re=v_ssm_b_re, ssm_b_im=v_ssm_b_im, ssm_c_re=v_ssm_c_re, ssm_c_im=v_ssm_c_im, ssm_d=v_ssm_d,
                     ssm_log_dt=v_ssm_log_dt, ssm_w_glu=v_ssm_w_glu, conv_w=v_conv_w, attn_sinks=v_attn_sinks, w_branch=v_w_branch,
                     w_out=v_w_out, norm_ffn=v_norm_ffn, w_ffn_in=v_w_ffn_in, w_ffn_out=v_w_ffn_out, norm_ple=v_norm_ple,
                     w_ple_gate=v_w_ple_gate, w_ple_proj=v_w_ple_proj, norm_final=v_norm_final)
    return _step(x, p, loss_target, weights, moments_m, moments_v)
```

```python
import functools
import math

import numpy as np

import jax
import jax.numpy as jnp
from jax import lax
from jax.experimental import pallas as pl
from jax.experimental.pallas import tpu as pltpu

F32, BF16 = jnp.float32, jnp.bfloat16
MESH = pl.DeviceIdType.MESH

D_MODEL = 1024
DEPTH = 4
PLE_DIM = 256
BRANCH = 512
N_GROUPS = 32
GROUP_CH = 16
N_STATE = 64
SSM_STATES = N_GROUPS * N_STATE
SSM_BLOCKS = 4
HEAD_DIM = 64
N_Q = 8
N_KV = 2
GQA = N_Q // N_KV
WINDOW = 128
ATTN_SCALE = 1.0 / math.sqrt(HEAD_DIM)
REL_BUCKETS = 32
REL_MAX_DIST = 128
FFN_HIDDEN = 2816
FFN_COLS = 1408
FFN_NCOL = FFN_HIDDEN // FFN_COLS
IN_WIDTH = 5888
RMS_EPS = 1e-6
NEG = -1e30

ADAM_LR, ADAM_B1, ADAM_B2, ADAM_EPS, ADAM_WD, ADAM_STEP = 0.001, 0.9, 0.999, 1e-08, 0.01, 10

N_SHARD = 4
LANES = 1024
COPY_CHUNKS = 4

GATES_WIDTH = 3 * D_MODEL
MAIN_WIDTH = IN_WIDTH - GATES_WIDTH
OFF_G, OFF_U, OFF_CB, OFF_CC, OFF_CX, OFF_Q, OFF_K, OFF_V = 0, 3072, 3584, 4096, 4608, 5120, 5632, 5760
U_BLK, CB_BLK, CC_BLK, CX_BLK = OFF_U // BRANCH, OFF_CB // BRANCH, OFF_CC // BRANCH, OFF_CX // BRANCH

BIG = (
    ("w_in", (D_MODEL, IN_WIDTH), 1),
    ("ssm_w_glu", (BRANCH, BRANCH), 0),
    ("conv_w", (3, BRANCH), 1),
    ("w_branch", (3, BRANCH, D_MODEL), 2),
    ("w_out", (D_MODEL, D_MODEL), 0),
    ("w_ffn_in", (D_MODEL, 2 * FFN_HIDDEN), 1),
    ("w_ffn_out", (FFN_HIDDEN, D_MODEL), 0),
    ("w_ple_gate", (D_MODEL, D_MODEL), 0),
    ("w_ple_proj", (PLE_DIM, D_MODEL), 1),
)
SMALL = ("rel_bias", "norm_mix", "ssm_lambda_re", "ssm_lambda_im", "ssm_b_re", "ssm_b_im", "ssm_c_re", "ssm_c_im",
         "ssm_d", "ssm_log_dt", "attn_sinks", "norm_ffn", "norm_ple", "norm_final")
WEIGHTS = ("rel_bias", "norm_mix", "w_in", "ssm_lambda_re", "ssm_lambda_im", "ssm_b_re", "ssm_b_im", "ssm_c_re",
           "ssm_c_im", "ssm_d", "ssm_log_dt", "ssm_w_glu", "conv_w", "attn_sinks", "w_branch", "w_out", "norm_ffn",
           "w_ffn_in", "w_ffn_out", "norm_ple", "w_ple_gate", "w_ple_proj", "norm_final")


def _c0(j):
    return 0


def _pick(n, cap, unit=128):
    if n <= cap:
        return n
    best = None
    for t in range(unit, cap + 1, unit):
        if n % t == 0:
            best = t
    assert best is not None, (n, cap, unit)
    return best


_DIMS = {"nn": ((1,), (0,)), "nt": ((1,), (1,)), "tn": ((0,), (0,))}


def _mm(a, b, mode, *, name, out_dtype=F32, add=None, tm=1024, tn=1024, tk=1024, b_k0=0, n_outer=False, rms_out=None, rms_back=None):
    if mode == "nn":
        (m, k), (k2, n) = a.shape, b.shape
    elif mode == "nt":
        (m, k), (n, k2) = a.shape, b.shape
    else:
        (k, m), (k2, n) = a.shape, b.shape
    assert k == k2 or (mode == "nt" and b_k0 + k <= k2), (a.shape, b.shape, mode)
    tm, tn, tk = _pick(m, tm, 128 if mode == "tn" else 8), _pick(n, tn), _pick(k, tk, 128 if mode != "tn" else 8)
    nk = k // tk
    assert b_k0 % tk == 0
    kb0 = b_k0 // tk
    def at(f):
        return (lambda j, i, kk: f(i, j, kk)) if n_outer else f

    a_spec = pl.BlockSpec((tk, tm), at(lambda i, j, kk: (kk, i))) if mode == "tn" else pl.BlockSpec((tm, tk), at(lambda i, j, kk: (i, kk)))
    b_spec = (pl.BlockSpec((tn, tk), at(lambda i, j, kk: (j, kb0 + kk))) if mode == "nt"
              else pl.BlockSpec((tk, tn), at(lambda i, j, kk: (kk, j))))
    o_spec = pl.BlockSpec((tm, tn), at(lambda i, j, kk: (i, j)))
    dims = (_DIMS[mode], ((), ()))
    has_add = add is not None
    fused_rows = rms_out is not None or rms_back is not None
    assert not fused_rows or (tn == n and not n_outer), "a fused RMSNorm needs whole rows in a tile"
    row_spec = pl.BlockSpec((1, n), at(lambda i, j, kk: (0, 0)))
    extra = [rms_out] if rms_out is not None else (list(rms_back) if rms_back is not None else [])
    extra_specs = [row_spec] if rms_out is not None else ([o_spec, o_spec, row_spec] if rms_back is not None else [])
    n_in = 2 + has_add + len(extra)

    def body(*refs):
        a_ref, b_ref = refs[0], refs[1]
        add_ref = refs[2] if has_add else None
        x_refs = refs[2 + has_add:n_in]
        o_refs, acc_ref = refs[n_in:-1], refs[-1]
        part = lax.dot_general(a_ref[...].astype(BF16), b_ref[...].astype(BF16), dims, preferred_element_type=F32)

        def finish(acc):
            if has_add:
                acc = acc + add_ref[...]
            if rms_back is not None:
                _, vjp = jax.vjp(_rms, x_refs[0][...], x_refs[2][...])
                dx, dg = vjp(acc)
                o_refs[0][...] = x_refs[1][...] + dx
                first = pl.program_id(0) == 0

                @pl.when(first)
                def _():
                    o_refs[1][...] = dg

                @pl.when(jnp.logical_not(first))
                def _():
                    o_refs[1][...] += dg
                return
            o_refs[0][...] = acc.astype(o_refs[0].dtype)
            if rms_out is not None:
                o_refs[1][...] = _rms(acc, x_refs[0][...]).astype(BF16)

        if nk == 1:
            finish(part)
        else:
            kk = pl.program_id(2)

            @pl.when(kk == 0)
            def _():
                acc_ref[...] = part

            @pl.when(kk > 0)
            def _():
                acc_ref[...] += part

            @pl.when(kk == nk - 1)
            def _():
                finish(acc_ref[...])

    operands = [a, b] + ([add] if has_add else []) + extra
    in_specs = [a_spec, b_spec] + ([o_spec] if has_add else []) + extra_specs
    out_specs, out_shape = [o_spec], [jax.ShapeDtypeStruct((m, n), out_dtype)]
    if rms_out is not None:
        out_specs, out_shape = out_specs + [o_spec], out_shape + [jax.ShapeDtypeStruct((m, n), BF16)]
    if rms_back is not None:
        out_specs, out_shape = out_specs + [row_spec], out_shape + [jax.ShapeDtypeStruct((1, n), F32)]
    res = pl.pallas_call(
        body, name=name, grid=(n // tn, m // tm, nk) if n_outer else (m // tm, n // tn, nk), in_specs=in_specs, out_specs=out_specs,
        out_shape=out_shape, scratch_shapes=[pltpu.VMEM((tm, tn) if nk > 1 else (8, 128), F32)],
        compiler_params=pltpu.CompilerParams(
            dimension_semantics=("arbitrary",) * 3 if rms_back is not None else ("parallel", "parallel", "arbitrary")),
    )(*operands)
    return res if fused_rows else res[0]


def _rw(fn, ins, outs, *, name, params=(), reds=(), tm=256, ncol=1, with_j=False):
    t = ins[0][0].shape[0]
    tm = _pick(t, tm, 8)
    nrow = t // tm
    n_in, n_p, n_out = len(ins), len(params), len(outs)

    in_specs = [pl.BlockSpec((tm, bw or arr.shape[1]), lambda j, i, cf=cf: (i, cf(j))) for arr, bw, cf in ins]
    in_specs += [pl.BlockSpec(p.shape, lambda j, i: (0, 0)) for p in params]
    out_specs = [pl.BlockSpec((tm, bw or w), lambda j, i, cf=cf: (i, cf(j))) for w, bw, _, cf in outs]
    out_specs += [pl.BlockSpec((shp[0], bw or shp[1]), lambda j, i, cf=cf: (0, cf(j))) for shp, bw, cf in reds]
    out_shape = [jax.ShapeDtypeStruct((t, w), dt) for w, _, dt, _ in outs]
    out_shape += [jax.ShapeDtypeStruct(shp, F32) for shp, _, _ in reds]

    def body(*refs):
        in_refs, p_refs = refs[:n_in], refs[n_in:n_in + n_p]
        o_refs, r_refs = refs[n_in + n_p:n_in + n_p + n_out], refs[n_in + n_p + n_out:]
        args = [r[...].astype(F32) for r in in_refs] + [r[...] for r in p_refs]
        if with_j:
            args = [pl.program_id(0)] + args
        o_vals, r_vals = fn(*args)
        for r, v in zip(o_refs, o_vals, strict=True):
            r[...] = v.astype(r.dtype)
        if r_refs:
            i = pl.program_id(1)
            for r, v in zip(r_refs, r_vals, strict=True):
                @pl.when(i == 0)
                def _(r=r, v=v):
                    r[...] = v

                @pl.when(i > 0)
                def _(r=r, v=v):
                    r[...] += v

    res = pl.pallas_call(
        body, name=name, grid=(ncol, nrow), in_specs=in_specs, out_specs=out_specs, out_shape=out_shape,
        compiler_params=pltpu.CompilerParams(dimension_semantics=("parallel", "arbitrary" if reds else "parallel")),
    )(*[a for a, _, _ in ins], *params)
    return res


def _rms(x, g):
    return x * lax.rsqrt(jnp.mean(x * x, axis=-1, keepdims=True) + RMS_EPS) * g


def _rms_fwd(x, g, name):
    return _rw(lambda xv, gv: ([_rms(xv, gv)], []), [(x, None, _c0)], [(D_MODEL, None, BF16, _c0)], params=[g], name=name)[0]


def _bd_apply(acts, mats, combos, mode, *, name, tm=512, col_blocks=None):
    t = acts[0].shape[0]
    tm = _pick(t, tm, 8)
    nb, r, c = mats[0].shape
    win, wout = (r, c) if mode == "nn" else (c, r)
    dims = (_DIMS[mode], ((), ()))
    n_a, n_m = len(acts), len(mats)

    def body(*refs):
        a_vals = [ar[...].astype(BF16) for ar in refs[:n_a]]
        m_refs, o_refs = refs[n_a:n_a + n_m], refs[n_a + n_m:]
        for o_ref, terms in zip(o_refs, combos, strict=True):
            for j in range(nb):
                acc = None
                for ai, mi in terms:
                    part = lax.dot_general(a_vals[ai][:, j * win:(j + 1) * win], m_refs[mi][j], dims, preferred_element_type=F32)
                    acc = part if acc is None else acc + part
                o_ref[:, j * wout:(j + 1) * wout] = acc

    return pl.pallas_call(
        body, name=name, grid=(t // tm,),
        in_specs=[pl.BlockSpec((tm, nb * win), lambda i, cb=cb: (i, cb)) for cb in (col_blocks or [0] * n_a)]
        + [pl.BlockSpec(m.shape, lambda i: (0, 0, 0)) for m in mats],
        out_specs=[pl.BlockSpec((tm, nb * wout), lambda i: (i, 0))] * len(combos),
        out_shape=[jax.ShapeDtypeStruct((t, nb * wout), F32)] * len(combos),
        compiler_params=pltpu.CompilerParams(dimension_semantics=("parallel",)),
    )(*acts, *mats)


def _bd_grads(arrs, widths, pairs, *, name, tk=512, col_blocks=None):
    t = arrs[0].shape[0]
    tk = _pick(t, tk, 8)
    n_a = len(arrs)
    dims = (_DIMS["tn"], ((), ()))

    def body(*refs):
        vals = [ar[...].astype(BF16) for ar in refs[:n_a]]
        o_refs = refs[n_a:]
        @pl.when(pl.program_id(0) == 0)
        def _():
            for o_ref in o_refs:
                o_ref[...] = jnp.zeros_like(o_ref)

        for o_ref, (ai, bi) in zip(o_refs, pairs, strict=True):
            wa, wb = widths[ai], widths[bi]
            for j in range(SSM_BLOCKS):
                o_ref[j] += lax.dot_general(vals[ai][:, j * wa:(j + 1) * wa], vals[bi][:, j * wb:(j + 1) * wb], dims,
                                            preferred_element_type=F32)

    return pl.pallas_call(
        body, name=name, grid=(t // tk,),
        in_specs=[pl.BlockSpec((tk, SSM_BLOCKS * w), lambda k, cb=cb: (k, cb)) for w, cb in zip(widths, col_blocks or [0] * n_a, strict=True)],
        out_specs=[pl.BlockSpec((SSM_BLOCKS, widths[ai], widths[bi]), lambda k: (0, 0, 0)) for ai, bi in pairs],
        out_shape=[jax.ShapeDtypeStruct((SSM_BLOCKS, widths[ai], widths[bi]), F32) for ai, bi in pairs],
        compiler_params=pltpu.CompilerParams(dimension_semantics=("arbitrary",)),
    )(*arrs)


SCAN_LW = 512
SCAN_ROWS = 512
_DOUBLING = ((1, 0), (2, 1), (4, 2))


def _scan(xr, xi, pr, pi, dr, di, *, reverse, name, hr=None, hi=None):
    t, s = xr.shape
    lc = _pick(t, SCAN_ROWS, 8)
    nt, ngroups = t // lc, lc // 8
    with_da = hr is not None

    def tmap(l, tt):
        return ((nt - 1 - tt) if reverse else tt, l)

    x_spec = pl.BlockSpec((lc, SCAN_LW), tmap)
    tab_spec = pl.BlockSpec((8, SCAN_LW), lambda l, tt: (0, l))

    def body(*refs):
        xr_ref, xi_ref, pr_ref, pi_ref, dr_ref, di_ref = refs[:6]
        if with_da:
            hr_ref, hi_ref, or_ref, oi_ref, ar_ref, ai_ref, cr_ref, ci_ref = refs[6:]
        else:
            or_ref, oi_ref, cr_ref, ci_ref = refs[6:]
        tt = pl.program_id(1)

        @pl.when(tt == 0)
        def _():
            cr_ref[...] = jnp.zeros_like(cr_ref)
            ci_ref[...] = jnp.zeros_like(ci_ref)
            if with_da:
                ar_ref[...] = jnp.zeros_like(ar_ref)
                ai_ref[...] = jnp.zeros_like(ai_ref)

        sub = lax.broadcasted_iota(jnp.int32, (8, SCAN_LW), 0)
        pw_r, pw_i = pr_ref[...], pi_ref[...]

        def step(g, carry):
            g = (ngroups - 1 - g) if reverse else g
            r0 = pl.multiple_of(g * 8, 8)
            vr, vi = xr_ref[pl.ds(r0, 8), :], xi_ref[pl.ds(r0, 8), :]
            for shift, row in _DOUBLING:
                a_r, a_i = dr_ref[row:row + 1, :], di_ref[row:row + 1, :]
                if reverse:
                    keep = sub < 8 - shift
                    sr, si = pltpu.roll(vr, 8 - shift, 0), pltpu.roll(vi, 8 - shift, 0)
                else:
                    keep = sub >= shift
                    sr, si = pltpu.roll(vr, shift, 0), pltpu.roll(vi, shift, 0)
                sr, si = jnp.where(keep, sr, 0.0), jnp.where(keep, si, 0.0)
                vr, vi = vr + a_r * sr - a_i * si, vi + a_r * si + a_i * sr
            if with_da:
                cr, ci, acc_r, acc_i = carry
            else:
                cr, ci = carry
            vr, vi = vr + pw_r * cr - pw_i * ci, vi + pw_r * ci + pw_i * cr
            or_ref[pl.ds(r0, 8), :] = vr
            oi_ref[pl.ds(r0, 8), :] = vi
            if with_da:
                nr = jnp.where(sub < 7, pltpu.roll(vr, 7, 0), cr)
                ni = jnp.where(sub < 7, pltpu.roll(vi, 7, 0), ci)
                h_r, h_i = hr_ref[pl.ds(r0, 8), :], hi_ref[pl.ds(r0, 8), :]
                acc_r = acc_r + h_r * nr + h_i * ni
                acc_i = acc_i + h_r * ni - h_i * nr
            edge = 0 if reverse else 7
            cr = jnp.broadcast_to(vr[edge:edge + 1, :], vr.shape)
            ci = jnp.broadcast_to(vi[edge:edge + 1, :], vi.shape)
            return (cr, ci, acc_r, acc_i) if with_da else (cr, ci)

        zero = jnp.zeros((8, SCAN_LW), F32)
        init = (cr_ref[...], ci_ref[...]) + ((zero, zero) if with_da else ())
        fin = lax.fori_loop(0, ngroups, step, init, unroll=2)
        cr_ref[...] = fin[0]
        ci_ref[...] = fin[1]
        if with_da:
            ar_ref[...] += fin[2]
            ai_ref[...] += fin[3]

    n_x = 4 if with_da else 2
    out_specs = [x_spec, x_spec] + ([tab_spec, tab_spec] if with_da else [])
    out_shape = [jax.ShapeDtypeStruct((t, s), F32)] * 2 + ([jax.ShapeDtypeStruct((8, s), F32)] * 2 if with_da else [])
    operands = [xr, xi, pr, pi, dr, di] + ([hr, hi] if with_da else [])
    return pl.pallas_call(
        body, name=name, grid=(s // SCAN_LW, nt),
        in_specs=[x_spec, x_spec] + [tab_spec] * 4 + [x_spec] * (n_x - 2),
        out_specs=out_specs, out_shape=out_shape,
        scratch_shapes=[pltpu.VMEM((8, SCAN_LW), F32), pltpu.VMEM((8, SCAN_LW), F32)],
        compiler_params=pltpu.CompilerParams(dimension_semantics=("parallel", "arbitrary")),
    )(*operands)


CONV_TM = 256
HALO = 16


def _conv_specs(t, tm):
    nrow = t // tm
    hb = tm // HALO

    def col(cidx):
        return pl.BlockSpec((tm, BRANCH), lambda i: (i, cidx))

    def prev(cidx):
        return pl.BlockSpec((HALO, BRANCH), lambda i: (jnp.maximum(i * hb - 1, 0), cidx))

    def nxt(cidx):
        return pl.BlockSpec((HALO, BRANCH), lambda i: (jnp.minimum((i + 1) * hb, nrow * hb - 1), cidx))

    return nrow, col, prev, nxt


def _conv_taps(cc, cx, cc_prev, cx_prev, first):
    tm = cc.shape[0]
    v = cc * cx
    halo = cc_prev * cx_prev * jnp.where(first, 0.0, 1.0)
    ext = jnp.concatenate([halo, v], axis=0)
    return v, pltpu.roll(ext, 1, 0)[HALO:HALO + tm], pltpu.roll(ext, 2, 0)[HALO:HALO + tm]


def _conv_fwd(z, conv_w, name):
    t = z.shape[0]
    tm = _pick(t, CONV_TM, HALO)
    nrow, col, prev, _ = _conv_specs(t, tm)

    def body(cb_ref, cc_ref, cx_ref, ccp_ref, cxp_ref, w_ref, o_ref):
        first = pl.program_id(0) == 0
        v, v1, v2 = _conv_taps(*(r[...].astype(F32) for r in (cc_ref, cx_ref, ccp_ref, cxp_ref)), first)
        y = w_ref[0:1, :] * v2 + w_ref[1:2, :] * v1 + w_ref[2:3, :] * v
        o_ref[...] = (cb_ref[...].astype(F32) * y).astype(o_ref.dtype)

    return pl.pallas_call(
        body, name=name, grid=(nrow,),
        in_specs=[col(CB_BLK), col(CC_BLK), col(CX_BLK), prev(CC_BLK), prev(CX_BLK), pl.BlockSpec((3, BRANCH), lambda i: (0, 0))],
        out_specs=pl.BlockSpec((tm, BRANCH), lambda i: (i, 0)), out_shape=jax.ShapeDtypeStruct((t, BRANCH), BF16),
        compiler_params=pltpu.CompilerParams(dimension_semantics=("parallel",)),
    )(z, z, z, z, z, conv_w)


def _conv_bwd(dyc, z, conv_w, name):
    t = z.shape[0]
    tm = _pick(t, CONV_TM, HALO)
    nrow, col, prev, nxt = _conv_specs(t, tm)
    d_cur = pl.BlockSpec((tm, BRANCH), lambda i: (i, 0))
    d_nxt = pl.BlockSpec((HALO, BRANCH), lambda i: (jnp.minimum((i + 1) * (tm // HALO), nrow * (tm // HALO) - 1), 0))

    def body(dy_ref, dyn_ref, cb_ref, cbn_ref, cc_ref, cx_ref, ccp_ref, cxp_ref, w_ref, dcb_ref, dcc_ref, dcx_ref, dw_ref):
        i = pl.program_id(0)
        cc, cx, cb = cc_ref[...].astype(F32), cx_ref[...].astype(F32), cb_ref[...].astype(F32)
        v, v1, v2 = _conv_taps(cc, cx, ccp_ref[...].astype(F32), cxp_ref[...].astype(F32), i == 0)
        w0, w1, w2 = w_ref[0:1, :], w_ref[1:2, :], w_ref[2:3, :]
        y = w0 * v2 + w1 * v1 + w2 * v
        dyc_v = dy_ref[...].astype(F32)
        dcb_ref[...] = (dyc_v * y).astype(dcb_ref.dtype)
        dy = dyc_v * cb
        halo = dyn_ref[...].astype(F32) * cbn_ref[...].astype(F32) * jnp.where(i == nrow - 1, 0.0, 1.0)
        ext = jnp.concatenate([dy, halo], axis=0)
        dy1 = pltpu.roll(ext, tm + HALO - 1, 0)[0:tm]
        dy2 = pltpu.roll(ext, tm + HALO - 2, 0)[0:tm]
        dv = w2 * dy + w1 * dy1 + w0 * dy2
        dcc_ref[...] = (dv * cx).astype(dcc_ref.dtype)
        dcx_ref[...] = (dv * cc).astype(dcx_ref.dtype)
        dw = jnp.concatenate([jnp.sum(dy * v2, axis=0, keepdims=True), jnp.sum(dy * v1, axis=0, keepdims=True),
                              jnp.sum(dy * v, axis=0, keepdims=True), jnp.zeros((5, BRANCH), F32)], axis=0)

        @pl.when(i == 0)
        def _():
            dw_ref[...] = dw

        @pl.when(i > 0)
        def _():
            dw_ref[...] += dw

    o_spec = pl.BlockSpec((tm, BRANCH), lambda i: (i, 0))
    return pl.pallas_call(
        body, name=name, grid=(nrow,),
        in_specs=[d_cur, d_nxt, col(CB_BLK), nxt(CB_BLK), col(CC_BLK), col(CX_BLK), prev(CC_BLK), prev(CX_BLK),
                  pl.BlockSpec((3, BRANCH), lambda i: (0, 0))],
        out_specs=[o_spec, o_spec, o_spec, pl.BlockSpec((8, BRANCH), lambda i: (0, 0))],
        out_shape=[jax.ShapeDtypeStruct((t, BRANCH), BF16)] * 3 + [jax.ShapeDtypeStruct((8, BRANCH), F32)],
        compiler_params=pltpu.CompilerParams(dimension_semantics=("arbitrary",)),
    )(dyc, dyc, z, z, z, z, z, z, conv_w)


ATTN_BLOCKS = 8
ATTN_BLOCKS_BWD = 1
GROUP_ROWS = GQA * WINDOW


def _attn_specs(nblk):
    rows = nblk * WINDOW
    q_spec = pl.BlockSpec((N_Q, rows, HEAD_DIM), lambda n: (0, n, 0))
    kv_cur = pl.BlockSpec((N_KV, rows, HEAD_DIM), lambda n: (0, n, 0))
    kv_prev = pl.BlockSpec((N_KV, WINDOW, HEAD_DIM), lambda n: (0, jnp.maximum(n * nblk - 1, 0), 0))
    bias_spec = pl.BlockSpec((N_Q, WINDOW, 2 * WINDOW), lambda n: (0, 0, 0))
    sink_spec = pl.BlockSpec((N_Q, 1), lambda n: (0, 0))
    return q_spec, kv_cur, kv_prev, bias_spec, sink_spec


def _attn_valid(first_key):
    qi = lax.broadcasted_iota(jnp.int32, (GROUP_ROWS, 2 * WINDOW), 0) & (WINDOW - 1)
    kj = lax.broadcasted_iota(jnp.int32, (GROUP_ROWS, 2 * WINDOW), 1)
    dist = qi + WINDOW - kj
    return (dist >= 0) & (dist < WINDOW) & (kj >= first_key)


def _attn_masks(n):
    return _attn_valid(jnp.where(n > 0, 0, WINDOW)), _attn_valid(0)


def _blk(b):
    return slice(b * WINDOW, (b + 1) * WINDOW)


def _group(ref, h, b, width):
    return ref[GQA * h:GQA * (h + 1), _blk(b)].reshape(GROUP_ROWS, width)


def _keys(prev_ref, cur_ref, h, b):
    prev = prev_ref[h] if b == 0 else cur_ref[h, _blk(b - 1)]
    return jnp.concatenate([prev, cur_ref[h, _blk(b)]], axis=0)


def _group_sinks(s_ref, h):
    return jnp.concatenate([jnp.broadcast_to(s_ref[GQA * h + g:GQA * h + g + 1, :], (WINDOW, 1)) for g in range(GQA)], axis=0)


def _attn_probs(q, kc, bias, sink, valid):
    s = lax.dot_general(q, kc, (_DIMS["nt"], ((), ())), preferred_element_type=F32) * ATTN_SCALE + bias
    s = jnp.where(valid, s, NEG)
    m = jnp.maximum(jnp.max(s, axis=1, keepdims=True), sink)
    p = jnp.exp(s - m)
    e_sink = jnp.exp(sink - m)
    inv = 1.0 / (jnp.sum(p, axis=1, keepdims=True) + e_sink)
    return p * inv, e_sink * inv


def _attn_fwd(qh, kh, vh, bias, sinks, name):
    t = qh.shape[1]
    nblk = min(ATTN_BLOCKS, t // WINDOW)
    q_spec, kv_cur, kv_prev, bias_spec, sink_spec = _attn_specs(nblk)

    def body(q_ref, kp_ref, kc_ref, vp_ref, vc_ref, b_ref, s_ref, o_ref):
        masks = _attn_masks(pl.program_id(0))
        for b in range(nblk):
            for h in range(N_KV):
                kc, vc = _keys(kp_ref, kc_ref, h, b), _keys(vp_ref, vc_ref, h, b)
                w, _ = _attn_probs(_group(q_ref, h, b, HEAD_DIM), kc, _group(b_ref, h, 0, 2 * WINDOW), _group_sinks(s_ref, h),
                                   masks[min(b, 1)])
                o = jnp.dot(w.astype(BF16), vc, preferred_element_type=F32)
                o_ref[GQA * h:GQA * (h + 1), _blk(b)] = o.reshape(GQA, WINDOW, HEAD_DIM).astype(o_ref.dtype)

    return pl.pallas_call(
        body, name=name, grid=(t // (nblk * WINDOW),),
        in_specs=[q_spec, kv_prev, kv_cur, kv_prev, kv_cur, bias_spec, sink_spec],
        out_specs=q_spec, out_shape=jax.ShapeDtypeStruct((N_Q, t, HEAD_DIM), BF16),
        compiler_params=pltpu.CompilerParams(dimension_semantics=("parallel",)),
    )(qh, kh, kh, vh, vh, bias, sinks)


def _attn_bwd(qh, kh, vh, doh, bias, sinks, name):
    t = qh.shape[1]
    nblk = min(ATTN_BLOCKS_BWD, t // WINDOW)
    nsteps = t // (nblk * WINDOW)
    q_spec, kv_cur, kv_prev, bias_spec, sink_spec = _attn_specs(nblk)

    def body(q_ref, kp_ref, kc_ref, vp_ref, vc_ref, do_ref, b_ref, s_ref,
             dq_ref, dkc_ref, dkp_ref, dvc_ref, dvp_ref, db_ref, ds_ref):
        n = pl.program_id(0)
        masks = _attn_masks(n)

        @pl.when(n == 0)
        def _():
            db_ref[...] = jnp.zeros_like(db_ref)
            ds_ref[...] = jnp.zeros_like(ds_ref)

        def scatter(part, h, b, cur_ref, prev_ref):
            if b == 0:
                prev_ref[h] = part[0:WINDOW]
            else:
                cur_ref[h, _blk(b - 1)] += part[0:WINDOW]
            cur_ref[h, _blk(b)] = part[WINDOW:2 * WINDOW]

        d_bias, d_sink = [None] * N_KV, [None] * N_KV
        for b in range(nblk):
            for h in range(N_KV):
                kc, vc = _keys(kp_ref, kc_ref, h, b), _keys(vp_ref, vc_ref, h, b)
                heads = slice(GQA * h, GQA * (h + 1))
                q, do = _group(q_ref, h, b, HEAD_DIM), _group(do_ref, h, b, HEAD_DIM)
                w, w_sink = _attn_probs(q, kc, _group(b_ref, h, 0, 2 * WINDOW), _group_sinks(s_ref, h), masks[min(b, 1)])
                dw = lax.dot_general(do, vc, (_DIMS["nt"], ((), ())), preferred_element_type=F32)
                delta = jnp.sum(w * dw, axis=1, keepdims=True)
                dscore = w * (dw - delta)
                d_sink[h] = -w_sink * delta if b == 0 else d_sink[h] - w_sink * delta
                d_bias[h] = dscore if b == 0 else d_bias[h] + dscore
                dsb = dscore.astype(BF16)
                dq_ref[heads, _blk(b)] = (jnp.dot(dsb, kc, preferred_element_type=F32) * ATTN_SCALE).reshape(GQA, WINDOW, HEAD_DIM)
                scatter(lax.dot_general(dsb, q, (_DIMS["tn"], ((), ())), preferred_element_type=F32) * ATTN_SCALE, h, b, dkc_ref, dkp_ref)
                scatter(lax.dot_general(w.astype(BF16), do, (_DIMS["tn"], ((), ())), preferred_element_type=F32), h, b, dvc_ref, dvp_ref)
        for h in range(N_KV):
            heads = slice(GQA * h, GQA * (h + 1))
            ds_ref[heads] += d_sink[h].reshape(GQA, WINDOW, 1)
            db_ref[heads] += d_bias[h].reshape(GQA, WINDOW, 2 * WINDOW)

    kv_shape = jax.ShapeDtypeStruct((N_KV, t, HEAD_DIM), F32)
    kv_prev_out = pl.BlockSpec((N_KV, WINDOW, HEAD_DIM), lambda n: (0, n, 0))
    kv_prev_shape = jax.ShapeDtypeStruct((N_KV, nsteps * WINDOW, HEAD_DIM), F32)
    return pl.pallas_call(
        body, name=name, grid=(nsteps,),
        in_specs=[q_spec, kv_prev, kv_cur, kv_prev, kv_cur, q_spec, bias_spec, sink_spec],
        out_specs=[q_spec, kv_cur, kv_prev_out, kv_cur, kv_prev_out, bias_spec, pl.BlockSpec((N_Q, WINDOW, 1), lambda n: (0, 0, 0))],
        out_shape=[jax.ShapeDtypeStruct((N_Q, t, HEAD_DIM), F32), kv_shape, kv_prev_shape, kv_shape, kv_prev_shape,
                   jax.ShapeDtypeStruct((N_Q, WINDOW, 2 * WINDOW), F32), jax.ShapeDtypeStruct((N_Q, WINDOW, 1), F32)],
        compiler_params=pltpu.CompilerParams(dimension_semantics=("arbitrary",)),
    )(qh, kh, kh, vh, vh, doh, bias, sinks)


def _heads(a, n_heads):
    t = a.shape[0]
    return a.astype(BF16).reshape(t, n_heads, HEAD_DIM).transpose(1, 0, 2)


def _unheads(a):
    n_heads, t, _ = a.shape
    return a.transpose(1, 0, 2).reshape(t, n_heads * HEAD_DIM)


def _shift_blocks(cur, prev):
    n_kv, t, d = cur.shape
    nsteps = prev.shape[1] // WINDOW
    nblk = t // (nsteps * WINDOW)
    late = jnp.concatenate([prev.reshape(n_kv, nsteps, WINDOW, d)[:, 1:], jnp.zeros((n_kv, 1, WINDOW, d), cur.dtype)], axis=1)
    delta = jnp.concatenate([jnp.zeros((n_kv, nsteps, nblk - 1, WINDOW, d), cur.dtype), late[:, :, None]], axis=2)
    return (cur.reshape(n_kv, nsteps, nblk, WINDOW, d) + delta).reshape(n_kv, t, d)


def _t5_bucket_table():
    qi = np.arange(WINDOW)[:, None]
    kj = np.arange(2 * WINDOW)[None, :]
    dist = np.clip(qi + WINDOW - kj, 0, REL_MAX_DIST - 1)
    exact = REL_BUCKETS // 2
    df = np.maximum(dist, 1).astype(np.float32)
    large = exact + (np.log(df / np.float32(exact)) / np.float32(math.log(REL_MAX_DIST / exact)) * (REL_BUCKETS - exact)).astype(np.int32)
    large = np.minimum(large, REL_BUCKETS - 1)
    bucket = np.where(dist < exact, dist, large)
    onehot = np.zeros((WINDOW * 2 * WINDOW, REL_BUCKETS), np.float32)
    onehot[np.arange(WINDOW * 2 * WINDOW), bucket.reshape(-1)] = 1.0
    return onehot


def _band_bias(rel_bias):
    onehot = jnp.asarray(_t5_bucket_table())
    sel = jnp.sum(onehot[:, :, None] * rel_bias[None, :, :], axis=1)
    return sel.T.reshape(N_Q, WINDOW, 2 * WINDOW)


def _block_diag(a):
    g, r, c = a.shape
    a4 = a.reshape(SSM_BLOCKS, g // SSM_BLOCKS, r, c)
    eye = jnp.eye(g // SSM_BLOCKS, dtype=a.dtype)
    full = a4[:, :, :, None, :] * eye[None, :, None, :, None]
    return full.reshape(SSM_BLOCKS, (g // SSM_BLOCKS) * r, (g // SSM_BLOCKS) * c)


def _ssm_disc(lam_re, lam_im, b_re, b_im, c_re, c_im, log_dt):
    dt = jnp.exp(log_dt)[:, None]
    mag = jnp.exp(lam_re * dt)
    ang = lam_im * dt
    a_re = mag * jnp.cos(ang)
    a_im = mag * jnp.sin(ang)
    den = lam_re * lam_re + lam_im * lam_im
    nr = a_re - 1.0
    coef_re = (nr * lam_re + a_im * lam_im) / den
    coef_im = (a_im * lam_re - nr * lam_im) / den
    bb_re = coef_re[..., None] * b_re - coef_im[..., None] * b_im
    bb_im = coef_re[..., None] * b_im + coef_im[..., None] * b_re
    wb_re = _block_diag(jnp.swapaxes(bb_re, 1, 2))
    wb_im = _block_diag(jnp.swapaxes(bb_im, 1, 2))
    cm_re = _block_diag(jnp.swapaxes(c_re, 1, 2))
    cm_imn = _block_diag(-jnp.swapaxes(c_im, 1, 2))
    return a_re.reshape(-1), a_im.reshape(-1), wb_re, wb_im, cm_re, cm_imn


def _scan_tables(a_re, a_im):
    pr, pi = [a_re], [a_im]
    for _ in range(7):
        pr, pi = pr + [pr[-1] * a_re - pi[-1] * a_im], pi + [pr[-1] * a_im + pi[-1] * a_re]
    pr, pi = jnp.stack(pr), jnp.stack(pi)
    pad = jnp.zeros((5,) + a_re.shape, F32)
    dr = jnp.concatenate([jnp.stack([pr[0], pr[1], pr[3]]), pad])
    di = jnp.concatenate([jnp.stack([pi[0], pi[1], pi[3]]), pad])
    fwd = (pr, pi, dr, di)
    rev = (pr[::-1], -pi[::-1], dr, -di)
    return jax.tree.map(lax.stop_gradient, (fwd, rev))


def _gate_col(r):
    return lambda j: OFF_G // D_MODEL + r


def _layer_fwd(x, h, p_i, w, bias, li, next_gain):
    nm = lambda s: f"{s}_l{li}"
    z = _mm(h, w["w_in"], "nn", tm=1024, tn=2944, n_outer=True, out_dtype=BF16, name=nm("mm_in"))
    bu_re, bu_im = _bd_apply([z], [w["wb_re"], w["wb_im"]], [[(0, 0)], [(0, 1)]], "nn", col_blocks=[U_BLK], name=nm("ssm_bu"))
    h_re, h_im = _scan(bu_re, bu_im, *w["scan_fwd"], reverse=False, name=nm("ssm_scan"))
    (y0,) = _bd_apply([h_re, h_im], [w["cm_re"], w["cm_imn"]], [[(0, 0), (1, 1)]], "nn", name=nm("ssm_c"))
    (y1,) = _rw(lambda a, u, d: ([jax.nn.gelu(a + d * u)], []),
                [(y0, None, _c0), (z, BRANCH, lambda j: U_BLK)], [(BRANCH, None, F32, _c0)],
                params=[w["ssm_d"]], name=nm("ssm_gelu"))
    gl = _mm(y1, w["ssm_w_glu"], "nn", name=nm("mm_glu"))
    (y_ssm,) = _rw(lambda a, b: ([a * jax.nn.sigmoid(b)], []), [(y1, None, _c0), (gl, None, _c0)],
                   [(BRANCH, None, BF16, _c0)], name=nm("ssm_glu"))
    y_conv = _conv_fwd(z, w["conv_w"], nm("conv_fwd"))
    kv_w = N_KV * HEAD_DIM
    q2, k2, v2 = _rw(lambda q, k, v: ([q, k, v], []),
                     [(z, BRANCH, lambda j: OFF_Q // BRANCH), (z, kv_w, lambda j: OFF_K // kv_w), (z, kv_w, lambda j: OFF_V // kv_w)],
                     [(BRANCH, None, BF16, _c0), (kv_w, None, BF16, _c0), (kv_w, None, BF16, _c0)], name=nm("qkv_bf16"))
    qh, kh, vh = _heads(q2, N_Q), _heads(k2, N_KV), _heads(v2, N_KV)
    y_attn = _unheads(_attn_fwd(qh, kh, vh, bias, w["sinks"], nm("attn_fwd")))
    ys = (y_ssm, y_conv, y_attn)
    bs = [_mm(ys[r], w["w_branch"][r], "nn", out_dtype=BF16, name=nm(f"mm_branch{r}")) for r in range(3)]

    def merge(g0, g1, g2, b0, b1, b2):
        return [jax.nn.sigmoid(g0) * b0 + jax.nn.sigmoid(g1) * b1 + jax.nn.sigmoid(g2) * b2], []

    (merged,) = _rw(merge, [(z, D_MODEL, _gate_col(r)) for r in range(3)] + [(b, None, _c0) for b in bs],
                    [(D_MODEL, None, BF16, _c0)], name=nm("merge"))
    x1, hf_in = _mm(merged, w["w_out"], "nn", add=x, rms_out=w["norm_ffn"], name=nm("mm_out"))
    hf = _mm(hf_in, w["w_ffn_in"], "nn", tn=1408, n_outer=True, out_dtype=BF16, name=nm("mm_ffn_in"))
    (act,) = _rw(lambda a, b: ([jax.nn.silu(a) * b], []), [(hf, FFN_COLS, lambda j: j), (hf, FFN_COLS, lambda j: FFN_NCOL + j)],
                 [(FFN_HIDDEN, FFN_COLS, BF16, lambda j: j)], ncol=FFN_NCOL, name=nm("swiglu"))
    x2, hp = _mm(act, w["w_ffn_out"], "nn", add=x1, tk=1408, rms_out=w["norm_ple"], name=nm("mm_ffn_out"))
    pgl = _mm(hp, w["w_ple_gate"], "nn", name=nm("mm_ple_gate"))
    pp = _mm(p_i, w["w_ple_proj"], "nn", name=nm("mm_ple_proj"))

    def ple_add(xv, a, b, *gain):
        x3v = xv + jax.nn.sigmoid(a) * b
        return [x3v] + [_rms(x3v, g) for g in gain], []

    has_next = next_gain is not None
    res = _rw(ple_add, [(x2, None, _c0), (pgl, None, _c0), (pp, None, _c0)],
              [(D_MODEL, None, F32, _c0)] + [(D_MODEL, None, BF16, _c0)] * has_next, params=[next_gain] * has_next, name=nm("ple_add"))
    x3, h_next = res[0], (res[1] if has_next else None)
    saved = dict(x=x, p=p_i, h=h, z=z, h_re=h_re, h_im=h_im, y0=y0, y1=y1, gl=gl, ys=ys, qh=qh, kh=kh, vh=vh,
                 bs=bs, merged=merged, x1=x1, hf_in=hf_in, hf=hf, act=act, x2=x2, hp=hp, pgl=pgl, pp=pp)
    return x3, h_next, saved


def _layer_bwd(dx3, s, w, bias, li):
    nm = lambda n: f"{n}_l{li}"
    g = {}
    z = s["z"]
    def ple_b(d, a, b):
        _, vjp = jax.vjp(lambda a_, b_: jax.nn.sigmoid(a_) * b_, a, b)
        return list(vjp(d)), []

    dpgl, dpp = _rw(ple_b, [(dx3, None, _c0), (s["pgl"], None, _c0), (s["pp"], None, _c0)],
                    [(D_MODEL, None, BF16, _c0)] * 2, name=nm("ple_bwd"))
    g["w_ple_proj"] = _mm(s["p"], dpp, "tn", name=nm("mmg_ple_proj"))
    g["w_ple_gate"] = _mm(s["hp"], dpgl, "tn", name=nm("mmg_ple_gate"))
    dx2, g["norm_ple"] = _mm(dpgl, w["w_ple_gate"], "nt", rms_back=(s["x2"], dx3, w["norm_ple"]), name=nm("mmb_ple_gate"))
    dact = _mm(dx2, w["w_ffn_out"], "nt", tn=1408, out_dtype=BF16, name=nm("mmb_ffn_out"))
    g["w_ffn_out"] = _mm(s["act"], dx2, "tn", tm=1408, name=nm("mmg_ffn_out"))

    def swiglu_b(a, b, d):
        _, vjp = jax.vjp(lambda a_, b_: jax.nn.silu(a_) * b_, a, b)
        return list(vjp(d)), []

    dhf_a, dhf_b = _rw(swiglu_b, [(s["hf"], FFN_COLS, lambda j: j), (s["hf"], FFN_COLS, lambda j: FFN_NCOL + j), (dact, FFN_COLS, lambda j: j)],
                       [(FFN_HIDDEN, FFN_COLS, BF16, lambda j: j)] * 2, ncol=FFN_NCOL, name=nm("swiglu_bwd"))
    g["w_ffn_in"] = jnp.concatenate([_mm(s["hf_in"], dhf_a, "tn", tn=1408, name=nm("mmg_ffn_in_a")),
                                     _mm(s["hf_in"], dhf_b, "tn", tn=1408, name=nm("mmg_ffn_in_b"))], axis=1)
    dhf_in = _mm(dhf_a, w["w_ffn_in"], "nt", tk=1408, name=nm("mmb_ffn_in_a"))
    dx1, g["norm_ffn"] = _mm(dhf_b, w["w_ffn_in"], "nt", tm=512, tk=1408, b_k0=FFN_HIDDEN, add=dhf_in,
                             rms_back=(s["x1"], dx2, w["norm_ffn"]), name=nm("mmb_ffn_in_b"))
    dmerged = _mm(dx1, w["w_out"], "nt", out_dtype=BF16, name=nm("mmb_out"))
    g["w_out"] = _mm(s["merged"], dx1, "tn", name=nm("mmg_out"))

    def merge_b(d, g0, g1, g2, b0, b1, b2):
        outs_g, outs_b = [], []
        for gate, br in ((g0, b0), (g1, b1), (g2, b2)):
            sg = jax.nn.sigmoid(gate)
            outs_g.append(d * br * sg * (1.0 - sg))
            outs_b.append(d * sg)
        return outs_g + outs_b, []

    res = _rw(merge_b, [(dmerged, None, _c0)] + [(z, D_MODEL, _gate_col(r)) for r in range(3)] + [(b, None, _c0) for b in s["bs"]],
              [(D_MODEL, None, BF16, _c0)] * 6, name=nm("merge_bwd"))
    dgates, dbs = res[:3], res[3:]
    dys = [_mm(dbs[r], w["w_branch"][r], "nt", out_dtype=BF16, name=nm(f"mmb_branch{r}")) for r in range(3)]
    g["w_branch"] = jnp.stack([_mm(s["ys"][r], dbs[r], "tn", name=nm(f"mmg_branch{r}")) for r in range(3)])
    doh = _heads(dys[2], N_Q)
    dqh, dkc, dkp, dvc, dvp, dbias, dsink = _attn_bwd(s["qh"], s["kh"], s["vh"], doh, bias, w["sinks"], nm("attn_bwd"))
    dq, dk, dv = _unheads(dqh), _unheads(_shift_blocks(dkc, dkp)), _unheads(_shift_blocks(dvc, dvp))
    g["sinks"] = jnp.sum(dsink, axis=(1, 2))
    dcb, dcc, dcx, dconv = _conv_bwd(dys[1], z, w["conv_w"], nm("conv_bwd"))
    g["conv_w"] = dconv[0:3]
    def glu_b(d, y1, gl):
        sg = jax.nn.sigmoid(gl)
        return [d * y1 * sg * (1.0 - sg), d * sg], []

    dgl, dy1a = _rw(glu_b, [(dys[0], None, _c0), (s["y1"], None, _c0), (s["gl"], None, _c0)],
                    [(BRANCH, None, BF16, _c0), (BRANCH, None, F32, _c0)], name=nm("ssm_glu_bwd"))
    g["ssm_w_glu"] = _mm(s["y1"], dgl, "tn", name=nm("mmg_glu"))
    dy1b = _mm(dgl, w["ssm_w_glu"], "nt", name=nm("mmb_glu"))

    def gelu_b(da, db, a, u, d):
        _, vjp = jax.vjp(lambda pre: jax.nn.gelu(pre), a + d * u)
        (dy0,) = vjp(da + db)
        return [dy0, dy0 * d], [jnp.sum(dy0 * u, axis=0, keepdims=True)]

    dy0, du_a, g["ssm_d"] = _rw(gelu_b, [(dy1a, None, _c0), (dy1b, None, _c0), (s["y0"], None, _c0), (z, BRANCH, lambda j: U_BLK)],
                                [(BRANCH, None, BF16, _c0), (BRANCH, None, F32, _c0)], params=[w["ssm_d"]],
                                reds=[((1, BRANCH), None, _c0)], name=nm("ssm_gelu_bwd"))
    dh_re, dh_im = _bd_apply([dy0], [w["cm_re"], w["cm_imn"]], [[(0, 0)], [(0, 1)]], "nt", name=nm("ssmb_c"))
    sb, cb = SSM_STATES // SSM_BLOCKS, BRANCH // SSM_BLOCKS
    g["cm_re"], g["cm_imn"] = _bd_grads([s["h_re"], s["h_im"], dy0], [sb, sb, cb], [(0, 2), (1, 2)], name=nm("ssmg_c"))
    l_re, l_im, da_re, da_im = _scan(dh_re, dh_im, *w["scan_rev"], reverse=True, hr=s["h_re"], hi=s["h_im"], name=nm("ssm_scan_bwd"))
    g["a_re"], g["a_im"] = jnp.sum(da_re, axis=0), jnp.sum(da_im, axis=0)
    (du_b,) = _bd_apply([l_re, l_im], [w["wb_re"], w["wb_im"]], [[(0, 0), (1, 1)]], "nt", name=nm("ssmb_bu"))
    g["wb_re"], g["wb_im"] = _bd_grads([z, l_re, l_im], [cb, sb, sb], [(0, 1), (0, 2)], col_blocks=[U_BLK, 0, 0], name=nm("ssmg_bu"))
    dz = jnp.concatenate(list(dgates) + [(du_a + du_b).astype(BF16), dcb, dcc, dcx, dq.astype(BF16), dk.astype(BF16), dv.astype(BF16)], axis=1)
    g["w_in"] = _mm(s["h"], dz, "tn", tm=512, tn=2944, name=nm("mmg_in"))
    dx, g["norm_mix"] = _mm(dz, w["w_in"], "nt", tm=512, tk=2944, rms_back=(s["x"], dx1, w["norm_mix"]), name=nm("mmb_in"))
    return dx, g, dbias


def _loss_and_seed(x, target, g_final):
    def fn(xv, tv, gv):
        y, vjp = jax.vjp(_rms, xv, gv)
        err = y - tv
        dx, dg = vjp(err * (1.0 / D_MODEL))
        return [dx], [jnp.sum(err * err, axis=0, keepdims=True) * (0.5 / D_MODEL), dg]

    return _rw(fn, [(x, None, _c0), (target, None, _c0)], [(D_MODEL, None, F32, _c0)], params=[g_final],
               reds=[((1, D_MODEL), None, _c0)] * 2, name="loss_head")


def _local_step(x, p, target, wt):
    bias, bias_vjp = jax.vjp(_band_bias, wt["rel_bias"])
    layers, disc_vjps = [], []
    for i in range(DEPTH):
        ssm_p = [wt[k][i] for k in ("ssm_lambda_re", "ssm_lambda_im", "ssm_b_re", "ssm_b_im", "ssm_c_re", "ssm_c_im", "ssm_log_dt")]
        (a_re, a_im, wb_re, wb_im, cm_re, cm_imn), disc_vjp = jax.vjp(_ssm_disc, *ssm_p)
        scan_fwd, scan_rev = _scan_tables(a_re, a_im)
        layers.append(dict(
            norm_mix=wt["norm_mix"][i][None], w_in=wt["w_in"][i], wb_re=wb_re.astype(BF16), wb_im=wb_im.astype(BF16),
            cm_re=cm_re.astype(BF16), cm_imn=cm_imn.astype(BF16), scan_fwd=scan_fwd, scan_rev=scan_rev,
            ssm_d=wt["ssm_d"][i][None], ssm_w_glu=wt["ssm_w_glu"][i], conv_w=wt["conv_w"][i],
            sinks=wt["attn_sinks"][i][:, None], w_branch=wt["w_branch"][i], w_out=wt["w_out"][i],
            norm_ffn=wt["norm_ffn"][i][None], w_ffn_in=wt["w_ffn_in"][i], w_ffn_out=wt["w_ffn_out"][i],
            norm_ple=wt["norm_ple"][i][None], w_ple_gate=wt["w_ple_gate"][i], w_ple_proj=wt["w_ple_proj"][i]))
        disc_vjps.append(disc_vjp)

    saved = []
    for i in range(DEPTH):
        h = _rms_fwd(x, layers[0]["norm_mix"], "rms_mix_l0") if i == 0 else h
        x, h, s = _layer_fwd(x, h, p[i], layers[i], bias, i, layers[i + 1]["norm_mix"] if i + 1 < DEPTH else None)
        saved.append(s)
    dx, loss_cols, g_final = _loss_and_seed(x, target, wt["norm_final"][None])
    loss = jnp.sum(loss_cols)

    per_layer = [None] * DEPTH
    dbias = None
    for i in reversed(range(DEPTH)):
        dx, g, db = _layer_bwd(dx, saved[i], layers[i], bias, i)
        dbias = db if dbias is None else dbias + db
        (g["ssm_lambda_re"], g["ssm_lambda_im"], g["ssm_b_re"], g["ssm_b_im"], g["ssm_c_re"], g["ssm_c_im"], g["ssm_log_dt"]) = \
            disc_vjps[i]((g.pop("a_re"), g.pop("a_im"), g.pop("wb_re"), g.pop("wb_im"), g.pop("cm_re"), g.pop("cm_imn")))
        g["attn_sinks"] = g.pop("sinks")
        for k in ("norm_mix", "norm_ffn", "norm_ple", "ssm_d"):
            g[k] = g[k][0]
        per_layer[i] = g
    big_names = [name for name, _, _ in BIG]
    big = {k: [per_layer[i][k] for i in range(DEPTH)] for k in big_names}
    small = {k: jnp.stack([per_layer[i][k] for i in range(DEPTH)]) for k in per_layer[0] if k not in big_names}
    (small["rel_bias"],) = bias_vjp(dbias)
    small["norm_final"] = g_final[0]
    return loss, dx, small, big


HBM_SPEC = pl.BlockSpec(memory_space=pltpu.HBM)


def _position():
    x, y, c = lax.axis_index("x"), lax.axis_index("y"), lax.axis_index("c")
    other_chips = [(1 - x, y), (x, 1 - y), (1 - x, 1 - y)]
    return x, y, c, other_chips


def _row_chunks(rows, n=COPY_CHUNKS):
    rq = rows // n
    assert rq * n == rows and rq % 16 == 0, rows
    return [pl.ds(q * rq, rq) for q in range(n)]


def _place(buf, val, idx, name):
    n, rows, width = buf.shape
    tm = _pick(rows, 512, 16)

    def body(idx_ref, buf_ref, v_ref, o_ref):
        o_ref[0] = v_ref[...]

    grid_spec = pltpu.PrefetchScalarGridSpec(
        num_scalar_prefetch=1, grid=(rows // tm,),
        in_specs=[pl.BlockSpec(memory_space=pl.ANY), pl.BlockSpec((tm, width), lambda i, idx_ref: (i, 0))],
        out_specs=pl.BlockSpec((1, tm, width), lambda i, idx_ref: (idx_ref[0], i, 0)))
    return pl.pallas_call(
        body, name=name, grid_spec=grid_spec, out_shape=jax.ShapeDtypeStruct(buf.shape, buf.dtype), input_output_aliases={1: 0},
        compiler_params=pltpu.CompilerParams(dimension_semantics=("arbitrary",)),
    )(jnp.asarray(idx, jnp.int32).reshape(1), buf, val)


def _allgather_weights(locals_):
    nb, nq = len(locals_), COPY_CHUNKS
    chunks = [_row_chunks(a.shape[1]) for a in locals_]

    def body(*refs):
        w_refs, out_refs = refs[:nb], refs[nb:2 * nb]
        send_sems, recv_sems = refs[2 * nb:]
        x, y, c, chips = _position()
        me = 2 * x + y
        sibling = (x, y, 1 - c)

        def copy(b, kind, q, src, dst, to):
            k = (b * 6 + kind) * nq + q
            return pltpu.make_async_remote_copy(src_ref=src, dst_ref=dst, send_sem=send_sems.at[k], recv_sem=recv_sems.at[k],
                                                device_id=to, device_id_type=MESH)

        first = [copy(b, j, q, w_refs[b].at[c, chunks[b][q]], out_refs[b].at[me, c, chunks[b][q]], (*chip, c))
                 for q in range(nq) for b in range(nb) for j, chip in enumerate(chips)]
        for cp in first:
            cp.start()
        passed = []
        for q in range(nq):
            for b in range(nb):
                for j, (px, py) in enumerate(chips):
                    landed = out_refs[b].at[2 * px + py, c, chunks[b][q]]
                    copy(b, j, q, landed, landed, (px, py, c)).wait_recv()
                    fwd = copy(b, 3 + j, q, landed, landed, sibling)
                    fwd.start()
                    passed.append(fwd)
        for q in range(nq):
            for b in range(nb):
                for j, (px, py) in enumerate(chips):
                    landed = out_refs[b].at[2 * px + py, 1 - c, chunks[b][q]]
                    copy(b, 3 + j, q, landed, landed, sibling).wait_recv()
        for cp in first + passed:
            cp.wait_send()

    return pl.pallas_call(
        body, name="allgather_weights", in_specs=[HBM_SPEC] * nb, out_specs=[HBM_SPEC] * nb,
        out_shape=[jax.ShapeDtypeStruct((N_SHARD,) + a.shape, a.dtype) for a in locals_],
        scratch_shapes=[pltpu.SemaphoreType.DMA((nb * 6 * nq,)), pltpu.SemaphoreType.DMA((nb * 6 * nq,))],
    )(*locals_)


def _sibling_exchange(bufs):
    nb, nq, ns = len(bufs), COPY_CHUNKS, N_SHARD
    chunks = [_row_chunks(a.shape[2]) for a in bufs]

    def body(*refs):
        g_refs, got_refs = refs[:nb], refs[nb:2 * nb]
        send_sems, recv_sems = refs[2 * nb:]
        x, y, c, _ = _position()
        swaps = [pltpu.make_async_remote_copy(src_ref=g_refs[b].at[s, 1 - c, chunks[b][q]], dst_ref=got_refs[b].at[s, chunks[b][q]],
                                              send_sem=send_sems.at[(b * ns + s) * nq + q], recv_sem=recv_sems.at[(b * ns + s) * nq + q],
                                              device_id=(x, y, 1 - c), device_id_type=MESH)
                 for b in range(nb) for s in range(ns) for q in range(nq)]
        for cp in swaps:
            cp.start()
        for cp in swaps:
            cp.wait()

    return pl.pallas_call(
        body, name="grad_sibling_exchange", in_specs=[HBM_SPEC] * nb, out_specs=[HBM_SPEC] * nb,
        out_shape=[jax.ShapeDtypeStruct((ns,) + a.shape[2:], a.dtype) for a in bufs],
        scratch_shapes=[pltpu.SemaphoreType.DMA((nb * ns * nq,)), pltpu.SemaphoreType.DMA((nb * ns * nq,))],
    )(*bufs)


def _chip_exchange(parts):
    nb, nq = len(parts), COPY_CHUNKS
    chunks = [_row_chunks(a.shape[1]) for a in parts]

    def body(*refs):
        b_refs, got_refs = refs[:nb], refs[nb:2 * nb]
        send_sems, recv_sems = refs[2 * nb:]
        x, y, c, chips = _position()
        sends = [pltpu.make_async_remote_copy(src_ref=b_refs[b].at[2 * px + py, chunks[b][q]], dst_ref=got_refs[b].at[j, chunks[b][q]],
                                              send_sem=send_sems.at[(b * 3 + j) * nq + q], recv_sem=recv_sems.at[(b * 3 + j) * nq + q],
                                              device_id=(px, py, c), device_id_type=MESH)
                 for q in range(nq) for b in range(nb) for j, (px, py) in enumerate(chips)]
        for cp in sends:
            cp.start()
        for cp in sends:
            cp.wait()

    return pl.pallas_call(
        body, name="grad_chip_exchange", in_specs=[HBM_SPEC] * nb, out_specs=[HBM_SPEC] * nb,
        out_shape=[jax.ShapeDtypeStruct((N_SHARD - 1,) + a.shape[1:], a.dtype) for a in parts],
        scratch_shapes=[pltpu.SemaphoreType.DMA((nb * 3 * nq,)), pltpu.SemaphoreType.DMA((nb * 3 * nq,))],
    )(*parts)


def _sibling_gather(halves):
    nb, nq = len(halves), 2 * COPY_CHUNKS
    chunks = [_row_chunks(a.shape[0], nq) for a in halves]

    def body(*refs):
        h_refs, out_refs = refs[:nb], refs[nb:2 * nb]
        send_sems, recv_sems = refs[2 * nb:]
        x, y, c, _ = _position()

        def chunk(b, q, half_idx):
            return pltpu.make_async_remote_copy(src_ref=h_refs[b].at[chunks[b][q]], dst_ref=out_refs[b].at[half_idx, chunks[b][q]],
                                                send_sem=send_sems.at[b * nq + q], recv_sem=recv_sems.at[b * nq + q],
                                                device_id=(x, y, 1 - c), device_id_type=MESH)

        pushes = [chunk(b, q, c) for b in range(nb) for q in range(nq)]
        for cp in pushes:
            cp.start()
        for b in range(nb):
            for q in range(nq):
                chunk(b, q, 1 - c).wait_recv()
        for cp in pushes:
            cp.wait_send()

    return pl.pallas_call(
        body, name="grad_sibling_gather", in_specs=[HBM_SPEC] * nb, out_specs=[HBM_SPEC] * nb,
        out_shape=[jax.ShapeDtypeStruct((2,) + a.shape, a.dtype) for a in halves],
        scratch_shapes=[pltpu.SemaphoreType.DMA((nb * nq,)), pltpu.SemaphoreType.DMA((nb * nq,))],
    )(*halves)


def _gather_partials(part):
    r, lanes = part.shape

    def body(p_ref, out_ref, send_sems, recv_sems):
        x, y, c, _ = _position()
        flips = [(fx, fy, fc) for fx in (0, 1) for fy in (0, 1) for fc in (0, 1)][1:]
        sends = []
        for k, (fx, fy, fc) in enumerate(flips):
            cp = pltpu.make_async_remote_copy(src_ref=p_ref, dst_ref=out_ref.at[4 * x + 2 * y + c], send_sem=send_sems.at[k],
                                              recv_sem=recv_sems.at[k], device_id=(x ^ fx, y ^ fy, c ^ fc), device_id_type=MESH)
            cp.start()
            sends.append(cp)
        for k, (fx, fy, fc) in enumerate(flips):
            src = out_ref.at[4 * (x ^ fx) + 2 * (y ^ fy) + (c ^ fc)]
            pltpu.make_async_remote_copy(src_ref=src, dst_ref=src, send_sem=send_sems.at[k], recv_sem=recv_sems.at[k],
                                         device_id=(x ^ fx, y ^ fy, c ^ fc), device_id_type=MESH).wait_recv()
        for cp in sends:
            cp.wait_send()

    return pl.pallas_call(
        body, name="small_gather_partials", in_specs=[HBM_SPEC], out_specs=HBM_SPEC,
        out_shape=jax.ShapeDtypeStruct((8, r, lanes), part.dtype),
        scratch_shapes=[pltpu.SemaphoreType.DMA((7,)), pltpu.SemaphoreType.DMA((7,))],
    )(part)


def _sum_leading(stack, name, also_bf16=False):
    k, r, lanes = stack.shape
    tm = _pick(r, 256, 16)
    outs = [jax.ShapeDtypeStruct((r, lanes), F32)] + ([jax.ShapeDtypeStruct((r, lanes), BF16)] if also_bf16 else [])

    def body(s_ref, *o_refs):
        acc = s_ref[0].astype(F32)
        for i in range(1, k):
            acc = acc + s_ref[i].astype(F32)
        for o in o_refs:
            o[...] = acc.astype(o.dtype)

    spec = pl.BlockSpec((tm, lanes), lambda i: (i, 0))
    return pl.pallas_call(
        body, name=name, grid=(r // tm,), in_specs=[pl.BlockSpec((k, tm, lanes), lambda i: (0, i, 0))],
        out_specs=[spec] * len(outs), out_shape=outs,
        compiler_params=pltpu.CompilerParams(dimension_semantics=("parallel",)),
    )(stack)


def _add_pair(g2, got, half, name):
    ns, _, rows, width = g2.shape
    tm = _pick(rows, 256, 16)
    spec = pl.BlockSpec((1, tm, width), lambda s, i, h_ref: (s, i, 0))

    def body(h_ref, a_ref, b_ref, f_ref, o_ref):
        acc = a_ref[0] + b_ref[...]
        f_ref[...] = acc
        o_ref[...] = acc.astype(BF16)

    grid_spec = pltpu.PrefetchScalarGridSpec(
        num_scalar_prefetch=1, grid=(ns, rows // tm),
        in_specs=[pl.BlockSpec((1, 1, tm, width), lambda s, i, h_ref: (s, h_ref[0], i, 0)), spec], out_specs=[spec, spec])
    return pl.pallas_call(
        body, name=name, grid_spec=grid_spec,
        out_shape=[jax.ShapeDtypeStruct(got.shape, F32), jax.ShapeDtypeStruct(got.shape, BF16)],
        compiler_params=pltpu.CompilerParams(dimension_semantics=("parallel", "parallel")),
    )(jnp.asarray(half, jnp.int32).reshape(1), g2, got)


def _add_own(parts, got, mine, name):
    _, rows, width = parts.shape
    tm = _pick(rows, 256, 16)

    def body(m_ref, p_ref, g_ref, out_ref):
        acc = p_ref[0]
        for j in range(g_ref.shape[0]):
            acc = acc + g_ref[j].astype(F32)
        out_ref[...] = acc

    grid_spec = pltpu.PrefetchScalarGridSpec(
        num_scalar_prefetch=1, grid=(rows // tm,),
        in_specs=[pl.BlockSpec((1, tm, width), lambda i, m_ref: (m_ref[0], i, 0)),
                  pl.BlockSpec((got.shape[0], tm, width), lambda i, m_ref: (0, i, 0))],
        out_specs=pl.BlockSpec((tm, width), lambda i, m_ref: (i, 0)))
    return pl.pallas_call(
        body, name=name, grid_spec=grid_spec, out_shape=jax.ShapeDtypeStruct((rows, width), F32),
        compiler_params=pltpu.CompilerParams(dimension_semantics=("parallel",)),
    )(jnp.asarray(mine, jnp.int32).reshape(1), parts, got)


def _local_shape(shape, axis):
    return tuple(d // N_SHARD if a == axis else d for a, d in enumerate(shape))


def _big_sizes():
    return [DEPTH * int(np.prod(_local_shape(shape, axis))) for _, shape, axis in FLAT_BIG]


COL_SHARDED = (("w_in", IN_WIDTH // N_SHARD), ("w_ffn_in", 2 * FFN_HIDDEN // N_SHARD))
FLAT_BIG = tuple(entry for entry in BIG if entry[0] not in [name for name, _ in COL_SHARDED])
FLAT_ROW_TILE = 256
ELEMENTWISE_BIG = ("conv_w",)


def _three_bf16(w):
    hi = w.astype(BF16)
    r1 = w - hi.astype(F32)
    mid = r1.astype(BF16)
    lo = (r1 - mid.astype(F32)).astype(BF16)
    return jnp.stack([hi, mid, lo], axis=-1)


PIECE_ROWS = 16


def _flat_layout(for_weights):
    pieces = []
    for (name, shape, axis), size in zip(FLAT_BIG, _big_sizes(), strict=True):
        n = size * (3 if for_weights and name in ELEMENTWISE_BIG else 1)
        rows = -(-n // (LANES * PIECE_ROWS)) * PIECE_ROWS
        pieces.append((name, shape, axis, n, rows))
    total = sum(p[-1] for p in pieces)
    half = -(-total // (2 * FLAT_ROW_TILE)) * FLAT_ROW_TILE
    return pieces, half


def _to_rows(flat, rows):
    lead, n = flat.shape[:-1], flat.shape[-1]
    fill = jnp.zeros(lead + (rows * LANES - n,), flat.dtype)
    return jnp.concatenate([flat, fill], axis=-1).reshape(lead + (rows, LANES))


def _pack_local_weights(wl):
    pieces, half = _flat_layout(True)
    parts = [_to_rows((_three_bf16(wl[name]) if name in ELEMENTWISE_BIG else wl[name].astype(BF16)).reshape(-1), rows)
             for name, _, _, _, rows in pieces]
    parts.append(jnp.zeros((2 * half - sum(p[-1] for p in pieces), LANES), BF16))
    return jnp.concatenate(parts, axis=0).reshape(2, half, LANES)


def _unpack_local(flat):
    pieces, _ = _flat_layout(False)
    flat = flat.reshape(-1, LANES)
    out, off = {}, 0
    for name, shape, axis, n, rows in pieces:
        out[name] = flat[off:off + rows].reshape(-1)[:n].reshape((DEPTH,) + _local_shape(shape, axis))
        off += rows
    return out


def _unpack_gathered(gathered):
    pieces, _ = _flat_layout(True)
    out, off = {}, 0
    for name, shape, axis, n, rows in pieces:
        local = (N_SHARD, DEPTH) + _local_shape(shape, axis)
        seg = gathered[:, off:off + rows].reshape(N_SHARD, -1)[:, :n]
        off += rows
        if name in ELEMENTWISE_BIG:
            parts = seg.reshape(local + (3,)).astype(F32)
            seg = (parts[..., 0] + parts[..., 1]) + parts[..., 2]
        else:
            seg = seg.reshape(local)
        out[name] = jnp.moveaxis(seg, 0, 1 + axis).reshape((DEPTH,) + shape)
    return out


def _col_segments(name, c):
    if name != "w_in":
        return [(s, 0, c, c * s) for s in range(N_SHARD)]
    segs = []
    for s in range(N_SHARD):
        lo, hi = c * s, c * (s + 1)
        if lo < MAIN_WIDTH:
            segs.append((s, 0, min(hi, MAIN_WIDTH) - lo, GATES_WIDTH + lo))
        if hi > MAIN_WIDTH:
            first = max(lo, MAIN_WIDTH)
            segs.append((s, first - lo, c, first - MAIN_WIDTH))
    return segs


def _join_col_shards(shards, segs, name):
    ns, depth, rows, c = shards.shape
    tm = _pick(rows, 256, 16)

    def body(i_ref, o_ref):
        for s, lo, hi, start in segs:
            o_ref[0, :, start:start + hi - lo] = i_ref[s, 0, :, lo:hi]

    return pl.pallas_call(
        body, name=name, grid=(depth, rows // tm),
        in_specs=[pl.BlockSpec((ns, 1, tm, c), lambda l, i: (0, l, i, 0))],
        out_specs=pl.BlockSpec((1, tm, ns * c), lambda l, i: (l, i, 0)),
        out_shape=jax.ShapeDtypeStruct((depth, rows, ns * c), shards.dtype),
        compiler_params=pltpu.CompilerParams(dimension_semantics=("parallel", "parallel")),
    )(shards)


def _split_col_shards(full, segs, layer, stacked, name):
    rows, width = full.shape
    c = width // N_SHARD
    tm = _pick(rows, 256, 16)

    def body(*refs):
        i_ref, o_ref = refs[0], refs[-1]
        for s, lo, hi, start in segs:
            o_ref[s, 0, :, lo:hi] = i_ref[:, start:start + hi - lo]

    return pl.pallas_call(
        body, name=name, grid=(rows // tm,),
        in_specs=[pl.BlockSpec((tm, width), lambda i: (i, 0))] + ([] if stacked is None else [pl.BlockSpec(memory_space=pl.ANY)]),
        out_specs=pl.BlockSpec((N_SHARD, 1, tm, c), lambda i: (0, layer, i, 0)),
        out_shape=jax.ShapeDtypeStruct((N_SHARD, DEPTH, rows, c), full.dtype),
        input_output_aliases={} if stacked is None else {1: 0},
        compiler_params=pltpu.CompilerParams(dimension_semantics=("parallel",)),
    )(*([full] if stacked is None else [full, stacked]))


def _pack_full_grads(big_grads):
    pieces, half = _flat_layout(False)
    parts = []
    for name, shape, axis, _, rows in pieces:
        per_layer = []
        for gfull in big_grads[name]:
            split = gfull.reshape(shape[:axis] + (N_SHARD, shape[axis] // N_SHARD) + shape[axis + 1:])
            per_layer.append(jnp.moveaxis(split, axis, 0).reshape(N_SHARD, -1))
        parts.append(_to_rows(jnp.concatenate(per_layer, axis=1), rows))
    parts.append(jnp.zeros((N_SHARD, 2 * half - sum(p[-1] for p in pieces), LANES), F32))
    return jnp.concatenate(parts, axis=1).reshape(N_SHARD, 2, half, LANES)


def _pack_small(grads):
    flat = jnp.concatenate([grads[name].reshape(-1) for name in SMALL])
    r = -(-flat.shape[0] // (8 * LANES)) * 8
    return jnp.pad(flat, (0, r * LANES - flat.shape[0])).reshape(r, LANES)


def _unpack_small(flat, like):
    flat = flat.reshape(-1)
    out, off = {}, 0
    for name in SMALL:
        size = int(np.prod(like[name].shape))
        out[name] = flat[off:off + size].reshape(like[name].shape)
        off += size
    return out


def _adamw(w, g, m, v, name):
    shape = w.shape
    cols = shape[-1]
    rows = int(np.prod(shape[:-1])) if len(shape) > 1 else 1
    w2, g2, m2, v2 = (a.reshape(rows, cols) for a in (w, g, m, v))

    def fn(wv, gv, mv, vv):
        mn = ADAM_B1 * mv + (1.0 - ADAM_B1) * gv
        vn = ADAM_B2 * vv + (1.0 - ADAM_B2) * jnp.square(gv)
        m_hat = mn / (1.0 - ADAM_B1 ** ADAM_STEP)
        v_hat = vn / (1.0 - ADAM_B2 ** ADAM_STEP)
        delta = -ADAM_LR * (m_hat / (jnp.sqrt(v_hat) + ADAM_EPS) + ADAM_WD * wv)
        return [delta, mn, vn], []

    tm = 256 if rows % 8 == 0 and rows > 256 else rows
    res = _rw(fn, [(a, None, _c0) for a in (w2, g2, m2, v2)], [(cols, None, F32, _c0)] * 3, tm=tm, name=name)
    return [r.reshape(shape) for r in res]


def _step(x, p, target, weights, moments_m, moments_v):
    xi, yi, ci = lax.axis_index("x"), lax.axis_index("y"), lax.axis_index("c")
    chip = 2 * xi + yi
    half_rows = DEPTH // 2 * D_MODEL
    locals_ = [_pack_local_weights(weights)] + [weights[name].astype(BF16).reshape(2, half_rows, c) for name, c in COL_SHARDED]
    gathered = _allgather_weights(locals_)
    gathered = [_place(g.reshape(N_SHARD, 2 * a.shape[1], a.shape[2]), a.reshape(2 * a.shape[1], a.shape[2]), chip, f"place_own_weights_{k}")
                for k, (g, a) in enumerate(zip(gathered, locals_, strict=True))]
    wt = dict(_unpack_gathered(gathered[0]))
    for (name, c), g in zip(COL_SHARDED, gathered[1:], strict=True):
        wt[name] = _join_col_shards(g.reshape(N_SHARD, DEPTH, D_MODEL, c), _col_segments(name, c), f"join_col_shards_{name}")
    for name in SMALL:
        wt[name] = weights[name]
    loss, dx, small_grads, big_grads = _local_step(x[0], p[:, 0], target[0], wt)
    loss = lax.psum(loss, ("x", "y", "c"))
    bufs = [_pack_full_grads(big_grads)]
    for name, c in COL_SHARDED:
        stacked = None
        for li, g in enumerate(big_grads[name]):
            stacked = _split_col_shards(g, _col_segments(name, c), li, stacked, f"split_col_shards_{name}_l{li}")
        bufs.append(stacked.reshape(N_SHARD, 2, half_rows, c))
    gots = _sibling_exchange(bufs)
    sums = [_add_pair(b, g, ci, f"grad_add_sibling_{k}") for k, (b, g) in enumerate(zip(bufs, gots, strict=True))]
    others = _chip_exchange([s_bf16 for _, s_bf16 in sums])
    halves = [_add_own(s_f32, o, chip, f"grad_add_chips_{k}") for k, ((s_f32, _), o) in enumerate(zip(sums, others, strict=True))]
    both = [_place(b, h, ci, f"place_own_half_{k}") for k, (b, h) in enumerate(zip(_sibling_gather(halves), halves, strict=True))]
    reduced = _unpack_local(both[0])
    for (name, c), b in zip(COL_SHARDED, both[1:], strict=True):
        reduced[name] = b.reshape(DEPTH, D_MODEL, c)
    small_part = _pack_small(small_grads)
    small_all = _place(_gather_partials(small_part), small_part, 4 * xi + 2 * yi + ci, "place_own_small")
    reduced.update(_unpack_small(_sum_leading(small_all, "small_sum")[0], {k: weights[k] for k in SMALL}))
    outs_g, outs_d, outs_m, outs_v = [], [], [], []
    for name in WEIGHTS:
        d, mn, vn = _adamw(weights[name], reduced[name], moments_m[name], moments_v[name], f"adamw_{name}")
        outs_g.append(reduced[name])
        outs_d.append(d)
        outs_m.append(mn)
        outs_v.append(vn)
    return (loss, dx[None], *outs_g, *outs_d, *outs_m, *outs_v)


def kernel(x, p, rel_bias, norm_mix, w_in, ssm_lambda_re, ssm_lambda_im, ssm_b_re, ssm_b_im, ssm_c_re, ssm_c_im, ssm_d, ssm_log_dt, ssm_w_glu, conv_w, attn_sinks, w_branch, w_out, norm_ffn, w_ffn_in, w_ffn_out, norm_ple, w_ple_gate, w_ple_proj, norm_final, loss_target, m_rel_bias, m_norm_mix, m_w_in, m_ssm_lambda_re, m_ssm_lambda_im, m_ssm_b_re, m_ssm_b_im, m_ssm_c_re, m_ssm_c_im, m_ssm_d, m_ssm_log_dt, m_ssm_w_glu, m_conv_w, m_attn_sinks, m_w_branch, m_w_out, m_norm_ffn, m_w_ffn_in, m_w_ffn_out, m_norm_ple, m_w_ple_gate, m_w_ple_proj, m_norm_final, v_rel_bias, v_norm_mix, v_w_in, v_ssm_lambda_re, v_ssm_lambda_im, v_ssm_b_re, v_ssm_b_im, v_ssm_c_re, v_ssm_c_im, v_ssm_d, v_ssm_log_dt, v_ssm_w_glu, v_conv_w, v_attn_sinks, v_w_branch, v_w_out, v_norm_ffn, v_w_ffn_in, v_w_ffn_out, v_norm_ple, v_w_ple_gate, v_w_ple_proj, v_norm_final):
    weights = dict(rel_bias=rel_bias, norm_mix=norm_mix, w_in=w_in, ssm_lambda_re=ssm_lambda_re, ssm_lambda_im=ssm_lambda_im,
                   ssm_b_re=ssm_b_re, ssm_b_im=ssm_b_im, ssm_c_re=ssm_c_re, ssm_c_im=ssm_c_im, ssm_d=ssm_d, ssm_log_dt=ssm_log_dt,
                   ssm_w_glu=ssm_w_glu, conv_w=conv_w, attn_sinks=attn_sinks, w_branch=w_branch, w_out=w_out, norm_ffn=norm_ffn,
                   w_ffn_in=w_ffn_in, w_ffn_out=w_ffn_out, norm_ple=norm_ple, w_ple_gate=w_ple_gate, w_ple_proj=w_ple_proj,
                   norm_final=norm_final)
    moments_m = dict(rel_bias=m_rel_bias, norm_mix=m_norm_mix, w_in=m_w_in, ssm_lambda_re=m_ssm_lambda_re, ssm_lambda_im=m_ssm_lambda_im,
                     ssm_b_re=m_ssm_b_re, ssm_b_im=m_ssm_b_im, ssm_c_re=m_ssm_c_re, ssm_c_im=m_ssm_c_im, ssm_d=m_ssm_d,
                     ssm_log_dt=m_ssm_log_dt, ssm_w_glu=m_ssm_w_glu, conv_w=m_conv_w, attn_sinks=m_attn_sinks, w_branch=m_w_branch,
                     w_out=m_w_out, norm_ffn=m_norm_ffn, w_ffn_in=m_w_ffn_in, w_ffn_out=m_w_ffn_out, norm_ple=m_norm_ple,
                     w_ple_gate=m_w_ple_gate, w_ple_proj=m_w_ple_proj, norm_final=m_norm_final)
    moments_v = dict(rel_bias=v_rel_bias, norm_mix=v_norm_mix, w_in=v_w_in, ssm_lambda_re=v_ssm_lambda_re, ssm_lambda_im=v_ssm_lambda_im,
                     ssm_b_re=v_ssm_b_re, ssm_b_im=v_ssm_b_im, ssm_c_re=v_ssm_c_re, ssm_c_im=v_ssm_c_im, ssm_d=v_ssm_d,
                     ssm_log_dt=v_ssm_log_dt, ssm_w_glu=v_ssm_w_glu, conv_w=v_conv_w, attn_sinks=v_attn_sinks, w_branch=v_w_branch,
                     w_out=v_w_out, norm_ffn=v_norm_ffn, w_ffn_in=v_w_ffn_in, w_ffn_out=v_w_ffn_out, norm_ple=v_norm_ple,
                     w_ple_gate=v_w_ple_gate, w_ple_proj=v_w_ple_proj, norm_final=v_norm_final)
    return _step(x, p, loss_target, weights, moments_m, moments_v)
```

```python
import functools
import math

import numpy as np

import jax
import jax.numpy as jnp
from jax import lax
from jax.experimental import pallas as pl
from jax.experimental.pallas import tpu as pltpu

F32, BF16 = jnp.float32, jnp.bfloat16
MESH = pl.DeviceIdType.MESH

D_MODEL = 1024
DEPTH = 4
PLE_DIM = 256
BRANCH = 512
N_GROUPS = 32
GROUP_CH = 16
N_STATE = 64
SSM_STATES = N_GROUPS * N_STATE
SSM_BLOCKS = 4
HEAD_DIM = 64
N_Q = 8
N_KV = 2
GQA = N_Q // N_KV
WINDOW = 128
ATTN_SCALE = 1.0 / math.sqrt(HEAD_DIM)
REL_BUCKETS = 32
REL_MAX_DIST = 128
FFN_HIDDEN = 2816
FFN_COLS = 1408
FFN_NCOL = FFN_HIDDEN // FFN_COLS
IN_WIDTH = 5888
RMS_EPS = 1e-6
NEG = -1e30

ADAM_LR, ADAM_B1, ADAM_B2, ADAM_EPS, ADAM_WD, ADAM_STEP = 0.001, 0.9, 0.999, 1e-08, 0.01, 10

N_SHARD = 4
LANES = 1024
COPY_CHUNKS = 4

GATES_WIDTH = 3 * D_MODEL
MAIN_WIDTH = IN_WIDTH - GATES_WIDTH
OFF_G, OFF_U, OFF_CB, OFF_CC, OFF_CX, OFF_Q, OFF_K, OFF_V = 0, 3072, 3584, 4096, 4608, 5120, 5632, 5760
U_BLK, CB_BLK, CC_BLK, CX_BLK = OFF_U // BRANCH, OFF_CB // BRANCH, OFF_CC // BRANCH, OFF_CX // BRANCH

BIG = (
    ("w_in", (D_MODEL, IN_WIDTH), 1),
    ("ssm_w_glu", (BRANCH, BRANCH), 0),
    ("conv_w", (3, BRANCH), 1),
    ("w_branch", (3, BRANCH, D_MODEL), 2),
    ("w_out", (D_MODEL, D_MODEL), 0),
    ("w_ffn_in", (D_MODEL, 2 * FFN_HIDDEN), 1),
    ("w_ffn_out", (FFN_HIDDEN, D_MODEL), 0),
    ("w_ple_gate", (D_MODEL, D_MODEL), 0),
    ("w_ple_proj", (PLE_DIM, D_MODEL), 1),
)
SMALL = ("rel_bias", "norm_mix", "ssm_lambda_re", "ssm_lambda_im", "ssm_b_re", "ssm_b_im", "ssm_c_re", "ssm_c_im",
         "ssm_d", "ssm_log_dt", "attn_sinks", "norm_ffn", "norm_ple", "norm_final")
WEIGHTS = ("rel_bias", "norm_mix", "w_in", "ssm_lambda_re", "ssm_lambda_im", "ssm_b_re", "ssm_b_im", "ssm_c_re",
           "ssm_c_im", "ssm_d", "ssm_log_dt", "ssm_w_glu", "conv_w", "attn_sinks", "w_branch", "w_out", "norm_ffn",
           "w_ffn_in", "w_ffn_out", "norm_ple", "w_ple_gate", "w_ple_proj", "norm_final")


def _c0(j):
    return 0


def _pick(n, cap, unit=128):
    if n <= cap:
        return n
    best = None
    for t in range(unit, cap + 1, unit):
        if n % t == 0:
            best = t
    assert best is not None, (n, cap, unit)
    return best


_DIMS = {"nn": ((1,), (0,)), "nt": ((1,), (1,)), "tn": ((0,), (0,))}


def _mm(a, b, mode, *, name, out_dtype=F32, add=None, tm=1024, tn=1024, tk=1024, b_k0=0, n_outer=False, rms_out=None, rms_back=None,
        back_bf16=False):
    if mode == "nn":
        (m, k), (k2, n) = a.shape, b.shape
    elif mode == "nt":
        (m, k), (n, k2) = a.shape, b.shape
    else:
        (k, m), (k2, n) = a.shape, b.shape
    assert k == k2 or (mode == "nt" and b_k0 + k <= k2), (a.shape, b.shape, mode)
    tm, tn, tk = _pick(m, tm, 128 if mode == "tn" else 8), _pick(n, tn), _pick(k, tk, 128 if mode != "tn" else 8)
    nk = k // tk
    assert b_k0 % tk == 0
    kb0 = b_k0 // tk
    def at(f):
        return (lambda j, i, kk: f(i, j, kk)) if n_outer else f

    a_spec = pl.BlockSpec((tk, tm), at(lambda i, j, kk: (kk, i))) if mode == "tn" else pl.BlockSpec((tm, tk), at(lambda i, j, kk: (i, kk)))
    b_spec = (pl.BlockSpec((tn, tk), at(lambda i, j, kk: (j, kb0 + kk))) if mode == "nt"
              else pl.BlockSpec((tk, tn), at(lambda i, j, kk: (kk, j))))
    o_spec = pl.BlockSpec((tm, tn), at(lambda i, j, kk: (i, j)))
    dims = (_DIMS[mode], ((), ()))
    has_add = add is not None
    fused_rows = rms_out is not None or rms_back is not None
    assert not fused_rows or (tn == n and not n_outer), "a fused RMSNorm needs whole rows in a tile"
    row_spec = pl.BlockSpec((1, n), at(lambda i, j, kk: (0, 0)))
    extra = [rms_out] if rms_out is not None else (list(rms_back) if rms_back is not None else [])
    extra_specs = [row_spec] if rms_out is not None else ([o_spec, o_spec, row_spec] if rms_back is not None else [])
    n_in = 2 + has_add + len(extra)

    def body(*refs):
        a_ref, b_ref = refs[0], refs[1]
        add_ref = refs[2] if has_add else None
        x_refs = refs[2 + has_add:n_in]
        o_refs, acc_ref = refs[n_in:-1], refs[-1]
        part = lax.dot_general(a_ref[...].astype(BF16), b_ref[...].astype(BF16), dims, preferred_element_type=F32)

        def finish(acc):
            if has_add:
                acc = acc + add_ref[...]
            if rms_back is not None:
                _, vjp = jax.vjp(_rms, x_refs[0][...], x_refs[2][...])
                dx, dg = vjp(acc)
                o_refs[0][...] = x_refs[1][...] + dx
                if back_bf16:
                    o_refs[2][...] = (x_refs[1][...] + dx).astype(BF16)
                first = pl.program_id(0) == 0

                @pl.when(first)
                def _():
                    o_refs[1][...] = dg

                @pl.when(jnp.logical_not(first))
                def _():
                    o_refs[1][...] += dg
                return
            o_refs[0][...] = acc.astype(o_refs[0].dtype)
            if rms_out is not None:
                o_refs[1][...] = _rms(acc, x_refs[0][...]).astype(BF16)

        if nk == 1:
            finish(part)
        else:
            kk = pl.program_id(2)

            @pl.when(kk == 0)
            def _():
                acc_ref[...] = part

            @pl.when(kk > 0)
            def _():
                acc_ref[...] += part

            @pl.when(kk == nk - 1)
            def _():
                finish(acc_ref[...])

    operands = [a, b] + ([add] if has_add else []) + extra
    in_specs = [a_spec, b_spec] + ([o_spec] if has_add else []) + extra_specs
    out_specs, out_shape = [o_spec], [jax.ShapeDtypeStruct((m, n), out_dtype)]
    if rms_out is not None:
        out_specs, out_shape = out_specs + [o_spec], out_shape + [jax.ShapeDtypeStruct((m, n), BF16)]
    if rms_back is not None:
        out_specs, out_shape = out_specs + [row_spec], out_shape + [jax.ShapeDtypeStruct((1, n), F32)]
        if back_bf16:
            out_specs, out_shape = out_specs + [o_spec], out_shape + [jax.ShapeDtypeStruct((m, n), BF16)]
    res = pl.pallas_call(
        body, name=name, grid=(n // tn, m // tm, nk) if n_outer else (m // tm, n // tn, nk), in_specs=in_specs, out_specs=out_specs,
        out_shape=out_shape, scratch_shapes=[pltpu.VMEM((tm, tn) if nk > 1 else (8, 128), F32)],
        compiler_params=pltpu.CompilerParams(
            dimension_semantics=("arbitrary",) * 3 if rms_back is not None else ("parallel", "parallel", "arbitrary")),
    )(*operands)
    return res if fused_rows else res[0]


def _rw(fn, ins, outs, *, name, params=(), reds=(), tm=256, ncol=1, with_j=False):
    t = ins[0][0].shape[0]
    tm = _pick(t, tm, 8)
    nrow = t // tm
    n_in, n_p, n_out = len(ins), len(params), len(outs)

    in_specs = [pl.BlockSpec((tm, bw or arr.shape[1]), lambda j, i, cf=cf: (i, cf(j))) for arr, bw, cf in ins]
    in_specs += [pl.BlockSpec(p.shape, lambda j, i: (0, 0)) for p in params]
    out_specs = [pl.BlockSpec((tm, bw or w), lambda j, i, cf=cf: (i, cf(j))) for w, bw, _, cf in outs]
    out_specs += [pl.BlockSpec((shp[0], bw or shp[1]), lambda j, i, cf=cf: (0, cf(j))) for shp, bw, cf in reds]
    out_shape = [jax.ShapeDtypeStruct((t, w), dt) for w, _, dt, _ in outs]
    out_shape += [jax.ShapeDtypeStruct(shp, F32) for shp, _, _ in reds]

    def body(*refs):
        in_refs, p_refs = refs[:n_in], refs[n_in:n_in + n_p]
        o_refs, r_refs = refs[n_in + n_p:n_in + n_p + n_out], refs[n_in + n_p + n_out:]
        args = [r[...].astype(F32) for r in in_refs] + [r[...] for r in p_refs]
        if with_j:
            args = [pl.program_id(0)] + args
        o_vals, r_vals = fn(*args)
        for r, v in zip(o_refs, o_vals, strict=True):
            r[...] = v.astype(r.dtype)
        if r_refs:
            i = pl.program_id(1)
            for r, v in zip(r_refs, r_vals, strict=True):
                @pl.when(i == 0)
                def _(r=r, v=v):
                    r[...] = v

                @pl.when(i > 0)
                def _(r=r, v=v):
                    r[...] += v

    res = pl.pallas_call(
        body, name=name, grid=(ncol, nrow), in_specs=in_specs, out_specs=out_specs, out_shape=out_shape,
        compiler_params=pltpu.CompilerParams(dimension_semantics=("parallel", "arbitrary" if reds else "parallel")),
    )(*[a for a, _, _ in ins], *params)
    return res


def _rms(x, g):
    return x * lax.rsqrt(jnp.mean(x * x, axis=-1, keepdims=True) + RMS_EPS) * g


def _rms_fwd(x, g, name):
    return _rw(lambda xv, gv: ([_rms(xv, gv)], []), [(x, None, _c0)], [(D_MODEL, None, BF16, _c0)], params=[g], name=name)[0]


def _bd_apply(acts, mats, combos, mode, *, name, tm=512, col_blocks=None):
    t = acts[0].shape[0]
    tm = _pick(t, tm, 8)
    nb, r, c = mats[0].shape
    win, wout = (r, c) if mode == "nn" else (c, r)
    dims = (_DIMS[mode], ((), ()))
    n_a, n_m = len(acts), len(mats)

    def body(*refs):
        a_vals = [ar[...].astype(BF16) for ar in refs[:n_a]]
        m_refs, o_refs = refs[n_a:n_a + n_m], refs[n_a + n_m:]
        for o_ref, terms in zip(o_refs, combos, strict=True):
            for j in range(nb):
                acc = None
                for ai, mi in terms:
                    part = lax.dot_general(a_vals[ai][:, j * win:(j + 1) * win], m_refs[mi][j], dims, preferred_element_type=F32)
                    acc = part if acc is None else acc + part
                o_ref[:, j * wout:(j + 1) * wout] = acc

    return pl.pallas_call(
        body, name=name, grid=(t // tm,),
        in_specs=[pl.BlockSpec((tm, nb * win), lambda i, cb=cb: (i, cb)) for cb in (col_blocks or [0] * n_a)]
        + [pl.BlockSpec(m.shape, lambda i: (0, 0, 0)) for m in mats],
        out_specs=[pl.BlockSpec((tm, nb * wout), lambda i: (i, 0))] * len(combos),
        out_shape=[jax.ShapeDtypeStruct((t, nb * wout), F32)] * len(combos),
        compiler_params=pltpu.CompilerParams(dimension_semantics=("parallel",)),
    )(*acts, *mats)


def _bd_grads(arrs, widths, pairs, *, name, tk=512, col_blocks=None):
    t = arrs[0].shape[0]
    tk = _pick(t, tk, 8)
    n_a = len(arrs)
    dims = (_DIMS["tn"], ((), ()))

    def body(*refs):
        vals = [ar[...].astype(BF16) for ar in refs[:n_a]]
        o_refs = refs[n_a:]
        @pl.when(pl.program_id(0) == 0)
        def _():
            for o_ref in o_refs:
                o_ref[...] = jnp.zeros_like(o_ref)

        for o_ref, (ai, bi) in zip(o_refs, pairs, strict=True):
            wa, wb = widths[ai], widths[bi]
            for j in range(SSM_BLOCKS):
                o_ref[j] += lax.dot_general(vals[ai][:, j * wa:(j + 1) * wa], vals[bi][:, j * wb:(j + 1) * wb], dims,
                                            preferred_element_type=F32)

    return pl.pallas_call(
        body, name=name, grid=(t // tk,),
        in_specs=[pl.BlockSpec((tk, SSM_BLOCKS * w), lambda k, cb=cb: (k, cb)) for w, cb in zip(widths, col_blocks or [0] * n_a, strict=True)],
        out_specs=[pl.BlockSpec((SSM_BLOCKS, widths[ai], widths[bi]), lambda k: (0, 0, 0)) for ai, bi in pairs],
        out_shape=[jax.ShapeDtypeStruct((SSM_BLOCKS, widths[ai], widths[bi]), F32) for ai, bi in pairs],
        compiler_params=pltpu.CompilerParams(dimension_semantics=("arbitrary",)),
    )(*arrs)


SCAN_LW = 512
SCAN_ROWS = 512
_DOUBLING = ((1, 0), (2, 1), (4, 2))


def _scan(xr, xi, pr, pi, dr, di, *, reverse, name, hr=None, hi=None):
    t, s = xr.shape
    lc = _pick(t, SCAN_ROWS, 8)
    nt, ngroups = t // lc, lc // 8
    with_da = hr is not None

    def tmap(l, tt):
        return ((nt - 1 - tt) if reverse else tt, l)

    x_spec = pl.BlockSpec((lc, SCAN_LW), tmap)
    tab_spec = pl.BlockSpec((8, SCAN_LW), lambda l, tt: (0, l))

    def body(*refs):
        xr_ref, xi_ref, pr_ref, pi_ref, dr_ref, di_ref = refs[:6]
        if with_da:
            hr_ref, hi_ref, or_ref, oi_ref, ar_ref, ai_ref, cr_ref, ci_ref = refs[6:]
        else:
            or_ref, oi_ref, cr_ref, ci_ref = refs[6:]
        tt = pl.program_id(1)

        @pl.when(tt == 0)
        def _():
            cr_ref[...] = jnp.zeros_like(cr_ref)
            ci_ref[...] = jnp.zeros_like(ci_ref)
            if with_da:
                ar_ref[...] = jnp.zeros_like(ar_ref)
                ai_ref[...] = jnp.zeros_like(ai_ref)

        sub = lax.broadcasted_iota(jnp.int32, (8, SCAN_LW), 0)
        pw_r, pw_i = pr_ref[...], pi_ref[...]

        def step(g, carry):
            g = (ngroups - 1 - g) if reverse else g
            r0 = pl.multiple_of(g * 8, 8)
            vr, vi = xr_ref[pl.ds(r0, 8), :], xi_ref[pl.ds(r0, 8), :]
            for shift, row in _DOUBLING:
                a_r, a_i = dr_ref[row:row + 1, :], di_ref[row:row + 1, :]
                if reverse:
                    keep = sub < 8 - shift
                    sr, si = pltpu.roll(vr, 8 - shift, 0), pltpu.roll(vi, 8 - shift, 0)
                else:
                    keep = sub >= shift
                    sr, si = pltpu.roll(vr, shift, 0), pltpu.roll(vi, shift, 0)
                sr, si = jnp.where(keep, sr, 0.0), jnp.where(keep, si, 0.0)
                vr, vi = vr + a_r * sr - a_i * si, vi + a_r * si + a_i * sr
            if with_da:
                cr, ci, acc_r, acc_i = carry
            else:
                cr, ci = carry
            vr, vi = vr + pw_r * cr - pw_i * ci, vi + pw_r * ci + pw_i * cr
            or_ref[pl.ds(r0, 8), :] = vr
            oi_ref[pl.ds(r0, 8), :] = vi
            if with_da:
                nr = jnp.where(sub < 7, pltpu.roll(vr, 7, 0), cr)
                ni = jnp.where(sub < 7, pltpu.roll(vi, 7, 0), ci)
                h_r, h_i = hr_ref[pl.ds(r0, 8), :], hi_ref[pl.ds(r0, 8), :]
                acc_r = acc_r + h_r * nr + h_i * ni
                acc_i = acc_i + h_r * ni - h_i * nr
            edge = 0 if reverse else 7
            cr = jnp.broadcast_to(vr[edge:edge + 1, :], vr.shape)
            ci = jnp.broadcast_to(vi[edge:edge + 1, :], vi.shape)
            return (cr, ci, acc_r, acc_i) if with_da else (cr, ci)

        zero = jnp.zeros((8, SCAN_LW), F32)
        init = (cr_ref[...], ci_ref[...]) + ((zero, zero) if with_da else ())
        fin = lax.fori_loop(0, ngroups, step, init, unroll=2)
        cr_ref[...] = fin[0]
        ci_ref[...] = fin[1]
        if with_da:
            ar_ref[...] += fin[2]
            ai_ref[...] += fin[3]

    n_x = 4 if with_da else 2
    out_specs = [x_spec, x_spec] + ([tab_spec, tab_spec] if with_da else [])
    out_shape = [jax.ShapeDtypeStruct((t, s), F32)] * 2 + ([jax.ShapeDtypeStruct((8, s), F32)] * 2 if with_da else [])
    operands = [xr, xi, pr, pi, dr, di] + ([hr, hi] if with_da else [])
    return pl.pallas_call(
        body, name=name, grid=(s // SCAN_LW, nt),
        in_specs=[x_spec, x_spec] + [tab_spec] * 4 + [x_spec] * (n_x - 2),
        out_specs=out_specs, out_shape=out_shape,
        scratch_shapes=[pltpu.VMEM((8, SCAN_LW), F32), pltpu.VMEM((8, SCAN_LW), F32)],
        compiler_params=pltpu.CompilerParams(dimension_semantics=("parallel", "arbitrary")),
    )(*operands)


CONV_TM = 256
HALO = 16


def _conv_specs(t, tm):
    nrow = t // tm
    hb = tm // HALO

    def col(cidx):
        return pl.BlockSpec((tm, BRANCH), lambda i: (i, cidx))

    def prev(cidx):
        return pl.BlockSpec((HALO, BRANCH), lambda i: (jnp.maximum(i * hb - 1, 0), cidx))

    def nxt(cidx):
        return pl.BlockSpec((HALO, BRANCH), lambda i: (jnp.minimum((i + 1) * hb, nrow * hb - 1), cidx))

    return nrow, col, prev, nxt


def _conv_taps(cc, cx, cc_prev, cx_prev, first):
    tm = cc.shape[0]
    v = cc * cx
    halo = cc_prev * cx_prev * jnp.where(first, 0.0, 1.0)
    ext = jnp.concatenate([halo, v], axis=0)
    return v, pltpu.roll(ext, 1, 0)[HALO:HALO + tm], pltpu.roll(ext, 2, 0)[HALO:HALO + tm]


def _conv_fwd(z, conv_w, name):
    t = z.shape[0]
    tm = _pick(t, CONV_TM, HALO)
    nrow, col, prev, _ = _conv_specs(t, tm)

    def body(cb_ref, cc_ref, cx_ref, ccp_ref, cxp_ref, w_ref, o_ref):
        first = pl.program_id(0) == 0
        v, v1, v2 = _conv_taps(*(r[...].astype(F32) for r in (cc_ref, cx_ref, ccp_ref, cxp_ref)), first)
        y = w_ref[0:1, :] * v2 + w_ref[1:2, :] * v1 + w_ref[2:3, :] * v
        o_ref[...] = (cb_ref[...].astype(F32) * y).astype(o_ref.dtype)

    return pl.pallas_call(
        body, name=name, grid=(nrow,),
        in_specs=[col(CB_BLK), col(CC_BLK), col(CX_BLK), prev(CC_BLK), prev(CX_BLK), pl.BlockSpec((3, BRANCH), lambda i: (0, 0))],
        out_specs=pl.BlockSpec((tm, BRANCH), lambda i: (i, 0)), out_shape=jax.ShapeDtypeStruct((t, BRANCH), BF16),
        compiler_params=pltpu.CompilerParams(dimension_semantics=("parallel",)),
    )(z, z, z, z, z, conv_w)


def _conv_bwd(dyc, z, conv_w, name):
    t = z.shape[0]
    tm = _pick(t, CONV_TM, HALO)
    nrow, col, prev, nxt = _conv_specs(t, tm)
    d_cur = pl.BlockSpec((tm, BRANCH), lambda i: (i, 0))
    d_nxt = pl.BlockSpec((HALO, BRANCH), lambda i: (jnp.minimum((i + 1) * (tm // HALO), nrow * (tm // HALO) - 1), 0))

    def body(dy_ref, dyn_ref, cb_ref, cbn_ref, cc_ref, cx_ref, ccp_ref, cxp_ref, w_ref, dcb_ref, dcc_ref, dcx_ref, dw_ref):
        i = pl.program_id(0)
        cc, cx, cb = cc_ref[...].astype(F32), cx_ref[...].astype(F32), cb_ref[...].astype(F32)
        v, v1, v2 = _conv_taps(cc, cx, ccp_ref[...].astype(F32), cxp_ref[...].astype(F32), i == 0)
        w0, w1, w2 = w_ref[0:1, :], w_ref[1:2, :], w_ref[2:3, :]
        y = w0 * v2 + w1 * v1 + w2 * v
        dyc_v = dy_ref[...].astype(F32)
        dcb_ref[...] = (dyc_v * y).astype(dcb_ref.dtype)
        dy = dyc_v * cb
        halo = dyn_ref[...].astype(F32) * cbn_ref[...].astype(F32) * jnp.where(i == nrow - 1, 0.0, 1.0)
        ext = jnp.concatenate([dy, halo], axis=0)
        dy1 = pltpu.roll(ext, tm + HALO - 1, 0)[0:tm]
        dy2 = pltpu.roll(ext, tm + HALO - 2, 0)[0:tm]
        dv = w2 * dy + w1 * dy1 + w0 * dy2
        dcc_ref[...] = (dv * cx).astype(dcc_ref.dtype)
        dcx_ref[...] = (dv * cc).astype(dcx_ref.dtype)
        dw = jnp.concatenate([jnp.sum(dy * v2, axis=0, keepdims=True), jnp.sum(dy * v1, axis=0, keepdims=True),
                              jnp.sum(dy * v, axis=0, keepdims=True), jnp.zeros((5, BRANCH), F32)], axis=0)

        @pl.when(i == 0)
        def _():
            dw_ref[...] = dw

        @pl.when(i > 0)
        def _():
            dw_ref[...] += dw

    o_spec = pl.BlockSpec((tm, BRANCH), lambda i: (i, 0))
    return pl.pallas_call(
        body, name=name, grid=(nrow,),
        in_specs=[d_cur, d_nxt, col(CB_BLK), nxt(CB_BLK), col(CC_BLK), col(CX_BLK), prev(CC_BLK), prev(CX_BLK),
                  pl.BlockSpec((3, BRANCH), lambda i: (0, 0))],
        out_specs=[o_spec, o_spec, o_spec, pl.BlockSpec((8, BRANCH), lambda i: (0, 0))],
        out_shape=[jax.ShapeDtypeStruct((t, BRANCH), BF16)] * 3 + [jax.ShapeDtypeStruct((8, BRANCH), F32)],
        compiler_params=pltpu.CompilerParams(dimension_semantics=("arbitrary",)),
    )(dyc, dyc, z, z, z, z, z, z, conv_w)


ATTN_BLOCKS = 8
ATTN_BLOCKS_BWD = 1
GROUP_ROWS = GQA * WINDOW


def _attn_specs(nblk):
    rows = nblk * WINDOW
    q_spec = pl.BlockSpec((N_Q, rows, HEAD_DIM), lambda n: (0, n, 0))
    kv_cur = pl.BlockSpec((N_KV, rows, HEAD_DIM), lambda n: (0, n, 0))
    kv_prev = pl.BlockSpec((N_KV, WINDOW, HEAD_DIM), lambda n: (0, jnp.maximum(n * nblk - 1, 0), 0))
    bias_spec = pl.BlockSpec((N_Q, WINDOW, 2 * WINDOW), lambda n: (0, 0, 0))
    sink_spec = pl.BlockSpec((N_Q, 1), lambda n: (0, 0))
    return q_spec, kv_cur, kv_prev, bias_spec, sink_spec


def _attn_valid(first_key):
    qi = lax.broadcasted_iota(jnp.int32, (GROUP_ROWS, 2 * WINDOW), 0) & (WINDOW - 1)
    kj = lax.broadcasted_iota(jnp.int32, (GROUP_ROWS, 2 * WINDOW), 1)
    dist = qi + WINDOW - kj
    return (dist >= 0) & (dist < WINDOW) & (kj >= first_key)


def _attn_masks(n):
    return _attn_valid(jnp.where(n > 0, 0, WINDOW)), _attn_valid(0)


def _blk(b):
    return slice(b * WINDOW, (b + 1) * WINDOW)


def _group(ref, h, b, width):
    return ref[GQA * h:GQA * (h + 1), _blk(b)].reshape(GROUP_ROWS, width)


def _keys(prev_ref, cur_ref, h, b):
    prev = prev_ref[h] if b == 0 else cur_ref[h, _blk(b - 1)]
    return jnp.concatenate([prev, cur_ref[h, _blk(b)]], axis=0)


def _group_sinks(s_ref, h):
    return jnp.concatenate([jnp.broadcast_to(s_ref[GQA * h + g:GQA * h + g + 1, :], (WINDOW, 1)) for g in range(GQA)], axis=0)


def _attn_probs(q, kc, bias, sink, valid):
    s = lax.dot_general(q, kc, (_DIMS["nt"], ((), ())), preferred_element_type=F32) * ATTN_SCALE + bias
    s = jnp.where(valid, s, NEG)
    m = jnp.maximum(jnp.max(s, axis=1, keepdims=True), sink)
    p = jnp.exp(s - m)
    e_sink = jnp.exp(sink - m)
    inv = 1.0 / (jnp.sum(p, axis=1, keepdims=True) + e_sink)
    return p * inv, e_sink * inv


def _attn_fwd(qh, kh, vh, bias, sinks, name):
    t = qh.shape[1]
    nblk = min(ATTN_BLOCKS, t // WINDOW)
    q_spec, kv_cur, kv_prev, bias_spec, sink_spec = _attn_specs(nblk)

    def body(q_ref, kp_ref, kc_ref, vp_ref, vc_ref, b_ref, s_ref, o_ref):
        masks = _attn_masks(pl.program_id(0))
        for b in range(nblk):
            for h in range(N_KV):
                kc, vc = _keys(kp_ref, kc_ref, h, b), _keys(vp_ref, vc_ref, h, b)
                w, _ = _attn_probs(_group(q_ref, h, b, HEAD_DIM), kc, _group(b_ref, h, 0, 2 * WINDOW), _group_sinks(s_ref, h),
                                   masks[min(b, 1)])
                o = jnp.dot(w.astype(BF16), vc, preferred_element_type=F32)
                o_ref[GQA * h:GQA * (h + 1), _blk(b)] = o.reshape(GQA, WINDOW, HEAD_DIM).astype(o_ref.dtype)

    return pl.pallas_call(
        body, name=name, grid=(t // (nblk * WINDOW),),
        in_specs=[q_spec, kv_prev, kv_cur, kv_prev, kv_cur, bias_spec, sink_spec],
        out_specs=q_spec, out_shape=jax.ShapeDtypeStruct((N_Q, t, HEAD_DIM), BF16),
        compiler_params=pltpu.CompilerParams(dimension_semantics=("parallel",)),
    )(qh, kh, kh, vh, vh, bias, sinks)


def _attn_bwd(qh, kh, vh, doh, bias, sinks, name):
    t = qh.shape[1]
    nblk = min(ATTN_BLOCKS_BWD, t // WINDOW)
    nsteps = t // (nblk * WINDOW)
    q_spec, kv_cur, kv_prev, bias_spec, sink_spec = _attn_specs(nblk)

    def body(q_ref, kp_ref, kc_ref, vp_ref, vc_ref, do_ref, b_ref, s_ref,
             dq_ref, dkc_ref, dkp_ref, dvc_ref, dvp_ref, db_ref, ds_ref):
        n = pl.program_id(0)
        masks = _attn_masks(n)

        @pl.when(n == 0)
        def _():
            db_ref[...] = jnp.zeros_like(db_ref)
            ds_ref[...] = jnp.zeros_like(ds_ref)

        def scatter(part, h, b, cur_ref, prev_ref):
            if b == 0:
                prev_ref[h] = part[0:WINDOW]
            else:
                cur_ref[h, _blk(b - 1)] += part[0:WINDOW]
            cur_ref[h, _blk(b)] = part[WINDOW:2 * WINDOW]

        d_bias, d_sink = [None] * N_KV, [None] * N_KV
        for b in range(nblk):
            for h in range(N_KV):
                kc, vc = _keys(kp_ref, kc_ref, h, b), _keys(vp_ref, vc_ref, h, b)
                heads = slice(GQA * h, GQA * (h + 1))
                q, do = _group(q_ref, h, b, HEAD_DIM), _group(do_ref, h, b, HEAD_DIM)
                w, w_sink = _attn_probs(q, kc, _group(b_ref, h, 0, 2 * WINDOW), _group_sinks(s_ref, h), masks[min(b, 1)])
                dw = lax.dot_general(do, vc, (_DIMS["nt"], ((), ())), preferred_element_type=F32)
                delta = jnp.sum(w * dw, axis=1, keepdims=True)
                dscore = w * (dw - delta)
                d_sink[h] = -w_sink * delta if b == 0 else d_sink[h] - w_sink * delta
                d_bias[h] = dscore if b == 0 else d_bias[h] + dscore
                dsb = dscore.astype(BF16)
                dq_ref[heads, _blk(b)] = (jnp.dot(dsb, kc, preferred_element_type=F32) * ATTN_SCALE).reshape(GQA, WINDOW, HEAD_DIM)
                scatter(lax.dot_general(dsb, q, (_DIMS["tn"], ((), ())), preferred_element_type=F32) * ATTN_SCALE, h, b, dkc_ref, dkp_ref)
                scatter(lax.dot_general(w.astype(BF16), do, (_DIMS["tn"], ((), ())), preferred_element_type=F32), h, b, dvc_ref, dvp_ref)
        for h in range(N_KV):
            heads = slice(GQA * h, GQA * (h + 1))
            ds_ref[heads] += d_sink[h].reshape(GQA, WINDOW, 1)
            db_ref[heads] += d_bias[h].reshape(GQA, WINDOW, 2 * WINDOW)

    kv_shape = jax.ShapeDtypeStruct((N_KV, t, HEAD_DIM), F32)
    kv_prev_out = pl.BlockSpec((N_KV, WINDOW, HEAD_DIM), lambda n: (0, n, 0))
    kv_prev_shape = jax.ShapeDtypeStruct((N_KV, nsteps * WINDOW, HEAD_DIM), F32)
    return pl.pallas_call(
        body, name=name, grid=(nsteps,),
        in_specs=[q_spec, kv_prev, kv_cur, kv_prev, kv_cur, q_spec, bias_spec, sink_spec],
        out_specs=[q_spec, kv_cur, kv_prev_out, kv_cur, kv_prev_out, bias_spec, pl.BlockSpec((N_Q, WINDOW, 1), lambda n: (0, 0, 0))],
        out_shape=[jax.ShapeDtypeStruct((N_Q, t, HEAD_DIM), F32), kv_shape, kv_prev_shape, kv_shape, kv_prev_shape,
                   jax.ShapeDtypeStruct((N_Q, WINDOW, 2 * WINDOW), F32), jax.ShapeDtypeStruct((N_Q, WINDOW, 1), F32)],
        compiler_params=pltpu.CompilerParams(dimension_semantics=("arbitrary",)),
    )(qh, kh, kh, vh, vh, doh, bias, sinks)


def _heads(a, n_heads):
    t = a.shape[0]
    return a.astype(BF16).reshape(t, n_heads, HEAD_DIM).transpose(1, 0, 2)


def _unheads(a):
    n_heads, t, _ = a.shape
    return a.transpose(1, 0, 2).reshape(t, n_heads * HEAD_DIM)


def _shift_blocks(cur, prev):
    n_kv, t, d = cur.shape
    nsteps = prev.shape[1] // WINDOW
    nblk = t // (nsteps * WINDOW)
    late = jnp.concatenate([prev.reshape(n_kv, nsteps, WINDOW, d)[:, 1:], jnp.zeros((n_kv, 1, WINDOW, d), cur.dtype)], axis=1)
    delta = jnp.concatenate([jnp.zeros((n_kv, nsteps, nblk - 1, WINDOW, d), cur.dtype), late[:, :, None]], axis=2)
    return (cur.reshape(n_kv, nsteps, nblk, WINDOW, d) + delta).reshape(n_kv, t, d)


def _t5_bucket_table():
    qi = np.arange(WINDOW)[:, None]
    kj = np.arange(2 * WINDOW)[None, :]
    dist = np.clip(qi + WINDOW - kj, 0, REL_MAX_DIST - 1)
    exact = REL_BUCKETS // 2
    df = np.maximum(dist, 1).astype(np.float32)
    large = exact + (np.log(df / np.float32(exact)) / np.float32(math.log(REL_MAX_DIST / exact)) * (REL_BUCKETS - exact)).astype(np.int32)
    large = np.minimum(large, REL_BUCKETS - 1)
    bucket = np.where(dist < exact, dist, large)
    onehot = np.zeros((WINDOW * 2 * WINDOW, REL_BUCKETS), np.float32)
    onehot[np.arange(WINDOW * 2 * WINDOW), bucket.reshape(-1)] = 1.0
    return onehot


def _band_bias(rel_bias):
    onehot = jnp.asarray(_t5_bucket_table())
    sel = jnp.sum(onehot[:, :, None] * rel_bias[None, :, :], axis=1)
    return sel.T.reshape(N_Q, WINDOW, 2 * WINDOW)


def _block_diag(a):
    g, r, c = a.shape
    a4 = a.reshape(SSM_BLOCKS, g // SSM_BLOCKS, r, c)
    eye = jnp.eye(g // SSM_BLOCKS, dtype=a.dtype)
    full = a4[:, :, :, None, :] * eye[None, :, None, :, None]
    return full.reshape(SSM_BLOCKS, (g // SSM_BLOCKS) * r, (g // SSM_BLOCKS) * c)


def _ssm_disc(lam_re, lam_im, b_re, b_im, c_re, c_im, log_dt):
    dt = jnp.exp(log_dt)[:, None]
    mag = jnp.exp(lam_re * dt)
    ang = lam_im * dt
    a_re = mag * jnp.cos(ang)
    a_im = mag * jnp.sin(ang)
    den = lam_re * lam_re + lam_im * lam_im
    nr = a_re - 1.0
    coef_re = (nr * lam_re + a_im * lam_im) / den
    coef_im = (a_im * lam_re - nr * lam_im) / den
    bb_re = coef_re[..., None] * b_re - coef_im[..., None] * b_im
    bb_im = coef_re[..., None] * b_im + coef_im[..., None] * b_re
    wb_re = _block_diag(jnp.swapaxes(bb_re, 1, 2))
    wb_im = _block_diag(jnp.swapaxes(bb_im, 1, 2))
    cm_re = _block_diag(jnp.swapaxes(c_re, 1, 2))
    cm_imn = _block_diag(-jnp.swapaxes(c_im, 1, 2))
    return a_re.reshape(-1), a_im.reshape(-1), wb_re, wb_im, cm_re, cm_imn


def _scan_tables(a_re, a_im):
    pr, pi = [a_re], [a_im]
    for _ in range(7):
        pr, pi = pr + [pr[-1] * a_re - pi[-1] * a_im], pi + [pr[-1] * a_im + pi[-1] * a_re]
    pr, pi = jnp.stack(pr), jnp.stack(pi)
    pad = jnp.zeros((5,) + a_re.shape, F32)
    dr = jnp.concatenate([jnp.stack([pr[0], pr[1], pr[3]]), pad])
    di = jnp.concatenate([jnp.stack([pi[0], pi[1], pi[3]]), pad])
    fwd = (pr, pi, dr, di)
    rev = (pr[::-1], -pi[::-1], dr, -di)
    return jax.tree.map(lax.stop_gradient, (fwd, rev))


def _gate_col(r):
    return lambda j: OFF_G // D_MODEL + r


def _layer_fwd(x, h, p_i, w, bias, li, next_gain):
    nm = lambda s: f"{s}_l{li}"
    z = _mm(h, w["w_in"], "nn", tm=1024, tn=2944, n_outer=True, out_dtype=BF16, name=nm("mm_in"))
    bu_re, bu_im = _bd_apply([z], [w["wb_re"], w["wb_im"]], [[(0, 0)], [(0, 1)]], "nn", col_blocks=[U_BLK], name=nm("ssm_bu"))
    h_re, h_im = _scan(bu_re, bu_im, *w["scan_fwd"], reverse=False, name=nm("ssm_scan"))
    (y0,) = _bd_apply([h_re, h_im], [w["cm_re"], w["cm_imn"]], [[(0, 0), (1, 1)]], "nn", name=nm("ssm_c"))
    (y1,) = _rw(lambda a, u, d: ([jax.nn.gelu(a + d * u)], []),
                [(y0, None, _c0), (z, BRANCH, lambda j: U_BLK)], [(BRANCH, None, F32, _c0)],
                params=[w["ssm_d"]], name=nm("ssm_gelu"))
    gl = _mm(y1, w["ssm_w_glu"], "nn", name=nm("mm_glu"))
    (y_ssm,) = _rw(lambda a, b: ([a * jax.nn.sigmoid(b)], []), [(y1, None, _c0), (gl, None, _c0)],
                   [(BRANCH, None, BF16, _c0)], name=nm("ssm_glu"))
    y_conv = _conv_fwd(z, w["conv_w"], nm("conv_fwd"))
    kv_w = N_KV * HEAD_DIM
    q2, k2, v2 = _rw(lambda q, k, v: ([q, k, v], []),
                     [(z, BRANCH, lambda j: OFF_Q // BRANCH), (z, kv_w, lambda j: OFF_K // kv_w), (z, kv_w, lambda j: OFF_V // kv_w)],
                     [(BRANCH, None, BF16, _c0), (kv_w, None, BF16, _c0), (kv_w, None, BF16, _c0)], name=nm("qkv_bf16"))
    qh, kh, vh = _heads(q2, N_Q), _heads(k2, N_KV), _heads(v2, N_KV)
    y_attn = _unheads(_attn_fwd(qh, kh, vh, bias, w["sinks"], nm("attn_fwd")))
    ys = (y_ssm, y_conv, y_attn)
    bs = [_mm(ys[r], w["w_branch"][r], "nn", out_dtype=BF16, name=nm(f"mm_branch{r}")) for r in range(3)]

    def merge(g0, g1, g2, b0, b1, b2):
        return [jax.nn.sigmoid(g0) * b0 + jax.nn.sigmoid(g1) * b1 + jax.nn.sigmoid(g2) * b2], []

    (merged,) = _rw(merge, [(z, D_MODEL, _gate_col(r)) for r in range(3)] + [(b, None, _c0) for b in bs],
                    [(D_MODEL, None, BF16, _c0)], name=nm("merge"))
    x1, hf_in = _mm(merged, w["w_out"], "nn", add=x, rms_out=w["norm_ffn"], name=nm("mm_out"))
    hf = _mm(hf_in, w["w_ffn_in"], "nn", tn=1408, n_outer=True, out_dtype=BF16, name=nm("mm_ffn_in"))
    (act,) = _rw(lambda a, b: ([jax.nn.silu(a) * b], []), [(hf, FFN_COLS, lambda j: j), (hf, FFN_COLS, lambda j: FFN_NCOL + j)],
                 [(FFN_HIDDEN, FFN_COLS, BF16, lambda j: j)], ncol=FFN_NCOL, name=nm("swiglu"))
    x2, hp = _mm(act, w["w_ffn_out"], "nn", add=x1, tk=1408, rms_out=w["norm_ple"], name=nm("mm_ffn_out"))
    pgl = _mm(hp, w["w_ple_gate"], "nn", name=nm("mm_ple_gate"))
    pp = _mm(p_i, w["w_ple_proj"], "nn", name=nm("mm_ple_proj"))

    def ple_add(xv, a, b, *gain):
        x3v = xv + jax.nn.sigmoid(a) * b
        return [x3v] + [_rms(x3v, g) for g in gain], []

    has_next = next_gain is not None
    res = _rw(ple_add, [(x2, None, _c0), (pgl, None, _c0), (pp, None, _c0)],
              [(D_MODEL, None, F32, _c0)] + [(D_MODEL, None, BF16, _c0)] * has_next, params=[next_gain] * has_next, name=nm("ple_add"))
    x3, h_next = res[0], (res[1] if has_next else None)
    saved = dict(x=x, p=p_i, h=h, z=z, h_re=h_re, h_im=h_im, y0=y0, y1=y1, gl=gl, ys=ys, qh=qh, kh=kh, vh=vh,
                 bs=bs, merged=merged, x1=x1, hf_in=hf_in, hf=hf, act=act, x2=x2, hp=hp, pgl=pgl, pp=pp)
    return x3, h_next, saved


def _layer_bwd(dx3, s, w, bias, li):
    nm = lambda n: f"{n}_l{li}"
    g = {}
    z = s["z"]
    def ple_b(d, a, b):
        _, vjp = jax.vjp(lambda a_, b_: jax.nn.sigmoid(a_) * b_, a, b)
        return list(vjp(d)), []

    dpgl, dpp = _rw(ple_b, [(dx3, None, _c0), (s["pgl"], None, _c0), (s["pp"], None, _c0)],
                    [(D_MODEL, None, BF16, _c0)] * 2, name=nm("ple_bwd"))
    g["w_ple_proj"] = _mm(s["p"], dpp, "tn", name=nm("mmg_ple_proj"))
    g["w_ple_gate"] = _mm(s["hp"], dpgl, "tn", name=nm("mmg_ple_gate"))
    dx2, g["norm_ple"], dx2_op = _mm(dpgl, w["w_ple_gate"], "nt", rms_back=(s["x2"], dx3, w["norm_ple"]), back_bf16=True,
                                     name=nm("mmb_ple_gate"))
    dact = _mm(dx2_op, w["w_ffn_out"], "nt", tn=1408, out_dtype=BF16, name=nm("mmb_ffn_out"))
    g["w_ffn_out"] = _mm(s["act"], dx2_op, "tn", tm=1408, name=nm("mmg_ffn_out"))

    def swiglu_b(a, b, d):
        _, vjp = jax.vjp(lambda a_, b_: jax.nn.silu(a_) * b_, a, b)
        return list(vjp(d)), []

    dhf_a, dhf_b = _rw(swiglu_b, [(s["hf"], FFN_COLS, lambda j: j), (s["hf"], FFN_COLS, lambda j: FFN_NCOL + j), (dact, FFN_COLS, lambda j: j)],
                       [(FFN_HIDDEN, FFN_COLS, BF16, lambda j: j)] * 2, ncol=FFN_NCOL, name=nm("swiglu_bwd"))
    g["w_ffn_in"] = jnp.concatenate([_mm(s["hf_in"], dhf_a, "tn", tn=1408, name=nm("mmg_ffn_in_a")),
                                     _mm(s["hf_in"], dhf_b, "tn", tn=1408, name=nm("mmg_ffn_in_b"))], axis=1)
    dhf_in = _mm(dhf_a, w["w_ffn_in"], "nt", tk=1408, name=nm("mmb_ffn_in_a"))
    dx1, g["norm_ffn"], dx1_op = _mm(dhf_b, w["w_ffn_in"], "nt", tm=512, tk=1408, b_k0=FFN_HIDDEN, add=dhf_in,
                                     rms_back=(s["x1"], dx2, w["norm_ffn"]), back_bf16=True, name=nm("mmb_ffn_in_b"))
    dmerged = _mm(dx1_op, w["w_out"], "nt", out_dtype=BF16, name=nm("mmb_out"))
    g["w_out"] = _mm(s["merged"], dx1_op, "tn", name=nm("mmg_out"))

    def merge_b(d, g0, g1, g2, b0, b1, b2):
        outs_g, outs_b = [], []
        for gate, br in ((g0, b0), (g1, b1), (g2, b2)):
            sg = jax.nn.sigmoid(gate)
            outs_g.append(d * br * sg * (1.0 - sg))
            outs_b.append(d * sg)
        return outs_g + outs_b, []

    res = _rw(merge_b, [(dmerged, None, _c0)] + [(z, D_MODEL, _gate_col(r)) for r in range(3)] + [(b, None, _c0) for b in s["bs"]],
              [(D_MODEL, None, BF16, _c0)] * 6, name=nm("merge_bwd"))
    dgates, dbs = res[:3], res[3:]
    dys = [_mm(dbs[r], w["w_branch"][r], "nt", out_dtype=BF16, name=nm(f"mmb_branch{r}")) for r in range(3)]
    g["w_branch"] = jnp.stack([_mm(s["ys"][r], dbs[r], "tn", name=nm(f"mmg_branch{r}")) for r in range(3)])
    doh = _heads(dys[2], N_Q)
    dqh, dkc, dkp, dvc, dvp, dbias, dsink = _attn_bwd(s["qh"], s["kh"], s["vh"], doh, bias, w["sinks"], nm("attn_bwd"))
    dq, dk, dv = _unheads(dqh), _unheads(_shift_blocks(dkc, dkp)), _unheads(_shift_blocks(dvc, dvp))
    g["sinks"] = jnp.sum(dsink, axis=(1, 2))
    dcb, dcc, dcx, dconv = _conv_bwd(dys[1], z, w["conv_w"], nm("conv_bwd"))
    g["conv_w"] = dconv[0:3]
    def glu_b(d, y1, gl):
        sg = jax.nn.sigmoid(gl)
        return [d * y1 * sg * (1.0 - sg), d * sg], []

    dgl, dy1a = _rw(glu_b, [(dys[0], None, _c0), (s["y1"], None, _c0), (s["gl"], None, _c0)],
                    [(BRANCH, None, BF16, _c0), (BRANCH, None, F32, _c0)], name=nm("ssm_glu_bwd"))
    g["ssm_w_glu"] = _mm(s["y1"], dgl, "tn", name=nm("mmg_glu"))
    dy1b = _mm(dgl, w["ssm_w_glu"], "nt", name=nm("mmb_glu"))

    def gelu_b(da, db, a, u, d):
        _, vjp = jax.vjp(lambda pre: jax.nn.gelu(pre), a + d * u)
        (dy0,) = vjp(da + db)
        return [dy0, dy0 * d], [jnp.sum(dy0 * u, axis=0, keepdims=True)]

    dy0, du_a, g["ssm_d"] = _rw(gelu_b, [(dy1a, None, _c0), (dy1b, None, _c0), (s["y0"], None, _c0), (z, BRANCH, lambda j: U_BLK)],
                                [(BRANCH, None, BF16, _c0), (BRANCH, None, F32, _c0)], params=[w["ssm_d"]],
                                reds=[((1, BRANCH), None, _c0)], name=nm("ssm_gelu_bwd"))
    dh_re, dh_im = _bd_apply([dy0], [w["cm_re"], w["cm_imn"]], [[(0, 0)], [(0, 1)]], "nt", name=nm("ssmb_c"))
    sb, cb = SSM_STATES // SSM_BLOCKS, BRANCH // SSM_BLOCKS
    g["cm_re"], g["cm_imn"] = _bd_grads([s["h_re"], s["h_im"], dy0], [sb, sb, cb], [(0, 2), (1, 2)], name=nm("ssmg_c"))
    l_re, l_im, da_re, da_im = _scan(dh_re, dh_im, *w["scan_rev"], reverse=True, hr=s["h_re"], hi=s["h_im"], name=nm("ssm_scan_bwd"))
    g["a_re"], g["a_im"] = jnp.sum(da_re, axis=0), jnp.sum(da_im, axis=0)
    (du_b,) = _bd_apply([l_re, l_im], [w["wb_re"], w["wb_im"]], [[(0, 0), (1, 1)]], "nt", name=nm("ssmb_bu"))
    g["wb_re"], g["wb_im"] = _bd_grads([z, l_re, l_im], [cb, sb, sb], [(0, 1), (0, 2)], col_blocks=[U_BLK, 0, 0], name=nm("ssmg_bu"))
    dz = jnp.concatenate(list(dgates) + [(du_a + du_b).astype(BF16), dcb, dcc, dcx, dq.astype(BF16), dk.astype(BF16), dv.astype(BF16)], axis=1)
    g["w_in"] = _mm(s["h"], dz, "tn", tm=512, tn=2944, name=nm("mmg_in"))
    dx, g["norm_mix"] = _mm(dz, w["w_in"], "nt", tm=512, tk=2944, rms_back=(s["x"], dx1, w["norm_mix"]), name=nm("mmb_in"))
    return dx, g, dbias


def _loss_and_seed(x, target, g_final):
    def fn(xv, tv, gv):
        y, vjp = jax.vjp(_rms, xv, gv)
        err = y - tv
        dx, dg = vjp(err * (1.0 / D_MODEL))
        return [dx], [jnp.sum(err * err, axis=0, keepdims=True) * (0.5 / D_MODEL), dg]

    return _rw(fn, [(x, None, _c0), (target, None, _c0)], [(D_MODEL, None, F32, _c0)], params=[g_final],
               reds=[((1, D_MODEL), None, _c0)] * 2, name="loss_head")


def _local_step(x, p, target, wt):
    bias, bias_vjp = jax.vjp(_band_bias, wt["rel_bias"])
    layers, disc_vjps = [], []
    for i in range(DEPTH):
        ssm_p = [wt[k][i] for k in ("ssm_lambda_re", "ssm_lambda_im", "ssm_b_re", "ssm_b_im", "ssm_c_re", "ssm_c_im", "ssm_log_dt")]
        (a_re, a_im, wb_re, wb_im, cm_re, cm_imn), disc_vjp = jax.vjp(_ssm_disc, *ssm_p)
        scan_fwd, scan_rev = _scan_tables(a_re, a_im)
        layers.append(dict(
            norm_mix=wt["norm_mix"][i][None], w_in=wt["w_in"][i], wb_re=wb_re.astype(BF16), wb_im=wb_im.astype(BF16),
            cm_re=cm_re.astype(BF16), cm_imn=cm_imn.astype(BF16), scan_fwd=scan_fwd, scan_rev=scan_rev,
            ssm_d=wt["ssm_d"][i][None], ssm_w_glu=wt["ssm_w_glu"][i], conv_w=wt["conv_w"][i],
            sinks=wt["attn_sinks"][i][:, None], w_branch=wt["w_branch"][i], w_out=wt["w_out"][i],
            norm_ffn=wt["norm_ffn"][i][None], w_ffn_in=wt["w_ffn_in"][i], w_ffn_out=wt["w_ffn_out"][i],
            norm_ple=wt["norm_ple"][i][None], w_ple_gate=wt["w_ple_gate"][i], w_ple_proj=wt["w_ple_proj"][i]))
        disc_vjps.append(disc_vjp)

    saved = []
    for i in range(DEPTH):
        h = _rms_fwd(x, layers[0]["norm_mix"], "rms_mix_l0") if i == 0 else h
        x, h, s = _layer_fwd(x, h, p[i], layers[i], bias, i, layers[i + 1]["norm_mix"] if i + 1 < DEPTH else None)
        saved.append(s)
    dx, loss_cols, g_final = _loss_and_seed(x, target, wt["norm_final"][None])
    loss = jnp.sum(loss_cols)

    per_layer = [None] * DEPTH
    dbias = None
    for i in reversed(range(DEPTH)):
        dx, g, db = _layer_bwd(dx, saved[i], layers[i], bias, i)
        dbias = db if dbias is None else dbias + db
        (g["ssm_lambda_re"], g["ssm_lambda_im"], g["ssm_b_re"], g["ssm_b_im"], g["ssm_c_re"], g["ssm_c_im"], g["ssm_log_dt"]) = \
            disc_vjps[i]((g.pop("a_re"), g.pop("a_im"), g.pop("wb_re"), g.pop("wb_im"), g.pop("cm_re"), g.pop("cm_imn")))
        g["attn_sinks"] = g.pop("sinks")
        for k in ("norm_mix", "norm_ffn", "norm_ple", "ssm_d"):
            g[k] = g[k][0]
        per_layer[i] = g
    big_names = [name for name, _, _ in BIG]
    big = {k: [per_layer[i][k] for i in range(DEPTH)] for k in big_names}
    small = {k: jnp.stack([per_layer[i][k] for i in range(DEPTH)]) for k in per_layer[0] if k not in big_names}
    (small["rel_bias"],) = bias_vjp(dbias)
    small["norm_final"] = g_final[0]
    return loss, dx, small, big


HBM_SPEC = pl.BlockSpec(memory_space=pltpu.HBM)


def _position():
    x, y, c = lax.axis_index("x"), lax.axis_index("y"), lax.axis_index("c")
    other_chips = [(1 - x, y), (x, 1 - y), (1 - x, 1 - y)]
    return x, y, c, other_chips


def _row_chunks(rows, n=COPY_CHUNKS):
    rq = rows // n
    assert rq * n == rows and rq % 16 == 0, rows
    return [pl.ds(q * rq, rq) for q in range(n)]


def _place(buf, val, idx, name):
    n, rows, width = buf.shape
    tm = _pick(rows, 512, 16)

    def body(idx_ref, buf_ref, v_ref, o_ref):
        o_ref[0] = v_ref[...]

    grid_spec = pltpu.PrefetchScalarGridSpec(
        num_scalar_prefetch=1, grid=(rows // tm,),
        in_specs=[pl.BlockSpec(memory_space=pl.ANY), pl.BlockSpec((tm, width), lambda i, idx_ref: (i, 0))],
        out_specs=pl.BlockSpec((1, tm, width), lambda i, idx_ref: (idx_ref[0], i, 0)))
    return pl.pallas_call(
        body, name=name, grid_spec=grid_spec, out_shape=jax.ShapeDtypeStruct(buf.shape, buf.dtype), input_output_aliases={1: 0},
        compiler_params=pltpu.CompilerParams(dimension_semantics=("arbitrary",)),
    )(jnp.asarray(idx, jnp.int32).reshape(1), buf, val)


def _allgather_weights(locals_):
    nb, nq = len(locals_), COPY_CHUNKS
    chunks = [_row_chunks(a.shape[1]) for a in locals_]

    def body(*refs):
        w_refs, out_refs = refs[:nb], refs[nb:2 * nb]
        send_sems, recv_sems = refs[2 * nb:]
        x, y, c, chips = _position()
        me = 2 * x + y
        sibling = (x, y, 1 - c)

        def copy(b, kind, q, src, dst, to):
            k = (b * 6 + kind) * nq + q
            return pltpu.make_async_remote_copy(src_ref=src, dst_ref=dst, send_sem=send_sems.at[k], recv_sem=recv_sems.at[k],
                                                device_id=to, device_id_type=MESH)

        first = [copy(b, j, q, w_refs[b].at[c, chunks[b][q]], out_refs[b].at[me, c, chunks[b][q]], (*chip, c))
                 for q in range(nq) for b in range(nb) for j, chip in enumerate(chips)]
        for cp in first:
            cp.start()
        passed = []
        for q in range(nq):
            for b in range(nb):
                for j, (px, py) in enumerate(chips):
                    landed = out_refs[b].at[2 * px + py, c, chunks[b][q]]
                    copy(b, j, q, landed, landed, (px, py, c)).wait_recv()
                    fwd = copy(b, 3 + j, q, landed, landed, sibling)
                    fwd.start()
                    passed.append(fwd)
        for q in range(nq):
            for b in range(nb):
                for j, (px, py) in enumerate(chips):
                    landed = out_refs[b].at[2 * px + py, 1 - c, chunks[b][q]]
                    copy(b, 3 + j, q, landed, landed, sibling).wait_recv()
        for cp in first + passed:
            cp.wait_send()

    return pl.pallas_call(
        body, name="allgather_weights", in_specs=[HBM_SPEC] * nb, out_specs=[HBM_SPEC] * nb,
        out_shape=[jax.ShapeDtypeStruct((N_SHARD,) + a.shape, a.dtype) for a in locals_],
        scratch_shapes=[pltpu.SemaphoreType.DMA((nb * 6 * nq,)), pltpu.SemaphoreType.DMA((nb * 6 * nq,))],
    )(*locals_)


def _sibling_exchange(bufs):
    nb, nq, ns = len(bufs), COPY_CHUNKS, N_SHARD
    chunks = [_row_chunks(a.shape[2]) for a in bufs]

    def body(*refs):
        g_refs, got_refs = refs[:nb], refs[nb:2 * nb]
        send_sems, recv_sems = refs[2 * nb:]
        x, y, c, _ = _position()
        swaps = [pltpu.make_async_remote_copy(src_ref=g_refs[b].at[s, 1 - c, chunks[b][q]], dst_ref=got_refs[b].at[s, chunks[b][q]],
                                              send_sem=send_sems.at[(b * ns + s) * nq + q], recv_sem=recv_sems.at[(b * ns + s) * nq + q],
                                              device_id=(x, y, 1 - c), device_id_type=MESH)
                 for b in range(nb) for s in range(ns) for q in range(nq)]
        for cp in swaps:
            cp.start()
        for cp in swaps:
            cp.wait()

    return pl.pallas_call(
        body, name="grad_sibling_exchange", in_specs=[HBM_SPEC] * nb, out_specs=[HBM_SPEC] * nb,
        out_shape=[jax.ShapeDtypeStruct((ns,) + a.shape[2:], a.dtype) for a in bufs],
        scratch_shapes=[pltpu.SemaphoreType.DMA((nb * ns * nq,)), pltpu.SemaphoreType.DMA((nb * ns * nq,))],
    )(*bufs)


def _chip_exchange(parts):
    nb, nq = len(parts), COPY_CHUNKS
    chunks = [_row_chunks(a.shape[1]) for a in parts]

    def body(*refs):
        b_refs, got_refs = refs[:nb], refs[nb:2 * nb]
        send_sems, recv_sems = refs[2 * nb:]
        x, y, c, chips = _position()
        sends = [pltpu.make_async_remote_copy(src_ref=b_refs[b].at[2 * px + py, chunks[b][q]], dst_ref=got_refs[b].at[j, chunks[b][q]],
                                              send_sem=send_sems.at[(b * 3 + j) * nq + q], recv_sem=recv_sems.at[(b * 3 + j) * nq + q],
                                              device_id=(px, py, c), device_id_type=MESH)
                 for q in range(nq) for b in range(nb) for j, (px, py) in enumerate(chips)]
        for cp in sends:
            cp.start()
        for cp in sends:
            cp.wait()

    return pl.pallas_call(
        body, name="grad_chip_exchange", in_specs=[HBM_SPEC] * nb, out_specs=[HBM_SPEC] * nb,
        out_shape=[jax.ShapeDtypeStruct((N_SHARD - 1,) + a.shape[1:], a.dtype) for a in parts],
        scratch_shapes=[pltpu.SemaphoreType.DMA((nb * 3 * nq,)), pltpu.SemaphoreType.DMA((nb * 3 * nq,))],
    )(*parts)


def _sibling_gather(halves):
    nb, nq = len(halves), 2 * COPY_CHUNKS
    chunks = [_row_chunks(a.shape[0], nq) for a in halves]

    def body(*refs):
        h_refs, out_refs = refs[:nb], refs[nb:2 * nb]
        send_sems, recv_sems = refs[2 * nb:]
        x, y, c, _ = _position()

        def chunk(b, q, half_idx):
            return pltpu.make_async_remote_copy(src_ref=h_refs[b].at[chunks[b][q]], dst_ref=out_refs[b].at[half_idx, chunks[b][q]],
                                                send_sem=send_sems.at[b * nq + q], recv_sem=recv_sems.at[b * nq + q],
                                                device_id=(x, y, 1 - c), device_id_type=MESH)

        pushes = [chunk(b, q, c) for b in range(nb) for q in range(nq)]
        for cp in pushes:
            cp.start()
        for b in range(nb):
            for q in range(nq):
                chunk(b, q, 1 - c).wait_recv()
        for cp in pushes:
            cp.wait_send()

    return pl.pallas_call(
        body, name="grad_sibling_gather", in_specs=[HBM_SPEC] * nb, out_specs=[HBM_SPEC] * nb,
        out_shape=[jax.ShapeDtypeStruct((2,) + a.shape, a.dtype) for a in halves],
        scratch_shapes=[pltpu.SemaphoreType.DMA((nb * nq,)), pltpu.SemaphoreType.DMA((nb * nq,))],
    )(*halves)


def _gather_partials(part):
    r, lanes = part.shape

    def body(p_ref, out_ref, send_sems, recv_sems):
        x, y, c, _ = _position()
        flips = [(fx, fy, fc) for fx in (0, 1) for fy in (0, 1) for fc in (0, 1)][1:]
        sends = []
        for k, (fx, fy, fc) in enumerate(flips):
            cp = pltpu.make_async_remote_copy(src_ref=p_ref, dst_ref=out_ref.at[4 * x + 2 * y + c], send_sem=send_sems.at[k],
                                              recv_sem=recv_sems.at[k], device_id=(x ^ fx, y ^ fy, c ^ fc), device_id_type=MESH)
            cp.start()
            sends.append(cp)
        for k, (fx, fy, fc) in enumerate(flips):
            src = out_ref.at[4 * (x ^ fx) + 2 * (y ^ fy) + (c ^ fc)]
            pltpu.make_async_remote_copy(src_ref=src, dst_ref=src, send_sem=send_sems.at[k], recv_sem=recv_sems.at[k],
                                         device_id=(x ^ fx, y ^ fy, c ^ fc), device_id_type=MESH).wait_recv()
        for cp in sends:
            cp.wait_send()

    return pl.pallas_call(
        body, name="small_gather_partials", in_specs=[HBM_SPEC], out_specs=HBM_SPEC,
        out_shape=jax.ShapeDtypeStruct((8, r, lanes), part.dtype),
        scratch_shapes=[pltpu.SemaphoreType.DMA((7,)), pltpu.SemaphoreType.DMA((7,))],
    )(part)


def _sum_leading(stack, name, also_bf16=False):
    k, r, lanes = stack.shape
    tm = _pick(r, 256, 16)
    outs = [jax.ShapeDtypeStruct((r, lanes), F32)] + ([jax.ShapeDtypeStruct((r, lanes), BF16)] if also_bf16 else [])

    def body(s_ref, *o_refs):
        acc = s_ref[0].astype(F32)
        for i in range(1, k):
            acc = acc + s_ref[i].astype(F32)
        for o in o_refs:
            o[...] = acc.astype(o.dtype)

    spec = pl.BlockSpec((tm, lanes), lambda i: (i, 0))
    return pl.pallas_call(
        body, name=name, grid=(r // tm,), in_specs=[pl.BlockSpec((k, tm, lanes), lambda i: (0, i, 0))],
        out_specs=[spec] * len(outs), out_shape=outs,
        compiler_params=pltpu.CompilerParams(dimension_semantics=("parallel",)),
    )(stack)


def _add_pair(g2, got, half, name):
    ns, _, rows, width = g2.shape
    tm = _pick(rows, 256, 16)
    spec = pl.BlockSpec((1, tm, width), lambda s, i, h_ref: (s, i, 0))

    def body(h_ref, a_ref, b_ref, f_ref, o_ref):
        acc = a_ref[0] + b_ref[...]
        f_ref[...] = acc
        o_ref[...] = acc.astype(BF16)

    grid_spec = pltpu.PrefetchScalarGridSpec(
        num_scalar_prefetch=1, grid=(ns, rows // tm),
        in_specs=[pl.BlockSpec((1, 1, tm, width), lambda s, i, h_ref: (s, h_ref[0], i, 0)), spec], out_specs=[spec, spec])
    return pl.pallas_call(
        body, name=name, grid_spec=grid_spec,
        out_shape=[jax.ShapeDtypeStruct(got.shape, F32), jax.ShapeDtypeStruct(got.shape, BF16)],
        compiler_params=pltpu.CompilerParams(dimension_semantics=("parallel", "parallel")),
    )(jnp.asarray(half, jnp.int32).reshape(1), g2, got)


def _add_own(parts, got, mine, name):
    _, rows, width = parts.shape
    tm = _pick(rows, 256, 16)

    def body(m_ref, p_ref, g_ref, out_ref):
        acc = p_ref[0]
        for j in range(g_ref.shape[0]):
            acc = acc + g_ref[j].astype(F32)
        out_ref[...] = acc

    grid_spec = pltpu.PrefetchScalarGridSpec(
        num_scalar_prefetch=1, grid=(rows // tm,),
        in_specs=[pl.BlockSpec((1, tm, width), lambda i, m_ref: (m_ref[0], i, 0)),
                  pl.BlockSpec((got.shape[0], tm, width), lambda i, m_ref: (0, i, 0))],
        out_specs=pl.BlockSpec((tm, width), lambda i, m_ref: (i, 0)))
    return pl.pallas_call(
        body, name=name, grid_spec=grid_spec, out_shape=jax.ShapeDtypeStruct((rows, width), F32),
        compiler_params=pltpu.CompilerParams(dimension_semantics=("parallel",)),
    )(jnp.asarray(mine, jnp.int32).reshape(1), parts, got)


def _local_shape(shape, axis):
    return tuple(d // N_SHARD if a == axis else d for a, d in enumerate(shape))


def _big_sizes():
    return [DEPTH * int(np.prod(_local_shape(shape, axis))) for _, shape, axis in FLAT_BIG]


COL_SHARDED = (("w_in", IN_WIDTH // N_SHARD), ("w_ffn_in", 2 * FFN_HIDDEN // N_SHARD))
FLAT_BIG = tuple(entry for entry in BIG if entry[0] not in [name for name, _ in COL_SHARDED])
FLAT_ROW_TILE = 256
ELEMENTWISE_BIG = ("conv_w",)


def _three_bf16(w):
    hi = w.astype(BF16)
    r1 = w - hi.astype(F32)
    mid = r1.astype(BF16)
    lo = (r1 - mid.astype(F32)).astype(BF16)
    return jnp.stack([hi, mid, lo], axis=-1)


PIECE_ROWS = 16


def _flat_layout(for_weights):
    pieces = []
    for (name, shape, axis), size in zip(FLAT_BIG, _big_sizes(), strict=True):
        n = size * (3 if for_weights and name in ELEMENTWISE_BIG else 1)
        rows = -(-n // (LANES * PIECE_ROWS)) * PIECE_ROWS
        pieces.append((name, shape, axis, n, rows))
    total = sum(p[-1] for p in pieces)
    half = -(-total // (2 * FLAT_ROW_TILE)) * FLAT_ROW_TILE
    return pieces, half


def _to_rows(flat, rows):
    lead, n = flat.shape[:-1], flat.shape[-1]
    fill = jnp.zeros(lead + (rows * LANES - n,), flat.dtype)
    return jnp.concatenate([flat, fill], axis=-1).reshape(lead + (rows, LANES))


def _pack_local_weights(wl):
    pieces, half = _flat_layout(True)
    parts = [_to_rows((_three_bf16(wl[name]) if name in ELEMENTWISE_BIG else wl[name].astype(BF16)).reshape(-1), rows)
             for name, _, _, _, rows in pieces]
    parts.append(jnp.zeros((2 * half - sum(p[-1] for p in pieces), LANES), BF16))
    return jnp.concatenate(parts, axis=0).reshape(2, half, LANES)


def _unpack_local(flat):
    pieces, _ = _flat_layout(False)
    flat = flat.reshape(-1, LANES)
    out, off = {}, 0
    for name, shape, axis, n, rows in pieces:
        out[name] = flat[off:off + rows].reshape(-1)[:n].reshape((DEPTH,) + _local_shape(shape, axis))
        off += rows
    return out


def _unpack_gathered(gathered):
    pieces, _ = _flat_layout(True)
    out, off = {}, 0
    for name, shape, axis, n, rows in pieces:
        local = (N_SHARD, DEPTH) + _local_shape(shape, axis)
        seg = gathered[:, off:off + rows].reshape(N_SHARD, -1)[:, :n]
        off += rows
        if name in ELEMENTWISE_BIG:
            parts = seg.reshape(local + (3,)).astype(F32)
            seg = (parts[..., 0] + parts[..., 1]) + parts[..., 2]
        else:
            seg = seg.reshape(local)
        out[name] = jnp.moveaxis(seg, 0, 1 + axis).reshape((DEPTH,) + shape)
    return out


def _col_segments(name, c):
    if name != "w_in":
        return [(s, 0, c, c * s) for s in range(N_SHARD)]
    segs = []
    for s in range(N_SHARD):
        lo, hi = c * s, c * (s + 1)
        if lo < MAIN_WIDTH:
            segs.append((s, 0, min(hi, MAIN_WIDTH) - lo, GATES_WIDTH + lo))
        if hi > MAIN_WIDTH:
            first = max(lo, MAIN_WIDTH)
            segs.append((s, first - lo, c, first - MAIN_WIDTH))
    return segs


def _join_col_shards(shards, segs, name):
    ns, depth, rows, c = shards.shape
    tm = _pick(rows, 256, 16)

    def body(i_ref, o_ref):
        for s, lo, hi, start in segs:
            o_ref[0, :, start:start + hi - lo] = i_ref[s, 0, :, lo:hi]

    return pl.pallas_call(
        body, name=name, grid=(depth, rows // tm),
        in_specs=[pl.BlockSpec((ns, 1, tm, c), lambda l, i: (0, l, i, 0))],
        out_specs=pl.BlockSpec((1, tm, ns * c), lambda l, i: (l, i, 0)),
        out_shape=jax.ShapeDtypeStruct((depth, rows, ns * c), shards.dtype),
        compiler_params=pltpu.CompilerParams(dimension_semantics=("parallel", "parallel")),
    )(shards)


def _split_col_shards(full, segs, layer, stacked, name):
    rows, width = full.shape
    c = width // N_SHARD
    tm = _pick(rows, 256, 16)

    def body(*refs):
        i_ref, o_ref = refs[0], refs[-1]
        for s, lo, hi, start in segs:
            o_ref[s, 0, :, lo:hi] = i_ref[:, start:start + hi - lo]

    return pl.pallas_call(
        body, name=name, grid=(rows // tm,),
        in_specs=[pl.BlockSpec((tm, width), lambda i: (i, 0))] + ([] if stacked is None else [pl.BlockSpec(memory_space=pl.ANY)]),
        out_specs=pl.BlockSpec((N_SHARD, 1, tm, c), lambda i: (0, layer, i, 0)),
        out_shape=jax.ShapeDtypeStruct((N_SHARD, DEPTH, rows, c), full.dtype),
        input_output_aliases={} if stacked is None else {1: 0},
        compiler_params=pltpu.CompilerParams(dimension_semantics=("parallel",)),
    )(*([full] if stacked is None else [full, stacked]))


def _pack_full_grads(big_grads):
    pieces, half = _flat_layout(False)
    parts = []
    for name, shape, axis, _, rows in pieces:
        per_layer = []
        for gfull in big_grads[name]:
            split = gfull.reshape(shape[:axis] + (N_SHARD, shape[axis] // N_SHARD) + shape[axis + 1:])
            per_layer.append(jnp.moveaxis(split, axis, 0).reshape(N_SHARD, -1))
        parts.append(_to_rows(jnp.concatenate(per_layer, axis=1), rows))
    parts.append(jnp.zeros((N_SHARD, 2 * half - sum(p[-1] for p in pieces), LANES), F32))
    return jnp.concatenate(parts, axis=1).reshape(N_SHARD, 2, half, LANES)


def _pack_small(grads):
    flat = jnp.concatenate([grads[name].reshape(-1) for name in SMALL])
    r = -(-flat.shape[0] // (8 * LANES)) * 8
    return jnp.pad(flat, (0, r * LANES - flat.shape[0])).reshape(r, LANES)


def _unpack_small(flat, like):
    flat = flat.reshape(-1)
    out, off = {}, 0
    for name in SMALL:
        size = int(np.prod(like[name].shape))
        out[name] = flat[off:off + size].reshape(like[name].shape)
        off += size
    return out


def _adamw(w, g, m, v, name):
    shape = w.shape
    cols = shape[-1]
    rows = int(np.prod(shape[:-1])) if len(shape) > 1 else 1
    w2, g2, m2, v2 = (a.reshape(rows, cols) for a in (w, g, m, v))

    def fn(wv, gv, mv, vv):
        mn = ADAM_B1 * mv + (1.0 - ADAM_B1) * gv
        vn = ADAM_B2 * vv + (1.0 - ADAM_B2) * jnp.square(gv)
        m_hat = mn / (1.0 - ADAM_B1 ** ADAM_STEP)
        v_hat = vn / (1.0 - ADAM_B2 ** ADAM_STEP)
        delta = -ADAM_LR * (m_hat / (jnp.sqrt(v_hat) + ADAM_EPS) + ADAM_WD * wv)
        return [delta, mn, vn], []

    tm = 256 if rows % 8 == 0 and rows > 256 else rows
    res = _rw(fn, [(a, None, _c0) for a in (w2, g2, m2, v2)], [(cols, None, F32, _c0)] * 3, tm=tm, name=name)
    return [r.reshape(shape) for r in res]


def _step(x, p, target, weights, moments_m, moments_v):
    xi, yi, ci = lax.axis_index("x"), lax.axis_index("y"), lax.axis_index("c")
    chip = 2 * xi + yi
    half_rows = DEPTH // 2 * D_MODEL
    locals_ = [_pack_local_weights(weights)] + [weights[name].astype(BF16).reshape(2, half_rows, c) for name, c in COL_SHARDED]
    gathered = _allgather_weights(locals_)
    gathered = [_place(g.reshape(N_SHARD, 2 * a.shape[1], a.shape[2]), a.reshape(2 * a.shape[1], a.shape[2]), chip, f"place_own_weights_{k}")
                for k, (g, a) in enumerate(zip(gathered, locals_, strict=True))]
    wt = dict(_unpack_gathered(gathered[0]))
    for (name, c), g in zip(COL_SHARDED, gathered[1:], strict=True):
        wt[name] = _join_col_shards(g.reshape(N_SHARD, DEPTH, D_MODEL, c), _col_segments(name, c), f"join_col_shards_{name}")
    for name in SMALL:
        wt[name] = weights[name]
    loss, dx, small_grads, big_grads = _local_step(x[0], p[:, 0], target[0], wt)
    loss = lax.psum(loss, ("x", "y", "c"))
    bufs = [_pack_full_grads(big_grads)]
    for name, c in COL_SHARDED:
        stacked = None
        for li, g in enumerate(big_grads[name]):
            stacked = _split_col_shards(g, _col_segments(name, c), li, stacked, f"split_col_shards_{name}_l{li}")
        bufs.append(stacked.reshape(N_SHARD, 2, half_rows, c))
    gots = _sibling_exchange(bufs)
    sums = [_add_pair(b, g, ci, f"grad_add_sibling_{k}") for k, (b, g) in enumerate(zip(bufs, gots, strict=True))]
    others = _chip_exchange([s_bf16 for _, s_bf16 in sums])
    halves = [_add_own(s_f32, o, chip, f"grad_add_chips_{k}") for k, ((s_f32, _), o) in enumerate(zip(sums, others, strict=True))]
    both = [_place(b, h, ci, f"place_own_half_{k}") for k, (b, h) in enumerate(zip(_sibling_gather(halves), halves, strict=True))]
    reduced = _unpack_local(both[0])
    for (name, c), b in zip(COL_SHARDED, both[1:], strict=True):
        reduced[name] = b.reshape(DEPTH, D_MODEL, c)
    small_part = _pack_small(small_grads)
    small_all = _place(_gather_partials(small_part), small_part, 4 * xi + 2 * yi + ci, "place_own_small")
    reduced.update(_unpack_small(_sum_leading(small_all, "small_sum")[0], {k: weights[k] for k in SMALL}))
    outs_g, outs_d, outs_m, outs_v = [], [], [], []
    for name in WEIGHTS:
        d, mn, vn = _adamw(weights[name], reduced[name], moments_m[name], moments_v[name], f"adamw_{name}")
        outs_g.append(reduced[name])
        outs_d.append(d)
        outs_m.append(mn)
        outs_v.append(vn)
    return (loss, dx[None], *outs_g, *outs_d, *outs_m, *outs_v)


def kernel(x, p, rel_bias, norm_mix, w_in, ssm_lambda_re, ssm_lambda_im, ssm_b_re, ssm_b_im, ssm_c_re, ssm_c_im, ssm_d, ssm_log_dt, ssm_w_glu, conv_w, attn_sinks, w_branch, w_out, norm_ffn, w_ffn_in, w_ffn_out, norm_ple, w_ple_gate, w_ple_proj, norm_final, loss_target, m_rel_bias, m_norm_mix, m_w_in, m_ssm_lambda_re, m_ssm_lambda_im, m_ssm_b_re, m_ssm_b_im, m_ssm_c_re, m_ssm_c_im, m_ssm_d, m_ssm_log_dt, m_ssm_w_glu, m_conv_w, m_attn_sinks, m_w_branch, m_w_out, m_norm_ffn, m_w_ffn_in, m_w_ffn_out, m_norm_ple, m_w_ple_gate, m_w_ple_proj, m_norm_final, v_rel_bias, v_norm_mix, v_w_in, v_ssm_lambda_re, v_ssm_lambda_im, v_ssm_b_re, v_ssm_b_im, v_ssm_c_re, v_ssm_c_im, v_ssm_d, v_ssm_log_dt, v_ssm_w_glu, v_conv_w, v_attn_sinks, v_w_branch, v_w_out, v_norm_ffn, v_w_ffn_in, v_w_ffn_out, v_norm_ple, v_w_ple_gate, v_w_ple_proj, v_norm_final):
    weights = dict(rel_bias=rel_bias, norm_mix=norm_mix, w_in=w_in, ssm_lambda_re=ssm_lambda_re, ssm_lambda_im=ssm_lambda_im,
                   ssm_b_re=ssm_b_re, ssm_b_im=ssm_b_im, ssm_c_re=ssm_c_re, ssm_c_im=ssm_c_im, ssm_d=ssm_d, ssm_log_dt=ssm_log_dt,
                   ssm_w_glu=ssm_w_glu, conv_w=conv_w, attn_sinks=attn_sinks, w_branch=w_branch, w_out=w_out, norm_ffn=norm_ffn,
                   w_ffn_in=w_ffn_in, w_ffn_out=w_ffn_out, norm_ple=norm_ple, w_ple_gate=w_ple_gate, w_ple_proj=w_ple_proj,
                   norm_final=norm_final)
    moments_m = dict(rel_bias=m_rel_bias, norm_mix=m_norm_mix, w_in=m_w_in, ssm_lambda_re=m_ssm_lambda_re, ssm_lambda_im=m_ssm_lambda_im,
                     ssm_b_re=m_ssm_b_re, ssm_b_im=m_ssm_b_im, ssm_c_re=m_ssm_c_re, ssm_c_im=m_ssm_c_im, ssm_d=m_ssm_d,
                     ssm_log_dt=m_ssm_log_dt, ssm_w_glu=m_ssm_w_glu, conv_w=m_conv_w, attn_sinks=m_attn_sinks, w_branch=m_w_branch,
                     w_out=m_w_out, norm_ffn=m_norm_ffn, w_ffn_in=m_w_ffn_in, w_ffn_out=m_w_ffn_out, norm_ple=m_norm_ple,
                     w_ple_gate=m_w_ple_gate, w_ple_proj=m_w_ple_proj, norm_final=m_norm_final)
    moments_v = dict(rel_bias=v_rel_bias, norm_mix=v_norm_mix, w_in=v_w_in, ssm_lambda_re=v_ssm_lambda_re, ssm_lambda_im=v_ssm_lambda_im,
                     ssm_b_re=v_ssm_b_re, ssm_b_im=v_ssm_b_im, ssm_c_re=v_ssm_c_re, ssm_c_im=v_ssm_c_im, ssm_d=v_ssm_d,
                     ssm_log_dt=v_ssm_log_dt, ssm_w_glu=v_ssm_w_glu, conv_w=v_conv_w, attn_sinks=v_attn_sinks, w_branch=v_w_branch,
                     w_out=v_w_out, norm_ffn=v_norm_ffn, w_ffn_in=v_w_ffn_in, w_ffn_out=v_w_ffn_out, norm_ple=v_norm_ple,
                     w_ple_gate=v_w_ple_gate, w_ple_proj=v_w_ple_proj, norm_final=v_norm_final)
    return _step(x, p, loss_target, weights, moments_m, moments_v)
```
